```python
import math
import jax
import jax.numpy as jnp
from jax import lax
import numpy as np

D_MODEL = 2048
BATCH = 4
SEQ = 2048
DEPTH = 1

GRID_W = 64
CTX_LEN = 256
N_MOD = 6
NORM_EPS = 1e-6
RW_HEAD = 64
RW_WIDTH = D_MODEL
RW_HEADS = RW_WIDTH // RW_HEAD
LORA_W = 96
LORA_A = 96
LORA_G = 256
LN_X_EPS = 64e-5
SSM_WIDTH = D_MODEL
SSM_HEADDIM = 64
SSM_HEADS = SSM_WIDTH // SSM_HEADDIM
SSM_GROUPS = 8
SSM_HPG = SSM_HEADS // SSM_GROUPS
SSM_STATE = 128
SSM_CONV = 5
SSM_CHUNK = 128
PEER_HEADS = 8
PEER_KEYS = 128
PEER_EXPERTS = PEER_KEYS * PEER_KEYS
PEER_TOPK = 16
PEER_QDIM = 256
PEER_HALF = PEER_QDIM // 2
PEER_BLOCK = 128
RW_COLS = 3 * RW_WIDTH + 2 * LORA_W + 2 * LORA_A + LORA_G
XBC_COLS = SSM_WIDTH + 2 * SSM_GROUPS * SSM_STATE
IN_COLS = RW_COLS + SSM_WIDTH + XBC_COLS + 2 * SSM_HEADS + 2 * D_MODEL
IN_SPLITS = (RW_COLS, RW_COLS + SSM_WIDTH, RW_COLS + SSM_WIDTH + XBC_COLS, RW_COLS + SSM_WIDTH + XBC_COLS + 2 * SSM_HEADS)
RW_SPLITS = (RW_WIDTH, 2 * RW_WIDTH, 3 * RW_WIDTH, 3 * RW_WIDTH + LORA_W, 3 * RW_WIDTH + 2 * LORA_W, 3 * RW_WIDTH + 2 * LORA_W + LORA_A, 3 * RW_WIDTH + 2 * LORA_W + 2 * LORA_A)

kernel_name = 'hybrid_rwkv7_ssd_peer_dit_block'


def rms_norm(u, gain):
    uf = u.astype(jnp.float32)
    uf = uf * lax.rsqrt(jnp.mean(uf * uf, axis=-1, keepdims=True) + NORM_EPS)
    return (uf * gain.astype(jnp.float32)).astype(u.dtype)


def centred_token_shift(p, mu):
    pad = jnp.pad(p, ((0, 0), (1, 1), (0, 0)))
    return p + mu * (0.5 * (pad[:, :-2] + pad[:, 2:]) - p)


def raster_to_column(u, rows):
    b, s, ch = u.shape
    return u.reshape(b, rows, GRID_W, ch).transpose(0, 2, 1, 3).reshape(b, s, ch)


def column_to_raster(u, rows):
    b, s, ch = u.shape
    return u.reshape(b, GRID_W, rows, ch).transpose(0, 2, 1, 3).reshape(b, s, ch)


def wkv7_scan(s0, r, decay, k, v, kk, kka, reverse):
    def step(state, inp):
        r_t, w_t, k_t, v_t, kk_t, kka_t = inp
        sa = jnp.einsum('bhvk,bhk->bhv', state, kk_t)
        state = state * w_t[:, :, None, :] - sa[..., None] * kka_t[:, :, None, :] + v_t[..., None] * k_t[:, :, None, :]
        return state, jnp.einsum('bhvk,bhk->bhv', state, r_t)
    seq_major = tuple(jnp.moveaxis(u, 1, 0) for u in (r, decay, k, v, kk, kka))
    s_last, y = lax.scan(step, s0, seq_major, reverse=reverse)
    return jnp.moveaxis(y, 0, 1), s_last


def rwkv7_branch(slab, s0_fwd, s0_bwd, w0, w2, a0, a2, g2, k_k, k_a, r_k, ln_w, ln_b):
    bsz, t, _ = slab.shape
    f32 = jnp.float32
    r, k, v, wd_f, wd_b, ad_f, ad_b, gd = jnp.split(slab.astype(f32), RW_SPLITS, axis=-1)
    heads = lambda u: u.reshape(bsz, t, RW_HEADS, RW_HEAD)
    kk = heads(k * k_k.astype(f32))
    kk = kk / jnp.maximum(jnp.sqrt(jnp.sum(kk * kk, axis=-1, keepdims=True)), 1e-12)

    def direction_inputs(wd, ad, d):
        w_log = -jax.nn.softplus(-(w0[d].astype(f32) + jnp.tanh(wd) @ w2[d].astype(f32))) - 0.5
        a = jax.nn.sigmoid(a0[d].astype(f32) + ad @ a2[d].astype(f32))
        k_dir = k * (1.0 + (a - 1.0) * k_a.astype(f32))
        return heads(jnp.exp(-jnp.exp(w_log))), heads(a), heads(k_dir)

    rh, vh = heads(r), heads(v)
    dec_f, a_f, k_f = direction_inputs(wd_f, ad_f, 0)
    dec_b, a_b, k_b = direction_inputs(wd_b, ad_b, 1)
    y_f, s_fwd = wkv7_scan(s0_fwd, rh, dec_f, k_f, vh, kk, kk * a_f, reverse=False)
    y_b, s_bwd = wkv7_scan(s0_bwd, rh, dec_b, k_b, vh, kk, kk * a_b, reverse=True)
    y = y_f + y_b
    mean = jnp.mean(y, axis=-1, keepdims=True)
    var = jnp.mean(jnp.square(y - mean), axis=-1, keepdims=True)
    y = ((y - mean) * lax.rsqrt(var + LN_X_EPS)).reshape(bsz, t, RW_WIDTH) * ln_w.astype(f32) + ln_b.astype(f32)
    bonus = jnp.sum(rh * (k_f + k_b) * r_k.astype(f32), axis=-1, keepdims=True) * vh
    gate = jax.nn.sigmoid(gd) @ g2.astype(f32)
    out = (y + bonus.reshape(bsz, t, RW_WIDTH)) * gate
    return out, s_fwd, s_bwd


def centred_depthwise_conv(u, w):
    ch = u.shape[-1]
    return lax.conv_general_dilated(u, w.astype(jnp.float32)[:, None, :], window_strides=(1,), padding=[(SSM_CONV // 2, SSM_CONV // 2)], dimension_numbers=('NWC', 'WIO', 'NWC'), feature_group_count=ch)


def ssd_chunked(xdt, log_a, bm, cm, h0):
    bsz, t = xdt.shape[:2]
    nc = t // SSM_CHUNK
    xdt = xdt.reshape(bsz, nc, SSM_CHUNK, SSM_GROUPS, SSM_HPG, SSM_HEADDIM)
    log_a = log_a.reshape(bsz, nc, SSM_CHUNK, SSM_GROUPS, SSM_HPG)
    bm = bm.reshape(bsz, nc, SSM_CHUNK, SSM_GROUPS, SSM_STATE)
    cm = cm.reshape(bsz, nc, SSM_CHUNK, SSM_GROUPS, SSM_STATE)
    cs = jnp.cumsum(log_a, axis=2)
    cs_t = jnp.moveaxis(cs, 2, -1)
    lower = jnp.tril(jnp.ones((SSM_CHUNK, SSM_CHUNK), dtype=bool))
    seg = jnp.exp(jnp.where(lower, cs_t[..., :, None] - cs_t[..., None, :], -jnp.inf))
    cb = jnp.einsum('bclgn,bcsgn->bcgls', cm, bm)
    y_diag = jnp.einsum('bcgels,bcsgep->bclgep', cb[:, :, :, None] * seg, xdt)
    to_end = jnp.exp(cs[:, :, -1:] - cs)
    states = jnp.einsum('bclgn,bclgep->bcgepn', bm, xdt * to_end[..., None])
    chunk_decay = jnp.exp(cs[:, :, -1])

    def carry_step(h, inp):
        st, dec = inp
        return h * dec[..., None, None] + st, h
    h_last, h_in = lax.scan(carry_step, h0, (jnp.moveaxis(states, 1, 0), jnp.moveaxis(chunk_decay, 1, 0)))
    h_in = jnp.moveaxis(h_in, 0, 1)
    y_off = jnp.einsum('bclgn,bcgepn->bclgep', cm, h_in) * jnp.exp(cs)[..., None]
    return (y_diag + y_off).reshape(bsz, t, SSM_GROUPS, SSM_HPG, SSM_HEADDIM), h_last


def mamba2_branch(xbc, dt_raw, h0_fwd, h0_bwd, conv_w, conv_b, dt_bias, a_log, d_skip):
    bsz, t, _ = xbc.shape
    f32 = jnp.float32
    xbc = jax.nn.silu(centred_depthwise_conv(xbc.astype(f32), conv_w) + conv_b.astype(f32))
    xs, bm, cm = jnp.split(xbc, (SSM_WIDTH, SSM_WIDTH + SSM_GROUPS * SSM_STATE), axis=-1)
    xs = xs.reshape(bsz, t, SSM_GROUPS, SSM_HPG, SSM_HEADDIM)
    bm = bm.reshape(bsz, t, SSM_GROUPS, SSM_STATE)
    cm = cm.reshape(bsz, t, SSM_GROUPS, SSM_STATE)
    dt = jax.nn.softplus(dt_raw.astype(f32).reshape(bsz, t, 2, SSM_HEADS) + dt_bias.astype(f32))
    dt = dt.reshape(bsz, t, 2, SSM_GROUPS, SSM_HPG)
    log_a = -jnp.exp(a_log.astype(f32)).reshape(2, SSM_GROUPS, SSM_HPG) * dt
    y_f, h_f = ssd_chunked(xs * dt[:, :, 0, ..., None], log_a[:, :, 0], bm, cm, h0_fwd)
    flip = lambda u: jnp.flip(u, axis=1)
    y_b, h_b = ssd_chunked(flip(xs * dt[:, :, 1, ..., None]), flip(log_a[:, :, 1]), flip(bm), flip(cm), h0_bwd)
    y = y_f + flip(y_b) + d_skip.astype(f32).reshape(SSM_GROUPS, SSM_HPG)[..., None] * xs
    return y.reshape(bsz, t, SSM_WIDTH), h_f, h_b


def gated_group_rmsnorm(y, z, gain):
    bsz, t, _ = y.shape
    u = (y * jax.nn.silu(z.astype(jnp.float32))).reshape(bsz, t, SSM_GROUPS, SSM_WIDTH // SSM_GROUPS)
    u = u * lax.rsqrt(jnp.mean(u * u, axis=-1, keepdims=True) + NORM_EPS)
    return u.reshape(bsz, t, SSM_WIDTH) * gain.astype(jnp.float32)


def merge_branches(y_a, y_b, gates, w_a, w_b, w_o):
    g_a, g_b = jnp.split(gates, 2, axis=-1)
    m = jax.nn.sigmoid(g_a) * (y_a @ w_a) + jax.nn.sigmoid(g_b) * (y_b @ w_b)
    return m @ w_o


def peer_ffn(h, wq, subkeys, u_tab, v_tab):
    bsz, t, d = h.shape
    n = bsz * t
    tok = h.reshape(n, d)
    q = (tok @ wq).astype(jnp.float32).reshape(n, PEER_HEADS, 2, PEER_HALF)
    s = jnp.einsum('nhsd,hskd->nhsk', q, subkeys.astype(jnp.float32))
    top_s, top_i = lax.top_k(s, PEER_TOPK)
    cand = (top_s[:, :, 0, :, None] + top_s[:, :, 1, None, :]).reshape(n, PEER_HEADS, PEER_TOPK * PEER_TOPK)
    best_s, best_j = lax.top_k(cand, PEER_TOPK)
    i1 = jnp.take_along_axis(top_i[:, :, 0], best_j // PEER_TOPK, axis=-1)
    i2 = jnp.take_along_axis(top_i[:, :, 1], best_j % PEER_TOPK, axis=-1)
    expert = i1 * PEER_KEYS + i2
    gate = jax.nn.softmax(best_s, axis=-1)

    def block(args):
        xb, eb, gb = args
        act = jax.nn.gelu(jnp.einsum('nd,nhkd->nhk', xb, jnp.take(u_tab, eb, axis=0)), approximate=False)
        return jnp.einsum('nhk,nhkd->nd', (gb * act).astype(v_tab.dtype), jnp.take(v_tab, eb, axis=0))
    nb = n // PEER_BLOCK
    out = lax.map(block, (tok.reshape(nb, PEER_BLOCK, d), expert.reshape(nb, PEER_BLOCK, PEER_HEADS, PEER_TOPK), gate.reshape(nb, PEER_BLOCK, PEER_HEADS, PEER_TOPK)))
    return out.reshape(bsz, t, d).astype(h.dtype)


def setup_inputs(seed: int = 0) -> dict:
    key = jax.random.key(seed)
    ks = iter(jax.random.split(key, 48))
    L, D = DEPTH, D_MODEL
    nrm = lambda shape, scale: scale * jax.random.normal(next(ks), shape, jnp.float32)
    uni = lambda shape, lo, hi: jax.random.uniform(next(ks), shape, jnp.float32, minval=lo, maxval=hi)
    dt_init = jnp.exp(uni((L, 2, SSM_HEADS), math.log(1e-3), math.log(1e-1)))
    return {
        'x': nrm((BATCH, SEQ, D), 1.0),
        'c': nrm((BATCH, D), 1.0),
        'ctx': nrm((BATCH, CTX_LEN, D), 1.0),
        'c_ctx': nrm((D,), 1.0),
        'w_mod': nrm((L, D, N_MOD * D), 0.5 * D ** -0.5),
        'b_mod': nrm((L, N_MOD * D), 0.02),
        'norm_pre1': 1.0 + nrm((L, D), 0.1),
        'norm_post1': 1.0 + nrm((L, D), 0.1),
        'norm_pre2': 1.0 + nrm((L, D), 0.1),
        'norm_post2': 1.0 + nrm((L, D), 0.1),
        'w_in': nrm((L, D, IN_COLS), D ** -0.5),
        'rw_mu': uni((L, RW_COLS), 0.0, 1.0),
        'rw_w0': uni((L, 2, RW_WIDTH), -5.0, -1.0),
        'rw_w2': nrm((L, 2, LORA_W, RW_WIDTH), 0.1 * LORA_W ** -0.5),
        'rw_a0': nrm((L, 2, RW_WIDTH), 0.5),
        'rw_a2': nrm((L, 2, LORA_A, RW_WIDTH), 0.1 * LORA_A ** -0.5),
        'rw_g2': nrm((L, LORA_G, RW_WIDTH), LORA_G ** -0.5),
        'rw_k_k': 0.85 + nrm((L, RW_WIDTH), 0.05),
        'rw_k_a': 1.0 + nrm((L, RW_WIDTH), 0.05),
        'rw_r_k': nrm((L, RW_HEADS, RW_HEAD), 0.1),
        'rw_ln_w': 1.0 + nrm((L, RW_WIDTH), 0.1),
        'rw_ln_b': nrm((L, RW_WIDTH), 0.02),
        'ssm_conv_w': nrm((L, SSM_CONV, XBC_COLS), SSM_CONV ** -0.5),
        'ssm_conv_b': nrm((L, XBC_COLS), 0.02),
        'ssm_dt_bias': dt_init + jnp.log(-jnp.expm1(-dt_init)),
        'ssm_a_log': jnp.log(uni((L, 2, SSM_HEADS), 1.0, 16.0)),
        'ssm_d': 1.0 + nrm((L, SSM_HEADS), 0.1),
        'ssm_norm_w': 1.0 + nrm((L, SSM_WIDTH), 0.1),
        'w_branch_a': nrm((L, RW_WIDTH, D), RW_WIDTH ** -0.5),
        'w_branch_b': nrm((L, SSM_WIDTH, D), SSM_WIDTH ** -0.5),
        'w_out': nrm((L, D, D), D ** -0.5),
        'peer_wq': nrm((L, D, PEER_HEADS * PEER_QDIM), D ** -0.5),
        'peer_subkeys': nrm((L, PEER_HEADS, 2, PEER_KEYS, PEER_HALF), PEER_HALF ** -0.5),
        'peer_u': nrm((L, PEER_EXPERTS, D), D ** -0.5),
        'peer_v': nrm((L, PEER_EXPERTS, D), D ** -0.5),
    }


def reference(x, c, ctx, c_ctx, w_mod, b_mod, norm_pre1, norm_post1, norm_pre2, norm_post2, w_in, rw_mu, rw_w0, rw_w2, rw_a0, rw_a2, rw_g2, rw_k_k, rw_k_a, rw_r_k, rw_ln_w, rw_ln_b, ssm_conv_w, ssm_conv_b, ssm_dt_bias, ssm_a_log, ssm_d, ssm_norm_w, w_branch_a, w_branch_b, w_out, peer_wq, peer_subkeys, peer_u, peer_v):
    act_dtype = x.dtype
    bsz, seq, _ = x.shape
    rows = seq // GRID_W
    rw_zero = jnp.zeros((bsz, RW_HEADS, RW_HEAD, RW_HEAD), jnp.float32)
    ssm_zero = jnp.zeros((bsz, SSM_GROUPS, SSM_HPG, SSM_HEADDIM, SSM_STATE), jnp.float32)
    for l in range(DEPTH):
        mod_x = (jax.nn.silu(c) @ w_mod[l] + b_mod[l])[:, None, :]
        mod_c = (jax.nn.silu(c_ctx) @ w_mod[l] + b_mod[l])[None, None, :]
        sh1x, sc1x, g1x, sh2x, sc2x, g2x = jnp.split(mod_x, N_MOD, axis=-1)
        sh1c, sc1c, g1c, sh2c, sc2c, g2c = jnp.split(mod_c, N_MOD, axis=-1)

        hx = rms_norm(x, norm_pre1[l]) * (1.0 + sc1x) + sh1x
        hc = rms_norm(ctx, norm_pre1[l]) * (1.0 + sc1c) + sh1c
        rw_x, z_x, xbc_x, dt_x, gate_x = jnp.split(hx @ w_in[l], IN_SPLITS, axis=-1)
        rw_c, z_c, xbc_c, dt_c, gate_c = jnp.split(hc @ w_in[l], IN_SPLITS, axis=-1)

        rw_params = (rw_w0[l], rw_w2[l], rw_a0[l], rw_a2[l], rw_g2[l], rw_k_k[l], rw_k_a[l], rw_r_k[l], rw_ln_w[l], rw_ln_b[l])
        ya_c, s_fwd, s_bwd = rwkv7_branch(centred_token_shift(rw_c, rw_mu[l]), rw_zero, rw_zero, *rw_params)
        ya_x, _, _ = rwkv7_branch(centred_token_shift(rw_x, rw_mu[l]), s_fwd, s_bwd, *rw_params)

        ssm_params = (ssm_conv_w[l], ssm_conv_b[l], ssm_dt_bias[l], ssm_a_log[l], ssm_d[l])
        yb_c, h_fwd, h_bwd = mamba2_branch(xbc_c, dt_c, ssm_zero, ssm_zero, *ssm_params)
        yb_x, _, _ = mamba2_branch(raster_to_column(xbc_x, rows), raster_to_column(dt_x, rows), h_fwd, h_bwd, *ssm_params)
        yb_x = column_to_raster(yb_x, rows)

        mix_x = merge_branches(ya_x.astype(act_dtype), gated_group_rmsnorm(yb_x, z_x, ssm_norm_w[l]).astype(act_dtype), gate_x, w_branch_a[l], w_branch_b[l], w_out[l])
        x = x + g1x * rms_norm(mix_x, norm_post1[l])

        h2x = rms_norm(x, norm_pre2[l]) * (1.0 + sc2x) + sh2x
        x = x + g2x * rms_norm(peer_ffn(h2x, peer_wq[l], peer_subkeys[l], peer_u[l], peer_v[l]), norm_post2[l])

        if l + 1 < DEPTH:
            mix_c = merge_branches(ya_c.astype(act_dtype), gated_group_rmsnorm(yb_c, z_c, ssm_norm_w[l]).astype(act_dtype), gate_c, w_branch_a[l], w_branch_b[l], w_out[l])
            ctx = ctx + g1c * rms_norm(mix_c, norm_post1[l])
            h2c = rms_norm(ctx, norm_pre2[l]) * (1.0 + sc2c) + sh2c
            ctx = ctx + g2c * rms_norm(peer_ffn(h2c, peer_wq[l], peer_subkeys[l], peer_u[l], peer_v[l]), norm_post2[l])
    return x
```

```python
import functools
import math

import jax
import jax.numpy as jnp
from jax import lax
from jax.experimental import pallas as pl
from jax.experimental.pallas import tpu as pltpu

D_MODEL = 2048
GRID_W = 64
N_MOD = 6
NORM_EPS = 1e-6
RW_HEAD = 64
RW_WIDTH = D_MODEL
RW_HEADS = RW_WIDTH // RW_HEAD
LORA_W = 96
LORA_A = 96
LORA_G = 256
LN_X_EPS = 64e-5
SSM_WIDTH = D_MODEL
SSM_HEADDIM = 64
SSM_HEADS = SSM_WIDTH // SSM_HEADDIM
SSM_GROUPS = 8
SSM_HPG = SSM_HEADS // SSM_GROUPS
SSM_STATE = 128
SSM_CONV = 5
SSM_CHUNK = 128
PEER_HEADS = 8
PEER_KEYS = 128
PEER_TOPK = 16
PEER_QDIM = 256
PEER_HALF = PEER_QDIM // 2
PEER_BLOCK = 128
RW_COLS = 3 * RW_WIDTH + 2 * LORA_W + 2 * LORA_A + LORA_G
XBC_COLS = SSM_WIDTH + 2 * SSM_GROUPS * SSM_STATE
IN_SPLITS = (RW_COLS, RW_COLS + SSM_WIDTH, RW_COLS + SSM_WIDTH + XBC_COLS, RW_COLS + SSM_WIDTH + XBC_COLS + 2 * SSM_HEADS)
RW_SPLITS = (RW_WIDTH, 2 * RW_WIDTH, 3 * RW_WIDTH, 3 * RW_WIDTH + LORA_W, 3 * RW_WIDTH + 2 * LORA_W, 3 * RW_WIDTH + 2 * LORA_W + LORA_A, 3 * RW_WIDTH + 2 * LORA_W + 2 * LORA_A)

LANES = 128
WKV_CHUNK = 64
WKV_PAIR = LANES // RW_HEAD
WKV_GROUPS_PER_STEP = 4

_F32 = jnp.float32
_BF16 = jnp.bfloat16
_NT = (((1,), (1,)), ((), ()))
_TN = (((0,), (0,)), ((), ()))


def _mm(a, b, dims=None):
    a = a.astype(_BF16)
    b = b.astype(_BF16)
    if dims is None:
        return jnp.dot(a, b, preferred_element_type=_F32)
    return lax.dot_general(a, b, dims, preferred_element_type=_F32)


def _wkv_kernel(r_ref, lw_ref, k_ref, v_ref, a_ref, b_ref, s0_ref, y_ref, s_ref, *, reverse, chunk, groups):
    L = chunk
    L2 = WKV_PAIR * L

    @pl.when(pl.program_id(2) == 0)
    def _():
        s_ref[...] = s0_ref[...]

    ti = lax.broadcasted_iota(jnp.int32, (L, L), 0)
    tj = lax.broadcasted_iota(jnp.int32, (L, L), 1)
    before_incl = (ti <= tj) if reverse else (ti >= tj)
    cum = jnp.where(before_incl, 1.0, 0.0).astype(_BF16)

    lw = lw_ref[0]
    hi = lw.astype(_BF16)
    rem = lw - hi.astype(_F32)
    mid = rem.astype(_BF16)
    lo = (rem - mid.astype(_F32)).astype(_BF16)
    cs = (jnp.dot(cum, hi, preferred_element_type=_F32)
          + jnp.dot(cum, mid, preferred_element_type=_F32)
          + jnp.dot(cum, lo, preferred_element_type=_F32))
    tot = cs[0:1] if reverse else cs[L - 1:L]
    half = 0.5 * tot
    e_pos = jnp.exp(cs - half)
    e_neg = jnp.exp(half - cs)
    e_prev = jnp.exp(cs - lw - half)
    e_end = jnp.exp(tot - cs)
    e_half = jnp.exp(half)
    e_tot = jnp.exp(tot)

    rt = r_ref[0] * e_pos
    at = a_ref[0] * e_prev
    kt = k_ref[0] * e_neg
    bt = b_ref[0] * e_neg
    ke = k_ref[0] * e_end
    be = b_ref[0] * e_end
    vv = v_ref[0]

    lane = lax.broadcasted_iota(jnp.int32, (L, LANES), 1)
    first_head = lane < RW_HEAD

    def stack(x):
        return jnp.concatenate([jnp.where(first_head, x, 0.0), jnp.where(first_head, 0.0, x)], axis=0)

    si = lax.broadcasted_iota(jnp.int32, (L2, L2), 0)
    sj = lax.broadcasted_iota(jnp.int32, (L2, L2), 1)
    same_head = (si < L) == (sj < L)
    pi = jnp.where(si < L, si, si - L)
    pj = jnp.where(sj < L, sj, sj - L)
    incl2 = same_head & ((pi <= pj) if reverse else (pi >= pj))
    strict2 = same_head & ((pi < pj) if reverse else (pi > pj))
    eye2 = jnp.where(si == sj, 1.0, 0.0)

    for g in range(groups):
        sl = slice(g * LANES, (g + 1) * LANES)
        at_s, rt_s = stack(at[:, sl]), stack(rt[:, sl])
        kt_s, bt_s = stack(kt[:, sl]), stack(bt[:, sl])
        v_s = stack(vv[:, sl]).astype(_BF16)
        lhs = jnp.concatenate([at_s, rt_s], axis=0).astype(_BF16)
        rhs = jnp.concatenate([kt_s, bt_s], axis=0).astype(_BF16)
        scores = _mm(lhs, rhs, _NT)
        a_k = jnp.where(strict2, scores[:L2, :L2], 0.0)
        a_b = jnp.where(strict2, scores[:L2, L2:], 0.0)
        r_k = jnp.where(incl2, scores[L2:, :L2], 0.0)
        r_b = jnp.where(incl2, scores[L2:, L2:], 0.0)

        state = s_ref[0, g]
        from_state = _mm(lhs, state * e_half[:, sl], _NT)

        power = -a_b
        inv = eye2 + power
        for _ in range(int(math.log2(L)) - 1):
            power = _mm(power, power)
            inv = inv + _mm(inv, power)
        u_s = _mm(inv, from_state[:L2] + _mm(a_k, v_s))

        vu = jnp.concatenate([v_s, u_s.astype(_BF16)], axis=0)
        y_s = from_state[L2:] + _mm(jnp.concatenate([r_k, -r_b], axis=1), vu)
        y_ref[0, :, sl] = y_s[:L] + y_s[L:]

        kb = jnp.concatenate([stack(ke[:, sl]), -stack(be[:, sl])], axis=0)
        s_ref[0, g] = state * e_tot[:, sl] + _mm(vu, kb, _TN)


def _wkv_scan(r, lw, k, v, a, b, s0, reverse):
    bsz, t, width = r.shape
    groups = WKV_GROUPS_PER_STEP
    assert t % WKV_CHUNK == 0 and width % (groups * LANES) == 0
    nc = t // WKV_CHUNK
    ngroup_steps = width // (groups * LANES)
    cidx = (lambda c: nc - 1 - c) if reverse else (lambda c: c)
    seq_spec = pl.BlockSpec((1, WKV_CHUNK, groups * LANES), lambda bi, gi, c: (bi, cidx(c), gi))
    st_spec = pl.BlockSpec((1, groups, LANES, LANES), lambda bi, gi, c: (bi, gi, 0, 0))
    return pl.pallas_call(
        functools.partial(_wkv_kernel, reverse=reverse, chunk=WKV_CHUNK, groups=groups),
        grid=(bsz, ngroup_steps, nc),
        in_specs=[seq_spec] * 6 + [st_spec],
        out_specs=[seq_spec, st_spec],
        out_shape=[jax.ShapeDtypeStruct((bsz, t, width), _F32), jax.ShapeDtypeStruct(s0.shape, _F32)],
        compiler_params=pltpu.CompilerParams(dimension_semantics=("parallel", "parallel", "arbitrary")),
        name="wkv7_rev" if reverse else "wkv7_fwd",
    )(r, lw, k, v, a, b, s0)


def _rms_norm(u, gain):
    uf = u.astype(_F32)
    uf = uf * lax.rsqrt(jnp.mean(uf * uf, axis=-1, keepdims=True) + NORM_EPS)
    return (uf * gain.astype(_F32)).astype(u.dtype)


def _centred_token_shift(p, mu):
    pad = jnp.pad(p, ((0, 0), (1, 1), (0, 0)))
    return p + mu * (0.5 * (pad[:, :-2] + pad[:, 2:]) - p)


def _raster_to_column(u, rows):
    b, s, ch = u.shape
    return u.reshape(b, rows, GRID_W, ch).transpose(0, 2, 1, 3).reshape(b, s, ch)


def _column_to_raster(u, rows):
    b, s, ch = u.shape
    return u.reshape(b, GRID_W, rows, ch).transpose(0, 2, 1, 3).reshape(b, s, ch)


def _rwkv7_branch(slab, s_fwd, s_bwd, w0, w2, a0, a2, g2, k_k, k_a, r_k, ln_w, ln_b):
    bsz, t, _ = slab.shape
    r, k, v, wd_f, wd_b, ad_f, ad_b, gd = jnp.split(slab.astype(_F32), RW_SPLITS, axis=-1)
    heads = lambda u: u.reshape(bsz, t, RW_HEADS, RW_HEAD)
    kk = heads(k * k_k)
    kk = (kk / jnp.maximum(jnp.sqrt(jnp.sum(kk * kk, axis=-1, keepdims=True)), 1e-12)).reshape(bsz, t, RW_WIDTH)

    def direction_inputs(wd, ad, d):
        w_log = -jax.nn.softplus(-(w0[d] + jnp.tanh(wd) @ w2[d])) - 0.5
        a = jax.nn.sigmoid(a0[d] + ad @ a2[d])
        k_dir = k * (1.0 + (a - 1.0) * k_a)
        return -jnp.exp(w_log), a, k_dir

    lw_f, a_f, k_f = direction_inputs(wd_f, ad_f, 0)
    lw_b, a_b, k_b = direction_inputs(wd_b, ad_b, 1)
    y_f, s_fwd = _wkv_scan(r, lw_f, k_f, v, kk, kk * a_f, s_fwd, reverse=False)
    y_b, s_bwd = _wkv_scan(r, lw_b, k_b, v, kk, kk * a_b, s_bwd, reverse=True)
    y = heads(y_f + y_b)
    mean = jnp.mean(y, axis=-1, keepdims=True)
    var = jnp.mean(jnp.square(y - mean), axis=-1, keepdims=True)
    y = ((y - mean) * lax.rsqrt(var + LN_X_EPS)).reshape(bsz, t, RW_WIDTH) * ln_w + ln_b
    bonus = jnp.sum(heads(r * (k_f + k_b)) * r_k, axis=-1, keepdims=True) * heads(v)
    gate = jax.nn.sigmoid(gd) @ g2
    out = (y + bonus.reshape(bsz, t, RW_WIDTH)) * gate
    return out, s_fwd, s_bwd


def _centred_depthwise_conv(u, w):
    ch = u.shape[-1]
    return lax.conv_general_dilated(u, w.astype(_F32)[:, None, :], window_strides=(1,), padding=[(SSM_CONV // 2, SSM_CONV // 2)], dimension_numbers=('NWC', 'WIO', 'NWC'), feature_group_count=ch)


def _ssd_chunked(xdt, log_a, bm, cm, h0):
    bsz, t = xdt.shape[:2]
    nc = t // SSM_CHUNK
    xdt = xdt.reshape(bsz, nc, SSM_CHUNK, SSM_GROUPS, SSM_HPG, SSM_HEADDIM)
    log_a = log_a.reshape(bsz, nc, SSM_CHUNK, SSM_GROUPS, SSM_HPG)
    bm = bm.reshape(bsz, nc, SSM_CHUNK, SSM_GROUPS, SSM_STATE)
    cm = cm.reshape(bsz, nc, SSM_CHUNK, SSM_GROUPS, SSM_STATE)
    cs = jnp.cumsum(log_a, axis=2)
    cs_t = jnp.moveaxis(cs, 2, -1)
    lower = jnp.tril(jnp.ones((SSM_CHUNK, SSM_CHUNK), dtype=bool))
    seg = jnp.exp(jnp.where(lower, cs_t[..., :, None] - cs_t[..., None, :], -jnp.inf))
    cb = jnp.einsum('bclgn,bcsgn->bcgls', cm, bm)
    y_diag = jnp.einsum('bcgels,bcsgep->bclgep', cb[:, :, :, None] * seg, xdt)
    to_end = jnp.exp(cs[:, :, -1:] - cs)
    states = jnp.einsum('bclgn,bclgep->bcgepn', bm, xdt * to_end[..., None])
    chunk_decay = jnp.exp(cs[:, :, -1])

    def carry_step(h, inp):
        st, dec = inp
        return h * dec[..., None, None] + st, h
    h_last, h_in = lax.scan(carry_step, h0, (jnp.moveaxis(states, 1, 0), jnp.moveaxis(chunk_decay, 1, 0)))
    h_in = jnp.moveaxis(h_in, 0, 1)
    y_off = jnp.einsum('bclgn,bcgepn->bclgep', cm, h_in) * jnp.exp(cs)[..., None]
    return (y_diag + y_off).reshape(bsz, t, SSM_GROUPS, SSM_HPG, SSM_HEADDIM), h_last


def _mamba2_branch(xbc, dt_raw, h0_fwd, h0_bwd, conv_w, conv_b, dt_bias, a_log, d_skip):
    bsz, t, _ = xbc.shape
    xbc = jax.nn.silu(_centred_depthwise_conv(xbc.astype(_F32), conv_w) + conv_b)
    xs, bm, cm = jnp.split(xbc, (SSM_WIDTH, SSM_WIDTH + SSM_GROUPS * SSM_STATE), axis=-1)
    xs = xs.reshape(bsz, t, SSM_GROUPS, SSM_HPG, SSM_HEADDIM)
    bm = bm.reshape(bsz, t, SSM_GROUPS, SSM_STATE)
    cm = cm.reshape(bsz, t, SSM_GROUPS, SSM_STATE)
    dt = jax.nn.softplus(dt_raw.astype(_F32).reshape(bsz, t, 2, SSM_HEADS) + dt_bias)
    dt = dt.reshape(bsz, t, 2, SSM_GROUPS, SSM_HPG)
    log_a = -jnp.exp(a_log).reshape(2, SSM_GROUPS, SSM_HPG) * dt
    y_f, h_f = _ssd_chunked(xs * dt[:, :, 0, ..., None], log_a[:, :, 0], bm, cm, h0_fwd)
    flip = lambda u: jnp.flip(u, axis=1)
    y_b, h_b = _ssd_chunked(flip(xs * dt[:, :, 1, ..., None]), flip(log_a[:, :, 1]), flip(bm), flip(cm), h0_bwd)
    y = y_f + flip(y_b) + d_skip.reshape(SSM_GROUPS, SSM_HPG)[..., None] * xs
    return y.reshape(bsz, t, SSM_WIDTH), h_f, h_b


def _gated_group_rmsnorm(y, z, gain):
    bsz, t, _ = y.shape
    u = (y * jax.nn.silu(z.astype(_F32))).reshape(bsz, t, SSM_GROUPS, SSM_WIDTH // SSM_GROUPS)
    u = u * lax.rsqrt(jnp.mean(u * u, axis=-1, keepdims=True) + NORM_EPS)
    return u.reshape(bsz, t, SSM_WIDTH) * gain


def _merge_branches(y_a, y_b, gates, w_a, w_b, w_o):
    g_a, g_b = jnp.split(gates, 2, axis=-1)
    m = jax.nn.sigmoid(g_a) * (y_a @ w_a) + jax.nn.sigmoid(g_b) * (y_b @ w_b)
    return m @ w_o


def _peer_ffn(h, wq, subkeys, u_tab, v_tab):
    bsz, t, d = h.shape
    n = bsz * t
    tok = h.reshape(n, d)
    q = (tok @ wq).astype(_F32).reshape(n, PEER_HEADS, 2, PEER_HALF)
    s = jnp.einsum('nhsd,hskd->nhsk', q, subkeys.astype(_F32))
    top_s, top_i = lax.top_k(s, PEER_TOPK)
    cand = (top_s[:, :, 0, :, None] + top_s[:, :, 1, None, :]).reshape(n, PEER_HEADS, PEER_TOPK * PEER_TOPK)
    best_s, best_j = lax.top_k(cand, PEER_TOPK)
    i1 = jnp.take_along_axis(top_i[:, :, 0], best_j // PEER_TOPK, axis=-1)
    i2 = jnp.take_along_axis(top_i[:, :, 1], best_j % PEER_TOPK, axis=-1)
    expert = i1 * PEER_KEYS + i2
    gate = jax.nn.softmax(best_s, axis=-1)

    def block(args):
        xb, eb, gb = args
        act = jax.nn.gelu(jnp.einsum('nd,nhkd->nhk', xb, jnp.take(u_tab, eb, axis=0)), approximate=False)
        return jnp.einsum('nhk,nhkd->nd', (gb * act).astype(v_tab.dtype), jnp.take(v_tab, eb, axis=0))
    nb = n // PEER_BLOCK
    out = lax.map(block, (tok.reshape(nb, PEER_BLOCK, d), expert.reshape(nb, PEER_BLOCK, PEER_HEADS, PEER_TOPK), gate.reshape(nb, PEER_BLOCK, PEER_HEADS, PEER_TOPK)))
    return out.reshape(bsz, t, d).astype(h.dtype)


def kernel(x, c, ctx, c_ctx, w_mod, b_mod, norm_pre1, norm_post1, norm_pre2, norm_post2, w_in, rw_mu, rw_w0, rw_w2, rw_a0, rw_a2, rw_g2, rw_k_k, rw_k_a, rw_r_k, rw_ln_w, rw_ln_b, ssm_conv_w, ssm_conv_b, ssm_dt_bias, ssm_a_log, ssm_d, ssm_norm_w, w_branch_a, w_branch_b, w_out, peer_wq, peer_subkeys, peer_u, peer_v):
    act_dtype = x.dtype
    bsz, seq, _ = x.shape
    rows = seq // GRID_W
    depth = w_mod.shape[0]
    assert depth == 1
    rw_zero = jnp.zeros((bsz, RW_WIDTH // LANES, LANES, LANES), _F32)
    ssm_zero = jnp.zeros((bsz, SSM_GROUPS, SSM_HPG, SSM_HEADDIM, SSM_STATE), _F32)
    l = 0
    mod_x = (jax.nn.silu(c) @ w_mod[l] + b_mod[l])[:, None, :]
    mod_c = (jax.nn.silu(c_ctx) @ w_mod[l] + b_mod[l])[None, None, :]
    sh1x, sc1x, g1x, sh2x, sc2x, g2x = jnp.split(mod_x, N_MOD, axis=-1)
    sh1c, sc1c, g1c, sh2c, sc2c, g2c = jnp.split(mod_c, N_MOD, axis=-1)

    hx = _rms_norm(x, norm_pre1[l]) * (1.0 + sc1x) + sh1x
    hc = _rms_norm(ctx, norm_pre1[l]) * (1.0 + sc1c) + sh1c
    rw_x, z_x, xbc_x, dt_x, gate_x = jnp.split(hx @ w_in[l], IN_SPLITS, axis=-1)
    rw_c, z_c, xbc_c, dt_c, gate_c = jnp.split(hc @ w_in[l], IN_SPLITS, axis=-1)

    rw_params = (rw_w0[l], rw_w2[l], rw_a0[l], rw_a2[l], rw_g2[l], rw_k_k[l], rw_k_a[l], rw_r_k[l], rw_ln_w[l], rw_ln_b[l])
    ya_c, s_fwd, s_bwd = _rwkv7_branch(_centred_token_shift(rw_c, rw_mu[l]), rw_zero, rw_zero, *rw_params)
    ya_x, _, _ = _rwkv7_branch(_centred_token_shift(rw_x, rw_mu[l]), s_fwd, s_bwd, *rw_params)

    ssm_params = (ssm_conv_w[l], ssm_conv_b[l], ssm_dt_bias[l], ssm_a_log[l], ssm_d[l])
    yb_c, h_fwd, h_bwd = _mamba2_branch(xbc_c, dt_c, ssm_zero, ssm_zero, *ssm_params)
    yb_x, _, _ = _mamba2_branch(_raster_to_column(xbc_x, rows), _raster_to_column(dt_x, rows), h_fwd, h_bwd, *ssm_params)
    yb_x = _column_to_raster(yb_x, rows)

    mix_x = _merge_branches(ya_x.astype(act_dtype), _gated_group_rmsnorm(yb_x, z_x, ssm_norm_w[l]).astype(act_dtype), gate_x, w_branch_a[l], w_branch_b[l], w_out[l])
    x = x + g1x * _rms_norm(mix_x, norm_post1[l])

    h2x = _rms_norm(x, norm_pre2[l]) * (1.0 + sc2x) + sh2x
    x = x + g2x * _rms_norm(_peer_ffn(h2x, peer_wq[l], peer_subkeys[l], peer_u[l], peer_v[l]), norm_post2[l])
    return x
```

```python
import functools
import math

import jax
import jax.numpy as jnp
from jax import lax
from jax.experimental import pallas as pl
from jax.experimental.pallas import tpu as pltpu

D_MODEL = 2048
GRID_W = 64
N_MOD = 6
NORM_EPS = 1e-6
RW_HEAD = 64
RW_WIDTH = D_MODEL
RW_HEADS = RW_WIDTH // RW_HEAD
LORA_W = 96
LORA_A = 96
LORA_G = 256
LN_X_EPS = 64e-5
SSM_WIDTH = D_MODEL
SSM_HEADDIM = 64
SSM_HEADS = SSM_WIDTH // SSM_HEADDIM
SSM_GROUPS = 8
SSM_HPG = SSM_HEADS // SSM_GROUPS
SSM_STATE = 128
SSM_CONV = 5
SSM_CHUNK = 128
PEER_HEADS = 8
PEER_KEYS = 128
PEER_TOPK = 16
PEER_QDIM = 256
PEER_HALF = PEER_QDIM // 2
PEER_BLOCK = 128
RW_COLS = 3 * RW_WIDTH + 2 * LORA_W + 2 * LORA_A + LORA_G
XBC_COLS = SSM_WIDTH + 2 * SSM_GROUPS * SSM_STATE
IN_SPLITS = (RW_COLS, RW_COLS + SSM_WIDTH, RW_COLS + SSM_WIDTH + XBC_COLS, RW_COLS + SSM_WIDTH + XBC_COLS + 2 * SSM_HEADS)
RW_SPLITS = (RW_WIDTH, 2 * RW_WIDTH, 3 * RW_WIDTH, 3 * RW_WIDTH + LORA_W, 3 * RW_WIDTH + 2 * LORA_W, 3 * RW_WIDTH + 2 * LORA_W + LORA_A, 3 * RW_WIDTH + 2 * LORA_W + 2 * LORA_A)

LANES = 128
WKV_CHUNK = 64
WKV_PAIR = LANES // RW_HEAD
WKV_GROUPS_PER_STEP = 4

_F32 = jnp.float32
_BF16 = jnp.bfloat16
_NT = (((1,), (1,)), ((), ()))
_TN = (((0,), (0,)), ((), ()))


def _mm(a, b, dims=None):
    a = a.astype(_BF16)
    b = b.astype(_BF16)
    if dims is None:
        return jnp.dot(a, b, preferred_element_type=_F32)
    return lax.dot_general(a, b, dims, preferred_element_type=_F32)


def _wkv_kernel(r_ref, lw_ref, k_ref, v_ref, a_ref, b_ref, s0_ref, y_ref, s_ref, *, reverse, chunk, groups):
    L = chunk
    L2 = WKV_PAIR * L

    @pl.when(pl.program_id(2) == 0)
    def _():
        s_ref[...] = s0_ref[...]

    ti = lax.broadcasted_iota(jnp.int32, (L, L), 0)
    tj = lax.broadcasted_iota(jnp.int32, (L, L), 1)
    before_incl = (ti <= tj) if reverse else (ti >= tj)
    cum = jnp.where(before_incl, 1.0, 0.0).astype(_BF16)

    lw = lw_ref[0]
    hi = lw.astype(_BF16)
    rem = lw - hi.astype(_F32)
    mid = rem.astype(_BF16)
    lo = (rem - mid.astype(_F32)).astype(_BF16)
    cs = (jnp.dot(cum, hi, preferred_element_type=_F32)
          + jnp.dot(cum, mid, preferred_element_type=_F32)
          + jnp.dot(cum, lo, preferred_element_type=_F32))
    tot = cs[0:1] if reverse else cs[L - 1:L]
    half = 0.5 * tot
    e_pos = jnp.exp(cs - half)
    e_neg = jnp.exp(half - cs)
    e_prev = jnp.exp(cs - lw - half)
    e_end = jnp.exp(tot - cs)
    e_half = jnp.exp(half)
    e_tot = jnp.exp(tot)

    rt = r_ref[0] * e_pos
    at = a_ref[0] * e_prev
    kt = k_ref[0] * e_neg
    bt = b_ref[0] * e_neg
    ke = k_ref[0] * e_end
    be = b_ref[0] * e_end
    vv = v_ref[0]

    lane = lax.broadcasted_iota(jnp.int32, (L, LANES), 1)
    first_head = lane < RW_HEAD

    def stack(x):
        return jnp.concatenate([jnp.where(first_head, x, 0.0), jnp.where(first_head, 0.0, x)], axis=0)

    si = lax.broadcasted_iota(jnp.int32, (L2, L2), 0)
    sj = lax.broadcasted_iota(jnp.int32, (L2, L2), 1)
    same_head = (si < L) == (sj < L)
    pi = jnp.where(si < L, si, si - L)
    pj = jnp.where(sj < L, sj, sj - L)
    incl2 = same_head & ((pi <= pj) if reverse else (pi >= pj))
    strict2 = same_head & ((pi < pj) if reverse else (pi > pj))
    eye2 = jnp.where(si == sj, 1.0, 0.0)

    for g in range(groups):
        sl = slice(g * LANES, (g + 1) * LANES)
        at_s, rt_s = stack(at[:, sl]), stack(rt[:, sl])
        kt_s, bt_s = stack(kt[:, sl]), stack(bt[:, sl])
        v_s = stack(vv[:, sl]).astype(_BF16)
        lhs = jnp.concatenate([at_s, rt_s], axis=0).astype(_BF16)
        rhs = jnp.concatenate([kt_s, bt_s], axis=0).astype(_BF16)
        scores = _mm(lhs, rhs, _NT)
        a_k = jnp.where(strict2, scores[:L2, :L2], 0.0)
        a_b = jnp.where(strict2, scores[:L2, L2:], 0.0)
        r_k = jnp.where(incl2, scores[L2:, :L2], 0.0)
        r_b = jnp.where(incl2, scores[L2:, L2:], 0.0)

        state = s_ref[0, g]
        from_state = _mm(lhs, state * e_half[:, sl], _NT)

        power = -a_b
        inv = eye2 + power
        for _ in range(int(math.log2(L)) - 1):
            power = _mm(power, power)
            inv = inv + _mm(inv, power)
        u_s = _mm(inv, from_state[:L2] + _mm(a_k, v_s))

        vu = jnp.concatenate([v_s, u_s.astype(_BF16)], axis=0)
        y_s = from_state[L2:] + _mm(jnp.concatenate([r_k, -r_b], axis=1), vu)
        y_ref[0, :, sl] = y_s[:L] + y_s[L:]

        kb = jnp.concatenate([stack(ke[:, sl]), -stack(be[:, sl])], axis=0)
        s_ref[0, g] = state * e_tot[:, sl] + _mm(vu, kb, _TN)


def _wkv_scan(r, lw, k, v, a, b, s0, reverse):
    bsz, t, width = r.shape
    groups = WKV_GROUPS_PER_STEP
    assert t % WKV_CHUNK == 0 and width % (groups * LANES) == 0
    nc = t // WKV_CHUNK
    ngroup_steps = width // (groups * LANES)
    cidx = (lambda c: nc - 1 - c) if reverse else (lambda c: c)
    seq_spec = pl.BlockSpec((1, WKV_CHUNK, groups * LANES), lambda bi, gi, c: (bi, cidx(c), gi))
    st_spec = pl.BlockSpec((1, groups, LANES, LANES), lambda bi, gi, c: (bi, gi, 0, 0))
    return pl.pallas_call(
        functools.partial(_wkv_kernel, reverse=reverse, chunk=WKV_CHUNK, groups=groups),
        grid=(bsz, ngroup_steps, nc),
        in_specs=[seq_spec] * 6 + [st_spec],
        out_specs=[seq_spec, st_spec],
        out_shape=[jax.ShapeDtypeStruct((bsz, t, width), _F32), jax.ShapeDtypeStruct(s0.shape, _F32)],
        compiler_params=pltpu.CompilerParams(dimension_semantics=("parallel", "parallel", "arbitrary")),
        name="wkv7_rev" if reverse else "wkv7_fwd",
    )(r, lw, k, v, a, b, s0)


def _rms_norm(u, gain):
    uf = u.astype(_F32)
    uf = uf * lax.rsqrt(jnp.mean(uf * uf, axis=-1, keepdims=True) + NORM_EPS)
    return (uf * gain.astype(_F32)).astype(u.dtype)


def _centred_token_shift(p, mu):
    pad = jnp.pad(p, ((0, 0), (1, 1), (0, 0)))
    return p + mu * (0.5 * (pad[:, :-2] + pad[:, 2:]) - p)


def _raster_to_column(u, rows):
    b, s, ch = u.shape
    return u.reshape(b, rows, GRID_W, ch).transpose(0, 2, 1, 3).reshape(b, s, ch)


def _column_to_raster(u, rows):
    b, s, ch = u.shape
    return u.reshape(b, GRID_W, rows, ch).transpose(0, 2, 1, 3).reshape(b, s, ch)


def _rwkv7_branch(slab, s_fwd, s_bwd, w0, w2, a0, a2, g2, k_k, k_a, r_k, ln_w, ln_b):
    bsz, t, _ = slab.shape
    r, k, v, wd_f, wd_b, ad_f, ad_b, gd = jnp.split(slab.astype(_F32), RW_SPLITS, axis=-1)
    heads = lambda u: u.reshape(bsz, t, RW_HEADS, RW_HEAD)
    kk = heads(k * k_k)
    kk = (kk / jnp.maximum(jnp.sqrt(jnp.sum(kk * kk, axis=-1, keepdims=True)), 1e-12)).reshape(bsz, t, RW_WIDTH)

    def direction_inputs(wd, ad, d):
        w_log = -jax.nn.softplus(-(w0[d] + jnp.tanh(wd) @ w2[d])) - 0.5
        a = jax.nn.sigmoid(a0[d] + ad @ a2[d])
        k_dir = k * (1.0 + (a - 1.0) * k_a)
        return -jnp.exp(w_log), a, k_dir

    lw_f, a_f, k_f = direction_inputs(wd_f, ad_f, 0)
    lw_b, a_b, k_b = direction_inputs(wd_b, ad_b, 1)
    y_f, s_fwd = _wkv_scan(r, lw_f, k_f, v, kk, kk * a_f, s_fwd, reverse=False)
    y_b, s_bwd = _wkv_scan(r, lw_b, k_b, v, kk, kk * a_b, s_bwd, reverse=True)
    y = heads(y_f + y_b)
    mean = jnp.mean(y, axis=-1, keepdims=True)
    var = jnp.mean(jnp.square(y - mean), axis=-1, keepdims=True)
    y = ((y - mean) * lax.rsqrt(var + LN_X_EPS)).reshape(bsz, t, RW_WIDTH) * ln_w + ln_b
    bonus = jnp.sum(heads(r * (k_f + k_b)) * r_k, axis=-1, keepdims=True) * heads(v)
    gate = jax.nn.sigmoid(gd) @ g2
    out = (y + bonus.reshape(bsz, t, RW_WIDTH)) * gate
    return out, s_fwd, s_bwd


def _centred_depthwise_conv(u, w):
    ch = u.shape[-1]
    return lax.conv_general_dilated(u, w.astype(_F32)[:, None, :], window_strides=(1,), padding=[(SSM_CONV // 2, SSM_CONV // 2)], dimension_numbers=('NWC', 'WIO', 'NWC'), feature_group_count=ch)


def _ssd_chunked(xdt, log_a, bm, cm, h0):
    bsz, t = xdt.shape[:2]
    nc = t // SSM_CHUNK
    xdt = xdt.reshape(bsz, nc, SSM_CHUNK, SSM_GROUPS, SSM_HPG, SSM_HEADDIM)
    log_a = log_a.reshape(bsz, nc, SSM_CHUNK, SSM_GROUPS, SSM_HPG)
    bm = bm.reshape(bsz, nc, SSM_CHUNK, SSM_GROUPS, SSM_STATE)
    cm = cm.reshape(bsz, nc, SSM_CHUNK, SSM_GROUPS, SSM_STATE)
    cs = jnp.cumsum(log_a, axis=2)
    cs_t = jnp.moveaxis(cs, 2, -1)
    lower = jnp.tril(jnp.ones((SSM_CHUNK, SSM_CHUNK), dtype=bool))
    seg = jnp.exp(jnp.where(lower, cs_t[..., :, None] - cs_t[..., None, :], -jnp.inf))
    cb = jnp.einsum('bclgn,bcsgn->bcgls', cm, bm)
    y_diag = jnp.einsum('bcgels,bcsgep->bclgep', cb[:, :, :, None] * seg, xdt)
    to_end = jnp.exp(cs[:, :, -1:] - cs)
    states = jnp.einsum('bclgn,bclgep->bcgepn', bm, xdt * to_end[..., None])
    chunk_decay = jnp.exp(cs[:, :, -1])

    def carry_step(h, inp):
        st, dec = inp
        return h * dec[..., None, None] + st, h
    h_last, h_in = lax.scan(carry_step, h0, (jnp.moveaxis(states, 1, 0), jnp.moveaxis(chunk_decay, 1, 0)))
    h_in = jnp.moveaxis(h_in, 0, 1)
    y_off = jnp.einsum('bclgn,bcgepn->bclgep', cm, h_in) * jnp.exp(cs)[..., None]
    return (y_diag + y_off).reshape(bsz, t, SSM_GROUPS, SSM_HPG, SSM_HEADDIM), h_last


def _mamba2_branch(xbc, dt_raw, h0_fwd, h0_bwd, conv_w, conv_b, dt_bias, a_log, d_skip):
    bsz, t, _ = xbc.shape
    xbc = jax.nn.silu(_centred_depthwise_conv(xbc.astype(_F32), conv_w) + conv_b)
    xs, bm, cm = jnp.split(xbc, (SSM_WIDTH, SSM_WIDTH + SSM_GROUPS * SSM_STATE), axis=-1)
    xs = xs.reshape(bsz, t, SSM_GROUPS, SSM_HPG, SSM_HEADDIM)
    bm = bm.reshape(bsz, t, SSM_GROUPS, SSM_STATE)
    cm = cm.reshape(bsz, t, SSM_GROUPS, SSM_STATE)
    dt = jax.nn.softplus(dt_raw.astype(_F32).reshape(bsz, t, 2, SSM_HEADS) + dt_bias)
    dt = dt.reshape(bsz, t, 2, SSM_GROUPS, SSM_HPG)
    log_a = -jnp.exp(a_log).reshape(2, SSM_GROUPS, SSM_HPG) * dt
    y_f, h_f = _ssd_chunked(xs * dt[:, :, 0, ..., None], log_a[:, :, 0], bm, cm, h0_fwd)
    flip = lambda u: jnp.flip(u, axis=1)
    y_b, h_b = _ssd_chunked(flip(xs * dt[:, :, 1, ..., None]), flip(log_a[:, :, 1]), flip(bm), flip(cm), h0_bwd)
    y = y_f + flip(y_b) + d_skip.reshape(SSM_GROUPS, SSM_HPG)[..., None] * xs
    return y.reshape(bsz, t, SSM_WIDTH), h_f, h_b


def _gated_group_rmsnorm(y, z, gain):
    bsz, t, _ = y.shape
    u = (y * jax.nn.silu(z.astype(_F32))).reshape(bsz, t, SSM_GROUPS, SSM_WIDTH // SSM_GROUPS)
    u = u * lax.rsqrt(jnp.mean(u * u, axis=-1, keepdims=True) + NORM_EPS)
    return u.reshape(bsz, t, SSM_WIDTH) * gain


def _merge_branches(y_a, y_b, gates, w_a, w_b, w_o):
    g_a, g_b = jnp.split(gates, 2, axis=-1)
    m = jax.nn.sigmoid(g_a) * (y_a @ w_a) + jax.nn.sigmoid(g_b) * (y_b @ w_b)
    return m @ w_o


PEER_ROUTE_TOKENS = 256
PEER_TOKEN_BLOCK = 512
PEER_EXPERT_BLOCK = 512
PEER_ROW_TILE = 16


def _top_values(x, count):
    rows = lax.broadcasted_iota(jnp.int32, x.shape, 0)
    vals = []
    for _ in range(count):
        m = jnp.max(x, axis=0, keepdims=True)
        vals.append(m)
        first = jnp.min(jnp.where(x == m, rows, x.shape[0]), axis=0, keepdims=True)
        x = jnp.where(rows == first, -jnp.inf, x)
    return vals


def _peer_route_kernel(h_ref, wqt_ref, sub_ref, s1_ref, e1_ref, s2_ref, e2_ref, thr_ref):
    qt = lax.dot_general(wqt_ref[...], h_ref[...].astype(_BF16), _NT, preferred_element_type=_F32)
    for h in range(PEER_HEADS):
        sc = []
        for s in range(2):
            lo = (2 * h + s) * PEER_HALF
            sc.append(jnp.dot(sub_ref[2 * h + s], qt[lo:lo + PEER_HALF].astype(_BF16), preferred_element_type=_F32))
        top_a = _top_values(sc[0], PEER_TOPK)
        top_b = jnp.concatenate(_top_values(sc[1], PEER_TOPK), axis=0)
        cand = [top_a[i] + top_b[:PEER_TOPK // (i + 1)] for i in range(PEER_TOPK)]
        cand = jnp.concatenate(cand, axis=0)
        best = _top_values(cand, PEER_TOPK)
        norm = best[0] * 0.0
        for val in best:
            norm = norm + jnp.exp(val - best[0])
        s1_ref[h] = sc[0]
        s2_ref[h] = sc[1]
        e1_ref[h] = jnp.exp(sc[0] - top_a[0]) / norm
        e2_ref[h] = jnp.exp(sc[1] - top_b[0:1])
        thr_ref[h:h + 1, :] = best[PEER_TOPK - 1]


def _peer_route(tok, wq, subkeys):
    n, d = tok.shape
    tr = PEER_ROUTE_TOKENS
    wqt = wq.T.astype(_BF16)
    sub = subkeys.reshape(PEER_HEADS * 2, PEER_KEYS, PEER_HALF).astype(_BF16)
    key_spec = pl.BlockSpec((PEER_HEADS, PEER_KEYS, tr), lambda i: (0, 0, i))
    key_shape = jax.ShapeDtypeStruct((PEER_HEADS, PEER_KEYS, n), _F32)
    return pl.pallas_call(
        _peer_route_kernel,
        grid=(n // tr,),
        in_specs=[pl.BlockSpec((tr, d), lambda i: (i, 0)),
                  pl.BlockSpec(wqt.shape, lambda i: (0, 0)),
                  pl.BlockSpec(sub.shape, lambda i: (0, 0, 0))],
        out_specs=[key_spec, key_spec, key_spec, key_spec, pl.BlockSpec((PEER_HEADS, tr), lambda i: (0, i))],
        out_shape=[key_shape, key_shape, key_shape, key_shape, jax.ShapeDtypeStruct((PEER_HEADS, n), _F32)],
        compiler_params=pltpu.CompilerParams(dimension_semantics=("parallel",), vmem_limit_bytes=48 * 2**20),
        name="peer_route",
    )(tok, wqt, sub)


def _gelu_exact(x):
    return 0.5 * x * (1.0 + lax.erf(x * (1.0 / math.sqrt(2.0))))


def _peer_expert_kernel(h_ref, u_ref, v_ref, s1_ref, e1_ref, s2_ref, e2_ref, thr_ref, o_ref, x_scr, a_scr, w_scr):
    eb = pl.program_id(1)

    @pl.when(eb == 0)
    def _():
        x_scr[...] = h_ref[...].astype(_BF16)
        o_ref[...] = jnp.zeros_like(o_ref)

    a_scr[...] = lax.dot_general(u_ref[...], x_scr[...], _NT, preferred_element_type=_F32)
    keys_per_step = PEER_EXPERT_BLOCK // PEER_KEYS
    rt = PEER_ROW_TILE

    def row_tile(r, carry):
        r0 = pl.multiple_of(r * rt, rt)
        for j in range(keys_per_step):
            i1 = eb * keys_per_step + j
            gate = None
            for h in range(PEER_HEADS):
                s2 = s2_ref[h, pl.ds(r0, rt), :]
                e2 = e2_ref[h, pl.ds(r0, rt), :]
                sel = (s1_ref[h, pl.ds(i1, 1), :] + s2) >= thr_ref[h:h + 1, :]
                term = jnp.where(sel, e1_ref[h, pl.ds(i1, 1), :] * e2, 0.0)
                gate = term if gate is None else gate + term
            rows = pl.ds(pl.multiple_of(j * PEER_KEYS + r0, rt), rt)
            w_scr[rows, :] = (gate * _gelu_exact(a_scr[rows, :])).astype(_BF16)
        return carry

    lax.fori_loop(0, PEER_KEYS // rt, row_tile, 0)
    o_ref[...] += lax.dot_general(w_scr[...], v_ref[...], _TN, preferred_element_type=_F32)


def _peer_ffn(h, wq, subkeys, u_tab, v_tab):
    bsz, t, d = h.shape
    n = bsz * t
    tok = h.reshape(n, d)
    s1, e1, s2, e2, thr = _peer_route(tok, wq, subkeys)
    tb, eb = PEER_TOKEN_BLOCK, PEER_EXPERT_BLOCK
    n_experts = u_tab.shape[0]
    key_spec = pl.BlockSpec((PEER_HEADS, PEER_KEYS, tb), lambda i, e: (0, 0, i))
    tab_spec = pl.BlockSpec((eb, d), lambda i, e: (e, 0))
    out = pl.pallas_call(
        _peer_expert_kernel,
        grid=(n // tb, n_experts // eb),
        in_specs=[pl.BlockSpec((tb, d), lambda i, e: (i, 0)), tab_spec, tab_spec,
                  key_spec, key_spec, key_spec, key_spec, pl.BlockSpec((PEER_HEADS, tb), lambda i, e: (0, i))],
        out_specs=pl.BlockSpec((tb, d), lambda i, e: (i, 0)),
        out_shape=jax.ShapeDtypeStruct((n, d), _F32),
        scratch_shapes=[pltpu.VMEM((tb, d), _BF16), pltpu.VMEM((eb, tb), _F32), pltpu.VMEM((eb, tb), _BF16)],
        compiler_params=pltpu.CompilerParams(dimension_semantics=("parallel", "arbitrary"), vmem_limit_bytes=56 * 2**20),
        name="peer_experts",
    )(tok, u_tab.astype(_BF16), v_tab.astype(_BF16), s1, e1, s2, e2, thr)
    return out.reshape(bsz, t, d).astype(h.dtype)


def kernel(x, c, ctx, c_ctx, w_mod, b_mod, norm_pre1, norm_post1, norm_pre2, norm_post2, w_in, rw_mu, rw_w0, rw_w2, rw_a0, rw_a2, rw_g2, rw_k_k, rw_k_a, rw_r_k, rw_ln_w, rw_ln_b, ssm_conv_w, ssm_conv_b, ssm_dt_bias, ssm_a_log, ssm_d, ssm_norm_w, w_branch_a, w_branch_b, w_out, peer_wq, peer_subkeys, peer_u, peer_v):
    act_dtype = x.dtype
    bsz, seq, _ = x.shape
    rows = seq // GRID_W
    depth = w_mod.shape[0]
    assert depth == 1
    rw_zero = jnp.zeros((bsz, RW_WIDTH // LANES, LANES, LANES), _F32)
    ssm_zero = jnp.zeros((bsz, SSM_GROUPS, SSM_HPG, SSM_HEADDIM, SSM_STATE), _F32)
    l = 0
    mod_x = (jax.nn.silu(c) @ w_mod[l] + b_mod[l])[:, None, :]
    mod_c = (jax.nn.silu(c_ctx) @ w_mod[l] + b_mod[l])[None, None, :]
    sh1x, sc1x, g1x, sh2x, sc2x, g2x = jnp.split(mod_x, N_MOD, axis=-1)
    sh1c, sc1c, g1c, sh2c, sc2c, g2c = jnp.split(mod_c, N_MOD, axis=-1)

    hx = _rms_norm(x, norm_pre1[l]) * (1.0 + sc1x) + sh1x
    hc = _rms_norm(ctx, norm_pre1[l]) * (1.0 + sc1c) + sh1c
    rw_x, z_x, xbc_x, dt_x, gate_x = jnp.split(hx @ w_in[l], IN_SPLITS, axis=-1)
    rw_c, z_c, xbc_c, dt_c, gate_c = jnp.split(hc @ w_in[l], IN_SPLITS, axis=-1)

    rw_params = (rw_w0[l], rw_w2[l], rw_a0[l], rw_a2[l], rw_g2[l], rw_k_k[l], rw_k_a[l], rw_r_k[l], rw_ln_w[l], rw_ln_b[l])
    ya_c, s_fwd, s_bwd = _rwkv7_branch(_centred_token_shift(rw_c, rw_mu[l]), rw_zero, rw_zero, *rw_params)
    ya_x, _, _ = _rwkv7_branch(_centred_token_shift(rw_x, rw_mu[l]), s_fwd, s_bwd, *rw_params)

    ssm_params = (ssm_conv_w[l], ssm_conv_b[l], ssm_dt_bias[l], ssm_a_log[l], ssm_d[l])
    yb_c, h_fwd, h_bwd = _mamba2_branch(xbc_c, dt_c, ssm_zero, ssm_zero, *ssm_params)
    yb_x, _, _ = _mamba2_branch(_raster_to_column(xbc_x, rows), _raster_to_column(dt_x, rows), h_fwd, h_bwd, *ssm_params)
    yb_x = _column_to_raster(yb_x, rows)

    mix_x = _merge_branches(ya_x.astype(act_dtype), _gated_group_rmsnorm(yb_x, z_x, ssm_norm_w[l]).astype(act_dtype), gate_x, w_branch_a[l], w_branch_b[l], w_out[l])
    x = x + g1x * _rms_norm(mix_x, norm_post1[l])

    h2x = _rms_norm(x, norm_pre2[l]) * (1.0 + sc2x) + sh2x
    x = x + g2x * _rms_norm(_peer_ffn(h2x, peer_wq[l], peer_subkeys[l], peer_u[l], peer_v[l]), norm_post2[l])
    return x
```

```python
import functools
import math

import jax
import jax.numpy as jnp
from jax import lax
from jax.experimental import pallas as pl
from jax.experimental.pallas import tpu as pltpu

D_MODEL = 2048
GRID_W = 64
N_MOD = 6
NORM_EPS = 1e-6
RW_HEAD = 64
RW_WIDTH = D_MODEL
RW_HEADS = RW_WIDTH // RW_HEAD
LORA_W = 96
LORA_A = 96
LORA_G = 256
LN_X_EPS = 64e-5
SSM_WIDTH = D_MODEL
SSM_HEADDIM = 64
SSM_HEADS = SSM_WIDTH // SSM_HEADDIM
SSM_GROUPS = 8
SSM_HPG = SSM_HEADS // SSM_GROUPS
SSM_STATE = 128
SSM_CONV = 5
SSM_CHUNK = 128
PEER_HEADS = 8
PEER_KEYS = 128
PEER_TOPK = 16
PEER_QDIM = 256
PEER_HALF = PEER_QDIM // 2
PEER_BLOCK = 128
RW_COLS = 3 * RW_WIDTH + 2 * LORA_W + 2 * LORA_A + LORA_G
XBC_COLS = SSM_WIDTH + 2 * SSM_GROUPS * SSM_STATE
IN_SPLITS = (RW_COLS, RW_COLS + SSM_WIDTH, RW_COLS + SSM_WIDTH + XBC_COLS, RW_COLS + SSM_WIDTH + XBC_COLS + 2 * SSM_HEADS)
RW_SPLITS = (RW_WIDTH, 2 * RW_WIDTH, 3 * RW_WIDTH, 3 * RW_WIDTH + LORA_W, 3 * RW_WIDTH + 2 * LORA_W, 3 * RW_WIDTH + 2 * LORA_W + LORA_A, 3 * RW_WIDTH + 2 * LORA_W + 2 * LORA_A)

LANES = 128
WKV_CHUNK = 64
WKV_PAIR = LANES // RW_HEAD
WKV_GROUPS_PER_STEP = 8

_F32 = jnp.float32
_BF16 = jnp.bfloat16
_NT = (((1,), (1,)), ((), ()))
_TN = (((0,), (0,)), ((), ()))


def _mm(a, b, dims=None):
    a = a.astype(_BF16)
    b = b.astype(_BF16)
    if dims is None:
        return jnp.dot(a, b, preferred_element_type=_F32)
    return lax.dot_general(a, b, dims, preferred_element_type=_F32)


def _wkv_kernel(r_ref, lw_ref, k_ref, v_ref, a_ref, b_ref, s0_ref, y_ref, s_ref, *, reverse, chunk, groups):
    L = chunk
    L2 = WKV_PAIR * L

    @pl.when(pl.program_id(2) == 0)
    def _():
        s_ref[...] = s0_ref[...]

    ti = lax.broadcasted_iota(jnp.int32, (L, L), 0)
    tj = lax.broadcasted_iota(jnp.int32, (L, L), 1)
    before_incl = (ti <= tj) if reverse else (ti >= tj)
    cum = jnp.where(before_incl, 1.0, 0.0).astype(_BF16)

    lw = lw_ref[0]
    hi = lw.astype(_BF16)
    rem = lw - hi.astype(_F32)
    mid = rem.astype(_BF16)
    lo = (rem - mid.astype(_F32)).astype(_BF16)
    cs = (jnp.dot(cum, hi, preferred_element_type=_F32)
          + jnp.dot(cum, mid, preferred_element_type=_F32)
          + jnp.dot(cum, lo, preferred_element_type=_F32))
    tot = cs[0:1] if reverse else cs[L - 1:L]
    half = 0.5 * tot
    e_pos = jnp.exp(cs - half)
    e_neg = jnp.exp(half - cs)
    e_prev = jnp.exp(cs - lw - half)
    e_end = jnp.exp(tot - cs)
    e_half = jnp.exp(half)
    e_tot = jnp.exp(tot)

    rt = r_ref[0] * e_pos
    at = a_ref[0] * e_prev
    kt = k_ref[0] * e_neg
    bt = b_ref[0] * e_neg
    ke = k_ref[0] * e_end
    be = b_ref[0] * e_end
    vv = v_ref[0]

    lane = lax.broadcasted_iota(jnp.int32, (L, LANES), 1)
    first_head = lane < RW_HEAD

    def stack(x):
        return jnp.concatenate([jnp.where(first_head, x, 0.0), jnp.where(first_head, 0.0, x)], axis=0)

    si = lax.broadcasted_iota(jnp.int32, (L2, L2), 0)
    sj = lax.broadcasted_iota(jnp.int32, (L2, L2), 1)
    same_head = (si < L) == (sj < L)
    pi = jnp.where(si < L, si, si - L)
    pj = jnp.where(sj < L, sj, sj - L)
    incl2 = same_head & ((pi <= pj) if reverse else (pi >= pj))
    strict2 = same_head & ((pi < pj) if reverse else (pi > pj))
    eye2 = jnp.where(si == sj, 1.0, 0.0)

    gs = range(groups)
    sls = [slice(g * LANES, (g + 1) * LANES) for g in gs]
    v_s = [stack(vv[:, sl]).astype(_BF16) for sl in sls]
    lhs = [jnp.concatenate([stack(at[:, sl]), stack(rt[:, sl])], axis=0).astype(_BF16) for sl in sls]
    rhs = [jnp.concatenate([stack(kt[:, sl]), stack(bt[:, sl])], axis=0).astype(_BF16) for sl in sls]
    state = [s_ref[0, g] for g in gs]
    scores = [_mm(lhs[g], rhs[g], _NT) for g in gs]
    from_state = [_mm(lhs[g], state[g] * e_half[:, sls[g]], _NT) for g in gs]

    power = [jnp.where(strict2, -scores[g][:L2, L2:], 0.0).astype(_BF16) for g in gs]
    inv = [eye2 + power[g] for g in gs]
    ak_v = [_mm(jnp.where(strict2, scores[g][:L2, :L2], 0.0), v_s[g]) for g in gs]
    for _ in range(int(math.log2(L)) - 1):
        power = [_mm(power[g], power[g]).astype(_BF16) for g in gs]
        inv = [inv[g] + _mm(inv[g], power[g]) for g in gs]
    u_s = [_mm(inv[g], from_state[g][:L2] + ak_v[g]) for g in gs]

    vu = [jnp.concatenate([v_s[g], u_s[g].astype(_BF16)], axis=0) for g in gs]
    for g in gs:
        r_kb = jnp.concatenate([jnp.where(incl2, scores[g][L2:, :L2], 0.0),
                                jnp.where(incl2, -scores[g][L2:, L2:], 0.0)], axis=1)
        y_s = from_state[g][L2:] + _mm(r_kb, vu[g])
        y_ref[0, :, sls[g]] = y_s[:L] + y_s[L:]
    for g in gs:
        kb = jnp.concatenate([stack(ke[:, sls[g]]), -stack(be[:, sls[g]])], axis=0)
        s_ref[0, g] = state[g] * e_tot[:, sls[g]] + _mm(vu[g], kb, _TN)


def _wkv_scan(r, lw, k, v, a, b, s0, reverse):
    bsz, t, width = r.shape
    groups = WKV_GROUPS_PER_STEP
    assert t % WKV_CHUNK == 0 and width % (groups * LANES) == 0
    nc = t // WKV_CHUNK
    ngroup_steps = width // (groups * LANES)
    cidx = (lambda c: nc - 1 - c) if reverse else (lambda c: c)
    seq_spec = pl.BlockSpec((1, WKV_CHUNK, groups * LANES), lambda bi, gi, c: (bi, cidx(c), gi))
    st_spec = pl.BlockSpec((1, groups, LANES, LANES), lambda bi, gi, c: (bi, gi, 0, 0))
    return pl.pallas_call(
        functools.partial(_wkv_kernel, reverse=reverse, chunk=WKV_CHUNK, groups=groups),
        grid=(bsz, ngroup_steps, nc),
        in_specs=[seq_spec] * 6 + [st_spec],
        out_specs=[seq_spec, st_spec],
        out_shape=[jax.ShapeDtypeStruct((bsz, t, width), _F32), jax.ShapeDtypeStruct(s0.shape, _F32)],
        compiler_params=pltpu.CompilerParams(dimension_semantics=("parallel", "parallel", "arbitrary")),
        name="wkv7_rev" if reverse else "wkv7_fwd",
    )(r, lw, k, v, a, b, s0)


def _rms_norm(u, gain):
    uf = u.astype(_F32)
    uf = uf * lax.rsqrt(jnp.mean(uf * uf, axis=-1, keepdims=True) + NORM_EPS)
    return (uf * gain.astype(_F32)).astype(u.dtype)


def _centred_token_shift(p, mu):
    pad = jnp.pad(p, ((0, 0), (1, 1), (0, 0)))
    return p + mu * (0.5 * (pad[:, :-2] + pad[:, 2:]) - p)


def _raster_to_column(u, rows):
    b, s, ch = u.shape
    return u.reshape(b, rows, GRID_W, ch).transpose(0, 2, 1, 3).reshape(b, s, ch)


def _column_to_raster(u, rows):
    b, s, ch = u.shape
    return u.reshape(b, GRID_W, rows, ch).transpose(0, 2, 1, 3).reshape(b, s, ch)


def _rwkv7_branch(slab, s_fwd, s_bwd, w0, w2, a0, a2, g2, k_k, k_a, r_k, ln_w, ln_b):
    bsz, t, _ = slab.shape
    r, k, v, wd_f, wd_b, ad_f, ad_b, gd = jnp.split(slab.astype(_F32), RW_SPLITS, axis=-1)
    heads = lambda u: u.reshape(bsz, t, RW_HEADS, RW_HEAD)
    kk = heads(k * k_k)
    kk = (kk / jnp.maximum(jnp.sqrt(jnp.sum(kk * kk, axis=-1, keepdims=True)), 1e-12)).reshape(bsz, t, RW_WIDTH)

    def direction_inputs(wd, ad, d):
        w_log = -jax.nn.softplus(-(w0[d] + jnp.tanh(wd) @ w2[d])) - 0.5
        a = jax.nn.sigmoid(a0[d] + ad @ a2[d])
        k_dir = k * (1.0 + (a - 1.0) * k_a)
        return -jnp.exp(w_log), a, k_dir

    lw_f, a_f, k_f = direction_inputs(wd_f, ad_f, 0)
    lw_b, a_b, k_b = direction_inputs(wd_b, ad_b, 1)
    y_f, s_fwd = _wkv_scan(r, lw_f, k_f, v, kk, kk * a_f, s_fwd, reverse=False)
    y_b, s_bwd = _wkv_scan(r, lw_b, k_b, v, kk, kk * a_b, s_bwd, reverse=True)
    y = heads(y_f + y_b)
    mean = jnp.mean(y, axis=-1, keepdims=True)
    var = jnp.mean(jnp.square(y - mean), axis=-1, keepdims=True)
    y = ((y - mean) * lax.rsqrt(var + LN_X_EPS)).reshape(bsz, t, RW_WIDTH) * ln_w + ln_b
    bonus = jnp.sum(heads(r * (k_f + k_b)) * r_k, axis=-1, keepdims=True) * heads(v)
    gate = jax.nn.sigmoid(gd) @ g2
    out = (y + bonus.reshape(bsz, t, RW_WIDTH)) * gate
    return out, s_fwd, s_bwd


def _centred_depthwise_conv(u, w):
    ch = u.shape[-1]
    return lax.conv_general_dilated(u, w.astype(_F32)[:, None, :], window_strides=(1,), padding=[(SSM_CONV // 2, SSM_CONV // 2)], dimension_numbers=('NWC', 'WIO', 'NWC'), feature_group_count=ch)


def _ssd_chunked(xdt, log_a, bm, cm, h0):
    bsz, t = xdt.shape[:2]
    nc = t // SSM_CHUNK
    xdt = xdt.reshape(bsz, nc, SSM_CHUNK, SSM_GROUPS, SSM_HPG, SSM_HEADDIM)
    log_a = log_a.reshape(bsz, nc, SSM_CHUNK, SSM_GROUPS, SSM_HPG)
    bm = bm.reshape(bsz, nc, SSM_CHUNK, SSM_GROUPS, SSM_STATE)
    cm = cm.reshape(bsz, nc, SSM_CHUNK, SSM_GROUPS, SSM_STATE)
    cs = jnp.cumsum(log_a, axis=2)
    cs_t = jnp.moveaxis(cs, 2, -1)
    lower = jnp.tril(jnp.ones((SSM_CHUNK, SSM_CHUNK), dtype=bool))
    seg = jnp.exp(jnp.where(lower, cs_t[..., :, None] - cs_t[..., None, :], -jnp.inf))
    cb = jnp.einsum('bclgn,bcsgn->bcgls', cm, bm)
    y_diag = jnp.einsum('bcgels,bcsgep->bclgep', cb[:, :, :, None] * seg, xdt)
    to_end = jnp.exp(cs[:, :, -1:] - cs)
    states = jnp.einsum('bclgn,bclgep->bcgepn', bm, xdt * to_end[..., None])
    chunk_decay = jnp.exp(cs[:, :, -1])

    def carry_step(h, inp):
        st, dec = inp
        return h * dec[..., None, None] + st, h
    h_last, h_in = lax.scan(carry_step, h0, (jnp.moveaxis(states, 1, 0), jnp.moveaxis(chunk_decay, 1, 0)))
    h_in = jnp.moveaxis(h_in, 0, 1)
    y_off = jnp.einsum('bclgn,bcgepn->bclgep', cm, h_in) * jnp.exp(cs)[..., None]
    return (y_diag + y_off).reshape(bsz, t, SSM_GROUPS, SSM_HPG, SSM_HEADDIM), h_last


def _mamba2_branch(xbc, dt_raw, h0_fwd, h0_bwd, conv_w, conv_b, dt_bias, a_log, d_skip):
    bsz, t, _ = xbc.shape
    xbc = jax.nn.silu(_centred_depthwise_conv(xbc.astype(_F32), conv_w) + conv_b)
    xs, bm, cm = jnp.split(xbc, (SSM_WIDTH, SSM_WIDTH + SSM_GROUPS * SSM_STATE), axis=-1)
    xs = xs.reshape(bsz, t, SSM_GROUPS, SSM_HPG, SSM_HEADDIM)
    bm = bm.reshape(bsz, t, SSM_GROUPS, SSM_STATE)
    cm = cm.reshape(bsz, t, SSM_GROUPS, SSM_STATE)
    dt = jax.nn.softplus(dt_raw.astype(_F32).reshape(bsz, t, 2, SSM_HEADS) + dt_bias)
    dt = dt.reshape(bsz, t, 2, SSM_GROUPS, SSM_HPG)
    log_a = -jnp.exp(a_log).reshape(2, SSM_GROUPS, SSM_HPG) * dt
    y_f, h_f = _ssd_chunked(xs * dt[:, :, 0, ..., None], log_a[:, :, 0], bm, cm, h0_fwd)
    flip = lambda u: jnp.flip(u, axis=1)
    y_b, h_b = _ssd_chunked(flip(xs * dt[:, :, 1, ..., None]), flip(log_a[:, :, 1]), flip(bm), flip(cm), h0_bwd)
    y = y_f + flip(y_b) + d_skip.reshape(SSM_GROUPS, SSM_HPG)[..., None] * xs
    return y.reshape(bsz, t, SSM_WIDTH), h_f, h_b


def _gated_group_rmsnorm(y, z, gain):
    bsz, t, _ = y.shape
    u = (y * jax.nn.silu(z.astype(_F32))).reshape(bsz, t, SSM_GROUPS, SSM_WIDTH // SSM_GROUPS)
    u = u * lax.rsqrt(jnp.mean(u * u, axis=-1, keepdims=True) + NORM_EPS)
    return u.reshape(bsz, t, SSM_WIDTH) * gain


def _merge_branches(y_a, y_b, gates, w_a, w_b, w_o):
    g_a, g_b = jnp.split(gates, 2, axis=-1)
    m = jax.nn.sigmoid(g_a) * (y_a @ w_a) + jax.nn.sigmoid(g_b) * (y_b @ w_b)
    return m @ w_o


PEER_ROUTE_TOKENS = 256
PEER_TOKEN_BLOCK = 512
PEER_EXPERT_BLOCK = 512
PEER_ROW_TILE = 16


def _top_values(x, count):
    rows = lax.broadcasted_iota(jnp.int32, x.shape, 0)
    vals = []
    for _ in range(count):
        m = jnp.max(x, axis=0, keepdims=True)
        vals.append(m)
        first = jnp.min(jnp.where(x == m, rows, x.shape[0]), axis=0, keepdims=True)
        x = jnp.where(rows == first, -jnp.inf, x)
    return vals


def _peer_route_kernel(h_ref, wqt_ref, sub_ref, s1_ref, e1_ref, s2_ref, e2_ref, thr_ref):
    qt = lax.dot_general(wqt_ref[...], h_ref[...].astype(_BF16), _NT, preferred_element_type=_F32)
    for h in range(PEER_HEADS):
        sc = []
        for s in range(2):
            lo = (2 * h + s) * PEER_HALF
            sc.append(jnp.dot(sub_ref[2 * h + s], qt[lo:lo + PEER_HALF].astype(_BF16), preferred_element_type=_F32))
        top_a = _top_values(sc[0], PEER_TOPK)
        top_b = jnp.concatenate(_top_values(sc[1], PEER_TOPK), axis=0)
        cand = [top_a[i] + top_b[:PEER_TOPK // (i + 1)] for i in range(PEER_TOPK)]
        cand = jnp.concatenate(cand, axis=0)
        best = _top_values(cand, PEER_TOPK)
        norm = best[0] * 0.0
        for val in best:
            norm = norm + jnp.exp(val - best[0])
        s1_ref[h] = sc[0]
        s2_ref[h] = sc[1]
        e1_ref[h] = jnp.exp(sc[0] - top_a[0]) / norm
        e2_ref[h] = jnp.exp(sc[1] - top_b[0:1])
        thr_ref[h:h + 1, :] = best[PEER_TOPK - 1]


def _peer_route(tok, wq, subkeys):
    n, d = tok.shape
    tr = PEER_ROUTE_TOKENS
    wqt = wq.T.astype(_BF16)
    sub = subkeys.reshape(PEER_HEADS * 2, PEER_KEYS, PEER_HALF).astype(_BF16)
    key_spec = pl.BlockSpec((PEER_HEADS, PEER_KEYS, tr), lambda i: (0, 0, i))
    key_shape = jax.ShapeDtypeStruct((PEER_HEADS, PEER_KEYS, n), _F32)
    return pl.pallas_call(
        _peer_route_kernel,
        grid=(n // tr,),
        in_specs=[pl.BlockSpec((tr, d), lambda i: (i, 0)),
                  pl.BlockSpec(wqt.shape, lambda i: (0, 0)),
                  pl.BlockSpec(sub.shape, lambda i: (0, 0, 0))],
        out_specs=[key_spec, key_spec, key_spec, key_spec, pl.BlockSpec((PEER_HEADS, tr), lambda i: (0, i))],
        out_shape=[key_shape, key_shape, key_shape, key_shape, jax.ShapeDtypeStruct((PEER_HEADS, n), _F32)],
        compiler_params=pltpu.CompilerParams(dimension_semantics=("parallel",), vmem_limit_bytes=48 * 2**20),
        name="peer_route",
    )(tok, wqt, sub)


def _gelu_exact(x):
    return 0.5 * x * (1.0 + lax.erf(x * (1.0 / math.sqrt(2.0))))


def _peer_expert_kernel(h_ref, u_ref, v_ref, s1_ref, e1_ref, s2_ref, e2_ref, thr_ref, o_ref, x_scr, a_scr, w_scr):
    eb = pl.program_id(1)

    @pl.when(eb == 0)
    def _():
        x_scr[...] = h_ref[...].astype(_BF16)
        o_ref[...] = jnp.zeros_like(o_ref)

    a_scr[...] = lax.dot_general(u_ref[...], x_scr[...], _NT, preferred_element_type=_F32)
    keys_per_step = PEER_EXPERT_BLOCK // PEER_KEYS
    rt = PEER_ROW_TILE

    def row_tile(r, carry):
        r0 = pl.multiple_of(r * rt, rt)
        for j in range(keys_per_step):
            i1 = eb * keys_per_step + j
            gate = None
            for h in range(PEER_HEADS):
                s2 = s2_ref[h, pl.ds(r0, rt), :]
                e2 = e2_ref[h, pl.ds(r0, rt), :]
                sel = (s1_ref[h, pl.ds(i1, 1), :] + s2) >= thr_ref[h:h + 1, :]
                term = jnp.where(sel, e1_ref[h, pl.ds(i1, 1), :] * e2, 0.0)
                gate = term if gate is None else gate + term
            rows = pl.ds(pl.multiple_of(j * PEER_KEYS + r0, rt), rt)
            w_scr[rows, :] = (gate * _gelu_exact(a_scr[rows, :])).astype(_BF16)
        return carry

    lax.fori_loop(0, PEER_KEYS // rt, row_tile, 0)
    o_ref[...] += lax.dot_general(w_scr[...], v_ref[...], _TN, preferred_element_type=_F32)


def _peer_ffn(h, wq, subkeys, u_tab, v_tab):
    bsz, t, d = h.shape
    n = bsz * t
    tok = h.reshape(n, d)
    s1, e1, s2, e2, thr = _peer_route(tok, wq, subkeys)
    tb, eb = PEER_TOKEN_BLOCK, PEER_EXPERT_BLOCK
    n_experts = u_tab.shape[0]
    key_spec = pl.BlockSpec((PEER_HEADS, PEER_KEYS, tb), lambda i, e: (0, 0, i))
    tab_spec = pl.BlockSpec((eb, d), lambda i, e: (e, 0))
    out = pl.pallas_call(
        _peer_expert_kernel,
        grid=(n // tb, n_experts // eb),
        in_specs=[pl.BlockSpec((tb, d), lambda i, e: (i, 0)), tab_spec, tab_spec,
                  key_spec, key_spec, key_spec, key_spec, pl.BlockSpec((PEER_HEADS, tb), lambda i, e: (0, i))],
        out_specs=pl.BlockSpec((tb, d), lambda i, e: (i, 0)),
        out_shape=jax.ShapeDtypeStruct((n, d), _F32),
        scratch_shapes=[pltpu.VMEM((tb, d), _BF16), pltpu.VMEM((eb, tb), _F32), pltpu.VMEM((eb, tb), _BF16)],
        compiler_params=pltpu.CompilerParams(dimension_semantics=("parallel", "arbitrary"), vmem_limit_bytes=56 * 2**20),
        name="peer_experts",
    )(tok, u_tab.astype(_BF16), v_tab.astype(_BF16), s1, e1, s2, e2, thr)
    return out.reshape(bsz, t, d).astype(h.dtype)


def kernel(x, c, ctx, c_ctx, w_mod, b_mod, norm_pre1, norm_post1, norm_pre2, norm_post2, w_in, rw_mu, rw_w0, rw_w2, rw_a0, rw_a2, rw_g2, rw_k_k, rw_k_a, rw_r_k, rw_ln_w, rw_ln_b, ssm_conv_w, ssm_conv_b, ssm_dt_bias, ssm_a_log, ssm_d, ssm_norm_w, w_branch_a, w_branch_b, w_out, peer_wq, peer_subkeys, peer_u, peer_v):
    act_dtype = x.dtype
    bsz, seq, _ = x.shape
    rows = seq // GRID_W
    depth = w_mod.shape[0]
    assert depth == 1
    rw_zero = jnp.zeros((bsz, RW_WIDTH // LANES, LANES, LANES), _F32)
    ssm_zero = jnp.zeros((bsz, SSM_GROUPS, SSM_HPG, SSM_HEADDIM, SSM_STATE), _F32)
    l = 0
    mod_x = (jax.nn.silu(c) @ w_mod[l] + b_mod[l])[:, None, :]
    mod_c = (jax.nn.silu(c_ctx) @ w_mod[l] + b_mod[l])[None, None, :]
    sh1x, sc1x, g1x, sh2x, sc2x, g2x = jnp.split(mod_x, N_MOD, axis=-1)
    sh1c, sc1c, g1c, sh2c, sc2c, g2c = jnp.split(mod_c, N_MOD, axis=-1)

    hx = _rms_norm(x, norm_pre1[l]) * (1.0 + sc1x) + sh1x
    hc = _rms_norm(ctx, norm_pre1[l]) * (1.0 + sc1c) + sh1c
    rw_x, z_x, xbc_x, dt_x, gate_x = jnp.split(hx @ w_in[l], IN_SPLITS, axis=-1)
    rw_c, z_c, xbc_c, dt_c, gate_c = jnp.split(hc @ w_in[l], IN_SPLITS, axis=-1)

    rw_params = (rw_w0[l], rw_w2[l], rw_a0[l], rw_a2[l], rw_g2[l], rw_k_k[l], rw_k_a[l], rw_r_k[l], rw_ln_w[l], rw_ln_b[l])
    ya_c, s_fwd, s_bwd = _rwkv7_branch(_centred_token_shift(rw_c, rw_mu[l]), rw_zero, rw_zero, *rw_params)
    ya_x, _, _ = _rwkv7_branch(_centred_token_shift(rw_x, rw_mu[l]), s_fwd, s_bwd, *rw_params)

    ssm_params = (ssm_conv_w[l], ssm_conv_b[l], ssm_dt_bias[l], ssm_a_log[l], ssm_d[l])
    yb_c, h_fwd, h_bwd = _mamba2_branch(xbc_c, dt_c, ssm_zero, ssm_zero, *ssm_params)
    yb_x, _, _ = _mamba2_branch(_raster_to_column(xbc_x, rows), _raster_to_column(dt_x, rows), h_fwd, h_bwd, *ssm_params)
    yb_x = _column_to_raster(yb_x, rows)

    mix_x = _merge_branches(ya_x.astype(act_dtype), _gated_group_rmsnorm(yb_x, z_x, ssm_norm_w[l]).astype(act_dtype), gate_x, w_branch_a[l], w_branch_b[l], w_out[l])
    x = x + g1x * _rms_norm(mix_x, norm_post1[l])

    h2x = _rms_norm(x, norm_pre2[l]) * (1.0 + sc2x) + sh2x
    x = x + g2x * _rms_norm(_peer_ffn(h2x, peer_wq[l], peer_subkeys[l], peer_u[l], peer_v[l]), norm_post2[l])
    return x
```

```python
import functools
import math

import jax
import jax.numpy as jnp
from jax import lax
from jax.experimental import pallas as pl
from jax.experimental.pallas import tpu as pltpu

D_MODEL = 2048
GRID_W = 64
N_MOD = 6
NORM_EPS = 1e-6
RW_HEAD = 64
RW_WIDTH = D_MODEL
RW_HEADS = RW_WIDTH // RW_HEAD
LORA_W = 96
LORA_A = 96
LORA_G = 256
LN_X_EPS = 64e-5
SSM_WIDTH = D_MODEL
SSM_HEADDIM = 64
SSM_HEADS = SSM_WIDTH // SSM_HEADDIM
SSM_GROUPS = 8
SSM_HPG = SSM_HEADS // SSM_GROUPS
SSM_STATE = 128
SSM_CONV = 5
SSM_CHUNK = 128
PEER_HEADS = 8
PEER_KEYS = 128
PEER_TOPK = 16
PEER_QDIM = 256
PEER_HALF = PEER_QDIM // 2
PEER_BLOCK = 128
RW_COLS = 3 * RW_WIDTH + 2 * LORA_W + 2 * LORA_A + LORA_G
XBC_COLS = SSM_WIDTH + 2 * SSM_GROUPS * SSM_STATE
IN_SPLITS = (RW_COLS, RW_COLS + SSM_WIDTH, RW_COLS + SSM_WIDTH + XBC_COLS, RW_COLS + SSM_WIDTH + XBC_COLS + 2 * SSM_HEADS)
RW_SPLITS = (RW_WIDTH, 2 * RW_WIDTH, 3 * RW_WIDTH, 3 * RW_WIDTH + LORA_W, 3 * RW_WIDTH + 2 * LORA_W, 3 * RW_WIDTH + 2 * LORA_W + LORA_A, 3 * RW_WIDTH + 2 * LORA_W + 2 * LORA_A)

LANES = 128
WKV_CHUNK = 64
WKV_PAIR = LANES // RW_HEAD
WKV_GROUPS_PER_STEP = 8

_F32 = jnp.float32
_BF16 = jnp.bfloat16
_NT = (((1,), (1,)), ((), ()))
_TN = (((0,), (0,)), ((), ()))


def _mm(a, b, dims=None):
    a = a.astype(_BF16)
    b = b.astype(_BF16)
    if dims is None:
        return jnp.dot(a, b, preferred_element_type=_F32)
    return lax.dot_general(a, b, dims, preferred_element_type=_F32)


def _wkv_kernel(r_ref, lw_ref, k_ref, v_ref, a_ref, b_ref, s0_ref, y_ref, s_ref, *, reverse, chunk, groups):
    L = chunk
    L2 = WKV_PAIR * L

    @pl.when(pl.program_id(2) == 0)
    def _():
        s_ref[...] = s0_ref[...]

    ti = lax.broadcasted_iota(jnp.int32, (L, L), 0)
    tj = lax.broadcasted_iota(jnp.int32, (L, L), 1)
    before_incl = (ti <= tj) if reverse else (ti >= tj)
    cum = jnp.where(before_incl, 1.0, 0.0).astype(_BF16)

    lw = lw_ref[0]
    hi = lw.astype(_BF16)
    rem = lw - hi.astype(_F32)
    mid = rem.astype(_BF16)
    lo = (rem - mid.astype(_F32)).astype(_BF16)
    cs = (jnp.dot(cum, hi, preferred_element_type=_F32)
          + jnp.dot(cum, mid, preferred_element_type=_F32)
          + jnp.dot(cum, lo, preferred_element_type=_F32))
    tot = cs[0:1] if reverse else cs[L - 1:L]
    half = 0.5 * tot
    e_pos = jnp.exp(cs - half)
    e_neg = jnp.exp(half - cs)
    e_prev = jnp.exp(cs - lw - half)
    e_end = jnp.exp(tot - cs)
    e_half = jnp.exp(half)
    e_tot = jnp.exp(tot)

    rt = r_ref[0] * e_pos
    at = a_ref[0] * e_prev
    kt = k_ref[0] * e_neg
    bt = b_ref[0] * e_neg
    ke = k_ref[0] * e_end
    be = b_ref[0] * e_end
    vv = v_ref[0]

    lane = lax.broadcasted_iota(jnp.int32, (L, LANES), 1)
    first_head = lane < RW_HEAD

    def stack(x):
        return jnp.concatenate([jnp.where(first_head, x, 0.0), jnp.where(first_head, 0.0, x)], axis=0)

    si = lax.broadcasted_iota(jnp.int32, (L2, L2), 0)
    sj = lax.broadcasted_iota(jnp.int32, (L2, L2), 1)
    same_head = (si < L) == (sj < L)
    pi = jnp.where(si < L, si, si - L)
    pj = jnp.where(sj < L, sj, sj - L)
    incl2 = same_head & ((pi <= pj) if reverse else (pi >= pj))
    strict2 = same_head & ((pi < pj) if reverse else (pi > pj))
    eye2 = jnp.where(si == sj, 1.0, 0.0)

    gs = range(groups)
    sls = [slice(g * LANES, (g + 1) * LANES) for g in gs]
    v_s = [stack(vv[:, sl]).astype(_BF16) for sl in sls]
    lhs = [jnp.concatenate([stack(at[:, sl]), stack(rt[:, sl])], axis=0).astype(_BF16) for sl in sls]
    rhs = [jnp.concatenate([stack(kt[:, sl]), stack(bt[:, sl])], axis=0).astype(_BF16) for sl in sls]
    state = [s_ref[0, g] for g in gs]
    scores = [_mm(lhs[g], rhs[g], _NT) for g in gs]
    from_state = [_mm(lhs[g], state[g] * e_half[:, sls[g]], _NT) for g in gs]

    power = [jnp.where(strict2, -scores[g][:L2, L2:], 0.0).astype(_BF16) for g in gs]
    inv = [eye2 + power[g] for g in gs]
    ak_v = [_mm(jnp.where(strict2, scores[g][:L2, :L2], 0.0), v_s[g]) for g in gs]
    for _ in range(int(math.log2(L)) - 1):
        power = [_mm(power[g], power[g]).astype(_BF16) for g in gs]
        inv = [inv[g] + _mm(inv[g], power[g]) for g in gs]
    u_s = [_mm(inv[g], from_state[g][:L2] + ak_v[g]) for g in gs]

    vu = [jnp.concatenate([v_s[g], u_s[g].astype(_BF16)], axis=0) for g in gs]
    for g in gs:
        r_kb = jnp.concatenate([jnp.where(incl2, scores[g][L2:, :L2], 0.0),
                                jnp.where(incl2, -scores[g][L2:, L2:], 0.0)], axis=1)
        y_s = from_state[g][L2:] + _mm(r_kb, vu[g])
        y_ref[0, :, sls[g]] = y_s[:L] + y_s[L:]
    for g in gs:
        kb = jnp.concatenate([stack(ke[:, sls[g]]), -stack(be[:, sls[g]])], axis=0)
        s_ref[0, g] = state[g] * e_tot[:, sls[g]] + _mm(vu[g], kb, _TN)


def _wkv_scan(r, lw, k, v, a, b, s0, reverse):
    bsz, t, width = r.shape
    groups = WKV_GROUPS_PER_STEP
    assert t % WKV_CHUNK == 0 and width % (groups * LANES) == 0
    nc = t // WKV_CHUNK
    ngroup_steps = width // (groups * LANES)
    cidx = (lambda c: nc - 1 - c) if reverse else (lambda c: c)
    seq_spec = pl.BlockSpec((1, WKV_CHUNK, groups * LANES), lambda bi, gi, c: (bi, cidx(c), gi))
    st_spec = pl.BlockSpec((1, groups, LANES, LANES), lambda bi, gi, c: (bi, gi, 0, 0))
    return pl.pallas_call(
        functools.partial(_wkv_kernel, reverse=reverse, chunk=WKV_CHUNK, groups=groups),
        grid=(bsz, ngroup_steps, nc),
        in_specs=[seq_spec] * 6 + [st_spec],
        out_specs=[seq_spec, st_spec],
        out_shape=[jax.ShapeDtypeStruct((bsz, t, width), _F32), jax.ShapeDtypeStruct(s0.shape, _F32)],
        compiler_params=pltpu.CompilerParams(dimension_semantics=("parallel", "parallel", "arbitrary")),
        name="wkv7_rev" if reverse else "wkv7_fwd",
    )(r, lw, k, v, a, b, s0)


MM_TILE_M = 1024
MM_TILE_N = 1024
NORM_TILE_M = 512
SMALL_COLS = 768


def _matmul_kernel(a_ref, w_ref, o_ref):
    o_ref[...] = jnp.dot(a_ref[...], w_ref[...], preferred_element_type=_F32)


def _matmul(a, w):
    m, k = a.shape
    n = w.shape[1]
    tm, tn = min(m, MM_TILE_M), min(n, MM_TILE_N)
    assert m % tm == 0 and n % tn == 0 and a.dtype == _BF16 and w.dtype == _BF16
    return pl.pallas_call(
        _matmul_kernel,
        grid=(n // tn, m // tm),
        in_specs=[pl.BlockSpec((tm, k), lambda j, i: (i, 0)), pl.BlockSpec((k, tn), lambda j, i: (0, j))],
        out_specs=pl.BlockSpec((tm, tn), lambda j, i: (i, j)),
        out_shape=jax.ShapeDtypeStruct((m, n), _F32),
        compiler_params=pltpu.CompilerParams(dimension_semantics=("parallel", "parallel"), vmem_limit_bytes=48 * 2**20),
        name="matmul",
    )(a, w)


def _norm_mod_kernel(x_ref, gain_ref, sc_ref, sh_ref, o_ref):
    x = x_ref[0]
    inv = lax.rsqrt(jnp.mean(x * x, axis=-1, keepdims=True) + NORM_EPS)
    o_ref[0] = ((x * inv * gain_ref[...]) * (1.0 + sc_ref[0]) + sh_ref[0]).astype(o_ref.dtype)


def _norm_mod(x, gain, scale, shift):
    bsz, t, d = x.shape
    tm = min(t, NORM_TILE_M)
    mod_spec = pl.BlockSpec((1, 1, d), (lambda b, i: (b, 0, 0)) if scale.shape[0] == bsz else (lambda b, i: (0, 0, 0)))
    return pl.pallas_call(
        _norm_mod_kernel,
        grid=(bsz, t // tm),
        in_specs=[pl.BlockSpec((1, tm, d), lambda b, i: (b, i, 0)), pl.BlockSpec((1, d), lambda b, i: (0, 0)), mod_spec, mod_spec],
        out_specs=pl.BlockSpec((1, tm, d), lambda b, i: (b, i, 0)),
        out_shape=jax.ShapeDtypeStruct((bsz, t, d), _BF16),
        compiler_params=pltpu.CompilerParams(dimension_semantics=("parallel", "parallel")),
        name="norm_mod",
    )(x, gain.reshape(1, d), scale, shift)


def _in_proj_weights(w):
    rw_end, z_end, xbc_end, dt_end = IN_SPLITS
    lora_dt = jnp.concatenate([w[:, RW_SPLITS[2]:rw_end], w[:, xbc_end:dt_end]], axis=1)
    lora_dt = jnp.pad(lora_dt, ((0, 0), (0, SMALL_COLS - lora_dt.shape[1])))
    parts = [w[:, :RW_SPLITS[0]], w[:, RW_SPLITS[0]:RW_SPLITS[1]], w[:, RW_SPLITS[1]:RW_SPLITS[2]],
             w[:, rw_end:z_end], w[:, z_end:xbc_end], w[:, dt_end:], lora_dt]
    return [p.astype(_BF16) for p in parts]


def _in_proj(h, weights):
    bsz, t, d = h.shape
    r, k, v, z, xbc, gate, lora_dt = [_matmul(h.reshape(bsz * t, d), w).reshape(bsz, t, -1) for w in weights]
    n_lora = RW_COLS - RW_SPLITS[2]
    return r, k, v, lora_dt[..., :n_lora], z, xbc, lora_dt[..., n_lora:n_lora + 2 * SSM_HEADS], gate


def _rms_norm(u, gain):
    uf = u.astype(_F32)
    uf = uf * lax.rsqrt(jnp.mean(uf * uf, axis=-1, keepdims=True) + NORM_EPS)
    return (uf * gain.astype(_F32)).astype(u.dtype)


def _centred_token_shift(p, mu):
    pad = jnp.pad(p, ((0, 0), (1, 1), (0, 0)))
    return p + mu * (0.5 * (pad[:, :-2] + pad[:, 2:]) - p)


def _raster_to_column(u, rows):
    b, s, ch = u.shape
    return u.reshape(b, rows, GRID_W, ch).transpose(0, 2, 1, 3).reshape(b, s, ch)


def _column_to_raster(u, rows):
    b, s, ch = u.shape
    return u.reshape(b, GRID_W, rows, ch).transpose(0, 2, 1, 3).reshape(b, s, ch)


def _rwkv7_branch(r, k, v, lora, mu, s_fwd, s_bwd, w0, w2, a0, a2, g2, k_k, k_a, r_k, ln_w, ln_b):
    bsz, t, _ = r.shape
    r = _centred_token_shift(r, mu[RW_SPLITS[0] - RW_WIDTH:RW_SPLITS[0]])
    k = _centred_token_shift(k, mu[RW_SPLITS[0]:RW_SPLITS[1]])
    v = _centred_token_shift(v, mu[RW_SPLITS[1]:RW_SPLITS[2]])
    lora = _centred_token_shift(lora, mu[RW_SPLITS[2]:])
    wd_f, wd_b, ad_f, ad_b, gd = jnp.split(lora, [s - RW_SPLITS[2] for s in RW_SPLITS[3:]], axis=-1)
    heads = lambda u: u.reshape(bsz, t, RW_HEADS, RW_HEAD)
    kk = heads(k * k_k)
    kk = (kk / jnp.maximum(jnp.sqrt(jnp.sum(kk * kk, axis=-1, keepdims=True)), 1e-12)).reshape(bsz, t, RW_WIDTH)

    def direction_inputs(wd, ad, d):
        w_log = -jax.nn.softplus(-(w0[d] + jnp.tanh(wd) @ w2[d])) - 0.5
        a = jax.nn.sigmoid(a0[d] + ad @ a2[d])
        k_dir = k * (1.0 + (a - 1.0) * k_a)
        return -jnp.exp(w_log), a, k_dir

    lw_f, a_f, k_f = direction_inputs(wd_f, ad_f, 0)
    lw_b, a_b, k_b = direction_inputs(wd_b, ad_b, 1)
    y_f, s_fwd = _wkv_scan(r, lw_f, k_f, v, kk, kk * a_f, s_fwd, reverse=False)
    y_b, s_bwd = _wkv_scan(r, lw_b, k_b, v, kk, kk * a_b, s_bwd, reverse=True)
    y = heads(y_f + y_b)
    mean = jnp.mean(y, axis=-1, keepdims=True)
    var = jnp.mean(jnp.square(y - mean), axis=-1, keepdims=True)
    y = ((y - mean) * lax.rsqrt(var + LN_X_EPS)).reshape(bsz, t, RW_WIDTH) * ln_w + ln_b
    bonus = jnp.sum(heads(r * (k_f + k_b)) * r_k, axis=-1, keepdims=True) * heads(v)
    gate = jax.nn.sigmoid(gd) @ g2
    out = (y + bonus.reshape(bsz, t, RW_WIDTH)) * gate
    return out, s_fwd, s_bwd


def _centred_depthwise_conv(u, w):
    ch = u.shape[-1]
    return lax.conv_general_dilated(u, w.astype(_F32)[:, None, :], window_strides=(1,), padding=[(SSM_CONV // 2, SSM_CONV // 2)], dimension_numbers=('NWC', 'WIO', 'NWC'), feature_group_count=ch)


def _split3(x):
    hi = x.astype(_BF16)
    rem = x - hi.astype(_F32)
    mid = rem.astype(_BF16)
    lo = (rem - mid.astype(_F32)).astype(_BF16)
    return hi, mid, lo


def _ssd_kernel(xs_ref, bm_ref, cm_ref, la_ref, lat_ref, dt_ref, h0_ref, y_ref, h_ref, *, reverse):
    L = SSM_CHUNK
    P = SSM_HEADDIM

    @pl.when(pl.program_id(1) == 0)
    def _():
        h_ref[...] = h0_ref[...]

    ti = lax.broadcasted_iota(jnp.int32, (L, L), 0)
    tj = lax.broadcasted_iota(jnp.int32, (L, L), 1)
    before_incl = (ti <= tj) if reverse else (ti >= tj)
    cum = jnp.where(before_incl, 1.0, 0.0).astype(_BF16)
    cum_t = jnp.where(before_incl, 0.0, 1.0).astype(_BF16) + jnp.where(ti == tj, 1.0, 0.0).astype(_BF16)

    la = la_ref[0]
    cs = sum(jnp.dot(cum, part, preferred_element_type=_F32) for part in _split3(la))
    cs_t = sum(jnp.dot(part, cum_t, preferred_element_type=_F32) for part in _split3(lat_ref[0]))
    tot = cs[0:1] if reverse else cs[L - 1:L]
    dt = dt_ref[0]

    hi = lax.broadcasted_iota(jnp.int32, (SSM_HEADS, SSM_WIDTH), 0)
    hj = lax.broadcasted_iota(jnp.int32, (SSM_HEADS, SSM_WIDTH), 1)
    lo_edge = hi * P
    widen = jnp.where((hj >= lo_edge) & (hj < lo_edge + P), 1.0, 0.0).astype(_BF16)

    def wide(cols):
        return sum(jnp.dot(part, widen, preferred_element_type=_F32) for part in _split3(cols))

    xs = xs_ref[0]
    xdt = xs * wide(dt)
    xdt_end = (xs * wide(dt * jnp.exp(tot - cs))).astype(_BF16)
    xdt = xdt.astype(_BF16)
    decay_in = wide(jnp.exp(cs))
    e_tot = jnp.exp(tot)

    pieces = []
    for g in range(SSM_GROUPS):
        bm = bm_ref[0, :, g * SSM_STATE:(g + 1) * SSM_STATE].astype(_BF16)
        cm = cm_ref[0, :, g * SSM_STATE:(g + 1) * SSM_STATE].astype(_BF16)
        cb = lax.dot_general(cm, bm, _NT, preferred_element_type=_F32)
        for e in range(SSM_HPG):
            h = g * SSM_HPG + e
            cols = slice(h * P, (h + 1) * P)
            seg = jnp.where(before_incl, jnp.exp(cs[:, h:h + 1] - cs_t[h:h + 1, :]), 0.0)
            state = h_ref[0, h]
            y_h = jnp.dot((cb * seg).astype(_BF16), xdt[:, cols], preferred_element_type=_F32)
            y_h = y_h + lax.dot_general(cm, state.astype(_BF16), _NT, preferred_element_type=_F32) * decay_in[:, cols]
            pieces.append(y_h)
            new = lax.dot_general(xdt_end[:, cols], bm, _TN, preferred_element_type=_F32)
            h_ref[0, h] = state * e_tot[:, h:h + 1] + new
    y_ref[0] = jnp.concatenate(pieces, axis=1)


def _ssd_scan(xs, bm, cm, log_a, dt, h0, reverse):
    bsz, t, width = xs.shape
    nc = t // SSM_CHUNK
    cidx = (lambda c: nc - 1 - c) if reverse else (lambda c: c)
    seq = lambda w: pl.BlockSpec((1, SSM_CHUNK, w), lambda b, c: (b, cidx(c), 0))
    st_spec = pl.BlockSpec((1,) + h0.shape[1:], lambda b, c: (b, 0, 0, 0))
    gn = SSM_GROUPS * SSM_STATE
    return pl.pallas_call(
        functools.partial(_ssd_kernel, reverse=reverse),
        grid=(bsz, nc),
        in_specs=[seq(width), seq(gn), seq(gn), seq(SSM_HEADS),
                  pl.BlockSpec((1, SSM_HEADS, SSM_CHUNK), lambda b, c: (b, 0, cidx(c))), seq(SSM_HEADS), st_spec],
        out_specs=[seq(width), st_spec],
        out_shape=[jax.ShapeDtypeStruct((bsz, t, width), _F32), jax.ShapeDtypeStruct(h0.shape, _F32)],
        compiler_params=pltpu.CompilerParams(dimension_semantics=("parallel", "arbitrary")),
        name="ssd_rev" if reverse else "ssd_fwd",
    )(xs, bm, cm, log_a, jnp.swapaxes(log_a, 1, 2), dt, h0)


def _mamba2_branch(xbc, dt_raw, h0_fwd, h0_bwd, conv_w, conv_b, dt_bias, a_log, d_skip):
    bsz, t, _ = xbc.shape
    xbc = jax.nn.silu(_centred_depthwise_conv(xbc.astype(_F32), conv_w) + conv_b)
    xs, bm, cm = jnp.split(xbc, (SSM_WIDTH, SSM_WIDTH + SSM_GROUPS * SSM_STATE), axis=-1)
    dt = jax.nn.softplus(dt_raw.astype(_F32).reshape(bsz, t, 2, SSM_HEADS) + dt_bias)
    log_a = -jnp.exp(a_log) * dt
    heads = lambda h: h.reshape(bsz, SSM_HEADS, SSM_HEADDIM, SSM_STATE)
    y_f, h_f = _ssd_scan(xs, bm, cm, log_a[:, :, 0], dt[:, :, 0], heads(h0_fwd), reverse=False)
    y_b, h_b = _ssd_scan(xs, bm, cm, log_a[:, :, 1], dt[:, :, 1], heads(h0_bwd), reverse=True)
    y = y_f + y_b + jnp.repeat(d_skip, SSM_HEADDIM) * xs
    return y, h_f, h_b


def _gated_group_rmsnorm(y, z, gain):
    bsz, t, _ = y.shape
    u = (y * jax.nn.silu(z.astype(_F32))).reshape(bsz, t, SSM_GROUPS, SSM_WIDTH // SSM_GROUPS)
    u = u * lax.rsqrt(jnp.mean(u * u, axis=-1, keepdims=True) + NORM_EPS)
    return u.reshape(bsz, t, SSM_WIDTH) * gain


def _merge_branches(y_a, y_b, gates, w_a, w_b, w_o):
    bsz, t, d = y_a.shape
    mm = lambda u, w: _matmul(u.reshape(bsz * t, -1).astype(_BF16), w.astype(_BF16)).reshape(bsz, t, -1)
    g_a, g_b = jnp.split(gates, 2, axis=-1)
    m = jax.nn.sigmoid(g_a) * mm(y_a, w_a) + jax.nn.sigmoid(g_b) * mm(y_b, w_b)
    return mm(m, w_o)


PEER_ROUTE_TOKENS = 256
PEER_TOKEN_BLOCK = 512
PEER_EXPERT_BLOCK = 512
PEER_ROW_TILE = 16


def _top_values(x, count):
    rows = lax.broadcasted_iota(jnp.int32, x.shape, 0)
    vals = []
    for _ in range(count):
        m = jnp.max(x, axis=0, keepdims=True)
        vals.append(m)
        first = jnp.min(jnp.where(x == m, rows, x.shape[0]), axis=0, keepdims=True)
        x = jnp.where(rows == first, -jnp.inf, x)
    return vals


def _peer_route_kernel(h_ref, wqt_ref, sub_ref, s1_ref, e1_ref, s2_ref, e2_ref, thr_ref):
    qt = lax.dot_general(wqt_ref[...], h_ref[...].astype(_BF16), _NT, preferred_element_type=_F32)
    for h in range(PEER_HEADS):
        sc = []
        for s in range(2):
            lo = (2 * h + s) * PEER_HALF
            sc.append(jnp.dot(sub_ref[2 * h + s], qt[lo:lo + PEER_HALF].astype(_BF16), preferred_element_type=_F32))
        top_a = _top_values(sc[0], PEER_TOPK)
        top_b = jnp.concatenate(_top_values(sc[1], PEER_TOPK), axis=0)
        cand = [top_a[i] + top_b[:PEER_TOPK // (i + 1)] for i in range(PEER_TOPK)]
        cand = jnp.concatenate(cand, axis=0)
        best = _top_values(cand, PEER_TOPK)
        norm = best[0] * 0.0
        for val in best:
            norm = norm + jnp.exp(val - best[0])
        s1_ref[h] = sc[0]
        s2_ref[h] = sc[1]
        e1_ref[h] = jnp.exp(sc[0] - top_a[0]) / norm
        e2_ref[h] = jnp.exp(sc[1] - top_b[0:1])
        thr_ref[h:h + 1, :] = best[PEER_TOPK - 1]


def _peer_route(tok, wq, subkeys):
    n, d = tok.shape
    tr = PEER_ROUTE_TOKENS
    wqt = wq.T.astype(_BF16)
    sub = subkeys.reshape(PEER_HEADS * 2, PEER_KEYS, PEER_HALF).astype(_BF16)
    key_spec = pl.BlockSpec((PEER_HEADS, PEER_KEYS, tr), lambda i: (0, 0, i))
    key_shape = jax.ShapeDtypeStruct((PEER_HEADS, PEER_KEYS, n), _F32)
    return pl.pallas_call(
        _peer_route_kernel,
        grid=(n // tr,),
        in_specs=[pl.BlockSpec((tr, d), lambda i: (i, 0)),
                  pl.BlockSpec(wqt.shape, lambda i: (0, 0)),
                  pl.BlockSpec(sub.shape, lambda i: (0, 0, 0))],
        out_specs=[key_spec, key_spec, key_spec, key_spec, pl.BlockSpec((PEER_HEADS, tr), lambda i: (0, i))],
        out_shape=[key_shape, key_shape, key_shape, key_shape, jax.ShapeDtypeStruct((PEER_HEADS, n), _F32)],
        compiler_params=pltpu.CompilerParams(dimension_semantics=("parallel",), vmem_limit_bytes=48 * 2**20),
        name="peer_route",
    )(tok, wqt, sub)


def _gelu_exact(x):
    return 0.5 * x * (1.0 + lax.erf(x * (1.0 / math.sqrt(2.0))))


def _peer_expert_kernel(h_ref, u_ref, v_ref, s1_ref, e1_ref, s2_ref, e2_ref, thr_ref, o_ref, x_scr, a_scr, w_scr):
    eb = pl.program_id(1)

    @pl.when(eb == 0)
    def _():
        x_scr[...] = h_ref[...].astype(_BF16)
        o_ref[...] = jnp.zeros_like(o_ref)

    a_scr[...] = lax.dot_general(u_ref[...], x_scr[...], _NT, preferred_element_type=_F32)
    keys_per_step = PEER_EXPERT_BLOCK // PEER_KEYS
    rt = PEER_ROW_TILE

    def row_tile(r, carry):
        r0 = pl.multiple_of(r * rt, rt)
        for j in range(keys_per_step):
            i1 = eb * keys_per_step + j
            gate = None
            for h in range(PEER_HEADS):
                s2 = s2_ref[h, pl.ds(r0, rt), :]
                e2 = e2_ref[h, pl.ds(r0, rt), :]
                sel = (s1_ref[h, pl.ds(i1, 1), :] + s2) >= thr_ref[h:h + 1, :]
                term = jnp.where(sel, e1_ref[h, pl.ds(i1, 1), :] * e2, 0.0)
                gate = term if gate is None else gate + term
            rows = pl.ds(pl.multiple_of(j * PEER_KEYS + r0, rt), rt)
            w_scr[rows, :] = (gate * _gelu_exact(a_scr[rows, :])).astype(_BF16)
        return carry

    lax.fori_loop(0, PEER_KEYS // rt, row_tile, 0)
    o_ref[...] += lax.dot_general(w_scr[...], v_ref[...], _TN, preferred_element_type=_F32)


def _peer_ffn(h, wq, subkeys, u_tab, v_tab):
    bsz, t, d = h.shape
    n = bsz * t
    tok = h.reshape(n, d)
    s1, e1, s2, e2, thr = _peer_route(tok, wq, subkeys)
    tb, eb = PEER_TOKEN_BLOCK, PEER_EXPERT_BLOCK
    n_experts = u_tab.shape[0]
    key_spec = pl.BlockSpec((PEER_HEADS, PEER_KEYS, tb), lambda i, e: (0, 0, i))
    tab_spec = pl.BlockSpec((eb, d), lambda i, e: (e, 0))
    out = pl.pallas_call(
        _peer_expert_kernel,
        grid=(n // tb, n_experts // eb),
        in_specs=[pl.BlockSpec((tb, d), lambda i, e: (i, 0)), tab_spec, tab_spec,
                  key_spec, key_spec, key_spec, key_spec, pl.BlockSpec((PEER_HEADS, tb), lambda i, e: (0, i))],
        out_specs=pl.BlockSpec((tb, d), lambda i, e: (i, 0)),
        out_shape=jax.ShapeDtypeStruct((n, d), _F32),
        scratch_shapes=[pltpu.VMEM((tb, d), _BF16), pltpu.VMEM((eb, tb), _F32), pltpu.VMEM((eb, tb), _BF16)],
        compiler_params=pltpu.CompilerParams(dimension_semantics=("parallel", "arbitrary"), vmem_limit_bytes=56 * 2**20),
        name="peer_experts",
    )(tok, u_tab.astype(_BF16), v_tab.astype(_BF16), s1, e1, s2, e2, thr)
    return out.reshape(bsz, t, d).astype(h.dtype)


def kernel(x, c, ctx, c_ctx, w_mod, b_mod, norm_pre1, norm_post1, norm_pre2, norm_post2, w_in, rw_mu, rw_w0, rw_w2, rw_a0, rw_a2, rw_g2, rw_k_k, rw_k_a, rw_r_k, rw_ln_w, rw_ln_b, ssm_conv_w, ssm_conv_b, ssm_dt_bias, ssm_a_log, ssm_d, ssm_norm_w, w_branch_a, w_branch_b, w_out, peer_wq, peer_subkeys, peer_u, peer_v):
    act_dtype = x.dtype
    bsz, seq, _ = x.shape
    rows = seq // GRID_W
    depth = w_mod.shape[0]
    assert depth == 1
    rw_zero = jnp.zeros((bsz, RW_WIDTH // LANES, LANES, LANES), _F32)
    ssm_zero = jnp.zeros((bsz, SSM_GROUPS, SSM_HPG, SSM_HEADDIM, SSM_STATE), _F32)
    l = 0
    mod_x = (jax.nn.silu(c) @ w_mod[l] + b_mod[l])[:, None, :]
    mod_c = (jax.nn.silu(c_ctx) @ w_mod[l] + b_mod[l])[None, None, :]
    sh1x, sc1x, g1x, sh2x, sc2x, g2x = jnp.split(mod_x, N_MOD, axis=-1)
    sh1c, sc1c, g1c, sh2c, sc2c, g2c = jnp.split(mod_c, N_MOD, axis=-1)

    in_w = _in_proj_weights(w_in[l])
    r_x, k_x, v_x, lora_x, z_x, xbc_x, dt_x, gate_x = _in_proj(_norm_mod(x, norm_pre1[l], sc1x, sh1x), in_w)
    r_c, k_c, v_c, lora_c, z_c, xbc_c, dt_c, gate_c = _in_proj(_norm_mod(ctx, norm_pre1[l], sc1c, sh1c), in_w)

    rw_params = (rw_w0[l], rw_w2[l], rw_a0[l], rw_a2[l], rw_g2[l], rw_k_k[l], rw_k_a[l], rw_r_k[l], rw_ln_w[l], rw_ln_b[l])
    ya_c, s_fwd, s_bwd = _rwkv7_branch(r_c, k_c, v_c, lora_c, rw_mu[l], rw_zero, rw_zero, *rw_params)
    ya_x, _, _ = _rwkv7_branch(r_x, k_x, v_x, lora_x, rw_mu[l], s_fwd, s_bwd, *rw_params)

    ssm_params = (ssm_conv_w[l], ssm_conv_b[l], ssm_dt_bias[l], ssm_a_log[l], ssm_d[l])
    yb_c, h_fwd, h_bwd = _mamba2_branch(xbc_c, dt_c, ssm_zero, ssm_zero, *ssm_params)
    yb_x, _, _ = _mamba2_branch(_raster_to_column(xbc_x, rows), _raster_to_column(dt_x, rows), h_fwd, h_bwd, *ssm_params)
    yb_x = _column_to_raster(yb_x, rows)

    mix_x = _merge_branches(ya_x.astype(act_dtype), _gated_group_rmsnorm(yb_x, z_x, ssm_norm_w[l]).astype(act_dtype), gate_x, w_branch_a[l], w_branch_b[l], w_out[l])
    x = x + g1x * _rms_norm(mix_x, norm_post1[l])

    h2x = _rms_norm(x, norm_pre2[l]) * (1.0 + sc2x) + sh2x
    x = x + g2x * _rms_norm(_peer_ffn(h2x, peer_wq[l], peer_subkeys[l], peer_u[l], peer_v[l]), norm_post2[l])
    return x
```

```python
import functools
import math

import jax
import jax.numpy as jnp
from jax import lax
from jax.experimental import pallas as pl
from jax.experimental.pallas import tpu as pltpu

D_MODEL = 2048
GRID_W = 64
N_MOD = 6
NORM_EPS = 1e-6
RW_HEAD = 64
RW_WIDTH = D_MODEL
RW_HEADS = RW_WIDTH // RW_HEAD
LORA_W = 96
LORA_A = 96
LORA_G = 256
LN_X_EPS = 64e-5
SSM_WIDTH = D_MODEL
SSM_HEADDIM = 64
SSM_HEADS = SSM_WIDTH // SSM_HEADDIM
SSM_GROUPS = 8
SSM_HPG = SSM_HEADS // SSM_GROUPS
SSM_STATE = 128
SSM_CONV = 5
SSM_CHUNK = 128
PEER_HEADS = 8
PEER_KEYS = 128
PEER_TOPK = 16
PEER_QDIM = 256
PEER_HALF = PEER_QDIM // 2
PEER_BLOCK = 128
RW_COLS = 3 * RW_WIDTH + 2 * LORA_W + 2 * LORA_A + LORA_G
XBC_COLS = SSM_WIDTH + 2 * SSM_GROUPS * SSM_STATE
IN_SPLITS = (RW_COLS, RW_COLS + SSM_WIDTH, RW_COLS + SSM_WIDTH + XBC_COLS, RW_COLS + SSM_WIDTH + XBC_COLS + 2 * SSM_HEADS)
RW_SPLITS = (RW_WIDTH, 2 * RW_WIDTH, 3 * RW_WIDTH, 3 * RW_WIDTH + LORA_W, 3 * RW_WIDTH + 2 * LORA_W, 3 * RW_WIDTH + 2 * LORA_W + LORA_A, 3 * RW_WIDTH + 2 * LORA_W + 2 * LORA_A)

LANES = 128
WKV_CHUNK = 64
WKV_PAIR = LANES // RW_HEAD
WKV_GROUPS_PER_STEP = 8

_F32 = jnp.float32
_BF16 = jnp.bfloat16
_NT = (((1,), (1,)), ((), ()))
_TN = (((0,), (0,)), ((), ()))


def _mm(a, b, dims=None):
    a = a.astype(_BF16)
    b = b.astype(_BF16)
    if dims is None:
        return jnp.dot(a, b, preferred_element_type=_F32)
    return lax.dot_general(a, b, dims, preferred_element_type=_F32)


def _wkv_kernel(r_ref, lw_ref, k_ref, v_ref, a_ref, b_ref, s0_ref, y_ref, s_ref, *, reverse, chunk, groups):
    L = chunk
    L2 = WKV_PAIR * L

    @pl.when(pl.program_id(2) == 0)
    def _():
        s_ref[...] = s0_ref[...]

    ti = lax.broadcasted_iota(jnp.int32, (L, L), 0)
    tj = lax.broadcasted_iota(jnp.int32, (L, L), 1)
    before_incl = (ti <= tj) if reverse else (ti >= tj)
    cum = jnp.where(before_incl, 1.0, 0.0).astype(_BF16)

    lw = lw_ref[0]
    hi = lw.astype(_BF16)
    rem = lw - hi.astype(_F32)
    mid = rem.astype(_BF16)
    lo = (rem - mid.astype(_F32)).astype(_BF16)
    cs = (jnp.dot(cum, hi, preferred_element_type=_F32)
          + jnp.dot(cum, mid, preferred_element_type=_F32)
          + jnp.dot(cum, lo, preferred_element_type=_F32))
    tot = cs[0:1] if reverse else cs[L - 1:L]
    half = 0.5 * tot
    e_pos = jnp.exp(cs - half)
    e_neg = jnp.exp(half - cs)
    e_prev = jnp.exp(cs - lw - half)
    e_end = jnp.exp(tot - cs)
    e_half = jnp.exp(half)
    e_tot = jnp.exp(tot)

    rt = r_ref[0] * e_pos
    at = a_ref[0] * e_prev
    kt = k_ref[0] * e_neg
    bt = b_ref[0] * e_neg
    ke = k_ref[0] * e_end
    be = b_ref[0] * e_end
    vv = v_ref[0]

    lane = lax.broadcasted_iota(jnp.int32, (L, LANES), 1)
    first_head = lane < RW_HEAD

    def stack(x):
        return jnp.concatenate([jnp.where(first_head, x, 0.0), jnp.where(first_head, 0.0, x)], axis=0)

    si = lax.broadcasted_iota(jnp.int32, (L2, L2), 0)
    sj = lax.broadcasted_iota(jnp.int32, (L2, L2), 1)
    same_head = (si < L) == (sj < L)
    pi = jnp.where(si < L, si, si - L)
    pj = jnp.where(sj < L, sj, sj - L)
    incl2 = same_head & ((pi <= pj) if reverse else (pi >= pj))
    strict2 = same_head & ((pi < pj) if reverse else (pi > pj))
    eye2 = jnp.where(si == sj, 1.0, 0.0)

    gs = range(groups)
    sls = [slice(g * LANES, (g + 1) * LANES) for g in gs]
    v_s = [stack(vv[:, sl]).astype(_BF16) for sl in sls]
    lhs = [jnp.concatenate([stack(at[:, sl]), stack(rt[:, sl])], axis=0).astype(_BF16) for sl in sls]
    rhs = [jnp.concatenate([stack(kt[:, sl]), stack(bt[:, sl])], axis=0).astype(_BF16) for sl in sls]
    state = [s_ref[0, g] for g in gs]
    scores = [_mm(lhs[g], rhs[g], _NT) for g in gs]
    from_state = [_mm(lhs[g], state[g] * e_half[:, sls[g]], _NT) for g in gs]

    power = [jnp.where(strict2, -scores[g][:L2, L2:], 0.0).astype(_BF16) for g in gs]
    inv = [eye2 + power[g] for g in gs]
    ak_v = [_mm(jnp.where(strict2, scores[g][:L2, :L2], 0.0), v_s[g]) for g in gs]
    for _ in range(int(math.log2(L)) - 1):
        power = [_mm(power[g], power[g]).astype(_BF16) for g in gs]
        inv = [inv[g] + _mm(inv[g], power[g]) for g in gs]
    u_s = [_mm(inv[g], from_state[g][:L2] + ak_v[g]) for g in gs]

    vu = [jnp.concatenate([v_s[g], u_s[g].astype(_BF16)], axis=0) for g in gs]
    for g in gs:
        r_kb = jnp.concatenate([jnp.where(incl2, scores[g][L2:, :L2], 0.0),
                                jnp.where(incl2, -scores[g][L2:, L2:], 0.0)], axis=1)
        y_s = from_state[g][L2:] + _mm(r_kb, vu[g])
        y_ref[0, :, sls[g]] = y_s[:L] + y_s[L:]
    for g in gs:
        kb = jnp.concatenate([stack(ke[:, sls[g]]), -stack(be[:, sls[g]])], axis=0)
        s_ref[0, g] = state[g] * e_tot[:, sls[g]] + _mm(vu[g], kb, _TN)


def _wkv_scan(r, lw, k, v, a, b, s0, reverse):
    bsz, t, width = r.shape
    groups = WKV_GROUPS_PER_STEP
    assert t % WKV_CHUNK == 0 and width % (groups * LANES) == 0
    nc = t // WKV_CHUNK
    ngroup_steps = width // (groups * LANES)
    cidx = (lambda c: nc - 1 - c) if reverse else (lambda c: c)
    seq_spec = pl.BlockSpec((1, WKV_CHUNK, groups * LANES), lambda bi, gi, c: (bi, cidx(c), gi))
    st_spec = pl.BlockSpec((1, groups, LANES, LANES), lambda bi, gi, c: (bi, gi, 0, 0))
    return pl.pallas_call(
        functools.partial(_wkv_kernel, reverse=reverse, chunk=WKV_CHUNK, groups=groups),
        grid=(bsz, ngroup_steps, nc),
        in_specs=[seq_spec] * 6 + [st_spec],
        out_specs=[seq_spec, st_spec],
        out_shape=[jax.ShapeDtypeStruct((bsz, t, width), _F32), jax.ShapeDtypeStruct(s0.shape, _F32)],
        compiler_params=pltpu.CompilerParams(dimension_semantics=("parallel", "parallel", "arbitrary")),
        name="wkv7_rev" if reverse else "wkv7_fwd",
    )(r, lw, k, v, a, b, s0)


MM_TILE_M = 1024
MM_TILE_N = 1024
NORM_TILE_M = 512
SMALL_COLS = 768


def _matmul_kernel(a_ref, w_ref, o_ref):
    o_ref[...] = jnp.dot(a_ref[...], w_ref[...], preferred_element_type=_F32)


def _matmul(a, w):
    m, k = a.shape
    n = w.shape[1]
    tm, tn = min(m, MM_TILE_M), min(n, MM_TILE_N)
    assert m % tm == 0 and n % tn == 0 and a.dtype == _BF16 and w.dtype == _BF16
    return pl.pallas_call(
        _matmul_kernel,
        grid=(n // tn, m // tm),
        in_specs=[pl.BlockSpec((tm, k), lambda j, i: (i, 0)), pl.BlockSpec((k, tn), lambda j, i: (0, j))],
        out_specs=pl.BlockSpec((tm, tn), lambda j, i: (i, j)),
        out_shape=jax.ShapeDtypeStruct((m, n), _F32),
        compiler_params=pltpu.CompilerParams(dimension_semantics=("parallel", "parallel"), vmem_limit_bytes=48 * 2**20),
        name="matmul",
    )(a, w)


def _norm_mod_kernel(x_ref, gain_ref, sc_ref, sh_ref, o_ref):
    x = x_ref[0]
    inv = lax.rsqrt(jnp.mean(x * x, axis=-1, keepdims=True) + NORM_EPS)
    o_ref[0] = ((x * inv * gain_ref[...]) * (1.0 + sc_ref[0]) + sh_ref[0]).astype(o_ref.dtype)


def _norm_mod(x, gain, scale, shift):
    bsz, t, d = x.shape
    tm = min(t, NORM_TILE_M)
    mod_spec = pl.BlockSpec((1, 1, d), (lambda b, i: (b, 0, 0)) if scale.shape[0] == bsz else (lambda b, i: (0, 0, 0)))
    return pl.pallas_call(
        _norm_mod_kernel,
        grid=(bsz, t // tm),
        in_specs=[pl.BlockSpec((1, tm, d), lambda b, i: (b, i, 0)), pl.BlockSpec((1, d), lambda b, i: (0, 0)), mod_spec, mod_spec],
        out_specs=pl.BlockSpec((1, tm, d), lambda b, i: (b, i, 0)),
        out_shape=jax.ShapeDtypeStruct((bsz, t, d), _BF16),
        compiler_params=pltpu.CompilerParams(dimension_semantics=("parallel", "parallel")),
        name="norm_mod",
    )(x, gain.reshape(1, d), scale, shift)


def _in_proj_weights(w):
    rw_end, z_end, xbc_end, dt_end = IN_SPLITS
    lora_dt = jnp.concatenate([w[:, RW_SPLITS[2]:rw_end], w[:, xbc_end:dt_end]], axis=1)
    lora_dt = jnp.pad(lora_dt, ((0, 0), (0, SMALL_COLS - lora_dt.shape[1])))
    parts = [w[:, :RW_SPLITS[0]], w[:, RW_SPLITS[0]:RW_SPLITS[1]], w[:, RW_SPLITS[1]:RW_SPLITS[2]],
             w[:, rw_end:z_end], w[:, z_end:xbc_end], w[:, dt_end:], lora_dt]
    return [p.astype(_BF16) for p in parts]


def _in_proj(h, weights):
    bsz, t, d = h.shape
    r, k, v, z, xbc, gate, lora_dt = [_matmul(h.reshape(bsz * t, d), w).reshape(bsz, t, -1) for w in weights]
    n_lora = RW_COLS - RW_SPLITS[2]
    return r, k, v, lora_dt, z, xbc, lora_dt[..., n_lora:n_lora + 2 * SSM_HEADS], gate


def _rms_norm(u, gain):
    uf = u.astype(_F32)
    uf = uf * lax.rsqrt(jnp.mean(uf * uf, axis=-1, keepdims=True) + NORM_EPS)
    return (uf * gain.astype(_F32)).astype(u.dtype)


def _centred_token_shift(p, mu):
    pad = jnp.pad(p, ((0, 0), (1, 1), (0, 0)))
    return p + mu * (0.5 * (pad[:, :-2] + pad[:, 2:]) - p)


def _raster_to_column(u, rows):
    b, s, ch = u.shape
    return u.reshape(b, rows, GRID_W, ch).transpose(0, 2, 1, 3).reshape(b, s, ch)


def _column_to_raster(u, rows):
    b, s, ch = u.shape
    return u.reshape(b, GRID_W, rows, ch).transpose(0, 2, 1, 3).reshape(b, s, ch)


RW_TILE_T = 128
LORA_COLS = RW_COLS - 3 * RW_WIDTH


def _head_matrices():
    head_of_lane = jnp.arange(RW_WIDTH) // RW_HEAD
    onehot = (head_of_lane[:, None] == jnp.arange(RW_HEADS)[None, :]).astype(_BF16)
    return onehot, onehot.T


def _head_sum(x, hsum):
    return sum(jnp.dot(part, hsum, preferred_element_type=_F32) for part in _split3(x))


def _head_widen(cols, hwide):
    return sum(jnp.dot(part, hwide, preferred_element_type=_F32) for part in _split3(cols))


def _token_shift_block(c_ref, p_ref, n_ref, mu, first, last):
    x = c_ref[0]
    tm = x.shape[0]
    row = lax.broadcasted_iota(jnp.int32, x.shape, 0)
    prev_row = jnp.where(first, 0.0, p_ref[0, SUBLANES - 1:SUBLANES, :])
    next_row = jnp.where(last, 0.0, n_ref[0, 0:1, :])
    prev = jnp.where(row == 0, prev_row, pltpu.roll(x, 1, 0))
    nxt = jnp.where(row == tm - 1, next_row, pltpu.roll(x, tm - 1, 0))
    return x + mu * (0.5 * (prev + nxt) - x)


def _rw_pre_kernel(rc, rp, rn, kc, kp, kn, vc, vp, vn, lc, lp, ln,
                   mu_r, mu_k, mu_v, mu_l, w0, a0, k_k, k_a, r_k, ln_b, w2, a2, g2, hsum, hwide,
                   r_o, v_o, kk_o, lwf_o, lwb_o, kf_o, kb_o, kkaf_o, kkab_o, gate_o, bonus_o):
    first = pl.program_id(1) == 0
    last = pl.program_id(1) == pl.num_programs(1) - 1
    r = _token_shift_block(rc, rp, rn, mu_r[...], first, last)
    k = _token_shift_block(kc, kp, kn, mu_k[...], first, last)
    v = _token_shift_block(vc, vp, vn, mu_v[...], first, last)
    lora = _token_shift_block(lc, lp, ln, mu_l[...], first, last)
    r_o[0] = r
    v_o[0] = v

    kk = k * k_k[...]
    norm = jnp.maximum(jnp.sqrt(_head_sum(kk * kk, hsum[...])), 1e-12)
    kk = kk * _head_widen(1.0 / norm, hwide[...])
    kk_o[0] = kk

    k_sum = None
    for d, (lw_o, kd_o, kka_o) in enumerate(((lwf_o, kf_o, kkaf_o), (lwb_o, kb_o, kkab_o))):
        wd = lora[:, d * LORA_W:(d + 1) * LORA_W]
        ad = lora[:, 2 * LORA_W + d * LORA_A:2 * LORA_W + (d + 1) * LORA_A]
        w_log = -jax.nn.softplus(-(w0[d:d + 1, :] + _mm(jnp.tanh(wd), w2[d]))) - 0.5
        lw_o[0] = -jnp.exp(w_log)
        a = jax.nn.sigmoid(a0[d:d + 1, :] + _mm(ad, a2[d]))
        k_dir = k * (1.0 + (a - 1.0) * k_a[...])
        kd_o[0] = k_dir
        kka_o[0] = kk * a
        k_sum = k_dir if k_sum is None else k_sum + k_dir

    gd = lora[:, 2 * LORA_W + 2 * LORA_A:2 * LORA_W + 2 * LORA_A + LORA_G]
    gate = _mm(jax.nn.sigmoid(gd), g2[...])
    bonus = _head_widen(_head_sum(r * k_sum * r_k[...], hsum[...]), hwide[...]) * v
    gate_o[0] = gate
    bonus_o[0] = (ln_b[...] + bonus) * gate


def _rw_pre(r, k, v, lora_dt, mu, w0, w2, a0, a2, g2, k_k, k_a, r_k, ln_b):
    bsz, t, width = r.shape
    tm = RW_TILE_T
    nt = t // tm
    blocks_per_tile = tm // SUBLANES
    last_block = t // SUBLANES - 1
    lcols = lora_dt.shape[-1]

    def specs(cols):
        return [pl.BlockSpec((1, tm, cols), lambda b, i: (b, i, 0)),
                pl.BlockSpec((1, SUBLANES, cols), lambda b, i: (b, jnp.maximum(i * blocks_per_tile - 1, 0), 0)),
                pl.BlockSpec((1, SUBLANES, cols), lambda b, i: (b, jnp.minimum((i + 1) * blocks_per_tile, last_block), 0))]

    def whole(a):
        return pl.BlockSpec(a.shape, lambda b, i: (0,) * a.ndim)

    row = lambda a: a.reshape(1, -1).astype(_F32)
    mu_l = jnp.pad(mu[RW_SPLITS[2]:], (0, lcols - LORA_COLS))
    hsum, hwide = _head_matrices()
    params = [row(mu[:RW_SPLITS[0]]), row(mu[RW_SPLITS[0]:RW_SPLITS[1]]), row(mu[RW_SPLITS[1]:RW_SPLITS[2]]), row(mu_l),
              w0.astype(_F32), a0.astype(_F32), row(k_k), row(k_a), row(r_k), row(ln_b),
              w2.astype(_BF16), a2.astype(_BF16), g2.astype(_BF16), hsum, hwide]
    out_spec = pl.BlockSpec((1, tm, width), lambda b, i: (b, i, 0))
    out_shape = jax.ShapeDtypeStruct((bsz, t, width), _F32)
    return pl.pallas_call(
        _rw_pre_kernel,
        grid=(bsz, nt),
        in_specs=specs(width) * 3 + specs(lcols) + [whole(p) for p in params],
        out_specs=[out_spec] * 11,
        out_shape=[out_shape] * 11,
        compiler_params=pltpu.CompilerParams(dimension_semantics=("parallel", "parallel"), vmem_limit_bytes=56 * 2**20),
        name="rw_pre",
    )(r, r, r, k, k, k, v, v, v, lora_dt, lora_dt, lora_dt, *params)


def _rw_post_kernel(yf, yb, gate, bonus, ln_w, hsum, hwide, o_ref):
    y = yf[0] + yb[0]
    mean = _head_widen(_head_sum(y, hsum[...]), hwide[...]) * (1.0 / RW_HEAD)
    cen = y - mean
    var = _head_widen(_head_sum(cen * cen, hsum[...]), hwide[...]) * (1.0 / RW_HEAD)
    o_ref[0] = (cen * lax.rsqrt(var + LN_X_EPS) * ln_w[...] * gate[0] + bonus[0]).astype(o_ref.dtype)


def _rw_post(y_f, y_b, gate, bonus, ln_w):
    bsz, t, width = y_f.shape
    tm = RW_TILE_T
    hsum, hwide = _head_matrices()
    spec = pl.BlockSpec((1, tm, width), lambda b, i: (b, i, 0))
    whole = lambda a: pl.BlockSpec(a.shape, lambda b, i: (0,) * a.ndim)
    ln_w = ln_w.reshape(1, width).astype(_F32)
    return pl.pallas_call(
        _rw_post_kernel,
        grid=(bsz, t // tm),
        in_specs=[spec] * 4 + [whole(ln_w), whole(hsum), whole(hwide)],
        out_specs=spec,
        out_shape=jax.ShapeDtypeStruct((bsz, t, width), _BF16),
        compiler_params=pltpu.CompilerParams(dimension_semantics=("parallel", "parallel")),
        name="rw_post",
    )(y_f, y_b, gate, bonus, ln_w, hsum, hwide)


def _rwkv7_branch(r, k, v, lora_dt, mu, s_fwd, s_bwd, w0, w2, a0, a2, g2, k_k, k_a, r_k, ln_w, ln_b, want_output):
    r, v, kk, lw_f, lw_b, k_f, k_b, kka_f, kka_b, gate, bonus = _rw_pre(
        r, k, v, lora_dt, mu, w0, w2, a0, a2, g2, k_k, k_a, r_k, ln_b)
    y_f, s_fwd = _wkv_scan(r, lw_f, k_f, v, kk, kka_f, s_fwd, reverse=False)
    y_b, s_bwd = _wkv_scan(r, lw_b, k_b, v, kk, kka_b, s_bwd, reverse=True)
    out = _rw_post(y_f, y_b, gate, bonus, ln_w) if want_output else None
    return out, s_fwd, s_bwd


def _centred_depthwise_conv(u, w):
    ch = u.shape[-1]
    return lax.conv_general_dilated(u, w.astype(_F32)[:, None, :], window_strides=(1,), padding=[(SSM_CONV // 2, SSM_CONV // 2)], dimension_numbers=('NWC', 'WIO', 'NWC'), feature_group_count=ch)


def _split3(x):
    hi = x.astype(_BF16)
    rem = x - hi.astype(_F32)
    mid = rem.astype(_BF16)
    lo = (rem - mid.astype(_F32)).astype(_BF16)
    return hi, mid, lo


def _ssd_kernel(xs_ref, bm_ref, cm_ref, la_ref, lat_ref, dt_ref, h0_ref, y_ref, h_ref, *, reverse):
    L = SSM_CHUNK
    P = SSM_HEADDIM

    @pl.when(pl.program_id(1) == 0)
    def _():
        h_ref[...] = h0_ref[...]

    ti = lax.broadcasted_iota(jnp.int32, (L, L), 0)
    tj = lax.broadcasted_iota(jnp.int32, (L, L), 1)
    before_incl = (ti <= tj) if reverse else (ti >= tj)
    cum = jnp.where(before_incl, 1.0, 0.0).astype(_BF16)
    cum_t = jnp.where(before_incl, 0.0, 1.0).astype(_BF16) + jnp.where(ti == tj, 1.0, 0.0).astype(_BF16)

    la = la_ref[0]
    cs = sum(jnp.dot(cum, part, preferred_element_type=_F32) for part in _split3(la))
    cs_t = sum(jnp.dot(part, cum_t, preferred_element_type=_F32) for part in _split3(lat_ref[0]))
    tot = cs[0:1] if reverse else cs[L - 1:L]
    dt = dt_ref[0]

    hi = lax.broadcasted_iota(jnp.int32, (SSM_HEADS, SSM_WIDTH), 0)
    hj = lax.broadcasted_iota(jnp.int32, (SSM_HEADS, SSM_WIDTH), 1)
    lo_edge = hi * P
    widen = jnp.where((hj >= lo_edge) & (hj < lo_edge + P), 1.0, 0.0).astype(_BF16)

    def wide(cols):
        return sum(jnp.dot(part, widen, preferred_element_type=_F32) for part in _split3(cols))

    xs = xs_ref[0]
    xdt = xs * wide(dt)
    xdt_end = (xs * wide(dt * jnp.exp(tot - cs))).astype(_BF16)
    xdt = xdt.astype(_BF16)
    decay_in = wide(jnp.exp(cs))
    e_tot = jnp.exp(tot)

    pieces = []
    for g in range(SSM_GROUPS):
        bm = bm_ref[0, :, g * SSM_STATE:(g + 1) * SSM_STATE].astype(_BF16)
        cm = cm_ref[0, :, g * SSM_STATE:(g + 1) * SSM_STATE].astype(_BF16)
        cb = lax.dot_general(cm, bm, _NT, preferred_element_type=_F32)
        for e in range(SSM_HPG):
            h = g * SSM_HPG + e
            cols = slice(h * P, (h + 1) * P)
            seg = jnp.where(before_incl, jnp.exp(cs[:, h:h + 1] - cs_t[h:h + 1, :]), 0.0)
            state = h_ref[0, h]
            y_h = jnp.dot((cb * seg).astype(_BF16), xdt[:, cols], preferred_element_type=_F32)
            y_h = y_h + lax.dot_general(cm, state.astype(_BF16), _NT, preferred_element_type=_F32) * decay_in[:, cols]
            pieces.append(y_h)
            new = lax.dot_general(xdt_end[:, cols], bm, _TN, preferred_element_type=_F32)
            h_ref[0, h] = state * e_tot[:, h:h + 1] + new
    y_ref[0] = jnp.concatenate(pieces, axis=1)


def _ssd_scan(xs, bm, cm, log_a, dt, h0, reverse):
    bsz, t, width = xs.shape
    nc = t // SSM_CHUNK
    cidx = (lambda c: nc - 1 - c) if reverse else (lambda c: c)
    seq = lambda w: pl.BlockSpec((1, SSM_CHUNK, w), lambda b, c: (b, cidx(c), 0))
    st_spec = pl.BlockSpec((1,) + h0.shape[1:], lambda b, c: (b, 0, 0, 0))
    gn = SSM_GROUPS * SSM_STATE
    return pl.pallas_call(
        functools.partial(_ssd_kernel, reverse=reverse),
        grid=(bsz, nc),
        in_specs=[seq(width), seq(gn), seq(gn), seq(SSM_HEADS),
                  pl.BlockSpec((1, SSM_HEADS, SSM_CHUNK), lambda b, c: (b, 0, cidx(c))), seq(SSM_HEADS), st_spec],
        out_specs=[seq(width), st_spec],
        out_shape=[jax.ShapeDtypeStruct((bsz, t, width), _F32), jax.ShapeDtypeStruct(h0.shape, _F32)],
        compiler_params=pltpu.CompilerParams(dimension_semantics=("parallel", "arbitrary")),
        name="ssd_rev" if reverse else "ssd_fwd",
    )(xs, bm, cm, log_a, jnp.swapaxes(log_a, 1, 2), dt, h0)


def _mamba2_branch(xbc, dt_raw, h0_fwd, h0_bwd, conv_w, conv_b, dt_bias, a_log, d_skip):
    bsz, t, _ = xbc.shape
    xbc = jax.nn.silu(_centred_depthwise_conv(xbc.astype(_F32), conv_w) + conv_b)
    xs, bm, cm = jnp.split(xbc, (SSM_WIDTH, SSM_WIDTH + SSM_GROUPS * SSM_STATE), axis=-1)
    dt = jax.nn.softplus(dt_raw.astype(_F32).reshape(bsz, t, 2, SSM_HEADS) + dt_bias)
    log_a = -jnp.exp(a_log) * dt
    heads = lambda h: h.reshape(bsz, SSM_HEADS, SSM_HEADDIM, SSM_STATE)
    y_f, h_f = _ssd_scan(xs, bm, cm, log_a[:, :, 0], dt[:, :, 0], heads(h0_fwd), reverse=False)
    y_b, h_b = _ssd_scan(xs, bm, cm, log_a[:, :, 1], dt[:, :, 1], heads(h0_bwd), reverse=True)
    y = y_f + y_b + jnp.repeat(d_skip, SSM_HEADDIM) * xs
    return y, h_f, h_b


def _gated_group_rmsnorm(y, z, gain):
    bsz, t, _ = y.shape
    u = (y * jax.nn.silu(z.astype(_F32))).reshape(bsz, t, SSM_GROUPS, SSM_WIDTH // SSM_GROUPS)
    u = u * lax.rsqrt(jnp.mean(u * u, axis=-1, keepdims=True) + NORM_EPS)
    return u.reshape(bsz, t, SSM_WIDTH) * gain


def _merge_branches(y_a, y_b, gates, w_a, w_b, w_o):
    bsz, t, d = y_a.shape
    mm = lambda u, w: _matmul(u.reshape(bsz * t, -1).astype(_BF16), w.astype(_BF16)).reshape(bsz, t, -1)
    g_a, g_b = jnp.split(gates, 2, axis=-1)
    m = jax.nn.sigmoid(g_a) * mm(y_a, w_a) + jax.nn.sigmoid(g_b) * mm(y_b, w_b)
    return mm(m, w_o)


PEER_ROUTE_TOKENS = 256
PEER_TOKEN_BLOCK = 512
PEER_EXPERT_BLOCK = 512
SUBLANES = 8


def _top_values(x, count):
    rows = lax.broadcasted_iota(jnp.int32, x.shape, 0)
    vals = []
    for _ in range(count):
        m = jnp.max(x, axis=0, keepdims=True)
        vals.append(m)
        first = jnp.min(jnp.where(x == m, rows, x.shape[0]), axis=0, keepdims=True)
        x = jnp.where(rows == first, -jnp.inf, x)
    return vals


def _peer_route_kernel(h_ref, wqt_ref, sub_ref, s1_ref, e1_ref, s2_ref, e2_ref, thr_ref):
    qt = lax.dot_general(wqt_ref[...], h_ref[...].astype(_BF16), _NT, preferred_element_type=_F32)
    for h in range(PEER_HEADS):
        sc = []
        for s in range(2):
            lo = (2 * h + s) * PEER_HALF
            sc.append(jnp.dot(sub_ref[2 * h + s], qt[lo:lo + PEER_HALF].astype(_BF16), preferred_element_type=_F32))
        top_a = _top_values(sc[0], PEER_TOPK)
        top_b = jnp.concatenate(_top_values(sc[1], PEER_TOPK), axis=0)
        cand = [top_a[i] + top_b[:PEER_TOPK // (i + 1)] for i in range(PEER_TOPK)]
        cand = jnp.concatenate(cand, axis=0)
        best = _top_values(cand, PEER_TOPK)
        norm = best[0] * 0.0
        for val in best:
            norm = norm + jnp.exp(val - best[0])
        s1_ref[h] = sc[0]
        s2_ref[h] = sc[1]
        e1_ref[h] = jnp.exp(sc[0] - top_a[0]) / norm
        e2_ref[h] = jnp.exp(sc[1] - top_b[0:1])
        thr_ref[h:h + 1, :] = best[PEER_TOPK - 1]


def _peer_route(tok, wq, subkeys):
    n, d = tok.shape
    tr = PEER_ROUTE_TOKENS
    wqt = wq.T.astype(_BF16)
    sub = subkeys.reshape(PEER_HEADS * 2, PEER_KEYS, PEER_HALF).astype(_BF16)
    key_spec = pl.BlockSpec((PEER_HEADS, PEER_KEYS, tr), lambda i: (0, 0, i))
    key_shape = jax.ShapeDtypeStruct((PEER_HEADS, PEER_KEYS, n), _F32)
    return pl.pallas_call(
        _peer_route_kernel,
        grid=(n // tr,),
        in_specs=[pl.BlockSpec((tr, d), lambda i: (i, 0)),
                  pl.BlockSpec(wqt.shape, lambda i: (0, 0)),
                  pl.BlockSpec(sub.shape, lambda i: (0, 0, 0))],
        out_specs=[key_spec, key_spec, key_spec, key_spec, pl.BlockSpec((PEER_HEADS, tr), lambda i: (0, i))],
        out_shape=[key_shape, key_shape, key_shape, key_shape, jax.ShapeDtypeStruct((PEER_HEADS, n), _F32)],
        compiler_params=pltpu.CompilerParams(dimension_semantics=("parallel",), vmem_limit_bytes=48 * 2**20),
        name="peer_route",
    )(tok, wqt, sub)


def _gelu_exact(x):
    return 0.5 * x * (1.0 + lax.erf(x * (1.0 / math.sqrt(2.0))))


def _peer_expert_kernel(h_ref, u_ref, v_ref, s1_ref, e1_ref, s2_ref, e2_ref, thr_ref, o_ref, x_scr, a_scr, w_scr, b_scr):
    eb = pl.program_id(1)

    @pl.when(eb == 0)
    def _():
        x_scr[...] = h_ref[...].astype(_BF16)
        o_ref[...] = jnp.zeros_like(o_ref)

    a_scr[...] = lax.dot_general(u_ref[...], x_scr[...], _NT, preferred_element_type=_F32)
    keys_per_step = PEER_EXPERT_BLOCK // PEER_KEYS
    sub = SUBLANES
    tb = a_scr.shape[1]

    for h in range(PEER_HEADS):
        b_scr[2, h] = jnp.broadcast_to(thr_ref[h:h + 1, :], (sub, tb))
        for j in range(keys_per_step):
            i1 = eb * keys_per_step + j
            b_scr[0, j * PEER_HEADS + h] = jnp.broadcast_to(s1_ref[h, pl.ds(i1, 1), :], (sub, tb))
            b_scr[1, j * PEER_HEADS + h] = jnp.broadcast_to(e1_ref[h, pl.ds(i1, 1), :], (sub, tb))

    def row_tile(r, carry):
        r0 = pl.multiple_of(r * (2 * sub), 2 * sub)
        halves = (pl.ds(r0, sub), pl.ds(r0 + sub, sub))
        for j in range(keys_per_step):
            gates = [None, None]
            for h in range(PEER_HEADS):
                s1 = b_scr[0, j * PEER_HEADS + h]
                e1 = b_scr[1, j * PEER_HEADS + h]
                thr = b_scr[2, h]
                for p, rows in enumerate(halves):
                    term = jnp.where((s1 + s2_ref[h, rows, :]) >= thr, e1 * e2_ref[h, rows, :], 0.0)
                    gates[p] = term if gates[p] is None else gates[p] + term
            rows = pl.ds(pl.multiple_of(j * PEER_KEYS + r0, 2 * sub), 2 * sub)
            gate = jnp.concatenate(gates, axis=0)
            w_scr[rows, :] = (gate * _gelu_exact(a_scr[rows, :])).astype(_BF16)
        return carry

    lax.fori_loop(0, PEER_KEYS // (2 * sub), row_tile, 0)
    o_ref[...] += lax.dot_general(w_scr[...], v_ref[...], _TN, preferred_element_type=_F32)


def _peer_ffn(h, wq, subkeys, u_tab, v_tab):
    bsz, t, d = h.shape
    n = bsz * t
    tok = h.reshape(n, d)
    s1, e1, s2, e2, thr = _peer_route(tok, wq, subkeys)
    tb, eb = PEER_TOKEN_BLOCK, PEER_EXPERT_BLOCK
    n_experts = u_tab.shape[0]
    key_spec = pl.BlockSpec((PEER_HEADS, PEER_KEYS, tb), lambda i, e: (0, 0, i))
    tab_spec = pl.BlockSpec((eb, d), lambda i, e: (e, 0))
    out = pl.pallas_call(
        _peer_expert_kernel,
        grid=(n // tb, n_experts // eb),
        in_specs=[pl.BlockSpec((tb, d), lambda i, e: (i, 0)), tab_spec, tab_spec,
                  key_spec, key_spec, key_spec, key_spec, pl.BlockSpec((PEER_HEADS, tb), lambda i, e: (0, i))],
        out_specs=pl.BlockSpec((tb, d), lambda i, e: (i, 0)),
        out_shape=jax.ShapeDtypeStruct((n, d), _F32),
        scratch_shapes=[pltpu.VMEM((tb, d), _BF16), pltpu.VMEM((eb, tb), _F32), pltpu.VMEM((eb, tb), _BF16),
                        pltpu.VMEM((3, (eb // PEER_KEYS) * PEER_HEADS, SUBLANES, tb), _F32)],
        compiler_params=pltpu.CompilerParams(dimension_semantics=("parallel", "arbitrary"), vmem_limit_bytes=56 * 2**20),
        name="peer_experts",
    )(tok, u_tab.astype(_BF16), v_tab.astype(_BF16), s1, e1, s2, e2, thr)
    return out.reshape(bsz, t, d).astype(h.dtype)


def kernel(x, c, ctx, c_ctx, w_mod, b_mod, norm_pre1, norm_post1, norm_pre2, norm_post2, w_in, rw_mu, rw_w0, rw_w2, rw_a0, rw_a2, rw_g2, rw_k_k, rw_k_a, rw_r_k, rw_ln_w, rw_ln_b, ssm_conv_w, ssm_conv_b, ssm_dt_bias, ssm_a_log, ssm_d, ssm_norm_w, w_branch_a, w_branch_b, w_out, peer_wq, peer_subkeys, peer_u, peer_v):
    act_dtype = x.dtype
    bsz, seq, _ = x.shape
    rows = seq // GRID_W
    depth = w_mod.shape[0]
    assert depth == 1
    rw_zero = jnp.zeros((bsz, RW_WIDTH // LANES, LANES, LANES), _F32)
    ssm_zero = jnp.zeros((bsz, SSM_GROUPS, SSM_HPG, SSM_HEADDIM, SSM_STATE), _F32)
    l = 0
    mod_x = (jax.nn.silu(c) @ w_mod[l] + b_mod[l])[:, None, :]
    mod_c = (jax.nn.silu(c_ctx) @ w_mod[l] + b_mod[l])[None, None, :]
    sh1x, sc1x, g1x, sh2x, sc2x, g2x = jnp.split(mod_x, N_MOD, axis=-1)
    sh1c, sc1c, g1c, sh2c, sc2c, g2c = jnp.split(mod_c, N_MOD, axis=-1)

    in_w = _in_proj_weights(w_in[l])
    r_x, k_x, v_x, lora_x, z_x, xbc_x, dt_x, gate_x = _in_proj(_norm_mod(x, norm_pre1[l], sc1x, sh1x), in_w)
    r_c, k_c, v_c, lora_c, z_c, xbc_c, dt_c, gate_c = _in_proj(_norm_mod(ctx, norm_pre1[l], sc1c, sh1c), in_w)

    rw_params = (rw_w0[l], rw_w2[l], rw_a0[l], rw_a2[l], rw_g2[l], rw_k_k[l], rw_k_a[l], rw_r_k[l], rw_ln_w[l], rw_ln_b[l])
    _, s_fwd, s_bwd = _rwkv7_branch(r_c, k_c, v_c, lora_c, rw_mu[l], rw_zero, rw_zero, *rw_params, want_output=False)
    ya_x, _, _ = _rwkv7_branch(r_x, k_x, v_x, lora_x, rw_mu[l], s_fwd, s_bwd, *rw_params, want_output=True)

    ssm_params = (ssm_conv_w[l], ssm_conv_b[l], ssm_dt_bias[l], ssm_a_log[l], ssm_d[l])
    yb_c, h_fwd, h_bwd = _mamba2_branch(xbc_c, dt_c, ssm_zero, ssm_zero, *ssm_params)
    yb_x, _, _ = _mamba2_branch(_raster_to_column(xbc_x, rows), _raster_to_column(dt_x, rows), h_fwd, h_bwd, *ssm_params)
    yb_x = _column_to_raster(yb_x, rows)

    mix_x = _merge_branches(ya_x.astype(act_dtype), _gated_group_rmsnorm(yb_x, z_x, ssm_norm_w[l]).astype(act_dtype), gate_x, w_branch_a[l], w_branch_b[l], w_out[l])
    x = x + g1x * _rms_norm(mix_x, norm_post1[l])

    h2x = _rms_norm(x, norm_pre2[l]) * (1.0 + sc2x) + sh2x
    x = x + g2x * _rms_norm(_peer_ffn(h2x, peer_wq[l], peer_subkeys[l], peer_u[l], peer_v[l]), norm_post2[l])
    return x
```

```python
import functools
import math

import jax
import jax.numpy as jnp
from jax import lax
from jax.experimental import pallas as pl
from jax.experimental.pallas import tpu as pltpu

D_MODEL = 2048
GRID_W = 64
N_MOD = 6
NORM_EPS = 1e-6
RW_HEAD = 64
RW_WIDTH = D_MODEL
RW_HEADS = RW_WIDTH // RW_HEAD
LORA_W = 96
LORA_A = 96
LORA_G = 256
LN_X_EPS = 64e-5
SSM_WIDTH = D_MODEL
SSM_HEADDIM = 64
SSM_HEADS = SSM_WIDTH // SSM_HEADDIM
SSM_GROUPS = 8
SSM_HPG = SSM_HEADS // SSM_GROUPS
SSM_STATE = 128
SSM_CONV = 5
SSM_CHUNK = 128
PEER_HEADS = 8
PEER_KEYS = 128
PEER_TOPK = 16
PEER_QDIM = 256
PEER_HALF = PEER_QDIM // 2
PEER_BLOCK = 128
RW_COLS = 3 * RW_WIDTH + 2 * LORA_W + 2 * LORA_A + LORA_G
XBC_COLS = SSM_WIDTH + 2 * SSM_GROUPS * SSM_STATE
IN_SPLITS = (RW_COLS, RW_COLS + SSM_WIDTH, RW_COLS + SSM_WIDTH + XBC_COLS, RW_COLS + SSM_WIDTH + XBC_COLS + 2 * SSM_HEADS)
RW_SPLITS = (RW_WIDTH, 2 * RW_WIDTH, 3 * RW_WIDTH, 3 * RW_WIDTH + LORA_W, 3 * RW_WIDTH + 2 * LORA_W, 3 * RW_WIDTH + 2 * LORA_W + LORA_A, 3 * RW_WIDTH + 2 * LORA_W + 2 * LORA_A)

LANES = 128
WKV_CHUNK = 64
WKV_PAIR = LANES // RW_HEAD
WKV_GROUPS_PER_STEP = 8

_F32 = jnp.float32
_BF16 = jnp.bfloat16
_NT = (((1,), (1,)), ((), ()))
_TN = (((0,), (0,)), ((), ()))


def _mm(a, b, dims=None):
    a = a.astype(_BF16)
    b = b.astype(_BF16)
    if dims is None:
        return jnp.dot(a, b, preferred_element_type=_F32)
    return lax.dot_general(a, b, dims, preferred_element_type=_F32)


def _wkv_kernel(r_ref, lw_ref, k_ref, v_ref, a_ref, b_ref, s0_ref, y_ref, s_ref, *, reverse, chunk, groups):
    L = chunk
    L2 = WKV_PAIR * L

    @pl.when(pl.program_id(2) == 0)
    def _():
        s_ref[...] = s0_ref[...]

    ti = lax.broadcasted_iota(jnp.int32, (L, L), 0)
    tj = lax.broadcasted_iota(jnp.int32, (L, L), 1)
    before_incl = (ti <= tj) if reverse else (ti >= tj)
    cum = jnp.where(before_incl, 1.0, 0.0).astype(_BF16)

    lw = lw_ref[0]
    hi = lw.astype(_BF16)
    rem = lw - hi.astype(_F32)
    mid = rem.astype(_BF16)
    lo = (rem - mid.astype(_F32)).astype(_BF16)
    cs = (jnp.dot(cum, hi, preferred_element_type=_F32)
          + jnp.dot(cum, mid, preferred_element_type=_F32)
          + jnp.dot(cum, lo, preferred_element_type=_F32))
    tot = cs[0:1] if reverse else cs[L - 1:L]
    half = 0.5 * tot
    e_pos = jnp.exp(cs - half)
    e_neg = jnp.exp(half - cs)
    e_prev = jnp.exp(cs - lw - half)
    e_end = jnp.exp(tot - cs)
    e_half = jnp.exp(half)
    e_tot = jnp.exp(tot)

    rt = r_ref[0] * e_pos
    at = a_ref[0] * e_prev
    kt = k_ref[0] * e_neg
    bt = b_ref[0] * e_neg
    ke = k_ref[0] * e_end
    be = b_ref[0] * e_end
    vv = v_ref[0]

    lane = lax.broadcasted_iota(jnp.int32, (L, LANES), 1)
    first_head = lane < RW_HEAD

    def stack(x):
        return jnp.concatenate([jnp.where(first_head, x, 0.0), jnp.where(first_head, 0.0, x)], axis=0)

    si = lax.broadcasted_iota(jnp.int32, (L2, L2), 0)
    sj = lax.broadcasted_iota(jnp.int32, (L2, L2), 1)
    same_head = (si < L) == (sj < L)
    pi = jnp.where(si < L, si, si - L)
    pj = jnp.where(sj < L, sj, sj - L)
    incl2 = same_head & ((pi <= pj) if reverse else (pi >= pj))
    strict2 = same_head & ((pi < pj) if reverse else (pi > pj))
    eye2 = jnp.where(si == sj, 1.0, 0.0)

    gs = range(groups)
    sls = [slice(g * LANES, (g + 1) * LANES) for g in gs]
    v_s = [stack(vv[:, sl]).astype(_BF16) for sl in sls]
    lhs = [jnp.concatenate([stack(at[:, sl]), stack(rt[:, sl])], axis=0).astype(_BF16) for sl in sls]
    rhs = [jnp.concatenate([stack(kt[:, sl]), stack(bt[:, sl])], axis=0).astype(_BF16) for sl in sls]
    state = [s_ref[0, g] for g in gs]
    scores = [_mm(lhs[g], rhs[g], _NT) for g in gs]
    from_state = [_mm(lhs[g], state[g] * e_half[:, sls[g]], _NT) for g in gs]

    power = [jnp.where(strict2, -scores[g][:L2, L2:], 0.0).astype(_BF16) for g in gs]
    inv = [eye2 + power[g] for g in gs]
    ak_v = [_mm(jnp.where(strict2, scores[g][:L2, :L2], 0.0), v_s[g]) for g in gs]
    for _ in range(int(math.log2(L)) - 1):
        power = [_mm(power[g], power[g]).astype(_BF16) for g in gs]
        inv = [inv[g] + _mm(inv[g], power[g]) for g in gs]
    u_s = [_mm(inv[g], from_state[g][:L2] + ak_v[g]) for g in gs]

    vu = [jnp.concatenate([v_s[g], u_s[g].astype(_BF16)], axis=0) for g in gs]
    for g in gs:
        r_kb = jnp.concatenate([jnp.where(incl2, scores[g][L2:, :L2], 0.0),
                                jnp.where(incl2, -scores[g][L2:, L2:], 0.0)], axis=1)
        y_s = from_state[g][L2:] + _mm(r_kb, vu[g])
        y_ref[0, :, sls[g]] = y_s[:L] + y_s[L:]
    for g in gs:
        kb = jnp.concatenate([stack(ke[:, sls[g]]), -stack(be[:, sls[g]])], axis=0)
        s_ref[0, g] = state[g] * e_tot[:, sls[g]] + _mm(vu[g], kb, _TN)


def _wkv_scan(r, lw, k, v, a, b, s0, reverse):
    bsz, t, width = r.shape
    groups = WKV_GROUPS_PER_STEP
    assert t % WKV_CHUNK == 0 and width % (groups * LANES) == 0
    nc = t // WKV_CHUNK
    ngroup_steps = width // (groups * LANES)
    cidx = (lambda c: nc - 1 - c) if reverse else (lambda c: c)
    seq_spec = pl.BlockSpec((1, WKV_CHUNK, groups * LANES), lambda bi, gi, c: (bi, cidx(c), gi))
    st_spec = pl.BlockSpec((1, groups, LANES, LANES), lambda bi, gi, c: (bi, gi, 0, 0))
    return pl.pallas_call(
        functools.partial(_wkv_kernel, reverse=reverse, chunk=WKV_CHUNK, groups=groups),
        grid=(bsz, ngroup_steps, nc),
        in_specs=[seq_spec] * 6 + [st_spec],
        out_specs=[seq_spec, st_spec],
        out_shape=[jax.ShapeDtypeStruct((bsz, t, width), _F32), jax.ShapeDtypeStruct(s0.shape, _F32)],
        compiler_params=pltpu.CompilerParams(dimension_semantics=("parallel", "parallel", "arbitrary")),
        name="wkv7_rev" if reverse else "wkv7_fwd",
    )(r, lw, k, v, a, b, s0)


MM_TILE_M = 1024
MM_TILE_N = 1024
NORM_TILE_M = 512
SMALL_COLS = 768


def _matmul_kernel(a_ref, w_ref, o_ref):
    o_ref[...] = jnp.dot(a_ref[...], w_ref[...], preferred_element_type=_F32)


def _matmul(a, w):
    m, k = a.shape
    n = w.shape[1]
    tm, tn = min(m, MM_TILE_M), min(n, MM_TILE_N)
    assert m % tm == 0 and n % tn == 0 and a.dtype == _BF16 and w.dtype == _BF16
    return pl.pallas_call(
        _matmul_kernel,
        grid=(n // tn, m // tm),
        in_specs=[pl.BlockSpec((tm, k), lambda j, i: (i, 0)), pl.BlockSpec((k, tn), lambda j, i: (0, j))],
        out_specs=pl.BlockSpec((tm, tn), lambda j, i: (i, j)),
        out_shape=jax.ShapeDtypeStruct((m, n), _F32),
        compiler_params=pltpu.CompilerParams(dimension_semantics=("parallel", "parallel"), vmem_limit_bytes=48 * 2**20),
        name="matmul",
    )(a, w)


def _norm_mod_kernel(x_ref, gain_ref, sc_ref, sh_ref, o_ref):
    x = x_ref[0]
    inv = lax.rsqrt(jnp.mean(x * x, axis=-1, keepdims=True) + NORM_EPS)
    o_ref[0] = ((x * inv * gain_ref[...]) * (1.0 + sc_ref[0]) + sh_ref[0]).astype(o_ref.dtype)


def _norm_mod(x, gain, scale, shift):
    bsz, t, d = x.shape
    tm = min(t, NORM_TILE_M)
    mod_spec = pl.BlockSpec((1, 1, d), (lambda b, i: (b, 0, 0)) if scale.shape[0] == bsz else (lambda b, i: (0, 0, 0)))
    return pl.pallas_call(
        _norm_mod_kernel,
        grid=(bsz, t // tm),
        in_specs=[pl.BlockSpec((1, tm, d), lambda b, i: (b, i, 0)), pl.BlockSpec((1, d), lambda b, i: (0, 0)), mod_spec, mod_spec],
        out_specs=pl.BlockSpec((1, tm, d), lambda b, i: (b, i, 0)),
        out_shape=jax.ShapeDtypeStruct((bsz, t, d), _BF16),
        compiler_params=pltpu.CompilerParams(dimension_semantics=("parallel", "parallel")),
        name="norm_mod",
    )(x, gain.reshape(1, d), scale, shift)


def _in_proj_weights(w):
    rw_end, z_end, xbc_end, dt_end = IN_SPLITS
    lora_dt = jnp.concatenate([w[:, RW_SPLITS[2]:rw_end], w[:, xbc_end:dt_end]], axis=1)
    lora_dt = jnp.pad(lora_dt, ((0, 0), (0, SMALL_COLS - lora_dt.shape[1])))
    parts = [w[:, :RW_SPLITS[0]], w[:, RW_SPLITS[0]:RW_SPLITS[1]], w[:, RW_SPLITS[1]:RW_SPLITS[2]],
             w[:, rw_end:z_end], w[:, z_end:xbc_end], w[:, dt_end:], lora_dt]
    return [p.astype(_BF16) for p in parts]


def _in_proj(h, weights):
    bsz, t, d = h.shape
    r, k, v, z, xbc, gate, lora_dt = [_matmul(h.reshape(bsz * t, d), w).reshape(bsz, t, -1) for w in weights]
    n_lora = RW_COLS - RW_SPLITS[2]
    return r, k, v, lora_dt, z, xbc, lora_dt[..., n_lora:n_lora + 2 * SSM_HEADS], gate


def _rms_norm(u, gain):
    uf = u.astype(_F32)
    uf = uf * lax.rsqrt(jnp.mean(uf * uf, axis=-1, keepdims=True) + NORM_EPS)
    return (uf * gain.astype(_F32)).astype(u.dtype)


def _centred_token_shift(p, mu):
    pad = jnp.pad(p, ((0, 0), (1, 1), (0, 0)))
    return p + mu * (0.5 * (pad[:, :-2] + pad[:, 2:]) - p)


def _raster_to_column(u, rows):
    b, s, ch = u.shape
    return u.reshape(b, rows, GRID_W, ch).transpose(0, 2, 1, 3).reshape(b, s, ch)


def _column_to_raster(u, rows):
    b, s, ch = u.shape
    return u.reshape(b, GRID_W, rows, ch).transpose(0, 2, 1, 3).reshape(b, s, ch)


RW_TILE_T = 128
LORA_COLS = RW_COLS - 3 * RW_WIDTH


def _head_matrices():
    head_of_lane = jnp.arange(RW_WIDTH) // RW_HEAD
    onehot = (head_of_lane[:, None] == jnp.arange(RW_HEADS)[None, :]).astype(_BF16)
    return onehot, onehot.T


def _head_sum(x, hsum):
    return sum(jnp.dot(part, hsum, preferred_element_type=_F32) for part in _split3(x))


def _head_widen(cols, hwide):
    return sum(jnp.dot(part, hwide, preferred_element_type=_F32) for part in _split3(cols))


def _token_shift_block(c_ref, p_ref, n_ref, mu, first, last):
    x = c_ref[0]
    tm = x.shape[0]
    row = lax.broadcasted_iota(jnp.int32, x.shape, 0)
    prev_row = jnp.where(first, 0.0, p_ref[0, SUBLANES - 1:SUBLANES, :])
    next_row = jnp.where(last, 0.0, n_ref[0, 0:1, :])
    prev = jnp.where(row == 0, prev_row, pltpu.roll(x, 1, 0))
    nxt = jnp.where(row == tm - 1, next_row, pltpu.roll(x, tm - 1, 0))
    return x + mu * (0.5 * (prev + nxt) - x)


def _rw_pre_kernel(rc, rp, rn, kc, kp, kn, vc, vp, vn, lc, lp, ln,
                   mu_r, mu_k, mu_v, mu_l, w0, a0, k_k, k_a, r_k, ln_b, w2, a2, g2, hsum, hwide,
                   r_o, v_o, kk_o, lwf_o, lwb_o, kf_o, kb_o, kkaf_o, kkab_o, gate_o, bonus_o):
    first = pl.program_id(1) == 0
    last = pl.program_id(1) == pl.num_programs(1) - 1
    r = _token_shift_block(rc, rp, rn, mu_r[...], first, last)
    k = _token_shift_block(kc, kp, kn, mu_k[...], first, last)
    v = _token_shift_block(vc, vp, vn, mu_v[...], first, last)
    lora = _token_shift_block(lc, lp, ln, mu_l[...], first, last)
    r_o[0] = r
    v_o[0] = v

    kk = k * k_k[...]
    norm = jnp.maximum(jnp.sqrt(_head_sum(kk * kk, hsum[...])), 1e-12)
    kk = kk * _head_widen(1.0 / norm, hwide[...])
    kk_o[0] = kk

    k_sum = None
    for d, (lw_o, kd_o, kka_o) in enumerate(((lwf_o, kf_o, kkaf_o), (lwb_o, kb_o, kkab_o))):
        wd = lora[:, d * LORA_W:(d + 1) * LORA_W]
        ad = lora[:, 2 * LORA_W + d * LORA_A:2 * LORA_W + (d + 1) * LORA_A]
        w_log = -jax.nn.softplus(-(w0[d:d + 1, :] + _mm(jnp.tanh(wd), w2[d]))) - 0.5
        lw_o[0] = -jnp.exp(w_log)
        a = jax.nn.sigmoid(a0[d:d + 1, :] + _mm(ad, a2[d]))
        k_dir = k * (1.0 + (a - 1.0) * k_a[...])
        kd_o[0] = k_dir
        kka_o[0] = kk * a
        k_sum = k_dir if k_sum is None else k_sum + k_dir

    gd = lora[:, 2 * LORA_W + 2 * LORA_A:2 * LORA_W + 2 * LORA_A + LORA_G]
    gate = _mm(jax.nn.sigmoid(gd), g2[...])
    bonus = _head_widen(_head_sum(r * k_sum * r_k[...], hsum[...]), hwide[...]) * v
    gate_o[0] = gate
    bonus_o[0] = (ln_b[...] + bonus) * gate


def _rw_pre(r, k, v, lora_dt, mu, w0, w2, a0, a2, g2, k_k, k_a, r_k, ln_b):
    bsz, t, width = r.shape
    tm = RW_TILE_T
    nt = t // tm
    blocks_per_tile = tm // SUBLANES
    last_block = t // SUBLANES - 1
    lcols = lora_dt.shape[-1]

    def specs(cols):
        return [pl.BlockSpec((1, tm, cols), lambda b, i: (b, i, 0)),
                pl.BlockSpec((1, SUBLANES, cols), lambda b, i: (b, jnp.maximum(i * blocks_per_tile - 1, 0), 0)),
                pl.BlockSpec((1, SUBLANES, cols), lambda b, i: (b, jnp.minimum((i + 1) * blocks_per_tile, last_block), 0))]

    def whole(a):
        return pl.BlockSpec(a.shape, lambda b, i: (0,) * a.ndim)

    row = lambda a: a.reshape(1, -1).astype(_F32)
    mu_l = jnp.pad(mu[RW_SPLITS[2]:], (0, lcols - LORA_COLS))
    hsum, hwide = _head_matrices()
    params = [row(mu[:RW_SPLITS[0]]), row(mu[RW_SPLITS[0]:RW_SPLITS[1]]), row(mu[RW_SPLITS[1]:RW_SPLITS[2]]), row(mu_l),
              w0.astype(_F32), a0.astype(_F32), row(k_k), row(k_a), row(r_k), row(ln_b),
              w2.astype(_BF16), a2.astype(_BF16), g2.astype(_BF16), hsum, hwide]
    out_spec = pl.BlockSpec((1, tm, width), lambda b, i: (b, i, 0))
    out_shape = jax.ShapeDtypeStruct((bsz, t, width), _F32)
    return pl.pallas_call(
        _rw_pre_kernel,
        grid=(bsz, nt),
        in_specs=specs(width) * 3 + specs(lcols) + [whole(p) for p in params],
        out_specs=[out_spec] * 11,
        out_shape=[out_shape] * 11,
        compiler_params=pltpu.CompilerParams(dimension_semantics=("parallel", "parallel"), vmem_limit_bytes=56 * 2**20),
        name="rw_pre",
    )(r, r, r, k, k, k, v, v, v, lora_dt, lora_dt, lora_dt, *params)


def _rw_post_kernel(yf, yb, gate, bonus, ln_w, hsum, hwide, o_ref):
    y = yf[0] + yb[0]
    mean = _head_widen(_head_sum(y, hsum[...]), hwide[...]) * (1.0 / RW_HEAD)
    cen = y - mean
    var = _head_widen(_head_sum(cen * cen, hsum[...]), hwide[...]) * (1.0 / RW_HEAD)
    o_ref[0] = (cen * lax.rsqrt(var + LN_X_EPS) * ln_w[...] * gate[0] + bonus[0]).astype(o_ref.dtype)


def _rw_post(y_f, y_b, gate, bonus, ln_w):
    bsz, t, width = y_f.shape
    tm = RW_TILE_T
    hsum, hwide = _head_matrices()
    spec = pl.BlockSpec((1, tm, width), lambda b, i: (b, i, 0))
    whole = lambda a: pl.BlockSpec(a.shape, lambda b, i: (0,) * a.ndim)
    ln_w = ln_w.reshape(1, width).astype(_F32)
    return pl.pallas_call(
        _rw_post_kernel,
        grid=(bsz, t // tm),
        in_specs=[spec] * 4 + [whole(ln_w), whole(hsum), whole(hwide)],
        out_specs=spec,
        out_shape=jax.ShapeDtypeStruct((bsz, t, width), _BF16),
        compiler_params=pltpu.CompilerParams(dimension_semantics=("parallel", "parallel")),
        name="rw_post",
    )(y_f, y_b, gate, bonus, ln_w, hsum, hwide)


def _rwkv7_branch(r, k, v, lora_dt, mu, s_fwd, s_bwd, w0, w2, a0, a2, g2, k_k, k_a, r_k, ln_w, ln_b, want_output):
    r, v, kk, lw_f, lw_b, k_f, k_b, kka_f, kka_b, gate, bonus = _rw_pre(
        r, k, v, lora_dt, mu, w0, w2, a0, a2, g2, k_k, k_a, r_k, ln_b)
    y_f, s_fwd = _wkv_scan(r, lw_f, k_f, v, kk, kka_f, s_fwd, reverse=False)
    y_b, s_bwd = _wkv_scan(r, lw_b, k_b, v, kk, kka_b, s_bwd, reverse=True)
    out = _rw_post(y_f, y_b, gate, bonus, ln_w) if want_output else None
    return out, s_fwd, s_bwd


def _centred_depthwise_conv(u, w):
    ch = u.shape[-1]
    return lax.conv_general_dilated(u, w.astype(_F32)[:, None, :], window_strides=(1,), padding=[(SSM_CONV // 2, SSM_CONV // 2)], dimension_numbers=('NWC', 'WIO', 'NWC'), feature_group_count=ch)


def _split3(x):
    hi = x.astype(_BF16)
    rem = x - hi.astype(_F32)
    mid = rem.astype(_BF16)
    lo = (rem - mid.astype(_F32)).astype(_BF16)
    return hi, mid, lo


def _ssd_kernel(xs_ref, bm_ref, cm_ref, la_ref, lat_ref, dt_ref, h0_ref, y_ref, h_ref, *, reverse):
    L = SSM_CHUNK
    P = SSM_HEADDIM

    @pl.when(pl.program_id(1) == 0)
    def _():
        h_ref[...] = h0_ref[...]

    ti = lax.broadcasted_iota(jnp.int32, (L, L), 0)
    tj = lax.broadcasted_iota(jnp.int32, (L, L), 1)
    before_incl = (ti <= tj) if reverse else (ti >= tj)
    cum = jnp.where(before_incl, 1.0, 0.0).astype(_BF16)
    cum_t = jnp.where(before_incl, 0.0, 1.0).astype(_BF16) + jnp.where(ti == tj, 1.0, 0.0).astype(_BF16)

    la = la_ref[0]
    cs = sum(jnp.dot(cum, part, preferred_element_type=_F32) for part in _split3(la))
    cs_t = sum(jnp.dot(part, cum_t, preferred_element_type=_F32) for part in _split3(lat_ref[0]))
    tot = cs[0:1] if reverse else cs[L - 1:L]
    dt = dt_ref[0]

    hi = lax.broadcasted_iota(jnp.int32, (SSM_HEADS, SSM_WIDTH), 0)
    hj = lax.broadcasted_iota(jnp.int32, (SSM_HEADS, SSM_WIDTH), 1)
    lo_edge = hi * P
    widen = jnp.where((hj >= lo_edge) & (hj < lo_edge + P), 1.0, 0.0).astype(_BF16)

    def wide(cols):
        return sum(jnp.dot(part, widen, preferred_element_type=_F32) for part in _split3(cols))

    xs = xs_ref[0]
    xdt = xs * wide(dt)
    xdt_end = (xs * wide(dt * jnp.exp(tot - cs))).astype(_BF16)
    xdt = xdt.astype(_BF16)
    decay_in = wide(jnp.exp(cs))
    e_tot = jnp.exp(tot)

    pieces = []
    for g in range(SSM_GROUPS):
        bm = bm_ref[0, :, g * SSM_STATE:(g + 1) * SSM_STATE].astype(_BF16)
        cm = cm_ref[0, :, g * SSM_STATE:(g + 1) * SSM_STATE].astype(_BF16)
        cb = lax.dot_general(cm, bm, _NT, preferred_element_type=_F32)
        for e in range(SSM_HPG):
            h = g * SSM_HPG + e
            cols = slice(h * P, (h + 1) * P)
            seg = jnp.where(before_incl, jnp.exp(cs[:, h:h + 1] - cs_t[h:h + 1, :]), 0.0)
            state = h_ref[0, h]
            y_h = jnp.dot((cb * seg).astype(_BF16), xdt[:, cols], preferred_element_type=_F32)
            y_h = y_h + lax.dot_general(cm, state.astype(_BF16), _NT, preferred_element_type=_F32) * decay_in[:, cols]
            pieces.append(y_h)
            new = lax.dot_general(xdt_end[:, cols], bm, _TN, preferred_element_type=_F32)
            h_ref[0, h] = state * e_tot[:, h:h + 1] + new
    y_ref[0] = jnp.concatenate(pieces, axis=1)


def _ssd_scan(xs, bm, cm, log_a, dt, h0, reverse):
    bsz, t, width = xs.shape
    nc = t // SSM_CHUNK
    cidx = (lambda c: nc - 1 - c) if reverse else (lambda c: c)
    seq = lambda w: pl.BlockSpec((1, SSM_CHUNK, w), lambda b, c: (b, cidx(c), 0))
    st_spec = pl.BlockSpec((1,) + h0.shape[1:], lambda b, c: (b, 0, 0, 0))
    gn = SSM_GROUPS * SSM_STATE
    return pl.pallas_call(
        functools.partial(_ssd_kernel, reverse=reverse),
        grid=(bsz, nc),
        in_specs=[seq(width), seq(gn), seq(gn), seq(SSM_HEADS),
                  pl.BlockSpec((1, SSM_HEADS, SSM_CHUNK), lambda b, c: (b, 0, cidx(c))), seq(SSM_HEADS), st_spec],
        out_specs=[seq(width), st_spec],
        out_shape=[jax.ShapeDtypeStruct((bsz, t, width), _F32), jax.ShapeDtypeStruct(h0.shape, _F32)],
        compiler_params=pltpu.CompilerParams(dimension_semantics=("parallel", "arbitrary")),
        name="ssd_rev" if reverse else "ssd_fwd",
    )(xs, bm, cm, log_a, jnp.swapaxes(log_a, 1, 2), dt, h0)


def _mamba2_branch(xbc, dt_raw, h0_fwd, h0_bwd, conv_w, conv_b, dt_bias, a_log, d_skip):
    bsz, t, _ = xbc.shape
    xbc = jax.nn.silu(_centred_depthwise_conv(xbc.astype(_F32), conv_w) + conv_b)
    xs, bm, cm = jnp.split(xbc, (SSM_WIDTH, SSM_WIDTH + SSM_GROUPS * SSM_STATE), axis=-1)
    dt = jax.nn.softplus(dt_raw.astype(_F32).reshape(bsz, t, 2, SSM_HEADS) + dt_bias)
    log_a = -jnp.exp(a_log) * dt
    heads = lambda h: h.reshape(bsz, SSM_HEADS, SSM_HEADDIM, SSM_STATE)
    y_f, h_f = _ssd_scan(xs, bm, cm, log_a[:, :, 0], dt[:, :, 0], heads(h0_fwd), reverse=False)
    y_b, h_b = _ssd_scan(xs, bm, cm, log_a[:, :, 1], dt[:, :, 1], heads(h0_bwd), reverse=True)
    y = y_f + y_b + jnp.repeat(d_skip, SSM_HEADDIM) * xs
    return y, h_f, h_b


def _gated_group_rmsnorm(y, z, gain):
    bsz, t, _ = y.shape
    u = (y * jax.nn.silu(z.astype(_F32))).reshape(bsz, t, SSM_GROUPS, SSM_WIDTH // SSM_GROUPS)
    u = u * lax.rsqrt(jnp.mean(u * u, axis=-1, keepdims=True) + NORM_EPS)
    return u.reshape(bsz, t, SSM_WIDTH) * gain


def _merge_branches(y_a, y_b, gates, w_a, w_b, w_o):
    bsz, t, d = y_a.shape
    mm = lambda u, w: _matmul(u.reshape(bsz * t, -1).astype(_BF16), w.astype(_BF16)).reshape(bsz, t, -1)
    g_a, g_b = jnp.split(gates, 2, axis=-1)
    m = jax.nn.sigmoid(g_a) * mm(y_a, w_a) + jax.nn.sigmoid(g_b) * mm(y_b, w_b)
    return mm(m, w_o)


PEER_ROUTE_TOKENS = 256
PEER_TOKEN_BLOCK = 512
PEER_EXPERT_BLOCK = 512
SUBLANES = 8


def _top_values(x, count):
    rows = lax.broadcasted_iota(jnp.int32, x.shape, 0)
    vals = []
    for _ in range(count):
        m = jnp.max(x, axis=0, keepdims=True)
        vals.append(m)
        first = jnp.min(jnp.where(x == m, rows, x.shape[0]), axis=0, keepdims=True)
        x = jnp.where(rows == first, -jnp.inf, x)
    return vals


def _peer_route_kernel(h_ref, wqt_ref, sub_ref, s1_ref, e1_ref, s2_ref, e2_ref, thr_ref):
    qt = lax.dot_general(wqt_ref[...], h_ref[...].astype(_BF16), _NT, preferred_element_type=_F32)
    for h in range(PEER_HEADS):
        sc = []
        for s in range(2):
            lo = (2 * h + s) * PEER_HALF
            sc.append(jnp.dot(sub_ref[2 * h + s], qt[lo:lo + PEER_HALF].astype(_BF16), preferred_element_type=_F32))
        top_a = _top_values(sc[0], PEER_TOPK)
        top_b = jnp.concatenate(_top_values(sc[1], PEER_TOPK), axis=0)
        cand = [top_a[i] + top_b[:PEER_TOPK // (i + 1)] for i in range(PEER_TOPK)]
        cand = jnp.concatenate(cand, axis=0)
        best = _top_values(cand, PEER_TOPK)
        norm = best[0] * 0.0
        for val in best:
            norm = norm + jnp.exp(val - best[0])
        s1_ref[h] = sc[0]
        s2_ref[h] = sc[1]
        e1_ref[h] = jnp.exp(sc[0] - top_a[0]) / norm
        e2_ref[h] = jnp.exp(sc[1] - top_b[0:1])
        thr_ref[h:h + 1, :] = best[PEER_TOPK - 1]


def _peer_route(tok, wq, subkeys):
    n, d = tok.shape
    tr = PEER_ROUTE_TOKENS
    wqt = wq.T.astype(_BF16)
    sub = subkeys.reshape(PEER_HEADS * 2, PEER_KEYS, PEER_HALF).astype(_BF16)
    key_spec = pl.BlockSpec((PEER_HEADS, PEER_KEYS, tr), lambda i: (0, 0, i))
    key_shape = jax.ShapeDtypeStruct((PEER_HEADS, PEER_KEYS, n), _F32)
    return pl.pallas_call(
        _peer_route_kernel,
        grid=(n // tr,),
        in_specs=[pl.BlockSpec((tr, d), lambda i: (i, 0)),
                  pl.BlockSpec(wqt.shape, lambda i: (0, 0)),
                  pl.BlockSpec(sub.shape, lambda i: (0, 0, 0))],
        out_specs=[key_spec, key_spec, key_spec, key_spec, pl.BlockSpec((PEER_HEADS, tr), lambda i: (0, i))],
        out_shape=[key_shape, key_shape, key_shape, key_shape, jax.ShapeDtypeStruct((PEER_HEADS, n), _F32)],
        compiler_params=pltpu.CompilerParams(dimension_semantics=("parallel",), vmem_limit_bytes=48 * 2**20),
        name="peer_route",
    )(tok, wqt, sub)


def _gelu_exact(x):
    return 0.5 * x * (1.0 + lax.erf(x * (1.0 / math.sqrt(2.0))))


def _peer_gate_block(a_ref, w_ref, b_ref, s1_ref, e1_ref, s2_ref, e2_ref, thr_ref, block):
    keys_per_block = PEER_EXPERT_BLOCK // PEER_KEYS
    sub = SUBLANES
    tb = a_ref.shape[1]
    for h in range(PEER_HEADS):
        b_ref[2, h] = jnp.broadcast_to(thr_ref[h:h + 1, :], (sub, tb))
        for j in range(keys_per_block):
            i1 = block * keys_per_block + j
            b_ref[0, j * PEER_HEADS + h] = jnp.broadcast_to(s1_ref[h, pl.ds(i1, 1), :], (sub, tb))
            b_ref[1, j * PEER_HEADS + h] = jnp.broadcast_to(e1_ref[h, pl.ds(i1, 1), :], (sub, tb))

    def row_tile(tile, carry):
        r0 = pl.multiple_of(tile * (2 * sub), 2 * sub)
        halves = (pl.ds(r0, sub), pl.ds(r0 + sub, sub))
        for lanes in (slice(0, tb // 2), slice(tb // 2, tb)):
            gates = [[None, None] for _ in range(keys_per_block)]
            for h in range(PEER_HEADS):
                thr = b_ref[2, h, :, lanes]
                s2 = [s2_ref[h, rows, lanes] for rows in halves]
                e2 = [e2_ref[h, rows, lanes] for rows in halves]
                for j in range(keys_per_block):
                    s1 = b_ref[0, j * PEER_HEADS + h, :, lanes]
                    e1 = b_ref[1, j * PEER_HEADS + h, :, lanes]
                    for p in range(2):
                        term = jnp.where((s1 + s2[p]) >= thr, e1 * e2[p], 0.0)
                        gates[j][p] = term if gates[j][p] is None else gates[j][p] + term
            for j in range(keys_per_block):
                rows = pl.ds(pl.multiple_of(j * PEER_KEYS + r0, 2 * sub), 2 * sub)
                w_ref[rows, lanes] = (jnp.concatenate(gates[j], axis=0) * _gelu_exact(a_ref[rows, lanes])).astype(_BF16)
        return carry

    lax.fori_loop(0, PEER_KEYS // (2 * sub), row_tile, 0)


def _peer_expert_kernel(x_ref, u_ref, v_ref, s1_ref, e1_ref, s2_ref, e2_ref, thr_ref, o_ref, a_scr, w_scr, b_scr):
    eb = pl.program_id(1)

    @pl.when(eb == 0)
    def _():
        o_ref[...] = jnp.zeros_like(o_ref)

    a_scr[...] = lax.dot_general(u_ref[...], x_ref[...], _NT, preferred_element_type=_F32)
    _peer_gate_block(a_scr, w_scr, b_scr, s1_ref, e1_ref, s2_ref, e2_ref, thr_ref, eb)
    o_ref[...] += lax.dot_general(w_scr[...], v_ref[...], _TN, preferred_element_type=_F32)


def _peer_ffn(h, wq, subkeys, u_tab, v_tab):
    bsz, t, d = h.shape
    n = bsz * t
    tok = h.reshape(n, d)
    s1, e1, s2, e2, thr = _peer_route(tok, wq, subkeys)
    tb, eb = PEER_TOKEN_BLOCK, PEER_EXPERT_BLOCK
    n_experts = u_tab.shape[0]
    key_spec = pl.BlockSpec((PEER_HEADS, PEER_KEYS, tb), lambda i, e: (0, 0, i))
    tab_spec = pl.BlockSpec((eb, d), lambda i, e: (e, 0))
    out = pl.pallas_call(
        _peer_expert_kernel,
        grid=(n // tb, n_experts // eb),
        in_specs=[pl.BlockSpec((tb, d), lambda i, e: (i, 0)), tab_spec, tab_spec,
                  key_spec, key_spec, key_spec, key_spec, pl.BlockSpec((PEER_HEADS, tb), lambda i, e: (0, i))],
        out_specs=pl.BlockSpec((tb, d), lambda i, e: (i, 0)),
        out_shape=jax.ShapeDtypeStruct((n, d), _F32),
        scratch_shapes=[pltpu.VMEM((eb, tb), _F32), pltpu.VMEM((eb, tb), _BF16),
                        pltpu.VMEM((3, (eb // PEER_KEYS) * PEER_HEADS, SUBLANES, tb), _F32)],
        compiler_params=pltpu.CompilerParams(dimension_semantics=("parallel", "arbitrary"), vmem_limit_bytes=56 * 2**20),
        name="peer_experts",
    )(tok, u_tab.astype(_BF16), v_tab.astype(_BF16), s1, e1, s2, e2, thr)
    return out.reshape(bsz, t, d)


def kernel(x, c, ctx, c_ctx, w_mod, b_mod, norm_pre1, norm_post1, norm_pre2, norm_post2, w_in, rw_mu, rw_w0, rw_w2, rw_a0, rw_a2, rw_g2, rw_k_k, rw_k_a, rw_r_k, rw_ln_w, rw_ln_b, ssm_conv_w, ssm_conv_b, ssm_dt_bias, ssm_a_log, ssm_d, ssm_norm_w, w_branch_a, w_branch_b, w_out, peer_wq, peer_subkeys, peer_u, peer_v):
    act_dtype = x.dtype
    bsz, seq, _ = x.shape
    rows = seq // GRID_W
    depth = w_mod.shape[0]
    assert depth == 1
    rw_zero = jnp.zeros((bsz, RW_WIDTH // LANES, LANES, LANES), _F32)
    ssm_zero = jnp.zeros((bsz, SSM_GROUPS, SSM_HPG, SSM_HEADDIM, SSM_STATE), _F32)
    l = 0
    mod_x = (jax.nn.silu(c) @ w_mod[l] + b_mod[l])[:, None, :]
    mod_c = (jax.nn.silu(c_ctx) @ w_mod[l] + b_mod[l])[None, None, :]
    sh1x, sc1x, g1x, sh2x, sc2x, g2x = jnp.split(mod_x, N_MOD, axis=-1)
    sh1c, sc1c, g1c, sh2c, sc2c, g2c = jnp.split(mod_c, N_MOD, axis=-1)

    in_w = _in_proj_weights(w_in[l])
    r_x, k_x, v_x, lora_x, z_x, xbc_x, dt_x, gate_x = _in_proj(_norm_mod(x, norm_pre1[l], sc1x, sh1x), in_w)
    r_c, k_c, v_c, lora_c, z_c, xbc_c, dt_c, gate_c = _in_proj(_norm_mod(ctx, norm_pre1[l], sc1c, sh1c), in_w)

    rw_params = (rw_w0[l], rw_w2[l], rw_a0[l], rw_a2[l], rw_g2[l], rw_k_k[l], rw_k_a[l], rw_r_k[l], rw_ln_w[l], rw_ln_b[l])
    _, s_fwd, s_bwd = _rwkv7_branch(r_c, k_c, v_c, lora_c, rw_mu[l], rw_zero, rw_zero, *rw_params, want_output=False)
    ya_x, _, _ = _rwkv7_branch(r_x, k_x, v_x, lora_x, rw_mu[l], s_fwd, s_bwd, *rw_params, want_output=True)

    ssm_params = (ssm_conv_w[l], ssm_conv_b[l], ssm_dt_bias[l], ssm_a_log[l], ssm_d[l])
    yb_c, h_fwd, h_bwd = _mamba2_branch(xbc_c, dt_c, ssm_zero, ssm_zero, *ssm_params)
    yb_x, _, _ = _mamba2_branch(_raster_to_column(xbc_x, rows), _raster_to_column(dt_x, rows), h_fwd, h_bwd, *ssm_params)
    yb_x = _column_to_raster(yb_x, rows)

    mix_x = _merge_branches(ya_x.astype(act_dtype), _gated_group_rmsnorm(yb_x, z_x, ssm_norm_w[l]).astype(act_dtype), gate_x, w_branch_a[l], w_branch_b[l], w_out[l])
    x = x + g1x * _rms_norm(mix_x, norm_post1[l])

    h2x = _norm_mod(x, norm_pre2[l], sc2x, sh2x)
    x = x + g2x * _rms_norm(_peer_ffn(h2x, peer_wq[l], peer_subkeys[l], peer_u[l], peer_v[l]), norm_post2[l])
    return x
```

```python
import functools
import math

import jax
import jax.numpy as jnp
from jax import lax
from jax.experimental import pallas as pl
from jax.experimental.pallas import tpu as pltpu

D_MODEL = 2048
GRID_W = 64
N_MOD = 6
NORM_EPS = 1e-6
RW_HEAD = 64
RW_WIDTH = D_MODEL
RW_HEADS = RW_WIDTH // RW_HEAD
LORA_W = 96
LORA_A = 96
LORA_G = 256
LN_X_EPS = 64e-5
SSM_WIDTH = D_MODEL
SSM_HEADDIM = 64
SSM_HEADS = SSM_WIDTH // SSM_HEADDIM
SSM_GROUPS = 8
SSM_HPG = SSM_HEADS // SSM_GROUPS
SSM_STATE = 128
SSM_CONV = 5
SSM_CHUNK = 128
PEER_HEADS = 8
PEER_KEYS = 128
PEER_TOPK = 16
PEER_QDIM = 256
PEER_HALF = PEER_QDIM // 2
PEER_BLOCK = 128
RW_COLS = 3 * RW_WIDTH + 2 * LORA_W + 2 * LORA_A + LORA_G
XBC_COLS = SSM_WIDTH + 2 * SSM_GROUPS * SSM_STATE
IN_SPLITS = (RW_COLS, RW_COLS + SSM_WIDTH, RW_COLS + SSM_WIDTH + XBC_COLS, RW_COLS + SSM_WIDTH + XBC_COLS + 2 * SSM_HEADS)
RW_SPLITS = (RW_WIDTH, 2 * RW_WIDTH, 3 * RW_WIDTH, 3 * RW_WIDTH + LORA_W, 3 * RW_WIDTH + 2 * LORA_W, 3 * RW_WIDTH + 2 * LORA_W + LORA_A, 3 * RW_WIDTH + 2 * LORA_W + 2 * LORA_A)

LANES = 128
WKV_CHUNK = 64
WKV_PAIR = LANES // RW_HEAD
WKV_GROUPS_PER_STEP = 8

_F32 = jnp.float32
_BF16 = jnp.bfloat16
_NT = (((1,), (1,)), ((), ()))
_TN = (((0,), (0,)), ((), ()))


def _mm(a, b, dims=None):
    a = a.astype(_BF16)
    b = b.astype(_BF16)
    if dims is None:
        return jnp.dot(a, b, preferred_element_type=_F32)
    return lax.dot_general(a, b, dims, preferred_element_type=_F32)


def _wkv_kernel(r_ref, lw_ref, k_ref, v_ref, a_ref, b_ref, s0_ref, y_ref, s_ref, *, reverse, chunk, groups):
    L = chunk
    L2 = WKV_PAIR * L

    @pl.when(pl.program_id(2) == 0)
    def _():
        s_ref[...] = s0_ref[...]

    ti = lax.broadcasted_iota(jnp.int32, (L, L), 0)
    tj = lax.broadcasted_iota(jnp.int32, (L, L), 1)
    before_incl = (ti <= tj) if reverse else (ti >= tj)
    cum = jnp.where(before_incl, 1.0, 0.0).astype(_BF16)

    lw = lw_ref[0]
    hi = lw.astype(_BF16)
    rem = lw - hi.astype(_F32)
    mid = rem.astype(_BF16)
    lo = (rem - mid.astype(_F32)).astype(_BF16)
    cs = (jnp.dot(cum, hi, preferred_element_type=_F32)
          + jnp.dot(cum, mid, preferred_element_type=_F32)
          + jnp.dot(cum, lo, preferred_element_type=_F32))
    tot = cs[0:1] if reverse else cs[L - 1:L]
    half = 0.5 * tot
    e_pos = jnp.exp(cs - half)
    e_neg = jnp.exp(half - cs)
    e_prev = jnp.exp(cs - lw - half)
    e_end = jnp.exp(tot - cs)
    e_half = jnp.exp(half)
    e_tot = jnp.exp(tot)

    rt = r_ref[0] * e_pos
    at = a_ref[0] * e_prev
    kt = k_ref[0] * e_neg
    bt = b_ref[0] * e_neg
    ke = k_ref[0] * e_end
    be = b_ref[0] * e_end
    vv = v_ref[0]

    lane = lax.broadcasted_iota(jnp.int32, (L, LANES), 1)
    first_head = lane < RW_HEAD

    def stack(x):
        return jnp.concatenate([jnp.where(first_head, x, 0.0), jnp.where(first_head, 0.0, x)], axis=0)

    si = lax.broadcasted_iota(jnp.int32, (L2, L2), 0)
    sj = lax.broadcasted_iota(jnp.int32, (L2, L2), 1)
    same_head = (si < L) == (sj < L)
    pi = jnp.where(si < L, si, si - L)
    pj = jnp.where(sj < L, sj, sj - L)
    incl2 = same_head & ((pi <= pj) if reverse else (pi >= pj))
    strict2 = same_head & ((pi < pj) if reverse else (pi > pj))
    eye2 = jnp.where(si == sj, 1.0, 0.0)

    gs = range(groups)
    sls = [slice(g * LANES, (g + 1) * LANES) for g in gs]
    v_s = [stack(vv[:, sl]).astype(_BF16) for sl in sls]
    lhs = [jnp.concatenate([stack(at[:, sl]), stack(rt[:, sl])], axis=0).astype(_BF16) for sl in sls]
    rhs = [jnp.concatenate([stack(kt[:, sl]), stack(bt[:, sl])], axis=0).astype(_BF16) for sl in sls]
    state = [s_ref[0, g] for g in gs]
    scores = [_mm(lhs[g], rhs[g], _NT) for g in gs]
    from_state = [_mm(lhs[g], state[g] * e_half[:, sls[g]], _NT) for g in gs]

    power = [jnp.where(strict2, -scores[g][:L2, L2:], 0.0).astype(_BF16) for g in gs]
    inv = [eye2 + power[g] for g in gs]
    ak_v = [_mm(jnp.where(strict2, scores[g][:L2, :L2], 0.0), v_s[g]) for g in gs]
    for _ in range(int(math.log2(L)) - 1):
        power = [_mm(power[g], power[g]).astype(_BF16) for g in gs]
        inv = [inv[g] + _mm(inv[g], power[g]) for g in gs]
    u_s = [_mm(inv[g], from_state[g][:L2] + ak_v[g]) for g in gs]

    vu = [jnp.concatenate([v_s[g], u_s[g].astype(_BF16)], axis=0) for g in gs]
    for g in gs:
        r_kb = jnp.concatenate([jnp.where(incl2, scores[g][L2:, :L2], 0.0),
                                jnp.where(incl2, -scores[g][L2:, L2:], 0.0)], axis=1)
        y_s = from_state[g][L2:] + _mm(r_kb, vu[g])
        y_ref[0, :, sls[g]] = y_s[:L] + y_s[L:]
    for g in gs:
        kb = jnp.concatenate([stack(ke[:, sls[g]]), -stack(be[:, sls[g]])], axis=0)
        s_ref[0, g] = state[g] * e_tot[:, sls[g]] + _mm(vu[g], kb, _TN)


def _wkv_scan(r, lw, k, v, a, b, s0, reverse):
    bsz, t, width = r.shape
    groups = WKV_GROUPS_PER_STEP
    assert t % WKV_CHUNK == 0 and width % (groups * LANES) == 0
    nc = t // WKV_CHUNK
    ngroup_steps = width // (groups * LANES)
    cidx = (lambda c: nc - 1 - c) if reverse else (lambda c: c)
    seq_spec = pl.BlockSpec((1, WKV_CHUNK, groups * LANES), lambda bi, gi, c: (bi, cidx(c), gi))
    st_spec = pl.BlockSpec((1, groups, LANES, LANES), lambda bi, gi, c: (bi, gi, 0, 0))
    return pl.pallas_call(
        functools.partial(_wkv_kernel, reverse=reverse, chunk=WKV_CHUNK, groups=groups),
        grid=(bsz, ngroup_steps, nc),
        in_specs=[seq_spec] * 6 + [st_spec],
        out_specs=[seq_spec, st_spec],
        out_shape=[jax.ShapeDtypeStruct((bsz, t, width), _F32), jax.ShapeDtypeStruct(s0.shape, _F32)],
        compiler_params=pltpu.CompilerParams(dimension_semantics=("parallel", "parallel", "arbitrary")),
        name="wkv7_rev" if reverse else "wkv7_fwd",
    )(r, lw, k, v, a, b, s0)


MM_TILE_M = 1024
MM_TILE_N = 1024
NORM_TILE_M = 512
SMALL_COLS = 768


def _matmul_kernel(a_ref, w_ref, o_ref):
    o_ref[...] = jnp.dot(a_ref[...], w_ref[...], preferred_element_type=_F32)


def _matmul(a, w):
    m, k = a.shape
    n = w.shape[1]
    tm, tn = min(m, MM_TILE_M), min(n, MM_TILE_N)
    assert m % tm == 0 and n % tn == 0 and a.dtype == _BF16 and w.dtype == _BF16
    return pl.pallas_call(
        _matmul_kernel,
        grid=(n // tn, m // tm),
        in_specs=[pl.BlockSpec((tm, k), lambda j, i: (i, 0)), pl.BlockSpec((k, tn), lambda j, i: (0, j))],
        out_specs=pl.BlockSpec((tm, tn), lambda j, i: (i, j)),
        out_shape=jax.ShapeDtypeStruct((m, n), _F32),
        compiler_params=pltpu.CompilerParams(dimension_semantics=("parallel", "parallel"), vmem_limit_bytes=48 * 2**20),
        name="matmul",
    )(a, w)


def _norm_mod_kernel(x_ref, gain_ref, sc_ref, sh_ref, o_ref):
    x = x_ref[0]
    inv = lax.rsqrt(jnp.mean(x * x, axis=-1, keepdims=True) + NORM_EPS)
    o_ref[0] = ((x * inv * gain_ref[...]) * (1.0 + sc_ref[0]) + sh_ref[0]).astype(o_ref.dtype)


def _norm_mod(x, gain, scale, shift):
    bsz, t, d = x.shape
    tm = min(t, NORM_TILE_M)
    mod_spec = pl.BlockSpec((1, 1, d), (lambda b, i: (b, 0, 0)) if scale.shape[0] == bsz else (lambda b, i: (0, 0, 0)))
    return pl.pallas_call(
        _norm_mod_kernel,
        grid=(bsz, t // tm),
        in_specs=[pl.BlockSpec((1, tm, d), lambda b, i: (b, i, 0)), pl.BlockSpec((1, d), lambda b, i: (0, 0)), mod_spec, mod_spec],
        out_specs=pl.BlockSpec((1, tm, d), lambda b, i: (b, i, 0)),
        out_shape=jax.ShapeDtypeStruct((bsz, t, d), _BF16),
        compiler_params=pltpu.CompilerParams(dimension_semantics=("parallel", "parallel")),
        name="norm_mod",
    )(x, gain.reshape(1, d), scale, shift)


def _in_proj_weights(w):
    rw_end, z_end, xbc_end, dt_end = IN_SPLITS
    lora_dt = jnp.concatenate([w[:, RW_SPLITS[2]:rw_end], w[:, xbc_end:dt_end]], axis=1)
    lora_dt = jnp.pad(lora_dt, ((0, 0), (0, SMALL_COLS - lora_dt.shape[1])))
    parts = [w[:, :RW_SPLITS[0]], w[:, RW_SPLITS[0]:RW_SPLITS[1]], w[:, RW_SPLITS[1]:RW_SPLITS[2]],
             w[:, rw_end:z_end], w[:, z_end:xbc_end], w[:, dt_end:], lora_dt]
    return [p.astype(_BF16) for p in parts]


def _in_proj(h, weights):
    bsz, t, d = h.shape
    r, k, v, z, xbc, gate, lora_dt = [_matmul(h.reshape(bsz * t, d), w).reshape(bsz, t, -1) for w in weights]
    n_lora = RW_COLS - RW_SPLITS[2]
    return r, k, v, lora_dt, z, xbc, lora_dt[..., n_lora:n_lora + 2 * SSM_HEADS], gate


def _rms_norm(u, gain):
    uf = u.astype(_F32)
    uf = uf * lax.rsqrt(jnp.mean(uf * uf, axis=-1, keepdims=True) + NORM_EPS)
    return (uf * gain.astype(_F32)).astype(u.dtype)


def _centred_token_shift(p, mu):
    pad = jnp.pad(p, ((0, 0), (1, 1), (0, 0)))
    return p + mu * (0.5 * (pad[:, :-2] + pad[:, 2:]) - p)


def _raster_to_column(u, rows):
    b, s, ch = u.shape
    return u.reshape(b, rows, GRID_W, ch).transpose(0, 2, 1, 3).reshape(b, s, ch)


def _column_to_raster(u, rows):
    b, s, ch = u.shape
    return u.reshape(b, GRID_W, rows, ch).transpose(0, 2, 1, 3).reshape(b, s, ch)


RW_TILE_T = 128
LORA_COLS = RW_COLS - 3 * RW_WIDTH


def _head_matrices():
    head_of_lane = jnp.arange(RW_WIDTH) // RW_HEAD
    onehot = (head_of_lane[:, None] == jnp.arange(RW_HEADS)[None, :]).astype(_BF16)
    return onehot, onehot.T


def _head_sum(x, hsum):
    return sum(jnp.dot(part, hsum, preferred_element_type=_F32) for part in _split3(x))


def _head_widen(cols, hwide):
    return sum(jnp.dot(part, hwide, preferred_element_type=_F32) for part in _split3(cols))


def _token_shift_block(c_ref, p_ref, n_ref, mu, first, last):
    x = c_ref[0]
    tm = x.shape[0]
    row = lax.broadcasted_iota(jnp.int32, x.shape, 0)
    prev_row = jnp.where(first, 0.0, p_ref[0, SUBLANES - 1:SUBLANES, :])
    next_row = jnp.where(last, 0.0, n_ref[0, 0:1, :])
    prev = jnp.where(row == 0, prev_row, pltpu.roll(x, 1, 0))
    nxt = jnp.where(row == tm - 1, next_row, pltpu.roll(x, tm - 1, 0))
    return x + mu * (0.5 * (prev + nxt) - x)


def _rw_pre_kernel(rc, rp, rn, kc, kp, kn, vc, vp, vn, lc, lp, ln,
                   mu_r, mu_k, mu_v, mu_l, w0, a0, k_k, k_a, r_k, ln_b, w2, a2, g2, hsum, hwide,
                   r_o, v_o, kk_o, lwf_o, lwb_o, kf_o, kb_o, kkaf_o, kkab_o, gate_o, bonus_o):
    first = pl.program_id(1) == 0
    last = pl.program_id(1) == pl.num_programs(1) - 1
    r = _token_shift_block(rc, rp, rn, mu_r[...], first, last)
    k = _token_shift_block(kc, kp, kn, mu_k[...], first, last)
    v = _token_shift_block(vc, vp, vn, mu_v[...], first, last)
    lora = _token_shift_block(lc, lp, ln, mu_l[...], first, last)
    r_o[0] = r
    v_o[0] = v

    kk = k * k_k[...]
    norm = jnp.maximum(jnp.sqrt(_head_sum(kk * kk, hsum[...])), 1e-12)
    kk = kk * _head_widen(1.0 / norm, hwide[...])
    kk_o[0] = kk

    k_sum = None
    for d, (lw_o, kd_o, kka_o) in enumerate(((lwf_o, kf_o, kkaf_o), (lwb_o, kb_o, kkab_o))):
        wd = lora[:, d * LORA_W:(d + 1) * LORA_W]
        ad = lora[:, 2 * LORA_W + d * LORA_A:2 * LORA_W + (d + 1) * LORA_A]
        w_log = -jax.nn.softplus(-(w0[d:d + 1, :] + _mm(jnp.tanh(wd), w2[d]))) - 0.5
        lw_o[0] = -jnp.exp(w_log)
        a = jax.nn.sigmoid(a0[d:d + 1, :] + _mm(ad, a2[d]))
        k_dir = k * (1.0 + (a - 1.0) * k_a[...])
        kd_o[0] = k_dir
        kka_o[0] = kk * a
        k_sum = k_dir if k_sum is None else k_sum + k_dir

    gd = lora[:, 2 * LORA_W + 2 * LORA_A:2 * LORA_W + 2 * LORA_A + LORA_G]
    gate = _mm(jax.nn.sigmoid(gd), g2[...])
    bonus = _head_widen(_head_sum(r * k_sum * r_k[...], hsum[...]), hwide[...]) * v
    gate_o[0] = gate
    bonus_o[0] = (ln_b[...] + bonus) * gate


def _rw_pre(r, k, v, lora_dt, mu, w0, w2, a0, a2, g2, k_k, k_a, r_k, ln_b):
    bsz, t, width = r.shape
    tm = RW_TILE_T
    nt = t // tm
    blocks_per_tile = tm // SUBLANES
    last_block = t // SUBLANES - 1
    lcols = lora_dt.shape[-1]

    def specs(cols):
        return [pl.BlockSpec((1, tm, cols), lambda b, i: (b, i, 0)),
                pl.BlockSpec((1, SUBLANES, cols), lambda b, i: (b, jnp.maximum(i * blocks_per_tile - 1, 0), 0)),
                pl.BlockSpec((1, SUBLANES, cols), lambda b, i: (b, jnp.minimum((i + 1) * blocks_per_tile, last_block), 0))]

    def whole(a):
        return pl.BlockSpec(a.shape, lambda b, i: (0,) * a.ndim)

    row = lambda a: a.reshape(1, -1).astype(_F32)
    mu_l = jnp.pad(mu[RW_SPLITS[2]:], (0, lcols - LORA_COLS))
    hsum, hwide = _head_matrices()
    params = [row(mu[:RW_SPLITS[0]]), row(mu[RW_SPLITS[0]:RW_SPLITS[1]]), row(mu[RW_SPLITS[1]:RW_SPLITS[2]]), row(mu_l),
              w0.astype(_F32), a0.astype(_F32), row(k_k), row(k_a), row(r_k), row(ln_b),
              w2.astype(_BF16), a2.astype(_BF16), g2.astype(_BF16), hsum, hwide]
    out_spec = pl.BlockSpec((1, tm, width), lambda b, i: (b, i, 0))
    out_shape = jax.ShapeDtypeStruct((bsz, t, width), _F32)
    return pl.pallas_call(
        _rw_pre_kernel,
        grid=(bsz, nt),
        in_specs=specs(width) * 3 + specs(lcols) + [whole(p) for p in params],
        out_specs=[out_spec] * 11,
        out_shape=[out_shape] * 11,
        compiler_params=pltpu.CompilerParams(dimension_semantics=("parallel", "parallel"), vmem_limit_bytes=56 * 2**20),
        name="rw_pre",
    )(r, r, r, k, k, k, v, v, v, lora_dt, lora_dt, lora_dt, *params)


def _rw_post_kernel(yf, yb, gate, bonus, ln_w, hsum, hwide, o_ref):
    y = yf[0] + yb[0]
    mean = _head_widen(_head_sum(y, hsum[...]), hwide[...]) * (1.0 / RW_HEAD)
    cen = y - mean
    var = _head_widen(_head_sum(cen * cen, hsum[...]), hwide[...]) * (1.0 / RW_HEAD)
    o_ref[0] = (cen * lax.rsqrt(var + LN_X_EPS) * ln_w[...] * gate[0] + bonus[0]).astype(o_ref.dtype)


def _rw_post(y_f, y_b, gate, bonus, ln_w):
    bsz, t, width = y_f.shape
    tm = RW_TILE_T
    hsum, hwide = _head_matrices()
    spec = pl.BlockSpec((1, tm, width), lambda b, i: (b, i, 0))
    whole = lambda a: pl.BlockSpec(a.shape, lambda b, i: (0,) * a.ndim)
    ln_w = ln_w.reshape(1, width).astype(_F32)
    return pl.pallas_call(
        _rw_post_kernel,
        grid=(bsz, t // tm),
        in_specs=[spec] * 4 + [whole(ln_w), whole(hsum), whole(hwide)],
        out_specs=spec,
        out_shape=jax.ShapeDtypeStruct((bsz, t, width), _BF16),
        compiler_params=pltpu.CompilerParams(dimension_semantics=("parallel", "parallel")),
        name="rw_post",
    )(y_f, y_b, gate, bonus, ln_w, hsum, hwide)


def _rwkv7_branch(r, k, v, lora_dt, mu, s_fwd, s_bwd, w0, w2, a0, a2, g2, k_k, k_a, r_k, ln_w, ln_b, want_output):
    r, v, kk, lw_f, lw_b, k_f, k_b, kka_f, kka_b, gate, bonus = _rw_pre(
        r, k, v, lora_dt, mu, w0, w2, a0, a2, g2, k_k, k_a, r_k, ln_b)
    y_f, s_fwd = _wkv_scan(r, lw_f, k_f, v, kk, kka_f, s_fwd, reverse=False)
    y_b, s_bwd = _wkv_scan(r, lw_b, k_b, v, kk, kka_b, s_bwd, reverse=True)
    out = _rw_post(y_f, y_b, gate, bonus, ln_w) if want_output else None
    return out, s_fwd, s_bwd


def _centred_depthwise_conv(u, w):
    ch = u.shape[-1]
    return lax.conv_general_dilated(u, w.astype(_F32)[:, None, :], window_strides=(1,), padding=[(SSM_CONV // 2, SSM_CONV // 2)], dimension_numbers=('NWC', 'WIO', 'NWC'), feature_group_count=ch)


def _split3(x):
    hi = x.astype(_BF16)
    rem = x - hi.astype(_F32)
    mid = rem.astype(_BF16)
    lo = (rem - mid.astype(_F32)).astype(_BF16)
    return hi, mid, lo


def _ssd_kernel(xs_ref, bm_ref, cm_ref, la_ref, lat_ref, dt_ref, h0_ref, y_ref, h_ref, *, reverse):
    L = SSM_CHUNK
    P = SSM_HEADDIM

    @pl.when(pl.program_id(1) == 0)
    def _():
        h_ref[...] = h0_ref[...]

    ti = lax.broadcasted_iota(jnp.int32, (L, L), 0)
    tj = lax.broadcasted_iota(jnp.int32, (L, L), 1)
    before_incl = (ti <= tj) if reverse else (ti >= tj)
    cum = jnp.where(before_incl, 1.0, 0.0).astype(_BF16)
    cum_t = jnp.where(before_incl, 0.0, 1.0).astype(_BF16) + jnp.where(ti == tj, 1.0, 0.0).astype(_BF16)

    la = la_ref[0]
    cs = sum(jnp.dot(cum, part, preferred_element_type=_F32) for part in _split3(la))
    cs_t = sum(jnp.dot(part, cum_t, preferred_element_type=_F32) for part in _split3(lat_ref[0]))
    tot = cs[0:1] if reverse else cs[L - 1:L]
    dt = dt_ref[0]

    hi = lax.broadcasted_iota(jnp.int32, (SSM_HEADS, SSM_WIDTH), 0)
    hj = lax.broadcasted_iota(jnp.int32, (SSM_HEADS, SSM_WIDTH), 1)
    lo_edge = hi * P
    widen = jnp.where((hj >= lo_edge) & (hj < lo_edge + P), 1.0, 0.0).astype(_BF16)

    def wide(cols):
        return sum(jnp.dot(part, widen, preferred_element_type=_F32) for part in _split3(cols))

    xs = xs_ref[0]
    xdt = xs * wide(dt)
    xdt_end = (xs * wide(dt * jnp.exp(tot - cs))).astype(_BF16)
    xdt = xdt.astype(_BF16)
    decay_in = wide(jnp.exp(cs))
    e_tot = jnp.exp(tot)

    pieces = []
    for g in range(SSM_GROUPS):
        bm = bm_ref[0, :, g * SSM_STATE:(g + 1) * SSM_STATE].astype(_BF16)
        cm = cm_ref[0, :, g * SSM_STATE:(g + 1) * SSM_STATE].astype(_BF16)
        cb = lax.dot_general(cm, bm, _NT, preferred_element_type=_F32)
        for e in range(SSM_HPG):
            h = g * SSM_HPG + e
            cols = slice(h * P, (h + 1) * P)
            seg = jnp.where(before_incl, jnp.exp(cs[:, h:h + 1] - cs_t[h:h + 1, :]), 0.0)
            state = h_ref[0, h]
            y_h = jnp.dot((cb * seg).astype(_BF16), xdt[:, cols], preferred_element_type=_F32)
            y_h = y_h + lax.dot_general(cm, state.astype(_BF16), _NT, preferred_element_type=_F32) * decay_in[:, cols]
            pieces.append(y_h)
            new = lax.dot_general(xdt_end[:, cols], bm, _TN, preferred_element_type=_F32)
            h_ref[0, h] = state * e_tot[:, h:h + 1] + new
    y_ref[0] = jnp.concatenate(pieces, axis=1)


def _ssd_scan(xs, bm, cm, log_a, dt, h0, reverse):
    bsz, t, width = xs.shape
    nc = t // SSM_CHUNK
    cidx = (lambda c: nc - 1 - c) if reverse else (lambda c: c)
    seq = lambda w: pl.BlockSpec((1, SSM_CHUNK, w), lambda b, c: (b, cidx(c), 0))
    st_spec = pl.BlockSpec((1,) + h0.shape[1:], lambda b, c: (b, 0, 0, 0))
    gn = SSM_GROUPS * SSM_STATE
    return pl.pallas_call(
        functools.partial(_ssd_kernel, reverse=reverse),
        grid=(bsz, nc),
        in_specs=[seq(width), seq(gn), seq(gn), seq(SSM_HEADS),
                  pl.BlockSpec((1, SSM_HEADS, SSM_CHUNK), lambda b, c: (b, 0, cidx(c))), seq(SSM_HEADS), st_spec],
        out_specs=[seq(width), st_spec],
        out_shape=[jax.ShapeDtypeStruct((bsz, t, width), _F32), jax.ShapeDtypeStruct(h0.shape, _F32)],
        compiler_params=pltpu.CompilerParams(dimension_semantics=("parallel", "arbitrary")),
        name="ssd_rev" if reverse else "ssd_fwd",
    )(xs, bm, cm, log_a, jnp.swapaxes(log_a, 1, 2), dt, h0)


def _mamba2_branch(xbc, dt_raw, h0_fwd, h0_bwd, conv_w, conv_b, dt_bias, a_log, d_skip):
    bsz, t, _ = xbc.shape
    xbc = jax.nn.silu(_centred_depthwise_conv(xbc.astype(_F32), conv_w) + conv_b)
    xs, bm, cm = jnp.split(xbc, (SSM_WIDTH, SSM_WIDTH + SSM_GROUPS * SSM_STATE), axis=-1)
    dt = jax.nn.softplus(dt_raw.astype(_F32).reshape(bsz, t, 2, SSM_HEADS) + dt_bias)
    log_a = -jnp.exp(a_log) * dt
    heads = lambda h: h.reshape(bsz, SSM_HEADS, SSM_HEADDIM, SSM_STATE)
    y_f, h_f = _ssd_scan(xs, bm, cm, log_a[:, :, 0], dt[:, :, 0], heads(h0_fwd), reverse=False)
    y_b, h_b = _ssd_scan(xs, bm, cm, log_a[:, :, 1], dt[:, :, 1], heads(h0_bwd), reverse=True)
    y = y_f + y_b + jnp.repeat(d_skip, SSM_HEADDIM) * xs
    return y, h_f, h_b


def _gated_group_rmsnorm(y, z, gain):
    bsz, t, _ = y.shape
    u = (y * jax.nn.silu(z.astype(_F32))).reshape(bsz, t, SSM_GROUPS, SSM_WIDTH // SSM_GROUPS)
    u = u * lax.rsqrt(jnp.mean(u * u, axis=-1, keepdims=True) + NORM_EPS)
    return u.reshape(bsz, t, SSM_WIDTH) * gain


def _merge_branches(y_a, y_b, gates, w_a, w_b, w_o):
    bsz, t, d = y_a.shape
    mm = lambda u, w: _matmul(u.reshape(bsz * t, -1).astype(_BF16), w.astype(_BF16)).reshape(bsz, t, -1)
    g_a, g_b = jnp.split(gates, 2, axis=-1)
    m = jax.nn.sigmoid(g_a) * mm(y_a, w_a) + jax.nn.sigmoid(g_b) * mm(y_b, w_b)
    return mm(m, w_o)


PEER_ROUTE_TOKENS = 256
PEER_TOKEN_BLOCK = 512
PEER_EXPERT_BLOCK = 512
SUBLANES = 8


def _top_values(x, count):
    rows = lax.broadcasted_iota(jnp.int32, x.shape, 0)
    rank = jnp.full(x.shape, float(count), _F32)
    vals = []
    for it in range(count):
        m = jnp.max(x, axis=0, keepdims=True)
        vals.append(m)
        first = jnp.min(jnp.where(x == m, rows, x.shape[0]), axis=0, keepdims=True)
        hit = rows == first
        rank = jnp.where(hit, float(it), rank)
        x = jnp.where(hit, -jnp.inf, x)
    return vals, rank, x


def _peer_route_kernel(h_ref, wqt_ref, sub_ref, cnt1_ref, e1_ref, rank2_ref, e2_ref):
    qt = lax.dot_general(wqt_ref[...], h_ref[...].astype(_BF16), _NT, preferred_element_type=_F32)
    for h in range(PEER_HEADS):
        sc = []
        for s in range(2):
            lo = (2 * h + s) * PEER_HALF
            sc.append(jnp.dot(sub_ref[2 * h + s], qt[lo:lo + PEER_HALF].astype(_BF16), preferred_element_type=_F32))
        top_a, rank_a, _ = _top_values(sc[0], PEER_TOPK)
        top_b, rank_b, _ = _top_values(sc[1], PEER_TOPK)
        top_b = jnp.concatenate(top_b, axis=0)
        width = [PEER_TOPK // (i + 1) for i in range(PEER_TOPK)]
        cand = jnp.concatenate([top_a[i] + top_b[:width[i]] for i in range(PEER_TOPK)], axis=0)
        best, _, left = _top_values(cand, PEER_TOPK)
        taken = jnp.where(left == cand, 0.0, 1.0)
        norm = best[0] * 0.0
        for val in best:
            norm = norm + jnp.exp(val - best[0])
        cnt1 = jnp.zeros_like(rank_a)
        start = 0
        for i in range(PEER_TOPK):
            used = jnp.sum(taken[start:start + width[i]], axis=0, keepdims=True)
            cnt1 = jnp.where(rank_a == float(i), used, cnt1)
            start += width[i]
        cnt1_ref[h] = cnt1.astype(cnt1_ref.dtype)
        rank2_ref[h] = rank_b.astype(rank2_ref.dtype)
        e1_ref[h] = (jnp.exp(sc[0] - top_a[0]) / norm).astype(e1_ref.dtype)
        e2_ref[h] = jnp.exp(sc[1] - top_b[0:1]).astype(e2_ref.dtype)


def _peer_route(tok, wq, subkeys):
    n, d = tok.shape
    tr = PEER_ROUTE_TOKENS
    wqt = wq.T.astype(_BF16)
    sub = subkeys.reshape(PEER_HEADS * 2, PEER_KEYS, PEER_HALF).astype(_BF16)
    key_spec = pl.BlockSpec((PEER_HEADS, PEER_KEYS, tr), lambda i: (0, 0, i))
    row_shape = jax.ShapeDtypeStruct((PEER_HEADS, PEER_KEYS, n), _F32)
    tile_shape = jax.ShapeDtypeStruct((PEER_HEADS, PEER_KEYS, n), _BF16)
    return pl.pallas_call(
        _peer_route_kernel,
        grid=(n // tr,),
        in_specs=[pl.BlockSpec((tr, d), lambda i: (i, 0)),
                  pl.BlockSpec(wqt.shape, lambda i: (0, 0)),
                  pl.BlockSpec(sub.shape, lambda i: (0, 0, 0))],
        out_specs=[key_spec] * 4,
        out_shape=[row_shape, row_shape, tile_shape, tile_shape],
        compiler_params=pltpu.CompilerParams(dimension_semantics=("parallel",), vmem_limit_bytes=48 * 2**20),
        name="peer_route",
    )(tok, wqt, sub)


def _gelu_exact(x):
    return 0.5 * x * (1.0 + lax.erf(x * (1.0 / math.sqrt(2.0))))


def _peer_gate_block(a_ref, w_ref, b_ref, cnt1_ref, e1_ref, rank2_ref, e2_ref, block):
    keys_per_block = PEER_EXPERT_BLOCK // PEER_KEYS
    rt = 2 * SUBLANES
    tb = a_ref.shape[1]
    for h in range(PEER_HEADS):
        for j in range(keys_per_block):
            i1 = block * keys_per_block + j
            b_ref[0, j * PEER_HEADS + h] = jnp.broadcast_to(cnt1_ref[h, pl.ds(i1, 1), :], (rt, tb)).astype(_BF16)
            b_ref[1, j * PEER_HEADS + h] = jnp.broadcast_to(e1_ref[h, pl.ds(i1, 1), :], (rt, tb)).astype(_BF16)

    def row_tile(tile, carry):
        r0 = pl.multiple_of(tile * rt, rt)
        gates = [None] * keys_per_block
        for h in range(PEER_HEADS):
            rank2 = rank2_ref[h, pl.ds(r0, rt), :]
            e2 = e2_ref[h, pl.ds(r0, rt), :]
            for j in range(keys_per_block):
                term = jnp.where(rank2 < b_ref[0, j * PEER_HEADS + h], b_ref[1, j * PEER_HEADS + h] * e2, jnp.zeros_like(e2))
                gates[j] = term if gates[j] is None else gates[j] + term
        for j in range(keys_per_block):
            rows = pl.ds(pl.multiple_of(j * PEER_KEYS + r0, rt), rt)
            w_ref[rows, :] = gates[j] * _gelu_exact(a_ref[rows, :]).astype(_BF16)
        return carry

    lax.fori_loop(0, PEER_KEYS // rt, row_tile, 0)


def _peer_expert_kernel(x_ref, u_ref, v_ref, cnt1_ref, e1_ref, rank2_ref, e2_ref, o_ref, a_scr, w_scr, b_scr):
    eb = pl.program_id(1)

    @pl.when(eb == 0)
    def _():
        o_ref[...] = jnp.zeros_like(o_ref)

    a_scr[...] = lax.dot_general(u_ref[...], x_ref[...], _NT, preferred_element_type=_F32)
    _peer_gate_block(a_scr, w_scr, b_scr, cnt1_ref, e1_ref, rank2_ref, e2_ref, eb)
    o_ref[...] += lax.dot_general(w_scr[...], v_ref[...], _TN, preferred_element_type=_F32)


def _peer_ffn(h, wq, subkeys, u_tab, v_tab):
    bsz, t, d = h.shape
    n = bsz * t
    tok = h.reshape(n, d)
    routing = _peer_route(tok, wq, subkeys)
    tb, eb = PEER_TOKEN_BLOCK, PEER_EXPERT_BLOCK
    n_experts = u_tab.shape[0]
    key_spec = pl.BlockSpec((PEER_HEADS, PEER_KEYS, tb), lambda i, e: (0, 0, i))
    tab_spec = pl.BlockSpec((eb, d), lambda i, e: (e, 0))
    out = pl.pallas_call(
        _peer_expert_kernel,
        grid=(n // tb, n_experts // eb),
        in_specs=[pl.BlockSpec((tb, d), lambda i, e: (i, 0)), tab_spec, tab_spec] + [key_spec] * 4,
        out_specs=pl.BlockSpec((tb, d), lambda i, e: (i, 0)),
        out_shape=jax.ShapeDtypeStruct((n, d), _F32),
        scratch_shapes=[pltpu.VMEM((eb, tb), _F32), pltpu.VMEM((eb, tb), _BF16),
                        pltpu.VMEM((2, (eb // PEER_KEYS) * PEER_HEADS, 2 * SUBLANES, tb), _BF16)],
        compiler_params=pltpu.CompilerParams(dimension_semantics=("parallel", "arbitrary"), vmem_limit_bytes=56 * 2**20),
        name="peer_experts",
    )(tok, u_tab.astype(_BF16), v_tab.astype(_BF16), *routing)
    return out.reshape(bsz, t, d)


def kernel(x, c, ctx, c_ctx, w_mod, b_mod, norm_pre1, norm_post1, norm_pre2, norm_post2, w_in, rw_mu, rw_w0, rw_w2, rw_a0, rw_a2, rw_g2, rw_k_k, rw_k_a, rw_r_k, rw_ln_w, rw_ln_b, ssm_conv_w, ssm_conv_b, ssm_dt_bias, ssm_a_log, ssm_d, ssm_norm_w, w_branch_a, w_branch_b, w_out, peer_wq, peer_subkeys, peer_u, peer_v):
    act_dtype = x.dtype
    bsz, seq, _ = x.shape
    rows = seq // GRID_W
    depth = w_mod.shape[0]
    assert depth == 1
    rw_zero = jnp.zeros((bsz, RW_WIDTH // LANES, LANES, LANES), _F32)
    ssm_zero = jnp.zeros((bsz, SSM_GROUPS, SSM_HPG, SSM_HEADDIM, SSM_STATE), _F32)
    l = 0
    mod_x = (jax.nn.silu(c) @ w_mod[l] + b_mod[l])[:, None, :]
    mod_c = (jax.nn.silu(c_ctx) @ w_mod[l] + b_mod[l])[None, None, :]
    sh1x, sc1x, g1x, sh2x, sc2x, g2x = jnp.split(mod_x, N_MOD, axis=-1)
    sh1c, sc1c, g1c, sh2c, sc2c, g2c = jnp.split(mod_c, N_MOD, axis=-1)

    in_w = _in_proj_weights(w_in[l])
    r_x, k_x, v_x, lora_x, z_x, xbc_x, dt_x, gate_x = _in_proj(_norm_mod(x, norm_pre1[l], sc1x, sh1x), in_w)
    r_c, k_c, v_c, lora_c, z_c, xbc_c, dt_c, gate_c = _in_proj(_norm_mod(ctx, norm_pre1[l], sc1c, sh1c), in_w)

    rw_params = (rw_w0[l], rw_w2[l], rw_a0[l], rw_a2[l], rw_g2[l], rw_k_k[l], rw_k_a[l], rw_r_k[l], rw_ln_w[l], rw_ln_b[l])
    _, s_fwd, s_bwd = _rwkv7_branch(r_c, k_c, v_c, lora_c, rw_mu[l], rw_zero, rw_zero, *rw_params, want_output=False)
    ya_x, _, _ = _rwkv7_branch(r_x, k_x, v_x, lora_x, rw_mu[l], s_fwd, s_bwd, *rw_params, want_output=True)

    ssm_params = (ssm_conv_w[l], ssm_conv_b[l], ssm_dt_bias[l], ssm_a_log[l], ssm_d[l])
    yb_c, h_fwd, h_bwd = _mamba2_branch(xbc_c, dt_c, ssm_zero, ssm_zero, *ssm_params)
    yb_x, _, _ = _mamba2_branch(_raster_to_column(xbc_x, rows), _raster_to_column(dt_x, rows), h_fwd, h_bwd, *ssm_params)
    yb_x = _column_to_raster(yb_x, rows)

    mix_x = _merge_branches(ya_x.astype(act_dtype), _gated_group_rmsnorm(yb_x, z_x, ssm_norm_w[l]).astype(act_dtype), gate_x, w_branch_a[l], w_branch_b[l], w_out[l])
    x = x + g1x * _rms_norm(mix_x, norm_post1[l])

    h2x = _norm_mod(x, norm_pre2[l], sc2x, sh2x)
    x = x + g2x * _rms_norm(_peer_ffn(h2x, peer_wq[l], peer_subkeys[l], peer_u[l], peer_v[l]), norm_post2[l])
    return x
```

```python
import functools
import math

import jax
import jax.numpy as jnp
from jax import lax
from jax.experimental import pallas as pl
from jax.experimental.pallas import tpu as pltpu

D_MODEL = 2048
GRID_W = 64
N_MOD = 6
NORM_EPS = 1e-6
RW_HEAD = 64
RW_WIDTH = D_MODEL
RW_HEADS = RW_WIDTH // RW_HEAD
LORA_W = 96
LORA_A = 96
LORA_G = 256
LN_X_EPS = 64e-5
SSM_WIDTH = D_MODEL
SSM_HEADDIM = 64
SSM_HEADS = SSM_WIDTH // SSM_HEADDIM
SSM_GROUPS = 8
SSM_HPG = SSM_HEADS // SSM_GROUPS
SSM_STATE = 128
SSM_CONV = 5
SSM_CHUNK = 128
PEER_HEADS = 8
PEER_KEYS = 128
PEER_TOPK = 16
PEER_QDIM = 256
PEER_HALF = PEER_QDIM // 2
PEER_BLOCK = 128
RW_COLS = 3 * RW_WIDTH + 2 * LORA_W + 2 * LORA_A + LORA_G
XBC_COLS = SSM_WIDTH + 2 * SSM_GROUPS * SSM_STATE
IN_SPLITS = (RW_COLS, RW_COLS + SSM_WIDTH, RW_COLS + SSM_WIDTH + XBC_COLS, RW_COLS + SSM_WIDTH + XBC_COLS + 2 * SSM_HEADS)
RW_SPLITS = (RW_WIDTH, 2 * RW_WIDTH, 3 * RW_WIDTH, 3 * RW_WIDTH + LORA_W, 3 * RW_WIDTH + 2 * LORA_W, 3 * RW_WIDTH + 2 * LORA_W + LORA_A, 3 * RW_WIDTH + 2 * LORA_W + 2 * LORA_A)

LANES = 128
WKV_CHUNK = 64
WKV_PAIR = LANES // RW_HEAD
WKV_GROUPS_PER_STEP = 8

_F32 = jnp.float32
_BF16 = jnp.bfloat16
_NT = (((1,), (1,)), ((), ()))
_TN = (((0,), (0,)), ((), ()))


def _mm(a, b, dims=None):
    a = a.astype(_BF16)
    b = b.astype(_BF16)
    if dims is None:
        return jnp.dot(a, b, preferred_element_type=_F32)
    return lax.dot_general(a, b, dims, preferred_element_type=_F32)


def _wkv_kernel(r_ref, lw_ref, k_ref, v_ref, a_ref, b_ref, s0_ref, y_ref, s_ref, *, reverse, chunk, groups):
    L = chunk
    L2 = WKV_PAIR * L

    @pl.when(pl.program_id(2) == 0)
    def _():
        s_ref[...] = s0_ref[...]

    ti = lax.broadcasted_iota(jnp.int32, (L, L), 0)
    tj = lax.broadcasted_iota(jnp.int32, (L, L), 1)
    before_incl = (ti <= tj) if reverse else (ti >= tj)
    cum = jnp.where(before_incl, 1.0, 0.0).astype(_BF16)

    lw = lw_ref[0]
    hi = lw.astype(_BF16)
    rem = lw - hi.astype(_F32)
    mid = rem.astype(_BF16)
    lo = (rem - mid.astype(_F32)).astype(_BF16)
    cs = (jnp.dot(cum, hi, preferred_element_type=_F32)
          + jnp.dot(cum, mid, preferred_element_type=_F32)
          + jnp.dot(cum, lo, preferred_element_type=_F32))
    tot = cs[0:1] if reverse else cs[L - 1:L]
    half = 0.5 * tot
    e_pos = jnp.exp(cs - half)
    e_neg = jnp.exp(half - cs)
    e_prev = jnp.exp(cs - lw - half)
    e_end = jnp.exp(tot - cs)
    e_half = jnp.exp(half)
    e_tot = jnp.exp(tot)

    rt = r_ref[0] * e_pos
    at = a_ref[0] * e_prev
    kt = k_ref[0] * e_neg
    bt = b_ref[0] * e_neg
    ke = k_ref[0] * e_end
    be = b_ref[0] * e_end
    vv = v_ref[0]

    lane = lax.broadcasted_iota(jnp.int32, (L, LANES), 1)
    first_head = lane < RW_HEAD

    def stack(x):
        return jnp.concatenate([jnp.where(first_head, x, 0.0), jnp.where(first_head, 0.0, x)], axis=0)

    si = lax.broadcasted_iota(jnp.int32, (L2, L2), 0)
    sj = lax.broadcasted_iota(jnp.int32, (L2, L2), 1)
    same_head = (si < L) == (sj < L)
    pi = jnp.where(si < L, si, si - L)
    pj = jnp.where(sj < L, sj, sj - L)
    incl2 = same_head & ((pi <= pj) if reverse else (pi >= pj))
    strict2 = same_head & ((pi < pj) if reverse else (pi > pj))
    eye2 = jnp.where(si == sj, 1.0, 0.0)

    gs = range(groups)
    sls = [slice(g * LANES, (g + 1) * LANES) for g in gs]
    v_s = [stack(vv[:, sl]).astype(_BF16) for sl in sls]
    lhs = [jnp.concatenate([stack(at[:, sl]), stack(rt[:, sl])], axis=0).astype(_BF16) for sl in sls]
    rhs = [jnp.concatenate([stack(kt[:, sl]), stack(bt[:, sl])], axis=0).astype(_BF16) for sl in sls]
    state = [s_ref[0, g] for g in gs]
    scores = [_mm(lhs[g], rhs[g], _NT) for g in gs]
    from_state = [_mm(lhs[g], state[g] * e_half[:, sls[g]], _NT) for g in gs]

    power = [jnp.where(strict2, -scores[g][:L2, L2:], 0.0).astype(_BF16) for g in gs]
    inv = [eye2 + power[g] for g in gs]
    ak_v = [_mm(jnp.where(strict2, scores[g][:L2, :L2], 0.0), v_s[g]) for g in gs]
    for _ in range(int(math.log2(L)) - 1):
        power = [_mm(power[g], power[g]).astype(_BF16) for g in gs]
        inv = [inv[g] + _mm(inv[g], power[g]) for g in gs]
    u_s = [_mm(inv[g], from_state[g][:L2] + ak_v[g]) for g in gs]

    vu = [jnp.concatenate([v_s[g], u_s[g].astype(_BF16)], axis=0) for g in gs]
    for g in gs:
        r_kb = jnp.concatenate([jnp.where(incl2, scores[g][L2:, :L2], 0.0),
                                jnp.where(incl2, -scores[g][L2:, L2:], 0.0)], axis=1)
        y_s = from_state[g][L2:] + _mm(r_kb, vu[g])
        y_ref[0, :, sls[g]] = y_s[:L] + y_s[L:]
    for g in gs:
        kb = jnp.concatenate([stack(ke[:, sls[g]]), -stack(be[:, sls[g]])], axis=0)
        s_ref[0, g] = state[g] * e_tot[:, sls[g]] + _mm(vu[g], kb, _TN)


def _wkv_scan(r, lw, k, v, a, b, s0, reverse):
    bsz, t, width = r.shape
    groups = WKV_GROUPS_PER_STEP
    assert t % WKV_CHUNK == 0 and width % (groups * LANES) == 0
    nc = t // WKV_CHUNK
    ngroup_steps = width // (groups * LANES)
    cidx = (lambda c: nc - 1 - c) if reverse else (lambda c: c)
    seq_spec = pl.BlockSpec((1, WKV_CHUNK, groups * LANES), lambda bi, gi, c: (bi, cidx(c), gi))
    st_spec = pl.BlockSpec((1, groups, LANES, LANES), lambda bi, gi, c: (bi, gi, 0, 0))
    return pl.pallas_call(
        functools.partial(_wkv_kernel, reverse=reverse, chunk=WKV_CHUNK, groups=groups),
        grid=(bsz, ngroup_steps, nc),
        in_specs=[seq_spec] * 6 + [st_spec],
        out_specs=[seq_spec, st_spec],
        out_shape=[jax.ShapeDtypeStruct((bsz, t, width), _F32), jax.ShapeDtypeStruct(s0.shape, _F32)],
        compiler_params=pltpu.CompilerParams(dimension_semantics=("parallel", "parallel", "arbitrary")),
        name="wkv7_rev" if reverse else "wkv7_fwd",
    )(r, lw, k, v, a, b, s0)


MM_TILE_M = 1024
MM_TILE_N = 1024
NORM_TILE_M = 512
SMALL_COLS = 768


def _matmul_kernel(a_ref, w_ref, o_ref):
    o_ref[...] = jnp.dot(a_ref[...], w_ref[...], preferred_element_type=_F32)


def _matmul(a, w):
    m, k = a.shape
    n = w.shape[1]
    tm, tn = min(m, MM_TILE_M), min(n, MM_TILE_N)
    assert m % tm == 0 and n % tn == 0 and a.dtype == _BF16 and w.dtype == _BF16
    return pl.pallas_call(
        _matmul_kernel,
        grid=(n // tn, m // tm),
        in_specs=[pl.BlockSpec((tm, k), lambda j, i: (i, 0)), pl.BlockSpec((k, tn), lambda j, i: (0, j))],
        out_specs=pl.BlockSpec((tm, tn), lambda j, i: (i, j)),
        out_shape=jax.ShapeDtypeStruct((m, n), _F32),
        compiler_params=pltpu.CompilerParams(dimension_semantics=("parallel", "parallel"), vmem_limit_bytes=48 * 2**20),
        name="matmul",
    )(a, w)


def _norm_mod_kernel(x_ref, gain_ref, sc_ref, sh_ref, o_ref):
    x = x_ref[0]
    inv = lax.rsqrt(jnp.mean(x * x, axis=-1, keepdims=True) + NORM_EPS)
    o_ref[0] = ((x * inv * gain_ref[...]) * (1.0 + sc_ref[0]) + sh_ref[0]).astype(o_ref.dtype)


def _norm_mod(x, gain, scale, shift):
    bsz, t, d = x.shape
    tm = min(t, NORM_TILE_M)
    mod_spec = pl.BlockSpec((1, 1, d), (lambda b, i: (b, 0, 0)) if scale.shape[0] == bsz else (lambda b, i: (0, 0, 0)))
    return pl.pallas_call(
        _norm_mod_kernel,
        grid=(bsz, t // tm),
        in_specs=[pl.BlockSpec((1, tm, d), lambda b, i: (b, i, 0)), pl.BlockSpec((1, d), lambda b, i: (0, 0)), mod_spec, mod_spec],
        out_specs=pl.BlockSpec((1, tm, d), lambda b, i: (b, i, 0)),
        out_shape=jax.ShapeDtypeStruct((bsz, t, d), _BF16),
        compiler_params=pltpu.CompilerParams(dimension_semantics=("parallel", "parallel")),
        name="norm_mod",
    )(x, gain.reshape(1, d), scale, shift)


def _residual_kernel(x_ref, u_ref, g_ref, gain_ref, *rest, with_next):
    u = u_ref[0]
    inv = lax.rsqrt(jnp.mean(u * u, axis=-1, keepdims=True) + NORM_EPS)
    y = x_ref[0] + g_ref[0] * (u * inv * gain_ref[...])
    if not with_next:
        rest[0][0] = y
        return
    gain2_ref, sc_ref, sh_ref, y_ref, h_ref = rest
    y_ref[0] = y
    inv2 = lax.rsqrt(jnp.mean(y * y, axis=-1, keepdims=True) + NORM_EPS)
    h_ref[0] = ((y * inv2 * gain2_ref[...]) * (1.0 + sc_ref[0]) + sh_ref[0]).astype(h_ref.dtype)


def _residual(x, u, g, gain, next_mod=None):
    bsz, t, d = x.shape
    tm = min(t, NORM_TILE_M)
    seq = pl.BlockSpec((1, tm, d), lambda b, i: (b, i, 0))
    per_batch = pl.BlockSpec((1, 1, d), lambda b, i: (b, 0, 0))
    row = pl.BlockSpec((1, d), lambda b, i: (0, 0))
    args, in_specs = [x, u, g, gain.reshape(1, d)], [seq, seq, per_batch, row]
    out_specs, out_shape = [seq], [jax.ShapeDtypeStruct((bsz, t, d), _F32)]
    if next_mod is not None:
        gain2, scale, shift = next_mod
        args += [gain2.reshape(1, d), scale, shift]
        in_specs += [row, per_batch, per_batch]
        out_specs.append(seq)
        out_shape.append(jax.ShapeDtypeStruct((bsz, t, d), _BF16))
    out = pl.pallas_call(
        functools.partial(_residual_kernel, with_next=next_mod is not None),
        grid=(bsz, t // tm),
        in_specs=in_specs,
        out_specs=out_specs,
        out_shape=out_shape,
        compiler_params=pltpu.CompilerParams(dimension_semantics=("parallel", "parallel")),
        name="residual",
    )(*args)
    return out if next_mod is not None else out[0]


def _in_proj_weights(w):
    rw_end, z_end, xbc_end, dt_end = IN_SPLITS
    lora_dt = jnp.concatenate([w[:, RW_SPLITS[2]:rw_end], w[:, xbc_end:dt_end]], axis=1)
    lora_dt = jnp.pad(lora_dt, ((0, 0), (0, SMALL_COLS - lora_dt.shape[1])))
    parts = [w[:, :RW_SPLITS[0]], w[:, RW_SPLITS[0]:RW_SPLITS[1]], w[:, RW_SPLITS[1]:RW_SPLITS[2]],
             w[:, rw_end:z_end], w[:, z_end:xbc_end], w[:, dt_end:], lora_dt]
    return [p.astype(_BF16) for p in parts]


def _in_proj(h, weights, h_ssm=None):
    bsz, t, d = h.shape
    mm = lambda u, w: _matmul(u.reshape(bsz * t, d), w).reshape(bsz, t, -1)
    w_r, w_k, w_v, w_z, w_xbc, w_gate, w_lora_dt = weights
    lora_dt = mm(h, w_lora_dt)
    dt_src = lora_dt if h_ssm is None else mm(h_ssm, w_lora_dt)
    h_ssm = h if h_ssm is None else h_ssm
    n_lora = RW_COLS - RW_SPLITS[2]
    return (mm(h, w_r), mm(h, w_k), mm(h, w_v), lora_dt, mm(h_ssm, w_z), mm(h_ssm, w_xbc),
            dt_src[..., n_lora:n_lora + 2 * SSM_HEADS], mm(h, w_gate))


def _rms_norm(u, gain):
    uf = u.astype(_F32)
    uf = uf * lax.rsqrt(jnp.mean(uf * uf, axis=-1, keepdims=True) + NORM_EPS)
    return (uf * gain.astype(_F32)).astype(u.dtype)


def _centred_token_shift(p, mu):
    pad = jnp.pad(p, ((0, 0), (1, 1), (0, 0)))
    return p + mu * (0.5 * (pad[:, :-2] + pad[:, 2:]) - p)


def _raster_to_column(u, rows):
    b, s, ch = u.shape
    return u.reshape(b, rows, GRID_W, ch).transpose(0, 2, 1, 3).reshape(b, s, ch)


def _column_to_raster(u, rows):
    b, s, ch = u.shape
    return u.reshape(b, GRID_W, rows, ch).transpose(0, 2, 1, 3).reshape(b, s, ch)


RW_TILE_T = 128
LORA_COLS = RW_COLS - 3 * RW_WIDTH


def _head_matrices():
    head_of_lane = jnp.arange(RW_WIDTH) // RW_HEAD
    onehot = (head_of_lane[:, None] == jnp.arange(RW_HEADS)[None, :]).astype(_BF16)
    return onehot, onehot.T


def _head_sum(x, hsum):
    return sum(jnp.dot(part, hsum, preferred_element_type=_F32) for part in _split3(x))


def _head_widen(cols, hwide):
    return sum(jnp.dot(part, hwide, preferred_element_type=_F32) for part in _split3(cols))


def _token_shift_block(c_ref, p_ref, n_ref, mu, first, last):
    x = c_ref[0]
    tm = x.shape[0]
    row = lax.broadcasted_iota(jnp.int32, x.shape, 0)
    prev_row = jnp.where(first, 0.0, p_ref[0, SUBLANES - 1:SUBLANES, :])
    next_row = jnp.where(last, 0.0, n_ref[0, 0:1, :])
    prev = jnp.where(row == 0, prev_row, pltpu.roll(x, 1, 0))
    nxt = jnp.where(row == tm - 1, next_row, pltpu.roll(x, tm - 1, 0))
    return x + mu * (0.5 * (prev + nxt) - x)


def _rw_pre_kernel(rc, rp, rn, kc, kp, kn, vc, vp, vn, lc, lp, ln,
                   mu_r, mu_k, mu_v, mu_l, w0, a0, k_k, k_a, r_k, ln_b, w2, a2, g2, hsum, hwide,
                   r_o, v_o, kk_o, lwf_o, lwb_o, kf_o, kb_o, kkaf_o, kkab_o, gate_o, bonus_o):
    first = pl.program_id(1) == 0
    last = pl.program_id(1) == pl.num_programs(1) - 1
    r = _token_shift_block(rc, rp, rn, mu_r[...], first, last)
    k = _token_shift_block(kc, kp, kn, mu_k[...], first, last)
    v = _token_shift_block(vc, vp, vn, mu_v[...], first, last)
    lora = _token_shift_block(lc, lp, ln, mu_l[...], first, last)
    r_o[0] = r
    v_o[0] = v

    kk = k * k_k[...]
    norm = jnp.maximum(jnp.sqrt(_head_sum(kk * kk, hsum[...])), 1e-12)
    kk = kk * _head_widen(1.0 / norm, hwide[...])
    kk_o[0] = kk

    k_sum = None
    for d, (lw_o, kd_o, kka_o) in enumerate(((lwf_o, kf_o, kkaf_o), (lwb_o, kb_o, kkab_o))):
        wd = lora[:, d * LORA_W:(d + 1) * LORA_W]
        ad = lora[:, 2 * LORA_W + d * LORA_A:2 * LORA_W + (d + 1) * LORA_A]
        w_log = -jax.nn.softplus(-(w0[d:d + 1, :] + _mm(jnp.tanh(wd), w2[d]))) - 0.5
        lw_o[0] = -jnp.exp(w_log)
        a = jax.nn.sigmoid(a0[d:d + 1, :] + _mm(ad, a2[d]))
        k_dir = k * (1.0 + (a - 1.0) * k_a[...])
        kd_o[0] = k_dir
        kka_o[0] = kk * a
        k_sum = k_dir if k_sum is None else k_sum + k_dir

    gd = lora[:, 2 * LORA_W + 2 * LORA_A:2 * LORA_W + 2 * LORA_A + LORA_G]
    gate = _mm(jax.nn.sigmoid(gd), g2[...])
    bonus = _head_widen(_head_sum(r * k_sum * r_k[...], hsum[...]), hwide[...]) * v
    gate_o[0] = gate
    bonus_o[0] = (ln_b[...] + bonus) * gate


def _rw_pre(r, k, v, lora_dt, mu, w0, w2, a0, a2, g2, k_k, k_a, r_k, ln_b):
    bsz, t, width = r.shape
    tm = RW_TILE_T
    nt = t // tm
    blocks_per_tile = tm // SUBLANES
    last_block = t // SUBLANES - 1
    lcols = lora_dt.shape[-1]

    def specs(cols):
        return [pl.BlockSpec((1, tm, cols), lambda b, i: (b, i, 0)),
                pl.BlockSpec((1, SUBLANES, cols), lambda b, i: (b, jnp.maximum(i * blocks_per_tile - 1, 0), 0)),
                pl.BlockSpec((1, SUBLANES, cols), lambda b, i: (b, jnp.minimum((i + 1) * blocks_per_tile, last_block), 0))]

    def whole(a):
        return pl.BlockSpec(a.shape, lambda b, i: (0,) * a.ndim)

    row = lambda a: a.reshape(1, -1).astype(_F32)
    mu_l = jnp.pad(mu[RW_SPLITS[2]:], (0, lcols - LORA_COLS))
    hsum, hwide = _head_matrices()
    params = [row(mu[:RW_SPLITS[0]]), row(mu[RW_SPLITS[0]:RW_SPLITS[1]]), row(mu[RW_SPLITS[1]:RW_SPLITS[2]]), row(mu_l),
              w0.astype(_F32), a0.astype(_F32), row(k_k), row(k_a), row(r_k), row(ln_b),
              w2.astype(_BF16), a2.astype(_BF16), g2.astype(_BF16), hsum, hwide]
    out_spec = pl.BlockSpec((1, tm, width), lambda b, i: (b, i, 0))
    out_shape = jax.ShapeDtypeStruct((bsz, t, width), _F32)
    return pl.pallas_call(
        _rw_pre_kernel,
        grid=(bsz, nt),
        in_specs=specs(width) * 3 + specs(lcols) + [whole(p) for p in params],
        out_specs=[out_spec] * 11,
        out_shape=[out_shape] * 11,
        compiler_params=pltpu.CompilerParams(dimension_semantics=("parallel", "parallel"), vmem_limit_bytes=56 * 2**20),
        name="rw_pre",
    )(r, r, r, k, k, k, v, v, v, lora_dt, lora_dt, lora_dt, *params)


def _rw_post_kernel(yf, yb, gate, bonus, ln_w, hsum, hwide, o_ref):
    y = yf[0] + yb[0]
    mean = _head_widen(_head_sum(y, hsum[...]), hwide[...]) * (1.0 / RW_HEAD)
    cen = y - mean
    var = _head_widen(_head_sum(cen * cen, hsum[...]), hwide[...]) * (1.0 / RW_HEAD)
    o_ref[0] = (cen * lax.rsqrt(var + LN_X_EPS) * ln_w[...] * gate[0] + bonus[0]).astype(o_ref.dtype)


def _rw_post(y_f, y_b, gate, bonus, ln_w):
    bsz, t, width = y_f.shape
    tm = RW_TILE_T
    hsum, hwide = _head_matrices()
    spec = pl.BlockSpec((1, tm, width), lambda b, i: (b, i, 0))
    whole = lambda a: pl.BlockSpec(a.shape, lambda b, i: (0,) * a.ndim)
    ln_w = ln_w.reshape(1, width).astype(_F32)
    return pl.pallas_call(
        _rw_post_kernel,
        grid=(bsz, t // tm),
        in_specs=[spec] * 4 + [whole(ln_w), whole(hsum), whole(hwide)],
        out_specs=spec,
        out_shape=jax.ShapeDtypeStruct((bsz, t, width), _BF16),
        compiler_params=pltpu.CompilerParams(dimension_semantics=("parallel", "parallel")),
        name="rw_post",
    )(y_f, y_b, gate, bonus, ln_w, hsum, hwide)


def _rwkv7_branch(r, k, v, lora_dt, mu, s_fwd, s_bwd, w0, w2, a0, a2, g2, k_k, k_a, r_k, ln_w, ln_b, want_output):
    r, v, kk, lw_f, lw_b, k_f, k_b, kka_f, kka_b, gate, bonus = _rw_pre(
        r, k, v, lora_dt, mu, w0, w2, a0, a2, g2, k_k, k_a, r_k, ln_b)
    y_f, s_fwd = _wkv_scan(r, lw_f, k_f, v, kk, kka_f, s_fwd, reverse=False)
    y_b, s_bwd = _wkv_scan(r, lw_b, k_b, v, kk, kka_b, s_bwd, reverse=True)
    out = _rw_post(y_f, y_b, gate, bonus, ln_w) if want_output else None
    return out, s_fwd, s_bwd


def _centred_depthwise_conv(u, w):
    ch = u.shape[-1]
    return lax.conv_general_dilated(u, w.astype(_F32)[:, None, :], window_strides=(1,), padding=[(SSM_CONV // 2, SSM_CONV // 2)], dimension_numbers=('NWC', 'WIO', 'NWC'), feature_group_count=ch)


def _split3(x):
    hi = x.astype(_BF16)
    rem = x - hi.astype(_F32)
    mid = rem.astype(_BF16)
    lo = (rem - mid.astype(_F32)).astype(_BF16)
    return hi, mid, lo


def _ssd_kernel(xs_ref, bm_ref, cm_ref, la_ref, lat_ref, dt_ref, h0_ref, y_ref, h_ref, *, reverse):
    L = SSM_CHUNK
    P = SSM_HEADDIM

    @pl.when(pl.program_id(1) == 0)
    def _():
        h_ref[...] = h0_ref[...]

    ti = lax.broadcasted_iota(jnp.int32, (L, L), 0)
    tj = lax.broadcasted_iota(jnp.int32, (L, L), 1)
    before_incl = (ti <= tj) if reverse else (ti >= tj)
    cum = jnp.where(before_incl, 1.0, 0.0).astype(_BF16)
    cum_t = jnp.where(before_incl, 0.0, 1.0).astype(_BF16) + jnp.where(ti == tj, 1.0, 0.0).astype(_BF16)

    la = la_ref[0]
    cs = sum(jnp.dot(cum, part, preferred_element_type=_F32) for part in _split3(la))
    cs_t = sum(jnp.dot(part, cum_t, preferred_element_type=_F32) for part in _split3(lat_ref[0]))
    tot = cs[0:1] if reverse else cs[L - 1:L]
    dt = dt_ref[0]

    hi = lax.broadcasted_iota(jnp.int32, (SSM_HEADS, SSM_WIDTH), 0)
    hj = lax.broadcasted_iota(jnp.int32, (SSM_HEADS, SSM_WIDTH), 1)
    lo_edge = hi * P
    widen = jnp.where((hj >= lo_edge) & (hj < lo_edge + P), 1.0, 0.0).astype(_BF16)

    def wide(cols):
        return sum(jnp.dot(part, widen, preferred_element_type=_F32) for part in _split3(cols))

    xs = xs_ref[0]
    xdt = xs * wide(dt)
    xdt_end = (xs * wide(dt * jnp.exp(tot - cs))).astype(_BF16)
    xdt = xdt.astype(_BF16)
    decay_in = wide(jnp.exp(cs))
    e_tot = jnp.exp(tot)

    pieces = []
    for g in range(SSM_GROUPS):
        bm = bm_ref[0, :, g * SSM_STATE:(g + 1) * SSM_STATE].astype(_BF16)
        cm = cm_ref[0, :, g * SSM_STATE:(g + 1) * SSM_STATE].astype(_BF16)
        cb = lax.dot_general(cm, bm, _NT, preferred_element_type=_F32)
        for e in range(SSM_HPG):
            h = g * SSM_HPG + e
            cols = slice(h * P, (h + 1) * P)
            seg = jnp.where(before_incl, jnp.exp(cs[:, h:h + 1] - cs_t[h:h + 1, :]), 0.0)
            state = h_ref[0, h]
            y_h = jnp.dot((cb * seg).astype(_BF16), xdt[:, cols], preferred_element_type=_F32)
            y_h = y_h + lax.dot_general(cm, state.astype(_BF16), _NT, preferred_element_type=_F32) * decay_in[:, cols]
            pieces.append(y_h)
            new = lax.dot_general(xdt_end[:, cols], bm, _TN, preferred_element_type=_F32)
            h_ref[0, h] = state * e_tot[:, h:h + 1] + new
    y_ref[0] = jnp.concatenate(pieces, axis=1)


def _ssd_scan(xs, bm, cm, log_a, dt, h0, reverse):
    bsz, t, width = xs.shape
    nc = t // SSM_CHUNK
    cidx = (lambda c: nc - 1 - c) if reverse else (lambda c: c)
    seq = lambda w: pl.BlockSpec((1, SSM_CHUNK, w), lambda b, c: (b, cidx(c), 0))
    st_spec = pl.BlockSpec((1,) + h0.shape[1:], lambda b, c: (b, 0, 0, 0))
    gn = SSM_GROUPS * SSM_STATE
    return pl.pallas_call(
        functools.partial(_ssd_kernel, reverse=reverse),
        grid=(bsz, nc),
        in_specs=[seq(width), seq(gn), seq(gn), seq(SSM_HEADS),
                  pl.BlockSpec((1, SSM_HEADS, SSM_CHUNK), lambda b, c: (b, 0, cidx(c))), seq(SSM_HEADS), st_spec],
        out_specs=[seq(width), st_spec],
        out_shape=[jax.ShapeDtypeStruct((bsz, t, width), _F32), jax.ShapeDtypeStruct(h0.shape, _F32)],
        compiler_params=pltpu.CompilerParams(dimension_semantics=("parallel", "arbitrary")),
        name="ssd_rev" if reverse else "ssd_fwd",
    )(xs, bm, cm, log_a, jnp.swapaxes(log_a, 1, 2), dt, h0)


def _mamba2_branch(xbc, dt_raw, h0_fwd, h0_bwd, conv_w, conv_b, dt_bias, a_log, d_skip):
    bsz, t, _ = xbc.shape
    xbc = jax.nn.silu(_centred_depthwise_conv(xbc.astype(_F32), conv_w) + conv_b)
    xs, bm, cm = jnp.split(xbc, (SSM_WIDTH, SSM_WIDTH + SSM_GROUPS * SSM_STATE), axis=-1)
    dt = jax.nn.softplus(dt_raw.astype(_F32).reshape(bsz, t, 2, SSM_HEADS) + dt_bias)
    log_a = -jnp.exp(a_log) * dt
    heads = lambda h: h.reshape(bsz, SSM_HEADS, SSM_HEADDIM, SSM_STATE)
    y_f, h_f = _ssd_scan(xs, bm, cm, log_a[:, :, 0], dt[:, :, 0], heads(h0_fwd), reverse=False)
    y_b, h_b = _ssd_scan(xs, bm, cm, log_a[:, :, 1], dt[:, :, 1], heads(h0_bwd), reverse=True)
    y = y_f + y_b + jnp.repeat(d_skip, SSM_HEADDIM) * xs
    return y, h_f, h_b


def _gated_group_rmsnorm(y, z, gain):
    bsz, t, _ = y.shape
    u = (y * jax.nn.silu(z.astype(_F32))).reshape(bsz, t, SSM_GROUPS, SSM_WIDTH // SSM_GROUPS)
    u = u * lax.rsqrt(jnp.mean(u * u, axis=-1, keepdims=True) + NORM_EPS)
    return u.reshape(bsz, t, SSM_WIDTH) * gain


def _merge_branches(y_a, y_b, gates, w_a, w_b, w_o):
    bsz, t, d = y_a.shape
    mm = lambda u, w: _matmul(u.reshape(bsz * t, -1).astype(_BF16), w.astype(_BF16)).reshape(bsz, t, -1)
    g_a, g_b = jnp.split(gates, 2, axis=-1)
    m = jax.nn.sigmoid(g_a) * mm(y_a, w_a) + jax.nn.sigmoid(g_b) * mm(y_b, w_b)
    return mm(m, w_o)


PEER_ROUTE_TOKENS = 256
PEER_TOKEN_BLOCK = 512
PEER_EXPERT_BLOCK = 1024
PEER_KEY_GROUP = 4
SUBLANES = 8


def _top_values(x, count):
    rows = lax.broadcasted_iota(jnp.int32, x.shape, 0)
    rank = jnp.full(x.shape, float(count), _F32)
    vals = []
    for it in range(count):
        m = jnp.max(x, axis=0, keepdims=True)
        vals.append(m)
        first = jnp.min(jnp.where(x == m, rows, x.shape[0]), axis=0, keepdims=True)
        hit = rows == first
        rank = jnp.where(hit, float(it), rank)
        x = jnp.where(hit, -jnp.inf, x)
    return vals, rank, x


def _peer_route_kernel(h_ref, wqt_ref, sub_ref, cnt1_ref, e1_ref, rank2_ref, e2_ref):
    qt = lax.dot_general(wqt_ref[...], h_ref[...].astype(_BF16), _NT, preferred_element_type=_F32)
    for h in range(PEER_HEADS):
        sc = []
        for s in range(2):
            lo = (2 * h + s) * PEER_HALF
            sc.append(jnp.dot(sub_ref[2 * h + s], qt[lo:lo + PEER_HALF].astype(_BF16), preferred_element_type=_F32))
        top_a, rank_a, _ = _top_values(sc[0], PEER_TOPK)
        top_b, rank_b, _ = _top_values(sc[1], PEER_TOPK)
        top_b = jnp.concatenate(top_b, axis=0)
        width = [PEER_TOPK // (i + 1) for i in range(PEER_TOPK)]
        cand = jnp.concatenate([top_a[i] + top_b[:width[i]] for i in range(PEER_TOPK)], axis=0)
        best, _, left = _top_values(cand, PEER_TOPK)
        taken = jnp.where(left == cand, 0.0, 1.0)
        norm = best[0] * 0.0
        for val in best:
            norm = norm + jnp.exp(val - best[0])
        cnt1 = jnp.zeros_like(rank_a)
        start = 0
        for i in range(PEER_TOPK):
            used = jnp.sum(taken[start:start + width[i]], axis=0, keepdims=True)
            cnt1 = jnp.where(rank_a == float(i), used, cnt1)
            start += width[i]
        cnt1_ref[h] = cnt1.astype(cnt1_ref.dtype)
        rank2_ref[h] = rank_b.astype(rank2_ref.dtype)
        e1_ref[h] = (jnp.exp(sc[0] - top_a[0]) / norm).astype(e1_ref.dtype)
        e2_ref[h] = jnp.exp(sc[1] - top_b[0:1]).astype(e2_ref.dtype)


def _peer_route(tok, wq, subkeys):
    n, d = tok.shape
    tr = PEER_ROUTE_TOKENS
    wqt = wq.T.astype(_BF16)
    sub = subkeys.reshape(PEER_HEADS * 2, PEER_KEYS, PEER_HALF).astype(_BF16)
    key_spec = pl.BlockSpec((PEER_HEADS, PEER_KEYS, tr), lambda i: (0, 0, i))
    row_shape = jax.ShapeDtypeStruct((PEER_HEADS, PEER_KEYS, n), _F32)
    tile_shape = jax.ShapeDtypeStruct((PEER_HEADS, PEER_KEYS, n), _BF16)
    return pl.pallas_call(
        _peer_route_kernel,
        grid=(n // tr,),
        in_specs=[pl.BlockSpec((tr, d), lambda i: (i, 0)),
                  pl.BlockSpec(wqt.shape, lambda i: (0, 0)),
                  pl.BlockSpec(sub.shape, lambda i: (0, 0, 0))],
        out_specs=[key_spec] * 4,
        out_shape=[row_shape, row_shape, tile_shape, tile_shape],
        compiler_params=pltpu.CompilerParams(dimension_semantics=("parallel",), vmem_limit_bytes=48 * 2**20),
        name="peer_route",
    )(tok, wqt, sub)


def _gelu_exact(x):
    return 0.5 * x * (1.0 + lax.erf(x * (1.0 / math.sqrt(2.0))))


def _peer_gate_block(a_ref, w_ref, b_ref, cnt1_ref, e1_ref, rank2_ref, e2_ref, block):
    keys_per_block = PEER_EXPERT_BLOCK // PEER_KEYS
    rt = 2 * SUBLANES
    tb = a_ref.shape[1]
    for h in range(PEER_HEADS):
        for j in range(keys_per_block):
            i1 = block * keys_per_block + j
            b_ref[0, j * PEER_HEADS + h] = jnp.broadcast_to(cnt1_ref[h, pl.ds(i1, 1), :], (rt, tb)).astype(_BF16)
            b_ref[1, j * PEER_HEADS + h] = jnp.broadcast_to(e1_ref[h, pl.ds(i1, 1), :], (rt, tb)).astype(_BF16)

    def row_tile(tile, carry):
        r0 = pl.multiple_of(tile * rt, rt)
        for j0 in range(0, keys_per_block, PEER_KEY_GROUP):
            group = range(j0, j0 + PEER_KEY_GROUP)
            gates = {j: None for j in group}
            for h in range(PEER_HEADS):
                rank2 = rank2_ref[h, pl.ds(r0, rt), :]
                e2 = e2_ref[h, pl.ds(r0, rt), :]
                for j in group:
                    term = jnp.where(rank2 < b_ref[0, j * PEER_HEADS + h], b_ref[1, j * PEER_HEADS + h] * e2, jnp.zeros_like(e2))
                    gates[j] = term if gates[j] is None else gates[j] + term
            for j in group:
                rows = pl.ds(pl.multiple_of(j * PEER_KEYS + r0, rt), rt)
                w_ref[rows, :] = gates[j] * _gelu_exact(a_ref[rows, :]).astype(_BF16)
        return carry

    lax.fori_loop(0, PEER_KEYS // rt, row_tile, 0)


def _peer_expert_kernel(x_ref, u_ref, v_ref, cnt1_ref, e1_ref, rank2_ref, e2_ref, o_ref, a_scr, w_scr, b_scr):
    eb = pl.program_id(1)

    @pl.when(eb == 0)
    def _():
        o_ref[...] = jnp.zeros_like(o_ref)

    a_scr[...] = lax.dot_general(u_ref[...], x_ref[...], _NT, preferred_element_type=_F32)
    _peer_gate_block(a_scr, w_scr, b_scr, cnt1_ref, e1_ref, rank2_ref, e2_ref, eb)
    o_ref[...] += lax.dot_general(w_scr[...], v_ref[...], _TN, preferred_element_type=_F32)


def _peer_ffn(h, wq, subkeys, u_tab, v_tab):
    bsz, t, d = h.shape
    n = bsz * t
    tok = h.reshape(n, d)
    routing = _peer_route(tok, wq, subkeys)
    tb, eb = PEER_TOKEN_BLOCK, PEER_EXPERT_BLOCK
    n_experts = u_tab.shape[0]
    key_spec = pl.BlockSpec((PEER_HEADS, PEER_KEYS, tb), lambda i, e: (0, 0, i))
    tab_spec = pl.BlockSpec((eb, d), lambda i, e: (e, 0))
    out = pl.pallas_call(
        _peer_expert_kernel,
        grid=(n // tb, n_experts // eb),
        in_specs=[pl.BlockSpec((tb, d), lambda i, e: (i, 0)), tab_spec, tab_spec] + [key_spec] * 4,
        out_specs=pl.BlockSpec((tb, d), lambda i, e: (i, 0)),
        out_shape=jax.ShapeDtypeStruct((n, d), _F32),
        scratch_shapes=[pltpu.VMEM((eb, tb), _F32), pltpu.VMEM((eb, tb), _BF16),
                        pltpu.VMEM((2, (eb // PEER_KEYS) * PEER_HEADS, 2 * SUBLANES, tb), _BF16)],
        compiler_params=pltpu.CompilerParams(dimension_semantics=("parallel", "arbitrary"), vmem_limit_bytes=56 * 2**20),
        name="peer_experts",
    )(tok, u_tab.astype(_BF16), v_tab.astype(_BF16), *routing)
    return out.reshape(bsz, t, d)


def kernel(x, c, ctx, c_ctx, w_mod, b_mod, norm_pre1, norm_post1, norm_pre2, norm_post2, w_in, rw_mu, rw_w0, rw_w2, rw_a0, rw_a2, rw_g2, rw_k_k, rw_k_a, rw_r_k, rw_ln_w, rw_ln_b, ssm_conv_w, ssm_conv_b, ssm_dt_bias, ssm_a_log, ssm_d, ssm_norm_w, w_branch_a, w_branch_b, w_out, peer_wq, peer_subkeys, peer_u, peer_v):
    act_dtype = x.dtype
    bsz, seq, _ = x.shape
    rows = seq // GRID_W
    depth = w_mod.shape[0]
    assert depth == 1
    rw_zero = jnp.zeros((bsz, RW_WIDTH // LANES, LANES, LANES), _F32)
    ssm_zero = jnp.zeros((bsz, SSM_GROUPS, SSM_HPG, SSM_HEADDIM, SSM_STATE), _F32)
    l = 0
    mod_x = (jax.nn.silu(c) @ w_mod[l] + b_mod[l])[:, None, :]
    mod_c = (jax.nn.silu(c_ctx) @ w_mod[l] + b_mod[l])[None, None, :]
    sh1x, sc1x, g1x, sh2x, sc2x, g2x = jnp.split(mod_x, N_MOD, axis=-1)
    sh1c, sc1c, g1c, sh2c, sc2c, g2c = jnp.split(mod_c, N_MOD, axis=-1)

    in_w = _in_proj_weights(w_in[l])
    hx = _norm_mod(x, norm_pre1[l], sc1x, sh1x)
    r_x, k_x, v_x, lora_x, z_x, xbc_x, dt_x, gate_x = _in_proj(hx, in_w, h_ssm=_raster_to_column(hx, rows))
    r_c, k_c, v_c, lora_c, z_c, xbc_c, dt_c, gate_c = _in_proj(_norm_mod(ctx, norm_pre1[l], sc1c, sh1c), in_w)

    rw_params = (rw_w0[l], rw_w2[l], rw_a0[l], rw_a2[l], rw_g2[l], rw_k_k[l], rw_k_a[l], rw_r_k[l], rw_ln_w[l], rw_ln_b[l])
    _, s_fwd, s_bwd = _rwkv7_branch(r_c, k_c, v_c, lora_c, rw_mu[l], rw_zero, rw_zero, *rw_params, want_output=False)
    ya_x, _, _ = _rwkv7_branch(r_x, k_x, v_x, lora_x, rw_mu[l], s_fwd, s_bwd, *rw_params, want_output=True)

    ssm_params = (ssm_conv_w[l], ssm_conv_b[l], ssm_dt_bias[l], ssm_a_log[l], ssm_d[l])
    yb_c, h_fwd, h_bwd = _mamba2_branch(xbc_c, dt_c, ssm_zero, ssm_zero, *ssm_params)
    yb_x, _, _ = _mamba2_branch(xbc_x, dt_x, h_fwd, h_bwd, *ssm_params)
    yb_x = _column_to_raster(_gated_group_rmsnorm(yb_x, z_x, ssm_norm_w[l]).astype(_BF16), rows)

    mix_x = _merge_branches(ya_x, yb_x, gate_x, w_branch_a[l], w_branch_b[l], w_out[l])
    x, h2x = _residual(x, mix_x, g1x, norm_post1[l], next_mod=(norm_pre2[l], sc2x, sh2x))
    return _residual(x, _peer_ffn(h2x, peer_wq[l], peer_subkeys[l], peer_u[l], peer_v[l]), g2x, norm_post2[l])
```

```python
import functools
import math

import jax
import jax.numpy as jnp
from jax import lax
from jax.experimental import pallas as pl
from jax.experimental.pallas import tpu as pltpu

D_MODEL = 2048
GRID_W = 64
N_MOD = 6
NORM_EPS = 1e-6
RW_HEAD = 64
RW_WIDTH = D_MODEL
RW_HEADS = RW_WIDTH // RW_HEAD
LORA_W = 96
LORA_A = 96
LORA_G = 256
LN_X_EPS = 64e-5
SSM_WIDTH = D_MODEL
SSM_HEADDIM = 64
SSM_HEADS = SSM_WIDTH // SSM_HEADDIM
SSM_GROUPS = 8
SSM_HPG = SSM_HEADS // SSM_GROUPS
SSM_STATE = 128
SSM_CONV = 5
SSM_CHUNK = 128
PEER_HEADS = 8
PEER_KEYS = 128
PEER_TOPK = 16
PEER_QDIM = 256
PEER_HALF = PEER_QDIM // 2
PEER_BLOCK = 128
RW_COLS = 3 * RW_WIDTH + 2 * LORA_W + 2 * LORA_A + LORA_G
XBC_COLS = SSM_WIDTH + 2 * SSM_GROUPS * SSM_STATE
IN_SPLITS = (RW_COLS, RW_COLS + SSM_WIDTH, RW_COLS + SSM_WIDTH + XBC_COLS, RW_COLS + SSM_WIDTH + XBC_COLS + 2 * SSM_HEADS)
RW_SPLITS = (RW_WIDTH, 2 * RW_WIDTH, 3 * RW_WIDTH, 3 * RW_WIDTH + LORA_W, 3 * RW_WIDTH + 2 * LORA_W, 3 * RW_WIDTH + 2 * LORA_W + LORA_A, 3 * RW_WIDTH + 2 * LORA_W + 2 * LORA_A)

LANES = 128
WKV_CHUNK = 64
WKV_PAIR = LANES // RW_HEAD
WKV_GROUPS_PER_STEP = 8

_F32 = jnp.float32
_BF16 = jnp.bfloat16
_NT = (((1,), (1,)), ((), ()))
_TN = (((0,), (0,)), ((), ()))


def _mm(a, b, dims=None):
    a = a.astype(_BF16)
    b = b.astype(_BF16)
    if dims is None:
        return jnp.dot(a, b, preferred_element_type=_F32)
    return lax.dot_general(a, b, dims, preferred_element_type=_F32)


def _wkv_kernel(r_ref, lw_ref, k_ref, v_ref, a_ref, b_ref, s0_ref, y_ref, s_ref, *, reverse, chunk, groups):
    L = chunk
    L2 = WKV_PAIR * L

    @pl.when(pl.program_id(2) == 0)
    def _():
        s_ref[...] = s0_ref[...]

    ti = lax.broadcasted_iota(jnp.int32, (L, L), 0)
    tj = lax.broadcasted_iota(jnp.int32, (L, L), 1)
    before_incl = (ti <= tj) if reverse else (ti >= tj)
    cum = jnp.where(before_incl, 1.0, 0.0).astype(_BF16)

    lw = lw_ref[0]
    hi = lw.astype(_BF16)
    rem = lw - hi.astype(_F32)
    mid = rem.astype(_BF16)
    lo = (rem - mid.astype(_F32)).astype(_BF16)
    cs = (jnp.dot(cum, hi, preferred_element_type=_F32)
          + jnp.dot(cum, mid, preferred_element_type=_F32)
          + jnp.dot(cum, lo, preferred_element_type=_F32))
    tot = cs[0:1] if reverse else cs[L - 1:L]
    half = 0.5 * tot
    e_pos = jnp.exp(cs - half)
    e_neg = jnp.exp(half - cs)
    e_prev = jnp.exp(cs - lw - half)
    e_end = jnp.exp(tot - cs)
    e_half = jnp.exp(half)
    e_tot = jnp.exp(tot)

    rt = r_ref[0] * e_pos
    at = a_ref[0] * e_prev
    kt = k_ref[0] * e_neg
    bt = b_ref[0] * e_neg
    ke = k_ref[0] * e_end
    be = b_ref[0] * e_end
    vv = v_ref[0]

    lane = lax.broadcasted_iota(jnp.int32, (L, LANES), 1)
    first_head = lane < RW_HEAD

    def stack(x):
        return jnp.concatenate([jnp.where(first_head, x, 0.0), jnp.where(first_head, 0.0, x)], axis=0)

    si = lax.broadcasted_iota(jnp.int32, (L2, L2), 0)
    sj = lax.broadcasted_iota(jnp.int32, (L2, L2), 1)
    same_head = (si < L) == (sj < L)
    pi = jnp.where(si < L, si, si - L)
    pj = jnp.where(sj < L, sj, sj - L)
    incl2 = same_head & ((pi <= pj) if reverse else (pi >= pj))
    strict2 = same_head & ((pi < pj) if reverse else (pi > pj))
    eye2 = jnp.where(si == sj, 1.0, 0.0)

    gs = range(groups)
    sls = [slice(g * LANES, (g + 1) * LANES) for g in gs]
    v_s = [stack(vv[:, sl]).astype(_BF16) for sl in sls]
    lhs = [jnp.concatenate([stack(at[:, sl]), stack(rt[:, sl])], axis=0).astype(_BF16) for sl in sls]
    rhs = [jnp.concatenate([stack(kt[:, sl]), stack(bt[:, sl])], axis=0).astype(_BF16) for sl in sls]
    state = [s_ref[0, g] for g in gs]
    scores = [_mm(lhs[g], rhs[g], _NT) for g in gs]
    from_state = [_mm(lhs[g], state[g] * e_half[:, sls[g]], _NT) for g in gs]

    power = [jnp.where(strict2, -scores[g][:L2, L2:], 0.0).astype(_BF16) for g in gs]
    inv = [eye2 + power[g] for g in gs]
    ak_v = [_mm(jnp.where(strict2, scores[g][:L2, :L2], 0.0), v_s[g]) for g in gs]
    for _ in range(int(math.log2(L)) - 1):
        power = [_mm(power[g], power[g]).astype(_BF16) for g in gs]
        inv = [inv[g] + _mm(inv[g], power[g]) for g in gs]
    u_s = [_mm(inv[g], from_state[g][:L2] + ak_v[g]) for g in gs]

    vu = [jnp.concatenate([v_s[g], u_s[g].astype(_BF16)], axis=0) for g in gs]
    for g in gs:
        r_kb = jnp.concatenate([jnp.where(incl2, scores[g][L2:, :L2], 0.0),
                                jnp.where(incl2, -scores[g][L2:, L2:], 0.0)], axis=1)
        y_s = from_state[g][L2:] + _mm(r_kb, vu[g])
        y_ref[0, :, sls[g]] = y_s[:L] + y_s[L:]
    for g in gs:
        kb = jnp.concatenate([stack(ke[:, sls[g]]), -stack(be[:, sls[g]])], axis=0)
        s_ref[0, g] = state[g] * e_tot[:, sls[g]] + _mm(vu[g], kb, _TN)


def _wkv_scan(r, lw, k, v, a, b, s0, reverse):
    bsz, t, width = r.shape
    groups = WKV_GROUPS_PER_STEP
    assert t % WKV_CHUNK == 0 and width % (groups * LANES) == 0
    nc = t // WKV_CHUNK
    ngroup_steps = width // (groups * LANES)
    cidx = (lambda c: nc - 1 - c) if reverse else (lambda c: c)
    seq_spec = pl.BlockSpec((1, WKV_CHUNK, groups * LANES), lambda bi, gi, c: (bi, cidx(c), gi))
    st_spec = pl.BlockSpec((1, groups, LANES, LANES), lambda bi, gi, c: (bi, gi, 0, 0))
    return pl.pallas_call(
        functools.partial(_wkv_kernel, reverse=reverse, chunk=WKV_CHUNK, groups=groups),
        grid=(bsz, ngroup_steps, nc),
        in_specs=[seq_spec] * 6 + [st_spec],
        out_specs=[seq_spec, st_spec],
        out_shape=[jax.ShapeDtypeStruct((bsz, t, width), _F32), jax.ShapeDtypeStruct(s0.shape, _F32)],
        compiler_params=pltpu.CompilerParams(dimension_semantics=("parallel", "parallel", "arbitrary")),
        name="wkv7_rev" if reverse else "wkv7_fwd",
    )(r, lw, k, v, a, b, s0)


MM_TILE_M = 1024
MM_TILE_N = 1024
NORM_TILE_M = 512
SMALL_COLS = 768


def _matmul_kernel(a_ref, w_ref, o_ref):
    o_ref[...] = jnp.dot(a_ref[...], w_ref[...], preferred_element_type=_F32)


def _matmul(a, w):
    m, k = a.shape
    n = w.shape[1]
    tm, tn = min(m, MM_TILE_M), min(n, MM_TILE_N)
    assert m % tm == 0 and n % tn == 0 and a.dtype == _BF16 and w.dtype == _BF16
    return pl.pallas_call(
        _matmul_kernel,
        grid=(n // tn, m // tm),
        in_specs=[pl.BlockSpec((tm, k), lambda j, i: (i, 0)), pl.BlockSpec((k, tn), lambda j, i: (0, j))],
        out_specs=pl.BlockSpec((tm, tn), lambda j, i: (i, j)),
        out_shape=jax.ShapeDtypeStruct((m, n), _F32),
        compiler_params=pltpu.CompilerParams(dimension_semantics=("parallel", "parallel"), vmem_limit_bytes=48 * 2**20),
        name="matmul",
    )(a, w)


def _norm_mod_kernel(x_ref, gain_ref, sc_ref, sh_ref, o_ref):
    x = x_ref[0]
    inv = lax.rsqrt(jnp.mean(x * x, axis=-1, keepdims=True) + NORM_EPS)
    o_ref[0] = ((x * inv * gain_ref[...]) * (1.0 + sc_ref[0]) + sh_ref[0]).astype(o_ref.dtype)


def _norm_mod(x, gain, scale, shift):
    bsz, t, d = x.shape
    tm = min(t, NORM_TILE_M)
    mod_spec = pl.BlockSpec((1, 1, d), (lambda b, i: (b, 0, 0)) if scale.shape[0] == bsz else (lambda b, i: (0, 0, 0)))
    return pl.pallas_call(
        _norm_mod_kernel,
        grid=(bsz, t // tm),
        in_specs=[pl.BlockSpec((1, tm, d), lambda b, i: (b, i, 0)), pl.BlockSpec((1, d), lambda b, i: (0, 0)), mod_spec, mod_spec],
        out_specs=pl.BlockSpec((1, tm, d), lambda b, i: (b, i, 0)),
        out_shape=jax.ShapeDtypeStruct((bsz, t, d), _BF16),
        compiler_params=pltpu.CompilerParams(dimension_semantics=("parallel", "parallel")),
        name="norm_mod",
    )(x, gain.reshape(1, d), scale, shift)


def _residual_kernel(x_ref, u_ref, g_ref, gain_ref, *rest, with_next):
    u = u_ref[0]
    inv = lax.rsqrt(jnp.mean(u * u, axis=-1, keepdims=True) + NORM_EPS)
    y = x_ref[0] + g_ref[0] * (u * inv * gain_ref[...])
    if not with_next:
        rest[0][0] = y
        return
    gain2_ref, sc_ref, sh_ref, y_ref, h_ref = rest
    y_ref[0] = y
    inv2 = lax.rsqrt(jnp.mean(y * y, axis=-1, keepdims=True) + NORM_EPS)
    h_ref[0] = ((y * inv2 * gain2_ref[...]) * (1.0 + sc_ref[0]) + sh_ref[0]).astype(h_ref.dtype)


def _residual(x, u, g, gain, next_mod=None):
    bsz, t, d = x.shape
    tm = min(t, NORM_TILE_M)
    seq = pl.BlockSpec((1, tm, d), lambda b, i: (b, i, 0))
    per_batch = pl.BlockSpec((1, 1, d), lambda b, i: (b, 0, 0))
    row = pl.BlockSpec((1, d), lambda b, i: (0, 0))
    args, in_specs = [x, u, g, gain.reshape(1, d)], [seq, seq, per_batch, row]
    out_specs, out_shape = [seq], [jax.ShapeDtypeStruct((bsz, t, d), _F32)]
    if next_mod is not None:
        gain2, scale, shift = next_mod
        args += [gain2.reshape(1, d), scale, shift]
        in_specs += [row, per_batch, per_batch]
        out_specs.append(seq)
        out_shape.append(jax.ShapeDtypeStruct((bsz, t, d), _BF16))
    out = pl.pallas_call(
        functools.partial(_residual_kernel, with_next=next_mod is not None),
        grid=(bsz, t // tm),
        in_specs=in_specs,
        out_specs=out_specs,
        out_shape=out_shape,
        compiler_params=pltpu.CompilerParams(dimension_semantics=("parallel", "parallel")),
        name="residual",
    )(*args)
    return out if next_mod is not None else out[0]


def _in_proj_weights(w):
    rw_end, z_end, xbc_end, dt_end = IN_SPLITS
    lora_dt = jnp.concatenate([w[:, RW_SPLITS[2]:rw_end], w[:, xbc_end:dt_end]], axis=1)
    lora_dt = jnp.pad(lora_dt, ((0, 0), (0, SMALL_COLS - lora_dt.shape[1])))
    parts = [w[:, :RW_SPLITS[0]], w[:, RW_SPLITS[0]:RW_SPLITS[1]], w[:, RW_SPLITS[1]:RW_SPLITS[2]],
             w[:, rw_end:z_end], w[:, z_end:xbc_end], w[:, dt_end:], lora_dt]
    return [p.astype(_BF16) for p in parts]


def _in_proj(h, weights, h_ssm=None):
    bsz, t, d = h.shape
    mm = lambda u, w: _matmul(u.reshape(bsz * t, d), w).reshape(bsz, t, -1)
    w_r, w_k, w_v, w_z, w_xbc, w_gate, w_lora_dt = weights
    lora_dt = mm(h, w_lora_dt)
    dt_src = lora_dt if h_ssm is None else mm(h_ssm, w_lora_dt)
    h_ssm = h if h_ssm is None else h_ssm
    n_lora = RW_COLS - RW_SPLITS[2]
    return (mm(h, w_r), mm(h, w_k), mm(h, w_v), lora_dt, mm(h_ssm, w_z), mm(h_ssm, w_xbc),
            dt_src[..., n_lora:n_lora + 2 * SSM_HEADS], mm(h, w_gate))


def _raster_to_column(u, rows):
    b, s, ch = u.shape
    return u.reshape(b, rows, GRID_W, ch).transpose(0, 2, 1, 3).reshape(b, s, ch)


def _column_to_raster(u, rows):
    b, s, ch = u.shape
    return u.reshape(b, GRID_W, rows, ch).transpose(0, 2, 1, 3).reshape(b, s, ch)


RW_TILE_T = 128
LORA_COLS = RW_COLS - 3 * RW_WIDTH


def _head_matrices(width=RW_WIDTH, group=RW_HEAD):
    group_of_lane = jnp.arange(width) // group
    onehot = (group_of_lane[:, None] == jnp.arange(width // group)[None, :]).astype(_BF16)
    return onehot, onehot.T


def _head_sum(x, hsum):
    return sum(jnp.dot(part, hsum, preferred_element_type=_F32) for part in _split3(x))


def _head_widen(cols, hwide):
    return sum(jnp.dot(part, hwide, preferred_element_type=_F32) for part in _split3(cols))


def _token_shift_block(c_ref, p_ref, n_ref, mu, first, last):
    x = c_ref[0]
    tm = x.shape[0]
    row = lax.broadcasted_iota(jnp.int32, x.shape, 0)
    prev_row = jnp.where(first, 0.0, p_ref[0, SUBLANES - 1:SUBLANES, :])
    next_row = jnp.where(last, 0.0, n_ref[0, 0:1, :])
    prev = jnp.where(row == 0, prev_row, pltpu.roll(x, 1, 0))
    nxt = jnp.where(row == tm - 1, next_row, pltpu.roll(x, tm - 1, 0))
    return x + mu * (0.5 * (prev + nxt) - x)


def _rw_pre_kernel(rc, rp, rn, kc, kp, kn, vc, vp, vn, lc, lp, ln,
                   mu_r, mu_k, mu_v, mu_l, w0, a0, k_k, k_a, r_k, ln_b, w2, a2, g2, hsum, hwide,
                   r_o, v_o, kk_o, lwf_o, lwb_o, kf_o, kb_o, kkaf_o, kkab_o, gate_o, bonus_o):
    first = pl.program_id(1) == 0
    last = pl.program_id(1) == pl.num_programs(1) - 1
    r = _token_shift_block(rc, rp, rn, mu_r[...], first, last)
    k = _token_shift_block(kc, kp, kn, mu_k[...], first, last)
    v = _token_shift_block(vc, vp, vn, mu_v[...], first, last)
    lora = _token_shift_block(lc, lp, ln, mu_l[...], first, last)
    r_o[0] = r
    v_o[0] = v

    kk = k * k_k[...]
    norm = jnp.maximum(jnp.sqrt(_head_sum(kk * kk, hsum[...])), 1e-12)
    kk = kk * _head_widen(1.0 / norm, hwide[...])
    kk_o[0] = kk

    k_sum = None
    for d, (lw_o, kd_o, kka_o) in enumerate(((lwf_o, kf_o, kkaf_o), (lwb_o, kb_o, kkab_o))):
        wd = lora[:, d * LORA_W:(d + 1) * LORA_W]
        ad = lora[:, 2 * LORA_W + d * LORA_A:2 * LORA_W + (d + 1) * LORA_A]
        w_log = -jax.nn.softplus(-(w0[d:d + 1, :] + _mm(jnp.tanh(wd), w2[d]))) - 0.5
        lw_o[0] = -jnp.exp(w_log)
        a = jax.nn.sigmoid(a0[d:d + 1, :] + _mm(ad, a2[d]))
        k_dir = k * (1.0 + (a - 1.0) * k_a[...])
        kd_o[0] = k_dir
        kka_o[0] = kk * a
        k_sum = k_dir if k_sum is None else k_sum + k_dir

    gd = lora[:, 2 * LORA_W + 2 * LORA_A:2 * LORA_W + 2 * LORA_A + LORA_G]
    gate = _mm(jax.nn.sigmoid(gd), g2[...])
    bonus = _head_widen(_head_sum(r * k_sum * r_k[...], hsum[...]), hwide[...]) * v
    gate_o[0] = gate
    bonus_o[0] = (ln_b[...] + bonus) * gate


def _rw_pre(r, k, v, lora_dt, mu, w0, w2, a0, a2, g2, k_k, k_a, r_k, ln_b):
    bsz, t, width = r.shape
    tm = RW_TILE_T
    nt = t // tm
    blocks_per_tile = tm // SUBLANES
    last_block = t // SUBLANES - 1
    lcols = lora_dt.shape[-1]

    def specs(cols):
        return [pl.BlockSpec((1, tm, cols), lambda b, i: (b, i, 0)),
                pl.BlockSpec((1, SUBLANES, cols), lambda b, i: (b, jnp.maximum(i * blocks_per_tile - 1, 0), 0)),
                pl.BlockSpec((1, SUBLANES, cols), lambda b, i: (b, jnp.minimum((i + 1) * blocks_per_tile, last_block), 0))]

    def whole(a):
        return pl.BlockSpec(a.shape, lambda b, i: (0,) * a.ndim)

    row = lambda a: a.reshape(1, -1).astype(_F32)
    mu_l = jnp.pad(mu[RW_SPLITS[2]:], (0, lcols - LORA_COLS))
    hsum, hwide = _head_matrices()
    params = [row(mu[:RW_SPLITS[0]]), row(mu[RW_SPLITS[0]:RW_SPLITS[1]]), row(mu[RW_SPLITS[1]:RW_SPLITS[2]]), row(mu_l),
              w0.astype(_F32), a0.astype(_F32), row(k_k), row(k_a), row(r_k), row(ln_b),
              w2.astype(_BF16), a2.astype(_BF16), g2.astype(_BF16), hsum, hwide]
    out_spec = pl.BlockSpec((1, tm, width), lambda b, i: (b, i, 0))
    out_shape = jax.ShapeDtypeStruct((bsz, t, width), _F32)
    return pl.pallas_call(
        _rw_pre_kernel,
        grid=(bsz, nt),
        in_specs=specs(width) * 3 + specs(lcols) + [whole(p) for p in params],
        out_specs=[out_spec] * 11,
        out_shape=[out_shape] * 11,
        compiler_params=pltpu.CompilerParams(dimension_semantics=("parallel", "parallel"), vmem_limit_bytes=56 * 2**20),
        name="rw_pre",
    )(r, r, r, k, k, k, v, v, v, lora_dt, lora_dt, lora_dt, *params)


def _rw_post_kernel(yf, yb, gate, bonus, ln_w, hsum, hwide, o_ref):
    y = yf[0] + yb[0]
    mean = _head_widen(_head_sum(y, hsum[...]), hwide[...]) * (1.0 / RW_HEAD)
    cen = y - mean
    var = _head_widen(_head_sum(cen * cen, hsum[...]), hwide[...]) * (1.0 / RW_HEAD)
    o_ref[0] = (cen * lax.rsqrt(var + LN_X_EPS) * ln_w[...] * gate[0] + bonus[0]).astype(o_ref.dtype)


def _rw_post(y_f, y_b, gate, bonus, ln_w):
    bsz, t, width = y_f.shape
    tm = RW_TILE_T
    hsum, hwide = _head_matrices()
    spec = pl.BlockSpec((1, tm, width), lambda b, i: (b, i, 0))
    whole = lambda a: pl.BlockSpec(a.shape, lambda b, i: (0,) * a.ndim)
    ln_w = ln_w.reshape(1, width).astype(_F32)
    return pl.pallas_call(
        _rw_post_kernel,
        grid=(bsz, t // tm),
        in_specs=[spec] * 4 + [whole(ln_w), whole(hsum), whole(hwide)],
        out_specs=spec,
        out_shape=jax.ShapeDtypeStruct((bsz, t, width), _BF16),
        compiler_params=pltpu.CompilerParams(dimension_semantics=("parallel", "parallel")),
        name="rw_post",
    )(y_f, y_b, gate, bonus, ln_w, hsum, hwide)


def _rwkv7_branch(r, k, v, lora_dt, mu, s_fwd, s_bwd, w0, w2, a0, a2, g2, k_k, k_a, r_k, ln_w, ln_b, want_output):
    r, v, kk, lw_f, lw_b, k_f, k_b, kka_f, kka_b, gate, bonus = _rw_pre(
        r, k, v, lora_dt, mu, w0, w2, a0, a2, g2, k_k, k_a, r_k, ln_b)
    y_f, s_fwd = _wkv_scan(r, lw_f, k_f, v, kk, kka_f, s_fwd, reverse=False)
    y_b, s_bwd = _wkv_scan(r, lw_b, k_b, v, kk, kka_b, s_bwd, reverse=True)
    out = _rw_post(y_f, y_b, gate, bonus, ln_w) if want_output else None
    return out, s_fwd, s_bwd


def _split3(x):
    hi = x.astype(_BF16)
    rem = x - hi.astype(_F32)
    mid = rem.astype(_BF16)
    lo = (rem - mid.astype(_F32)).astype(_BF16)
    return hi, mid, lo


def _ssd_kernel(xs_ref, bm_ref, cm_ref, la_ref, lat_ref, dt_ref, h0_ref, y_ref, h_ref, *, reverse):
    L = SSM_CHUNK
    P = SSM_HEADDIM

    @pl.when(pl.program_id(1) == 0)
    def _():
        h_ref[...] = h0_ref[...]

    ti = lax.broadcasted_iota(jnp.int32, (L, L), 0)
    tj = lax.broadcasted_iota(jnp.int32, (L, L), 1)
    before_incl = (ti <= tj) if reverse else (ti >= tj)
    cum = jnp.where(before_incl, 1.0, 0.0).astype(_BF16)
    cum_t = jnp.where(before_incl, 0.0, 1.0).astype(_BF16) + jnp.where(ti == tj, 1.0, 0.0).astype(_BF16)

    la = la_ref[0]
    cs = sum(jnp.dot(cum, part, preferred_element_type=_F32) for part in _split3(la))
    cs_t = sum(jnp.dot(part, cum_t, preferred_element_type=_F32) for part in _split3(lat_ref[0]))
    tot = cs[0:1] if reverse else cs[L - 1:L]
    dt = dt_ref[0]

    hi = lax.broadcasted_iota(jnp.int32, (SSM_HEADS, SSM_WIDTH), 0)
    hj = lax.broadcasted_iota(jnp.int32, (SSM_HEADS, SSM_WIDTH), 1)
    lo_edge = hi * P
    widen = jnp.where((hj >= lo_edge) & (hj < lo_edge + P), 1.0, 0.0).astype(_BF16)

    def wide(cols):
        return sum(jnp.dot(part, widen, preferred_element_type=_F32) for part in _split3(cols))

    xs = xs_ref[0]
    xdt = xs * wide(dt)
    xdt_end = (xs * wide(dt * jnp.exp(tot - cs))).astype(_BF16)
    xdt = xdt.astype(_BF16)
    decay_in = wide(jnp.exp(cs))
    e_tot = jnp.exp(tot)

    pieces = []
    for g in range(SSM_GROUPS):
        bm = bm_ref[0, :, g * SSM_STATE:(g + 1) * SSM_STATE].astype(_BF16)
        cm = cm_ref[0, :, g * SSM_STATE:(g + 1) * SSM_STATE].astype(_BF16)
        cb = lax.dot_general(cm, bm, _NT, preferred_element_type=_F32)
        for e in range(SSM_HPG):
            h = g * SSM_HPG + e
            cols = slice(h * P, (h + 1) * P)
            seg = jnp.where(before_incl, jnp.exp(cs[:, h:h + 1] - cs_t[h:h + 1, :]), 0.0)
            state = h_ref[0, h]
            y_h = jnp.dot((cb * seg).astype(_BF16), xdt[:, cols], preferred_element_type=_F32)
            y_h = y_h + lax.dot_general(cm, state.astype(_BF16), _NT, preferred_element_type=_F32) * decay_in[:, cols]
            pieces.append(y_h)
            new = lax.dot_general(xdt_end[:, cols], bm, _TN, preferred_element_type=_F32)
            h_ref[0, h] = state * e_tot[:, h:h + 1] + new
    y_ref[0] = jnp.concatenate(pieces, axis=1)


def _ssd_scan(xs, bm, cm, log_a, dt, h0, reverse):
    bsz, t, width = xs.shape
    nc = t // SSM_CHUNK
    cidx = (lambda c: nc - 1 - c) if reverse else (lambda c: c)
    seq = lambda w: pl.BlockSpec((1, SSM_CHUNK, w), lambda b, c: (b, cidx(c), 0))
    st_spec = pl.BlockSpec((1,) + h0.shape[1:], lambda b, c: (b, 0, 0, 0))
    gn = SSM_GROUPS * SSM_STATE
    return pl.pallas_call(
        functools.partial(_ssd_kernel, reverse=reverse),
        grid=(bsz, nc),
        in_specs=[seq(width), seq(gn), seq(gn), seq(SSM_HEADS),
                  pl.BlockSpec((1, SSM_HEADS, SSM_CHUNK), lambda b, c: (b, 0, cidx(c))), seq(SSM_HEADS), st_spec],
        out_specs=[seq(width), st_spec],
        out_shape=[jax.ShapeDtypeStruct((bsz, t, width), _F32), jax.ShapeDtypeStruct(h0.shape, _F32)],
        compiler_params=pltpu.CompilerParams(dimension_semantics=("parallel", "arbitrary")),
        name="ssd_rev" if reverse else "ssd_fwd",
    )(xs, bm, cm, log_a, jnp.swapaxes(log_a, 1, 2), dt, h0)


SSM_TILE_T = 128


def _ssm_pre_kernel(xc, xp, xn, w_ref, b_ref, xs_o, bm_o, cm_o):
    first = pl.program_id(1) == 0
    last = pl.program_id(1) == pl.num_programs(1) - 1
    x = xc[0]
    tm = x.shape[0]
    row = lax.broadcasted_iota(jnp.int32, x.shape, 0)
    prev = jnp.where(first, 0.0, xp[0])
    nxt = jnp.where(last, 0.0, xn[0])
    half = SSM_CONV // 2
    acc = w_ref[half:half + 1, :] * x
    for off in range(1, half + 1):
        back = pltpu.roll(x, off, 0)
        fwd = pltpu.roll(x, tm - off, 0)
        for i in range(off):
            back = jnp.where(row == i, prev[SUBLANES - off + i:SUBLANES - off + i + 1, :], back)
            fwd = jnp.where(row == tm - off + i, nxt[i:i + 1, :], fwd)
        acc = acc + w_ref[half - off:half - off + 1, :] * back + w_ref[half + off:half + off + 1, :] * fwd
    u = acc + b_ref[...]
    u = u * jax.nn.sigmoid(u)
    gn = SSM_GROUPS * SSM_STATE
    xs_o[0] = u[:, :SSM_WIDTH]
    bm_o[0] = u[:, SSM_WIDTH:SSM_WIDTH + gn]
    cm_o[0] = u[:, SSM_WIDTH + gn:]


def _ssm_pre(xbc, conv_w, conv_b):
    bsz, t, cols = xbc.shape
    tm = SSM_TILE_T
    blocks_per_tile = tm // SUBLANES
    last_block = t // SUBLANES - 1
    gn = SSM_GROUPS * SSM_STATE
    seq = lambda c: pl.BlockSpec((1, tm, c), lambda b, i: (b, i, 0))
    whole = lambda a: pl.BlockSpec(a.shape, lambda b, i: (0,) * a.ndim)
    conv_w = conv_w.astype(_F32)
    conv_b = conv_b.reshape(1, cols).astype(_F32)
    return pl.pallas_call(
        _ssm_pre_kernel,
        grid=(bsz, t // tm),
        in_specs=[seq(cols),
                  pl.BlockSpec((1, SUBLANES, cols), lambda b, i: (b, jnp.maximum(i * blocks_per_tile - 1, 0), 0)),
                  pl.BlockSpec((1, SUBLANES, cols), lambda b, i: (b, jnp.minimum((i + 1) * blocks_per_tile, last_block), 0)),
                  whole(conv_w), whole(conv_b)],
        out_specs=[seq(SSM_WIDTH), seq(gn), seq(gn)],
        out_shape=[jax.ShapeDtypeStruct((bsz, t, c), _F32) for c in (SSM_WIDTH, gn, gn)],
        compiler_params=pltpu.CompilerParams(dimension_semantics=("parallel", "parallel"), vmem_limit_bytes=48 * 2**20),
        name="ssm_pre",
    )(xbc, xbc, xbc, conv_w, conv_b)


def _ssm_post_kernel(yf, yb, xs, z, d_ref, gain_ref, gsum, gwide, o_ref):
    zz = z[0]
    u = (yf[0] + yb[0] + d_ref[...] * xs[0]) * (zz * jax.nn.sigmoid(zz))
    ms = _head_widen(_head_sum(u * u, gsum[...]), gwide[...]) * (SSM_GROUPS / SSM_WIDTH)
    o_ref[0] = (u * lax.rsqrt(ms + NORM_EPS) * gain_ref[...]).astype(o_ref.dtype)


def _ssm_post(y_f, y_b, xs, z, d_skip, gain):
    bsz, t, width = y_f.shape
    tm = SSM_TILE_T
    gsum, gwide = _head_matrices(width, width // SSM_GROUPS)
    seq = pl.BlockSpec((1, tm, width), lambda b, i: (b, i, 0))
    whole = lambda a: pl.BlockSpec(a.shape, lambda b, i: (0,) * a.ndim)
    d_wide = jnp.repeat(d_skip, SSM_HEADDIM).reshape(1, width).astype(_F32)
    gain = gain.reshape(1, width).astype(_F32)
    return pl.pallas_call(
        _ssm_post_kernel,
        grid=(bsz, t // tm),
        in_specs=[seq] * 4 + [whole(d_wide), whole(gain), whole(gsum), whole(gwide)],
        out_specs=seq,
        out_shape=jax.ShapeDtypeStruct((bsz, t, width), _BF16),
        compiler_params=pltpu.CompilerParams(dimension_semantics=("parallel", "parallel")),
        name="ssm_post",
    )(y_f, y_b, xs, z, d_wide, gain, gsum, gwide)


def _mamba2_branch(xbc, dt_raw, z, h0_fwd, h0_bwd, conv_w, conv_b, dt_bias, a_log, d_skip, norm_w, want_output):
    bsz, t, _ = xbc.shape
    xs, bm, cm = _ssm_pre(xbc, conv_w, conv_b)
    dt = jax.nn.softplus(dt_raw.astype(_F32).reshape(bsz, t, 2, SSM_HEADS) + dt_bias)
    log_a = -jnp.exp(a_log) * dt
    heads = lambda h: h.reshape(bsz, SSM_HEADS, SSM_HEADDIM, SSM_STATE)
    y_f, h_f = _ssd_scan(xs, bm, cm, log_a[:, :, 0], dt[:, :, 0], heads(h0_fwd), reverse=False)
    y_b, h_b = _ssd_scan(xs, bm, cm, log_a[:, :, 1], dt[:, :, 1], heads(h0_bwd), reverse=True)
    out = _ssm_post(y_f, y_b, xs, z, d_skip, norm_w) if want_output else None
    return out, h_f, h_b


def _merge_branches(y_a, y_b, gates, w_a, w_b, w_o):
    bsz, t, d = y_a.shape
    mm = lambda u, w: _matmul(u.reshape(bsz * t, -1).astype(_BF16), w.astype(_BF16)).reshape(bsz, t, -1)
    g_a, g_b = jnp.split(gates, 2, axis=-1)
    m = jax.nn.sigmoid(g_a) * mm(y_a, w_a) + jax.nn.sigmoid(g_b) * mm(y_b, w_b)
    return mm(m, w_o)


PEER_ROUTE_TOKENS = 256
PEER_TOKEN_BLOCK = 512
PEER_EXPERT_BLOCK = 1024
PEER_KEY_GROUP = 4
SUBLANES = 8


def _top_values(x, count):
    rows = lax.broadcasted_iota(jnp.int32, x.shape, 0)
    rank = jnp.full(x.shape, float(count), _F32)
    vals = []
    for it in range(count):
        m = jnp.max(x, axis=0, keepdims=True)
        vals.append(m)
        first = jnp.min(jnp.where(x == m, rows, x.shape[0]), axis=0, keepdims=True)
        hit = rows == first
        rank = jnp.where(hit, float(it), rank)
        x = jnp.where(hit, -jnp.inf, x)
    return vals, rank, x


def _peer_route_kernel(h_ref, wqt_ref, sub_ref, cnt1_ref, e1_ref, rank2_ref, e2_ref):
    qt = lax.dot_general(wqt_ref[...], h_ref[...].astype(_BF16), _NT, preferred_element_type=_F32)
    for h in range(PEER_HEADS):
        sc = []
        for s in range(2):
            lo = (2 * h + s) * PEER_HALF
            sc.append(jnp.dot(sub_ref[2 * h + s], qt[lo:lo + PEER_HALF].astype(_BF16), preferred_element_type=_F32))
        top_a, rank_a, _ = _top_values(sc[0], PEER_TOPK)
        top_b, rank_b, _ = _top_values(sc[1], PEER_TOPK)
        top_b = jnp.concatenate(top_b, axis=0)
        width = [PEER_TOPK // (i + 1) for i in range(PEER_TOPK)]
        cand = jnp.concatenate([top_a[i] + top_b[:width[i]] for i in range(PEER_TOPK)], axis=0)
        best, _, left = _top_values(cand, PEER_TOPK)
        taken = jnp.where(left == cand, 0.0, 1.0)
        norm = best[0] * 0.0
        for val in best:
            norm = norm + jnp.exp(val - best[0])
        cnt1 = jnp.zeros_like(rank_a)
        start = 0
        for i in range(PEER_TOPK):
            used = jnp.sum(taken[start:start + width[i]], axis=0, keepdims=True)
            cnt1 = jnp.where(rank_a == float(i), used, cnt1)
            start += width[i]
        cnt1_ref[h] = cnt1.astype(cnt1_ref.dtype)
        rank2_ref[h] = rank_b.astype(rank2_ref.dtype)
        e1_ref[h] = (jnp.exp(sc[0] - top_a[0]) / norm).astype(e1_ref.dtype)
        e2_ref[h] = jnp.exp(sc[1] - top_b[0:1]).astype(e2_ref.dtype)


def _peer_route(tok, wq, subkeys):
    n, d = tok.shape
    tr = PEER_ROUTE_TOKENS
    wqt = wq.T.astype(_BF16)
    sub = subkeys.reshape(PEER_HEADS * 2, PEER_KEYS, PEER_HALF).astype(_BF16)
    key_spec = pl.BlockSpec((PEER_HEADS, PEER_KEYS, tr), lambda i: (0, 0, i))
    row_shape = jax.ShapeDtypeStruct((PEER_HEADS, PEER_KEYS, n), _F32)
    tile_shape = jax.ShapeDtypeStruct((PEER_HEADS, PEER_KEYS, n), _BF16)
    return pl.pallas_call(
        _peer_route_kernel,
        grid=(n // tr,),
        in_specs=[pl.BlockSpec((tr, d), lambda i: (i, 0)),
                  pl.BlockSpec(wqt.shape, lambda i: (0, 0)),
                  pl.BlockSpec(sub.shape, lambda i: (0, 0, 0))],
        out_specs=[key_spec] * 4,
        out_shape=[row_shape, row_shape, tile_shape, tile_shape],
        compiler_params=pltpu.CompilerParams(dimension_semantics=("parallel",), vmem_limit_bytes=48 * 2**20),
        name="peer_route",
    )(tok, wqt, sub)


def _gelu_exact(x):
    return 0.5 * x * (1.0 + lax.erf(x * (1.0 / math.sqrt(2.0))))


def _peer_gate_block(a_ref, w_ref, b_ref, cnt1_ref, e1_ref, rank2_ref, e2_ref, block):
    keys_per_block = PEER_EXPERT_BLOCK // PEER_KEYS
    rt = 2 * SUBLANES
    tb = a_ref.shape[1]
    for h in range(PEER_HEADS):
        for j in range(keys_per_block):
            i1 = block * keys_per_block + j
            b_ref[0, j * PEER_HEADS + h] = jnp.broadcast_to(cnt1_ref[h, pl.ds(i1, 1), :], (rt, tb)).astype(_BF16)
            b_ref[1, j * PEER_HEADS + h] = jnp.broadcast_to(e1_ref[h, pl.ds(i1, 1), :], (rt, tb)).astype(_BF16)

    def row_tile(tile, carry):
        r0 = pl.multiple_of(tile * rt, rt)
        for j0 in range(0, keys_per_block, PEER_KEY_GROUP):
            group = range(j0, j0 + PEER_KEY_GROUP)
            gates = {j: None for j in group}
            for h in range(PEER_HEADS):
                rank2 = rank2_ref[h, pl.ds(r0, rt), :]
                e2 = e2_ref[h, pl.ds(r0, rt), :]
                for j in group:
                    term = jnp.where(rank2 < b_ref[0, j * PEER_HEADS + h], b_ref[1, j * PEER_HEADS + h] * e2, jnp.zeros_like(e2))
                    gates[j] = term if gates[j] is None else gates[j] + term
            for j in group:
                rows = pl.ds(pl.multiple_of(j * PEER_KEYS + r0, rt), rt)
                w_ref[rows, :] = gates[j] * _gelu_exact(a_ref[rows, :]).astype(_BF16)
        return carry

    lax.fori_loop(0, PEER_KEYS // rt, row_tile, 0)


def _peer_expert_kernel(x_ref, u_ref, v_ref, cnt1_ref, e1_ref, rank2_ref, e2_ref, o_ref, a_scr, w_scr, b_scr):
    eb = pl.program_id(1)

    @pl.when(eb == 0)
    def _():
        o_ref[...] = jnp.zeros_like(o_ref)

    a_scr[...] = lax.dot_general(u_ref[...], x_ref[...], _NT, preferred_element_type=_F32)
    _peer_gate_block(a_scr, w_scr, b_scr, cnt1_ref, e1_ref, rank2_ref, e2_ref, eb)
    o_ref[...] += lax.dot_general(w_scr[...], v_ref[...], _TN, preferred_element_type=_F32)


def _peer_ffn(h, wq, subkeys, u_tab, v_tab):
    bsz, t, d = h.shape
    n = bsz * t
    tok = h.reshape(n, d)
    routing = _peer_route(tok, wq, subkeys)
    tb, eb = PEER_TOKEN_BLOCK, PEER_EXPERT_BLOCK
    n_experts = u_tab.shape[0]
    key_spec = pl.BlockSpec((PEER_HEADS, PEER_KEYS, tb), lambda i, e: (0, 0, i))
    tab_spec = pl.BlockSpec((eb, d), lambda i, e: (e, 0))
    out = pl.pallas_call(
        _peer_expert_kernel,
        grid=(n // tb, n_experts // eb),
        in_specs=[pl.BlockSpec((tb, d), lambda i, e: (i, 0)), tab_spec, tab_spec] + [key_spec] * 4,
        out_specs=pl.BlockSpec((tb, d), lambda i, e: (i, 0)),
        out_shape=jax.ShapeDtypeStruct((n, d), _F32),
        scratch_shapes=[pltpu.VMEM((eb, tb), _F32), pltpu.VMEM((eb, tb), _BF16),
                        pltpu.VMEM((2, (eb // PEER_KEYS) * PEER_HEADS, 2 * SUBLANES, tb), _BF16)],
        compiler_params=pltpu.CompilerParams(dimension_semantics=("parallel", "arbitrary"), vmem_limit_bytes=56 * 2**20),
        name="peer_experts",
    )(tok, u_tab.astype(_BF16), v_tab.astype(_BF16), *routing)
    return out.reshape(bsz, t, d)


def kernel(x, c, ctx, c_ctx, w_mod, b_mod, norm_pre1, norm_post1, norm_pre2, norm_post2, w_in, rw_mu, rw_w0, rw_w2, rw_a0, rw_a2, rw_g2, rw_k_k, rw_k_a, rw_r_k, rw_ln_w, rw_ln_b, ssm_conv_w, ssm_conv_b, ssm_dt_bias, ssm_a_log, ssm_d, ssm_norm_w, w_branch_a, w_branch_b, w_out, peer_wq, peer_subkeys, peer_u, peer_v):
    bsz, seq, _ = x.shape
    rows = seq // GRID_W
    depth = w_mod.shape[0]
    assert depth == 1
    rw_zero = jnp.zeros((bsz, RW_WIDTH // LANES, LANES, LANES), _F32)
    ssm_zero = jnp.zeros((bsz, SSM_GROUPS, SSM_HPG, SSM_HEADDIM, SSM_STATE), _F32)
    l = 0
    mod_x = (jax.nn.silu(c) @ w_mod[l] + b_mod[l])[:, None, :]
    mod_c = (jax.nn.silu(c_ctx) @ w_mod[l] + b_mod[l])[None, None, :]
    sh1x, sc1x, g1x, sh2x, sc2x, g2x = jnp.split(mod_x, N_MOD, axis=-1)
    sh1c, sc1c, g1c, sh2c, sc2c, g2c = jnp.split(mod_c, N_MOD, axis=-1)

    in_w = _in_proj_weights(w_in[l])
    hx = _norm_mod(x, norm_pre1[l], sc1x, sh1x)
    r_x, k_x, v_x, lora_x, z_x, xbc_x, dt_x, gate_x = _in_proj(hx, in_w, h_ssm=_raster_to_column(hx, rows))
    r_c, k_c, v_c, lora_c, z_c, xbc_c, dt_c, gate_c = _in_proj(_norm_mod(ctx, norm_pre1[l], sc1c, sh1c), in_w)

    rw_params = (rw_w0[l], rw_w2[l], rw_a0[l], rw_a2[l], rw_g2[l], rw_k_k[l], rw_k_a[l], rw_r_k[l], rw_ln_w[l], rw_ln_b[l])
    _, s_fwd, s_bwd = _rwkv7_branch(r_c, k_c, v_c, lora_c, rw_mu[l], rw_zero, rw_zero, *rw_params, want_output=False)
    ya_x, _, _ = _rwkv7_branch(r_x, k_x, v_x, lora_x, rw_mu[l], s_fwd, s_bwd, *rw_params, want_output=True)

    ssm_params = (ssm_conv_w[l], ssm_conv_b[l], ssm_dt_bias[l], ssm_a_log[l], ssm_d[l], ssm_norm_w[l])
    _, h_fwd, h_bwd = _mamba2_branch(xbc_c, dt_c, z_c, ssm_zero, ssm_zero, *ssm_params, want_output=False)
    yb_x, _, _ = _mamba2_branch(xbc_x, dt_x, z_x, h_fwd, h_bwd, *ssm_params, want_output=True)
    yb_x = _column_to_raster(yb_x, rows)

    mix_x = _merge_branches(ya_x, yb_x, gate_x, w_branch_a[l], w_branch_b[l], w_out[l])
    x, h2x = _residual(x, mix_x, g1x, norm_post1[l], next_mod=(norm_pre2[l], sc2x, sh2x))
    return _residual(x, _peer_ffn(h2x, peer_wq[l], peer_subkeys[l], peer_u[l], peer_v[l]), g2x, norm_post2[l])
```

```python
import functools
import math

import jax
import jax.numpy as jnp
from jax import lax
from jax.experimental import pallas as pl
from jax.experimental.pallas import tpu as pltpu

D_MODEL = 2048
GRID_W = 64
N_MOD = 6
NORM_EPS = 1e-6
RW_HEAD = 64
RW_WIDTH = D_MODEL
RW_HEADS = RW_WIDTH // RW_HEAD
LORA_W = 96
LORA_A = 96
LORA_G = 256
LN_X_EPS = 64e-5
SSM_WIDTH = D_MODEL
SSM_HEADDIM = 64
SSM_HEADS = SSM_WIDTH // SSM_HEADDIM
SSM_GROUPS = 8
SSM_HPG = SSM_HEADS // SSM_GROUPS
SSM_STATE = 128
SSM_CONV = 5
SSM_CHUNK = 128
PEER_HEADS = 8
PEER_KEYS = 128
PEER_TOPK = 16
PEER_QDIM = 256
PEER_HALF = PEER_QDIM // 2
PEER_BLOCK = 128
RW_COLS = 3 * RW_WIDTH + 2 * LORA_W + 2 * LORA_A + LORA_G
XBC_COLS = SSM_WIDTH + 2 * SSM_GROUPS * SSM_STATE
IN_SPLITS = (RW_COLS, RW_COLS + SSM_WIDTH, RW_COLS + SSM_WIDTH + XBC_COLS, RW_COLS + SSM_WIDTH + XBC_COLS + 2 * SSM_HEADS)
RW_SPLITS = (RW_WIDTH, 2 * RW_WIDTH, 3 * RW_WIDTH, 3 * RW_WIDTH + LORA_W, 3 * RW_WIDTH + 2 * LORA_W, 3 * RW_WIDTH + 2 * LORA_W + LORA_A, 3 * RW_WIDTH + 2 * LORA_W + 2 * LORA_A)

LANES = 128
WKV_CHUNK = 64
WKV_PAIR = LANES // RW_HEAD
WKV_GROUPS_PER_STEP = 16

_F32 = jnp.float32
_BF16 = jnp.bfloat16
_NT = (((1,), (1,)), ((), ()))
_TN = (((0,), (0,)), ((), ()))


def _mm(a, b, dims=None):
    a = a.astype(_BF16)
    b = b.astype(_BF16)
    if dims is None:
        return jnp.dot(a, b, preferred_element_type=_F32)
    return lax.dot_general(a, b, dims, preferred_element_type=_F32)


def _wkv_kernel(r_ref, lw_ref, k_ref, v_ref, a_ref, b_ref, s0_ref, y_ref, s_ref, *, reverse, chunk, groups):
    L = chunk
    L2 = WKV_PAIR * L

    @pl.when(pl.program_id(2) == 0)
    def _():
        s_ref[...] = s0_ref[...]

    ti = lax.broadcasted_iota(jnp.int32, (L, L), 0)
    tj = lax.broadcasted_iota(jnp.int32, (L, L), 1)
    before_incl = (ti <= tj) if reverse else (ti >= tj)
    cum = jnp.where(before_incl, 1.0, 0.0).astype(_BF16)

    lw = lw_ref[0]
    hi = lw.astype(_BF16)
    rem = lw - hi.astype(_F32)
    mid = rem.astype(_BF16)
    lo = (rem - mid.astype(_F32)).astype(_BF16)
    cs = (jnp.dot(cum, hi, preferred_element_type=_F32)
          + jnp.dot(cum, mid, preferred_element_type=_F32)
          + jnp.dot(cum, lo, preferred_element_type=_F32))
    tot = cs[0:1] if reverse else cs[L - 1:L]
    half = 0.5 * tot
    e_pos = jnp.exp(cs - half)
    e_neg = jnp.exp(half - cs)
    e_prev = jnp.exp(cs - lw - half)
    e_end = jnp.exp(tot - cs)
    e_half = jnp.exp(half)
    e_tot = jnp.exp(tot)

    rt = r_ref[0] * e_pos
    at = a_ref[0] * e_prev
    kt = k_ref[0] * e_neg
    bt = b_ref[0] * e_neg
    ke = k_ref[0] * e_end
    be = b_ref[0] * e_end
    vv = v_ref[0]

    lane = lax.broadcasted_iota(jnp.int32, (L, LANES), 1)
    first_head = lane < RW_HEAD

    def stack(x):
        return jnp.concatenate([jnp.where(first_head, x, 0.0), jnp.where(first_head, 0.0, x)], axis=0)

    si = lax.broadcasted_iota(jnp.int32, (L2, L2), 0)
    sj = lax.broadcasted_iota(jnp.int32, (L2, L2), 1)
    same_head = (si < L) == (sj < L)
    pi = jnp.where(si < L, si, si - L)
    pj = jnp.where(sj < L, sj, sj - L)
    incl2 = same_head & ((pi <= pj) if reverse else (pi >= pj))
    strict2 = same_head & ((pi < pj) if reverse else (pi > pj))
    eye2 = jnp.where(si == sj, 1.0, 0.0)

    gs = range(groups)
    sls = [slice(g * LANES, (g + 1) * LANES) for g in gs]
    v_s = [stack(vv[:, sl]).astype(_BF16) for sl in sls]
    lhs = [jnp.concatenate([stack(at[:, sl]), stack(rt[:, sl])], axis=0).astype(_BF16) for sl in sls]
    rhs = [jnp.concatenate([stack(kt[:, sl]), stack(bt[:, sl])], axis=0).astype(_BF16) for sl in sls]
    state = [s_ref[0, g] for g in gs]
    scores = [_mm(lhs[g], rhs[g], _NT) for g in gs]
    from_state = [_mm(lhs[g], state[g] * e_half[:, sls[g]], _NT) for g in gs]

    power = [jnp.where(strict2, -scores[g][:L2, L2:], 0.0).astype(_BF16) for g in gs]
    inv = [eye2 + power[g] for g in gs]
    ak_v = [_mm(jnp.where(strict2, scores[g][:L2, :L2], 0.0), v_s[g]) for g in gs]
    for _ in range(int(math.log2(L)) - 1):
        power = [_mm(power[g], power[g]).astype(_BF16) for g in gs]
        inv = [inv[g] + _mm(inv[g], power[g]) for g in gs]
    u_s = [_mm(inv[g], from_state[g][:L2] + ak_v[g]) for g in gs]

    vu = [jnp.concatenate([v_s[g], u_s[g].astype(_BF16)], axis=0) for g in gs]
    for g in gs:
        r_kb = jnp.concatenate([jnp.where(incl2, scores[g][L2:, :L2], 0.0),
                                jnp.where(incl2, -scores[g][L2:, L2:], 0.0)], axis=1)
        y_s = from_state[g][L2:] + _mm(r_kb, vu[g])
        y_ref[0, :, sls[g]] = y_s[:L] + y_s[L:]
    for g in gs:
        kb = jnp.concatenate([stack(ke[:, sls[g]]), -stack(be[:, sls[g]])], axis=0)
        s_ref[0, g] = state[g] * e_tot[:, sls[g]] + _mm(vu[g], kb, _TN)


def _wkv_scan(r, lw, k, v, a, b, s0, reverse):
    bsz, t, width = r.shape
    groups = WKV_GROUPS_PER_STEP
    assert t % WKV_CHUNK == 0 and width % (groups * LANES) == 0
    nc = t // WKV_CHUNK
    ngroup_steps = width // (groups * LANES)
    cidx = (lambda c: nc - 1 - c) if reverse else (lambda c: c)
    seq_spec = pl.BlockSpec((1, WKV_CHUNK, groups * LANES), lambda bi, gi, c: (bi, cidx(c), gi))
    st_spec = pl.BlockSpec((1, groups, LANES, LANES), lambda bi, gi, c: (bi, gi, 0, 0))
    return pl.pallas_call(
        functools.partial(_wkv_kernel, reverse=reverse, chunk=WKV_CHUNK, groups=groups),
        grid=(bsz, ngroup_steps, nc),
        in_specs=[seq_spec] * 6 + [st_spec],
        out_specs=[seq_spec, st_spec],
        out_shape=[jax.ShapeDtypeStruct((bsz, t, width), _F32), jax.ShapeDtypeStruct(s0.shape, _F32)],
        compiler_params=pltpu.CompilerParams(dimension_semantics=("parallel", "parallel", "arbitrary")),
        name="wkv7_rev" if reverse else "wkv7_fwd",
    )(r, lw, k, v, a, b, s0)


MM_TILE_M = 1024
MM_TILE_N = 1024
NORM_TILE_M = 512
SMALL_COLS = 768


def _matmul_kernel(a_ref, w_ref, o_ref):
    o_ref[...] = jnp.dot(a_ref[...], w_ref[...], preferred_element_type=_F32)


def _matmul(a, w):
    m, k = a.shape
    n = w.shape[1]
    tm, tn = min(m, MM_TILE_M), min(n, MM_TILE_N)
    assert m % tm == 0 and n % tn == 0 and a.dtype == _BF16 and w.dtype == _BF16
    return pl.pallas_call(
        _matmul_kernel,
        grid=(n // tn, m // tm),
        in_specs=[pl.BlockSpec((tm, k), lambda j, i: (i, 0)), pl.BlockSpec((k, tn), lambda j, i: (0, j))],
        out_specs=pl.BlockSpec((tm, tn), lambda j, i: (i, j)),
        out_shape=jax.ShapeDtypeStruct((m, n), _F32),
        compiler_params=pltpu.CompilerParams(dimension_semantics=("parallel", "parallel"), vmem_limit_bytes=48 * 2**20),
        name="matmul",
    )(a, w)


def _norm_mod_kernel(x_ref, gain_ref, sc_ref, sh_ref, o_ref):
    x = x_ref[0]
    inv = lax.rsqrt(jnp.mean(x * x, axis=-1, keepdims=True) + NORM_EPS)
    o_ref[0] = ((x * inv * gain_ref[...]) * (1.0 + sc_ref[0]) + sh_ref[0]).astype(o_ref.dtype)


def _norm_mod(x, gain, scale, shift):
    bsz, t, d = x.shape
    tm = min(t, NORM_TILE_M)
    mod_spec = pl.BlockSpec((1, 1, d), (lambda b, i: (b, 0, 0)) if scale.shape[0] == bsz else (lambda b, i: (0, 0, 0)))
    return pl.pallas_call(
        _norm_mod_kernel,
        grid=(bsz, t // tm),
        in_specs=[pl.BlockSpec((1, tm, d), lambda b, i: (b, i, 0)), pl.BlockSpec((1, d), lambda b, i: (0, 0)), mod_spec, mod_spec],
        out_specs=pl.BlockSpec((1, tm, d), lambda b, i: (b, i, 0)),
        out_shape=jax.ShapeDtypeStruct((bsz, t, d), _BF16),
        compiler_params=pltpu.CompilerParams(dimension_semantics=("parallel", "parallel")),
        name="norm_mod",
    )(x, gain.reshape(1, d), scale, shift)


def _residual_kernel(x_ref, u_ref, g_ref, gain_ref, *rest, with_next):
    u = u_ref[0]
    inv = lax.rsqrt(jnp.mean(u * u, axis=-1, keepdims=True) + NORM_EPS)
    y = x_ref[0] + g_ref[0] * (u * inv * gain_ref[...])
    if not with_next:
        rest[0][0] = y
        return
    gain2_ref, sc_ref, sh_ref, y_ref, h_ref = rest
    y_ref[0] = y
    inv2 = lax.rsqrt(jnp.mean(y * y, axis=-1, keepdims=True) + NORM_EPS)
    h_ref[0] = ((y * inv2 * gain2_ref[...]) * (1.0 + sc_ref[0]) + sh_ref[0]).astype(h_ref.dtype)


def _residual(x, u, g, gain, next_mod=None):
    bsz, t, d = x.shape
    tm = min(t, NORM_TILE_M)
    seq = pl.BlockSpec((1, tm, d), lambda b, i: (b, i, 0))
    per_batch = pl.BlockSpec((1, 1, d), lambda b, i: (b, 0, 0))
    row = pl.BlockSpec((1, d), lambda b, i: (0, 0))
    args, in_specs = [x, u, g, gain.reshape(1, d)], [seq, seq, per_batch, row]
    out_specs, out_shape = [seq], [jax.ShapeDtypeStruct((bsz, t, d), _F32)]
    if next_mod is not None:
        gain2, scale, shift = next_mod
        args += [gain2.reshape(1, d), scale, shift]
        in_specs += [row, per_batch, per_batch]
        out_specs.append(seq)
        out_shape.append(jax.ShapeDtypeStruct((bsz, t, d), _BF16))
    out = pl.pallas_call(
        functools.partial(_residual_kernel, with_next=next_mod is not None),
        grid=(bsz, t // tm),
        in_specs=in_specs,
        out_specs=out_specs,
        out_shape=out_shape,
        compiler_params=pltpu.CompilerParams(dimension_semantics=("parallel", "parallel")),
        name="residual",
    )(*args)
    return out if next_mod is not None else out[0]


def _in_proj_weights(w):
    rw_end, z_end, xbc_end, dt_end = IN_SPLITS
    lora_dt = jnp.concatenate([w[:, RW_SPLITS[2]:rw_end], w[:, xbc_end:dt_end]], axis=1)
    lora_dt = jnp.pad(lora_dt, ((0, 0), (0, SMALL_COLS - lora_dt.shape[1])))
    parts = [w[:, :RW_SPLITS[0]], w[:, RW_SPLITS[0]:RW_SPLITS[1]], w[:, RW_SPLITS[1]:RW_SPLITS[2]],
             w[:, rw_end:z_end], w[:, z_end:xbc_end], w[:, dt_end:], lora_dt]
    return [p.astype(_BF16) for p in parts]


def _in_proj(h, weights, h_ssm=None):
    bsz, t, d = h.shape
    mm = lambda u, w: _matmul(u.reshape(bsz * t, d), w).reshape(bsz, t, -1)
    w_r, w_k, w_v, w_z, w_xbc, w_gate, w_lora_dt = weights
    lora_dt = mm(h, w_lora_dt)
    dt_src = lora_dt if h_ssm is None else mm(h_ssm, w_lora_dt)
    h_ssm = h if h_ssm is None else h_ssm
    n_lora = RW_COLS - RW_SPLITS[2]
    return (mm(h, w_r), mm(h, w_k), mm(h, w_v), lora_dt, mm(h_ssm, w_z), mm(h_ssm, w_xbc),
            dt_src[..., n_lora:n_lora + 2 * SSM_HEADS], mm(h, w_gate))


def _raster_to_column(u, rows):
    b, s, ch = u.shape
    return u.reshape(b, rows, GRID_W, ch).transpose(0, 2, 1, 3).reshape(b, s, ch)


def _column_to_raster(u, rows):
    b, s, ch = u.shape
    return u.reshape(b, GRID_W, rows, ch).transpose(0, 2, 1, 3).reshape(b, s, ch)


RW_TILE_T = 128
LORA_COLS = RW_COLS - 3 * RW_WIDTH


def _head_matrices(width=RW_WIDTH, group=RW_HEAD):
    group_of_lane = jnp.arange(width) // group
    onehot = (group_of_lane[:, None] == jnp.arange(width // group)[None, :]).astype(_BF16)
    return onehot, onehot.T


def _head_sum(x, hsum):
    return sum(jnp.dot(part, hsum, preferred_element_type=_F32) for part in _split3(x))


def _head_widen(cols, hwide):
    return sum(jnp.dot(part, hwide, preferred_element_type=_F32) for part in _split3(cols))


def _token_shift_block(c_ref, p_ref, n_ref, mu, first, last):
    x = c_ref[0]
    tm = x.shape[0]
    row = lax.broadcasted_iota(jnp.int32, x.shape, 0)
    prev_row = jnp.where(first, 0.0, p_ref[0, SUBLANES - 1:SUBLANES, :])
    next_row = jnp.where(last, 0.0, n_ref[0, 0:1, :])
    prev = jnp.where(row == 0, prev_row, pltpu.roll(x, 1, 0))
    nxt = jnp.where(row == tm - 1, next_row, pltpu.roll(x, tm - 1, 0))
    return x + mu * (0.5 * (prev + nxt) - x)


def _rw_pre_kernel(rc, rp, rn, kc, kp, kn, vc, vp, vn, lc, lp, ln,
                   mu_r, mu_k, mu_v, mu_l, w0, a0, k_k, k_a, r_k, ln_b, w2, a2, g2, hsum, hwide,
                   r_o, v_o, kk_o, lwf_o, lwb_o, kf_o, kb_o, kkaf_o, kkab_o, gate_o, bonus_o):
    first = pl.program_id(1) == 0
    last = pl.program_id(1) == pl.num_programs(1) - 1
    r = _token_shift_block(rc, rp, rn, mu_r[...], first, last)
    k = _token_shift_block(kc, kp, kn, mu_k[...], first, last)
    v = _token_shift_block(vc, vp, vn, mu_v[...], first, last)
    lora = _token_shift_block(lc, lp, ln, mu_l[...], first, last)
    r_o[0] = r
    v_o[0] = v

    kk = k * k_k[...]
    norm = jnp.maximum(jnp.sqrt(_head_sum(kk * kk, hsum[...])), 1e-12)
    kk = kk * _head_widen(1.0 / norm, hwide[...])
    kk_o[0] = kk

    k_sum = None
    for d, (lw_o, kd_o, kka_o) in enumerate(((lwf_o, kf_o, kkaf_o), (lwb_o, kb_o, kkab_o))):
        wd = lora[:, d * LORA_W:(d + 1) * LORA_W]
        ad = lora[:, 2 * LORA_W + d * LORA_A:2 * LORA_W + (d + 1) * LORA_A]
        w_log = -jax.nn.softplus(-(w0[d:d + 1, :] + _mm(jnp.tanh(wd), w2[d]))) - 0.5
        lw_o[0] = -jnp.exp(w_log)
        a = jax.nn.sigmoid(a0[d:d + 1, :] + _mm(ad, a2[d]))
        k_dir = k * (1.0 + (a - 1.0) * k_a[...])
        kd_o[0] = k_dir
        kka_o[0] = kk * a
        k_sum = k_dir if k_sum is None else k_sum + k_dir

    gd = lora[:, 2 * LORA_W + 2 * LORA_A:2 * LORA_W + 2 * LORA_A + LORA_G]
    gate = _mm(jax.nn.sigmoid(gd), g2[...])
    bonus = _head_widen(_head_sum(r * k_sum * r_k[...], hsum[...]), hwide[...]) * v
    gate_o[0] = gate
    bonus_o[0] = (ln_b[...] + bonus) * gate


def _rw_pre(r, k, v, lora_dt, mu, w0, w2, a0, a2, g2, k_k, k_a, r_k, ln_b):
    bsz, t, width = r.shape
    tm = RW_TILE_T
    nt = t // tm
    blocks_per_tile = tm // SUBLANES
    last_block = t // SUBLANES - 1
    lcols = lora_dt.shape[-1]

    def specs(cols):
        return [pl.BlockSpec((1, tm, cols), lambda b, i: (b, i, 0)),
                pl.BlockSpec((1, SUBLANES, cols), lambda b, i: (b, jnp.maximum(i * blocks_per_tile - 1, 0), 0)),
                pl.BlockSpec((1, SUBLANES, cols), lambda b, i: (b, jnp.minimum((i + 1) * blocks_per_tile, last_block), 0))]

    def whole(a):
        return pl.BlockSpec(a.shape, lambda b, i: (0,) * a.ndim)

    row = lambda a: a.reshape(1, -1).astype(_F32)
    mu_l = jnp.pad(mu[RW_SPLITS[2]:], (0, lcols - LORA_COLS))
    hsum, hwide = _head_matrices()
    params = [row(mu[:RW_SPLITS[0]]), row(mu[RW_SPLITS[0]:RW_SPLITS[1]]), row(mu[RW_SPLITS[1]:RW_SPLITS[2]]), row(mu_l),
              w0.astype(_F32), a0.astype(_F32), row(k_k), row(k_a), row(r_k), row(ln_b),
              w2.astype(_BF16), a2.astype(_BF16), g2.astype(_BF16), hsum, hwide]
    out_spec = pl.BlockSpec((1, tm, width), lambda b, i: (b, i, 0))
    out_shape = jax.ShapeDtypeStruct((bsz, t, width), _F32)
    return pl.pallas_call(
        _rw_pre_kernel,
        grid=(bsz, nt),
        in_specs=specs(width) * 3 + specs(lcols) + [whole(p) for p in params],
        out_specs=[out_spec] * 11,
        out_shape=[out_shape] * 11,
        compiler_params=pltpu.CompilerParams(dimension_semantics=("parallel", "parallel"), vmem_limit_bytes=56 * 2**20),
        name="rw_pre",
    )(r, r, r, k, k, k, v, v, v, lora_dt, lora_dt, lora_dt, *params)


def _rw_post_kernel(yf, yb, gate, bonus, ln_w, hsum, hwide, o_ref):
    y = yf[0] + yb[0]
    mean = _head_widen(_head_sum(y, hsum[...]), hwide[...]) * (1.0 / RW_HEAD)
    cen = y - mean
    var = _head_widen(_head_sum(cen * cen, hsum[...]), hwide[...]) * (1.0 / RW_HEAD)
    o_ref[0] = (cen * lax.rsqrt(var + LN_X_EPS) * ln_w[...] * gate[0] + bonus[0]).astype(o_ref.dtype)


def _rw_post(y_f, y_b, gate, bonus, ln_w):
    bsz, t, width = y_f.shape
    tm = RW_TILE_T
    hsum, hwide = _head_matrices()
    spec = pl.BlockSpec((1, tm, width), lambda b, i: (b, i, 0))
    whole = lambda a: pl.BlockSpec(a.shape, lambda b, i: (0,) * a.ndim)
    ln_w = ln_w.reshape(1, width).astype(_F32)
    return pl.pallas_call(
        _rw_post_kernel,
        grid=(bsz, t // tm),
        in_specs=[spec] * 4 + [whole(ln_w), whole(hsum), whole(hwide)],
        out_specs=spec,
        out_shape=jax.ShapeDtypeStruct((bsz, t, width), _BF16),
        compiler_params=pltpu.CompilerParams(dimension_semantics=("parallel", "parallel")),
        name="rw_post",
    )(y_f, y_b, gate, bonus, ln_w, hsum, hwide)


def _rwkv7_branch(r, k, v, lora_dt, mu, s_fwd, s_bwd, w0, w2, a0, a2, g2, k_k, k_a, r_k, ln_w, ln_b, want_output):
    r, v, kk, lw_f, lw_b, k_f, k_b, kka_f, kka_b, gate, bonus = _rw_pre(
        r, k, v, lora_dt, mu, w0, w2, a0, a2, g2, k_k, k_a, r_k, ln_b)
    y_f, s_fwd = _wkv_scan(r, lw_f, k_f, v, kk, kka_f, s_fwd, reverse=False)
    y_b, s_bwd = _wkv_scan(r, lw_b, k_b, v, kk, kka_b, s_bwd, reverse=True)
    out = _rw_post(y_f, y_b, gate, bonus, ln_w) if want_output else None
    return out, s_fwd, s_bwd


def _split3(x):
    hi = x.astype(_BF16)
    rem = x - hi.astype(_F32)
    mid = rem.astype(_BF16)
    lo = (rem - mid.astype(_F32)).astype(_BF16)
    return hi, mid, lo


def _ssd_kernel(xs_ref, bm_ref, cm_ref, la_ref, lat_ref, dt_ref, h0_ref, y_ref, h_ref, *, reverse):
    L = SSM_CHUNK
    P = SSM_HEADDIM

    @pl.when(pl.program_id(1) == 0)
    def _():
        h_ref[...] = h0_ref[...]

    ti = lax.broadcasted_iota(jnp.int32, (L, L), 0)
    tj = lax.broadcasted_iota(jnp.int32, (L, L), 1)
    before_incl = (ti <= tj) if reverse else (ti >= tj)
    cum = jnp.where(before_incl, 1.0, 0.0).astype(_BF16)
    cum_t = jnp.where(before_incl, 0.0, 1.0).astype(_BF16) + jnp.where(ti == tj, 1.0, 0.0).astype(_BF16)

    la = la_ref[0]
    cs = sum(jnp.dot(cum, part, preferred_element_type=_F32) for part in _split3(la))
    cs_t = sum(jnp.dot(part, cum_t, preferred_element_type=_F32) for part in _split3(lat_ref[0]))
    tot = cs[0:1] if reverse else cs[L - 1:L]
    dt = dt_ref[0]

    hi = lax.broadcasted_iota(jnp.int32, (SSM_HEADS, SSM_WIDTH), 0)
    hj = lax.broadcasted_iota(jnp.int32, (SSM_HEADS, SSM_WIDTH), 1)
    lo_edge = hi * P
    widen = jnp.where((hj >= lo_edge) & (hj < lo_edge + P), 1.0, 0.0).astype(_BF16)

    def wide(cols):
        return sum(jnp.dot(part, widen, preferred_element_type=_F32) for part in _split3(cols))

    xs = xs_ref[0]
    xdt = xs * wide(dt)
    xdt_end = (xs * wide(dt * jnp.exp(tot - cs))).astype(_BF16)
    xdt = xdt.astype(_BF16)
    decay_in = wide(jnp.exp(cs))
    e_tot = jnp.exp(tot)

    pieces = []
    for g in range(SSM_GROUPS):
        bm = bm_ref[0, :, g * SSM_STATE:(g + 1) * SSM_STATE].astype(_BF16)
        cm = cm_ref[0, :, g * SSM_STATE:(g + 1) * SSM_STATE].astype(_BF16)
        cb = lax.dot_general(cm, bm, _NT, preferred_element_type=_F32)
        for e in range(SSM_HPG):
            h = g * SSM_HPG + e
            cols = slice(h * P, (h + 1) * P)
            seg = jnp.where(before_incl, jnp.exp(cs[:, h:h + 1] - cs_t[h:h + 1, :]), 0.0)
            state = h_ref[0, h]
            y_h = jnp.dot((cb * seg).astype(_BF16), xdt[:, cols], preferred_element_type=_F32)
            y_h = y_h + lax.dot_general(cm, state.astype(_BF16), _NT, preferred_element_type=_F32) * decay_in[:, cols]
            pieces.append(y_h)
            new = lax.dot_general(xdt_end[:, cols], bm, _TN, preferred_element_type=_F32)
            h_ref[0, h] = state * e_tot[:, h:h + 1] + new
    y_ref[0] = jnp.concatenate(pieces, axis=1)


def _ssd_scan(xs, bm, cm, log_a, dt, h0, reverse):
    bsz, t, width = xs.shape
    nc = t // SSM_CHUNK
    cidx = (lambda c: nc - 1 - c) if reverse else (lambda c: c)
    seq = lambda w: pl.BlockSpec((1, SSM_CHUNK, w), lambda b, c: (b, cidx(c), 0))
    st_spec = pl.BlockSpec((1,) + h0.shape[1:], lambda b, c: (b, 0, 0, 0))
    gn = SSM_GROUPS * SSM_STATE
    return pl.pallas_call(
        functools.partial(_ssd_kernel, reverse=reverse),
        grid=(bsz, nc),
        in_specs=[seq(width), seq(gn), seq(gn), seq(SSM_HEADS),
                  pl.BlockSpec((1, SSM_HEADS, SSM_CHUNK), lambda b, c: (b, 0, cidx(c))), seq(SSM_HEADS), st_spec],
        out_specs=[seq(width), st_spec],
        out_shape=[jax.ShapeDtypeStruct((bsz, t, width), _F32), jax.ShapeDtypeStruct(h0.shape, _F32)],
        compiler_params=pltpu.CompilerParams(dimension_semantics=("parallel", "arbitrary")),
        name="ssd_rev" if reverse else "ssd_fwd",
    )(xs, bm, cm, log_a, jnp.swapaxes(log_a, 1, 2), dt, h0)


SSM_TILE_T = 128


def _ssm_pre_kernel(xc, xp, xn, w_ref, b_ref, xs_o, bm_o, cm_o):
    first = pl.program_id(1) == 0
    last = pl.program_id(1) == pl.num_programs(1) - 1
    x = xc[0]
    tm = x.shape[0]
    row = lax.broadcasted_iota(jnp.int32, x.shape, 0)
    prev = jnp.where(first, 0.0, xp[0])
    nxt = jnp.where(last, 0.0, xn[0])
    half = SSM_CONV // 2
    acc = w_ref[half:half + 1, :] * x
    for off in range(1, half + 1):
        back = pltpu.roll(x, off, 0)
        fwd = pltpu.roll(x, tm - off, 0)
        for i in range(off):
            back = jnp.where(row == i, prev[SUBLANES - off + i:SUBLANES - off + i + 1, :], back)
            fwd = jnp.where(row == tm - off + i, nxt[i:i + 1, :], fwd)
        acc = acc + w_ref[half - off:half - off + 1, :] * back + w_ref[half + off:half + off + 1, :] * fwd
    u = acc + b_ref[...]
    u = u * jax.nn.sigmoid(u)
    gn = SSM_GROUPS * SSM_STATE
    xs_o[0] = u[:, :SSM_WIDTH]
    bm_o[0] = u[:, SSM_WIDTH:SSM_WIDTH + gn]
    cm_o[0] = u[:, SSM_WIDTH + gn:]


def _ssm_pre(xbc, conv_w, conv_b):
    bsz, t, cols = xbc.shape
    tm = SSM_TILE_T
    blocks_per_tile = tm // SUBLANES
    last_block = t // SUBLANES - 1
    gn = SSM_GROUPS * SSM_STATE
    seq = lambda c: pl.BlockSpec((1, tm, c), lambda b, i: (b, i, 0))
    whole = lambda a: pl.BlockSpec(a.shape, lambda b, i: (0,) * a.ndim)
    conv_w = conv_w.astype(_F32)
    conv_b = conv_b.reshape(1, cols).astype(_F32)
    return pl.pallas_call(
        _ssm_pre_kernel,
        grid=(bsz, t // tm),
        in_specs=[seq(cols),
                  pl.BlockSpec((1, SUBLANES, cols), lambda b, i: (b, jnp.maximum(i * blocks_per_tile - 1, 0), 0)),
                  pl.BlockSpec((1, SUBLANES, cols), lambda b, i: (b, jnp.minimum((i + 1) * blocks_per_tile, last_block), 0)),
                  whole(conv_w), whole(conv_b)],
        out_specs=[seq(SSM_WIDTH), seq(gn), seq(gn)],
        out_shape=[jax.ShapeDtypeStruct((bsz, t, c), _F32) for c in (SSM_WIDTH, gn, gn)],
        compiler_params=pltpu.CompilerParams(dimension_semantics=("parallel", "parallel"), vmem_limit_bytes=48 * 2**20),
        name="ssm_pre",
    )(xbc, xbc, xbc, conv_w, conv_b)


def _ssm_post_kernel(yf, yb, xs, z, d_ref, gain_ref, gsum, gwide, o_ref):
    zz = z[0]
    u = (yf[0] + yb[0] + d_ref[...] * xs[0]) * (zz * jax.nn.sigmoid(zz))
    ms = _head_widen(_head_sum(u * u, gsum[...]), gwide[...]) * (SSM_GROUPS / SSM_WIDTH)
    o_ref[0] = (u * lax.rsqrt(ms + NORM_EPS) * gain_ref[...]).astype(o_ref.dtype)


def _ssm_post(y_f, y_b, xs, z, d_skip, gain):
    bsz, t, width = y_f.shape
    tm = SSM_TILE_T
    gsum, gwide = _head_matrices(width, width // SSM_GROUPS)
    seq = pl.BlockSpec((1, tm, width), lambda b, i: (b, i, 0))
    whole = lambda a: pl.BlockSpec(a.shape, lambda b, i: (0,) * a.ndim)
    d_wide = jnp.repeat(d_skip, SSM_HEADDIM).reshape(1, width).astype(_F32)
    gain = gain.reshape(1, width).astype(_F32)
    return pl.pallas_call(
        _ssm_post_kernel,
        grid=(bsz, t // tm),
        in_specs=[seq] * 4 + [whole(d_wide), whole(gain), whole(gsum), whole(gwide)],
        out_specs=seq,
        out_shape=jax.ShapeDtypeStruct((bsz, t, width), _BF16),
        compiler_params=pltpu.CompilerParams(dimension_semantics=("parallel", "parallel")),
        name="ssm_post",
    )(y_f, y_b, xs, z, d_wide, gain, gsum, gwide)


def _mamba2_branch(xbc, dt_raw, z, h0_fwd, h0_bwd, conv_w, conv_b, dt_bias, a_log, d_skip, norm_w, want_output):
    bsz, t, _ = xbc.shape
    xs, bm, cm = _ssm_pre(xbc, conv_w, conv_b)
    dt = jax.nn.softplus(dt_raw.astype(_F32).reshape(bsz, t, 2, SSM_HEADS) + dt_bias)
    log_a = -jnp.exp(a_log) * dt
    heads = lambda h: h.reshape(bsz, SSM_HEADS, SSM_HEADDIM, SSM_STATE)
    y_f, h_f = _ssd_scan(xs, bm, cm, log_a[:, :, 0], dt[:, :, 0], heads(h0_fwd), reverse=False)
    y_b, h_b = _ssd_scan(xs, bm, cm, log_a[:, :, 1], dt[:, :, 1], heads(h0_bwd), reverse=True)
    out = _ssm_post(y_f, y_b, xs, z, d_skip, norm_w) if want_output else None
    return out, h_f, h_b


def _merge_branches(y_a, y_b, gates, w_a, w_b, w_o):
    bsz, t, d = y_a.shape
    mm = lambda u, w: _matmul(u.reshape(bsz * t, -1).astype(_BF16), w.astype(_BF16)).reshape(bsz, t, -1)
    g_a, g_b = jnp.split(gates, 2, axis=-1)
    m = jax.nn.sigmoid(g_a) * mm(y_a, w_a) + jax.nn.sigmoid(g_b) * mm(y_b, w_b)
    return mm(m, w_o)


PEER_ROUTE_TOKENS = 256
PEER_TOKEN_BLOCK = 512
PEER_EXPERT_BLOCK = 1024
PEER_KEY_GROUP = 4
SUBLANES = 8


def _top_values(x, count):
    rows = lax.broadcasted_iota(jnp.int32, x.shape, 0)
    rank = jnp.full(x.shape, float(count), _F32)
    vals = []
    for it in range(count):
        m = jnp.max(x, axis=0, keepdims=True)
        vals.append(m)
        first = jnp.min(jnp.where(x == m, rows, x.shape[0]), axis=0, keepdims=True)
        hit = rows == first
        rank = jnp.where(hit, float(it), rank)
        x = jnp.where(hit, -jnp.inf, x)
    return vals, rank, x


def _peer_route_kernel(h_ref, wqt_ref, sub_ref, cnt1_ref, e1_ref, rank2_ref, e2_ref):
    qt = lax.dot_general(wqt_ref[...], h_ref[...].astype(_BF16), _NT, preferred_element_type=_F32)
    for h in range(PEER_HEADS):
        sc = []
        for s in range(2):
            lo = (2 * h + s) * PEER_HALF
            sc.append(jnp.dot(sub_ref[2 * h + s], qt[lo:lo + PEER_HALF].astype(_BF16), preferred_element_type=_F32))
        top_a, rank_a, _ = _top_values(sc[0], PEER_TOPK)
        top_b, rank_b, _ = _top_values(sc[1], PEER_TOPK)
        top_b = jnp.concatenate(top_b, axis=0)
        width = [PEER_TOPK // (i + 1) for i in range(PEER_TOPK)]
        cand = jnp.concatenate([top_a[i] + top_b[:width[i]] for i in range(PEER_TOPK)], axis=0)
        best, _, left = _top_values(cand, PEER_TOPK)
        taken = jnp.where(left == cand, 0.0, 1.0)
        norm = best[0] * 0.0
        for val in best:
            norm = norm + jnp.exp(val - best[0])
        cnt1 = jnp.zeros_like(rank_a)
        start = 0
        for i in range(PEER_TOPK):
            used = jnp.sum(taken[start:start + width[i]], axis=0, keepdims=True)
            cnt1 = jnp.where(rank_a == float(i), used, cnt1)
            start += width[i]
        cnt1_ref[h] = cnt1.astype(cnt1_ref.dtype)
        rank2_ref[h] = rank_b.astype(rank2_ref.dtype)
        e1_ref[h] = (jnp.exp(sc[0] - top_a[0]) / norm).astype(e1_ref.dtype)
        e2_ref[h] = jnp.exp(sc[1] - top_b[0:1]).astype(e2_ref.dtype)


def _peer_route(tok, wq, subkeys):
    n, d = tok.shape
    tr = PEER_ROUTE_TOKENS
    wqt = wq.T.astype(_BF16)
    sub = subkeys.reshape(PEER_HEADS * 2, PEER_KEYS, PEER_HALF).astype(_BF16)
    key_spec = pl.BlockSpec((PEER_HEADS, PEER_KEYS, tr), lambda i: (0, 0, i))
    row_shape = jax.ShapeDtypeStruct((PEER_HEADS, PEER_KEYS, n), _F32)
    tile_shape = jax.ShapeDtypeStruct((PEER_HEADS, PEER_KEYS, n), _BF16)
    return pl.pallas_call(
        _peer_route_kernel,
        grid=(n // tr,),
        in_specs=[pl.BlockSpec((tr, d), lambda i: (i, 0)),
                  pl.BlockSpec(wqt.shape, lambda i: (0, 0)),
                  pl.BlockSpec(sub.shape, lambda i: (0, 0, 0))],
        out_specs=[key_spec] * 4,
        out_shape=[row_shape, row_shape, tile_shape, tile_shape],
        compiler_params=pltpu.CompilerParams(dimension_semantics=("parallel",), vmem_limit_bytes=48 * 2**20),
        name="peer_route",
    )(tok, wqt, sub)


def _gelu_exact(x):
    return 0.5 * x * (1.0 + lax.erf(x * (1.0 / math.sqrt(2.0))))


def _peer_gate_block(a_ref, w_ref, b_ref, cnt1_ref, e1_ref, rank2_ref, e2_ref, block):
    keys_per_block = PEER_EXPERT_BLOCK // PEER_KEYS
    rt = 2 * SUBLANES
    tb = a_ref.shape[1]
    for h in range(PEER_HEADS):
        for j in range(keys_per_block):
            i1 = block * keys_per_block + j
            b_ref[0, j * PEER_HEADS + h] = jnp.broadcast_to(cnt1_ref[h, pl.ds(i1, 1), :], (rt, tb)).astype(_BF16)
            b_ref[1, j * PEER_HEADS + h] = jnp.broadcast_to(e1_ref[h, pl.ds(i1, 1), :], (rt, tb)).astype(_BF16)

    def row_tile(tile, carry):
        r0 = pl.multiple_of(tile * rt, rt)
        for j0 in range(0, keys_per_block, PEER_KEY_GROUP):
            group = range(j0, j0 + PEER_KEY_GROUP)
            gates = {j: None for j in group}
            for h in range(PEER_HEADS):
                rank2 = rank2_ref[h, pl.ds(r0, rt), :]
                e2 = e2_ref[h, pl.ds(r0, rt), :]
                for j in group:
                    term = jnp.where(rank2 < b_ref[0, j * PEER_HEADS + h], b_ref[1, j * PEER_HEADS + h] * e2, jnp.zeros_like(e2))
                    gates[j] = term if gates[j] is None else gates[j] + term
            for j in group:
                rows = pl.ds(pl.multiple_of(j * PEER_KEYS + r0, rt), rt)
                w_ref[rows, :] = gates[j] * _gelu_exact(a_ref[rows, :]).astype(_BF16)
        return carry

    lax.fori_loop(0, PEER_KEYS // rt, row_tile, 0)


def _peer_expert_kernel(x_ref, u_ref, v_ref, cnt1_ref, e1_ref, rank2_ref, e2_ref, o_ref, a_scr, w_scr, b_scr):
    eb = pl.program_id(1)

    @pl.when(eb == 0)
    def _():
        o_ref[...] = jnp.zeros_like(o_ref)

    a_scr[...] = lax.dot_general(u_ref[...], x_ref[...], _NT, preferred_element_type=_F32)
    _peer_gate_block(a_scr, w_scr, b_scr, cnt1_ref, e1_ref, rank2_ref, e2_ref, eb)
    o_ref[...] += lax.dot_general(w_scr[...], v_ref[...], _TN, preferred_element_type=_F32)


def _peer_ffn(h, wq, subkeys, u_tab, v_tab):
    bsz, t, d = h.shape
    n = bsz * t
    tok = h.reshape(n, d)
    routing = _peer_route(tok, wq, subkeys)
    tb, eb = PEER_TOKEN_BLOCK, PEER_EXPERT_BLOCK
    n_experts = u_tab.shape[0]
    key_spec = pl.BlockSpec((PEER_HEADS, PEER_KEYS, tb), lambda i, e: (0, 0, i))
    tab_spec = pl.BlockSpec((eb, d), lambda i, e: (e, 0))
    out = pl.pallas_call(
        _peer_expert_kernel,
        grid=(n // tb, n_experts // eb),
        in_specs=[pl.BlockSpec((tb, d), lambda i, e: (i, 0)), tab_spec, tab_spec] + [key_spec] * 4,
        out_specs=pl.BlockSpec((tb, d), lambda i, e: (i, 0)),
        out_shape=jax.ShapeDtypeStruct((n, d), _F32),
        scratch_shapes=[pltpu.VMEM((eb, tb), _F32), pltpu.VMEM((eb, tb), _BF16),
                        pltpu.VMEM((2, (eb // PEER_KEYS) * PEER_HEADS, 2 * SUBLANES, tb), _BF16)],
        compiler_params=pltpu.CompilerParams(dimension_semantics=("parallel", "arbitrary"), vmem_limit_bytes=56 * 2**20),
        name="peer_experts",
    )(tok, u_tab.astype(_BF16), v_tab.astype(_BF16), *routing)
    return out.reshape(bsz, t, d)


def kernel(x, c, ctx, c_ctx, w_mod, b_mod, norm_pre1, norm_post1, norm_pre2, norm_post2, w_in, rw_mu, rw_w0, rw_w2, rw_a0, rw_a2, rw_g2, rw_k_k, rw_k_a, rw_r_k, rw_ln_w, rw_ln_b, ssm_conv_w, ssm_conv_b, ssm_dt_bias, ssm_a_log, ssm_d, ssm_norm_w, w_branch_a, w_branch_b, w_out, peer_wq, peer_subkeys, peer_u, peer_v):
    bsz, seq, _ = x.shape
    rows = seq // GRID_W
    depth = w_mod.shape[0]
    assert depth == 1
    rw_zero = jnp.zeros((bsz, RW_WIDTH // LANES, LANES, LANES), _F32)
    ssm_zero = jnp.zeros((bsz, SSM_GROUPS, SSM_HPG, SSM_HEADDIM, SSM_STATE), _F32)
    l = 0
    mod_x = (jax.nn.silu(c) @ w_mod[l] + b_mod[l])[:, None, :]
    mod_c = (jax.nn.silu(c_ctx) @ w_mod[l] + b_mod[l])[None, None, :]
    sh1x, sc1x, g1x, sh2x, sc2x, g2x = jnp.split(mod_x, N_MOD, axis=-1)
    sh1c, sc1c, g1c, sh2c, sc2c, g2c = jnp.split(mod_c, N_MOD, axis=-1)

    in_w = _in_proj_weights(w_in[l])
    hx = _norm_mod(x, norm_pre1[l], sc1x, sh1x)
    r_x, k_x, v_x, lora_x, z_x, xbc_x, dt_x, gate_x = _in_proj(hx, in_w, h_ssm=_raster_to_column(hx, rows))
    r_c, k_c, v_c, lora_c, z_c, xbc_c, dt_c, gate_c = _in_proj(_norm_mod(ctx, norm_pre1[l], sc1c, sh1c), in_w)

    rw_params = (rw_w0[l], rw_w2[l], rw_a0[l], rw_a2[l], rw_g2[l], rw_k_k[l], rw_k_a[l], rw_r_k[l], rw_ln_w[l], rw_ln_b[l])
    _, s_fwd, s_bwd = _rwkv7_branch(r_c, k_c, v_c, lora_c, rw_mu[l], rw_zero, rw_zero, *rw_params, want_output=False)
    ya_x, _, _ = _rwkv7_branch(r_x, k_x, v_x, lora_x, rw_mu[l], s_fwd, s_bwd, *rw_params, want_output=True)

    ssm_params = (ssm_conv_w[l], ssm_conv_b[l], ssm_dt_bias[l], ssm_a_log[l], ssm_d[l], ssm_norm_w[l])
    _, h_fwd, h_bwd = _mamba2_branch(xbc_c, dt_c, z_c, ssm_zero, ssm_zero, *ssm_params, want_output=False)
    yb_x, _, _ = _mamba2_branch(xbc_x, dt_x, z_x, h_fwd, h_bwd, *ssm_params, want_output=True)
    yb_x = _column_to_raster(yb_x, rows)

    mix_x = _merge_branches(ya_x, yb_x, gate_x, w_branch_a[l], w_branch_b[l], w_out[l])
    x, h2x = _residual(x, mix_x, g1x, norm_post1[l], next_mod=(norm_pre2[l], sc2x, sh2x))
    return _residual(x, _peer_ffn(h2x, peer_wq[l], peer_subkeys[l], peer_u[l], peer_v[l]), g2x, norm_post2[l])
```

```python
import functools
import math

import jax
import jax.numpy as jnp
from jax import lax
from jax.experimental import pallas as pl
from jax.experimental.pallas import tpu as pltpu

D_MODEL = 2048
GRID_W = 64
N_MOD = 6
NORM_EPS = 1e-6
RW_HEAD = 64
RW_WIDTH = D_MODEL
RW_HEADS = RW_WIDTH // RW_HEAD
LORA_W = 96
LORA_A = 96
LORA_G = 256
LN_X_EPS = 64e-5
SSM_WIDTH = D_MODEL
SSM_HEADDIM = 64
SSM_HEADS = SSM_WIDTH // SSM_HEADDIM
SSM_GROUPS = 8
SSM_HPG = SSM_HEADS // SSM_GROUPS
SSM_STATE = 128
SSM_CONV = 5
SSM_CHUNK = 128
PEER_HEADS = 8
PEER_KEYS = 128
PEER_TOPK = 16
PEER_QDIM = 256
PEER_HALF = PEER_QDIM // 2
PEER_BLOCK = 128
RW_COLS = 3 * RW_WIDTH + 2 * LORA_W + 2 * LORA_A + LORA_G
XBC_COLS = SSM_WIDTH + 2 * SSM_GROUPS * SSM_STATE
IN_SPLITS = (RW_COLS, RW_COLS + SSM_WIDTH, RW_COLS + SSM_WIDTH + XBC_COLS, RW_COLS + SSM_WIDTH + XBC_COLS + 2 * SSM_HEADS)
RW_SPLITS = (RW_WIDTH, 2 * RW_WIDTH, 3 * RW_WIDTH, 3 * RW_WIDTH + LORA_W, 3 * RW_WIDTH + 2 * LORA_W, 3 * RW_WIDTH + 2 * LORA_W + LORA_A, 3 * RW_WIDTH + 2 * LORA_W + 2 * LORA_A)

LANES = 128
WKV_CHUNK = 64
WKV_PAIR = LANES // RW_HEAD
WKV_GROUPS_PER_STEP = 16

_F32 = jnp.float32
_BF16 = jnp.bfloat16
_NT = (((1,), (1,)), ((), ()))
_TN = (((0,), (0,)), ((), ()))


def _mm(a, b, dims=None):
    a = a.astype(_BF16)
    b = b.astype(_BF16)
    if dims is None:
        return jnp.dot(a, b, preferred_element_type=_F32)
    return lax.dot_general(a, b, dims, preferred_element_type=_F32)


def _wkv_kernel(r_ref, lw_ref, k_ref, v_ref, a_ref, b_ref, s0_ref, y_ref, s_ref, *, reverse, chunk, groups):
    L = chunk
    L2 = WKV_PAIR * L

    @pl.when(pl.program_id(2) == 0)
    def _():
        s_ref[...] = s0_ref[...]

    ti = lax.broadcasted_iota(jnp.int32, (L, L), 0)
    tj = lax.broadcasted_iota(jnp.int32, (L, L), 1)
    before_incl = (ti <= tj) if reverse else (ti >= tj)
    cum = jnp.where(before_incl, 1.0, 0.0).astype(_BF16)

    lw = lw_ref[0]
    hi = lw.astype(_BF16)
    rem = lw - hi.astype(_F32)
    mid = rem.astype(_BF16)
    lo = (rem - mid.astype(_F32)).astype(_BF16)
    cs = (jnp.dot(cum, hi, preferred_element_type=_F32)
          + jnp.dot(cum, mid, preferred_element_type=_F32)
          + jnp.dot(cum, lo, preferred_element_type=_F32))
    tot = cs[0:1] if reverse else cs[L - 1:L]
    half = 0.5 * tot
    e_pos = jnp.exp(cs - half)
    e_neg = jnp.exp(half - cs)
    e_prev = jnp.exp(cs - lw - half)
    e_end = jnp.exp(tot - cs)
    e_half = jnp.exp(half)
    e_tot = jnp.exp(tot)

    rt = r_ref[0] * e_pos
    at = a_ref[0] * e_prev
    kt = k_ref[0] * e_neg
    bt = b_ref[0] * e_neg
    ke = k_ref[0] * e_end
    be = b_ref[0] * e_end
    vv = v_ref[0]

    lane = lax.broadcasted_iota(jnp.int32, (L, LANES), 1)
    first_head = lane < RW_HEAD

    def stack(x):
        return jnp.concatenate([jnp.where(first_head, x, 0.0), jnp.where(first_head, 0.0, x)], axis=0)

    si = lax.broadcasted_iota(jnp.int32, (L2, L2), 0)
    sj = lax.broadcasted_iota(jnp.int32, (L2, L2), 1)
    same_head = (si < L) == (sj < L)
    pi = jnp.where(si < L, si, si - L)
    pj = jnp.where(sj < L, sj, sj - L)
    incl2 = same_head & ((pi <= pj) if reverse else (pi >= pj))
    strict2 = same_head & ((pi < pj) if reverse else (pi > pj))
    eye2 = jnp.where(si == sj, 1.0, 0.0)

    gs = range(groups)
    sls = [slice(g * LANES, (g + 1) * LANES) for g in gs]
    v_s = [stack(vv[:, sl]).astype(_BF16) for sl in sls]
    lhs = [jnp.concatenate([stack(at[:, sl]), stack(rt[:, sl])], axis=0).astype(_BF16) for sl in sls]
    rhs = [jnp.concatenate([stack(kt[:, sl]), stack(bt[:, sl])], axis=0).astype(_BF16) for sl in sls]
    state = [s_ref[0, g] for g in gs]
    scores = [_mm(lhs[g], rhs[g], _NT) for g in gs]
    from_state = [_mm(lhs[g], state[g] * e_half[:, sls[g]], _NT) for g in gs]

    power = [jnp.where(strict2, -scores[g][:L2, L2:], 0.0).astype(_BF16) for g in gs]
    inv = [eye2 + power[g] for g in gs]
    ak_v = [_mm(jnp.where(strict2, scores[g][:L2, :L2], 0.0), v_s[g]) for g in gs]
    for _ in range(int(math.log2(L)) - 1):
        power = [_mm(power[g], power[g]).astype(_BF16) for g in gs]
        inv = [inv[g] + _mm(inv[g], power[g]) for g in gs]
    u_s = [_mm(inv[g], from_state[g][:L2] + ak_v[g]) for g in gs]

    vu = [jnp.concatenate([v_s[g], u_s[g].astype(_BF16)], axis=0) for g in gs]
    for g in gs:
        r_kb = jnp.concatenate([jnp.where(incl2, scores[g][L2:, :L2], 0.0),
                                jnp.where(incl2, -scores[g][L2:, L2:], 0.0)], axis=1)
        y_s = from_state[g][L2:] + _mm(r_kb, vu[g])
        y_ref[0, :, sls[g]] = y_s[:L] + y_s[L:]
    for g in gs:
        kb = jnp.concatenate([stack(ke[:, sls[g]]), -stack(be[:, sls[g]])], axis=0)
        s_ref[0, g] = state[g] * e_tot[:, sls[g]] + _mm(vu[g], kb, _TN)


def _wkv_scan(r, lw, k, v, a, b, s0, reverse):
    bsz, t, width = r.shape
    groups = WKV_GROUPS_PER_STEP
    assert t % WKV_CHUNK == 0 and width % (groups * LANES) == 0
    nc = t // WKV_CHUNK
    ngroup_steps = width // (groups * LANES)
    cidx = (lambda c: nc - 1 - c) if reverse else (lambda c: c)
    seq_spec = pl.BlockSpec((1, WKV_CHUNK, groups * LANES), lambda bi, gi, c: (bi, cidx(c), gi))
    st_spec = pl.BlockSpec((1, groups, LANES, LANES), lambda bi, gi, c: (bi, gi, 0, 0))
    return pl.pallas_call(
        functools.partial(_wkv_kernel, reverse=reverse, chunk=WKV_CHUNK, groups=groups),
        grid=(bsz, ngroup_steps, nc),
        in_specs=[seq_spec] * 6 + [st_spec],
        out_specs=[seq_spec, st_spec],
        out_shape=[jax.ShapeDtypeStruct((bsz, t, width), _F32), jax.ShapeDtypeStruct(s0.shape, _F32)],
        compiler_params=pltpu.CompilerParams(dimension_semantics=("parallel", "parallel", "arbitrary")),
        name="wkv7_rev" if reverse else "wkv7_fwd",
    )(r, lw, k, v, a, b, s0)


MM_TILE_M = 1024
MM_TILE_N = 1024
NORM_TILE_M = 512
SMALL_COLS = 768


def _matmul_kernel(a_ref, w_ref, o_ref):
    o_ref[...] = jnp.dot(a_ref[...], w_ref[...], preferred_element_type=_F32)


def _matmul(a, w):
    m, k = a.shape
    n = w.shape[1]
    tm, tn = min(m, MM_TILE_M), min(n, MM_TILE_N)
    assert m % tm == 0 and n % tn == 0 and a.dtype == _BF16 and w.dtype == _BF16
    return pl.pallas_call(
        _matmul_kernel,
        grid=(n // tn, m // tm),
        in_specs=[pl.BlockSpec((tm, k), lambda j, i: (i, 0)), pl.BlockSpec((k, tn), lambda j, i: (0, j))],
        out_specs=pl.BlockSpec((tm, tn), lambda j, i: (i, j)),
        out_shape=jax.ShapeDtypeStruct((m, n), _F32),
        compiler_params=pltpu.CompilerParams(dimension_semantics=("parallel", "parallel"), vmem_limit_bytes=48 * 2**20),
        name="matmul",
    )(a, w)


def _norm_mod_kernel(x_ref, gain_ref, sc_ref, sh_ref, o_ref):
    x = x_ref[0]
    inv = lax.rsqrt(jnp.mean(x * x, axis=-1, keepdims=True) + NORM_EPS)
    o_ref[0] = ((x * inv * gain_ref[...]) * (1.0 + sc_ref[0]) + sh_ref[0]).astype(o_ref.dtype)


def _norm_mod(x, gain, scale, shift):
    bsz, t, d = x.shape
    tm = min(t, NORM_TILE_M)
    mod_spec = pl.BlockSpec((1, 1, d), (lambda b, i: (b, 0, 0)) if scale.shape[0] == bsz else (lambda b, i: (0, 0, 0)))
    return pl.pallas_call(
        _norm_mod_kernel,
        grid=(bsz, t // tm),
        in_specs=[pl.BlockSpec((1, tm, d), lambda b, i: (b, i, 0)), pl.BlockSpec((1, d), lambda b, i: (0, 0)), mod_spec, mod_spec],
        out_specs=pl.BlockSpec((1, tm, d), lambda b, i: (b, i, 0)),
        out_shape=jax.ShapeDtypeStruct((bsz, t, d), _BF16),
        compiler_params=pltpu.CompilerParams(dimension_semantics=("parallel", "parallel")),
        name="norm_mod",
    )(x, gain.reshape(1, d), scale, shift)


def _residual_kernel(x_ref, u_ref, g_ref, gain_ref, *rest, with_next):
    u = u_ref[0]
    inv = lax.rsqrt(jnp.mean(u * u, axis=-1, keepdims=True) + NORM_EPS)
    y = x_ref[0] + g_ref[0] * (u * inv * gain_ref[...])
    if not with_next:
        rest[0][0] = y
        return
    gain2_ref, sc_ref, sh_ref, y_ref, h_ref = rest
    y_ref[0] = y
    inv2 = lax.rsqrt(jnp.mean(y * y, axis=-1, keepdims=True) + NORM_EPS)
    h_ref[0] = ((y * inv2 * gain2_ref[...]) * (1.0 + sc_ref[0]) + sh_ref[0]).astype(h_ref.dtype)


def _residual(x, u, g, gain, next_mod=None):
    bsz, t, d = x.shape
    tm = min(t, NORM_TILE_M)
    seq = pl.BlockSpec((1, tm, d), lambda b, i: (b, i, 0))
    per_batch = pl.BlockSpec((1, 1, d), lambda b, i: (b, 0, 0))
    row = pl.BlockSpec((1, d), lambda b, i: (0, 0))
    args, in_specs = [x, u, g, gain.reshape(1, d)], [seq, seq, per_batch, row]
    out_specs, out_shape = [seq], [jax.ShapeDtypeStruct((bsz, t, d), _F32)]
    if next_mod is not None:
        gain2, scale, shift = next_mod
        args += [gain2.reshape(1, d), scale, shift]
        in_specs += [row, per_batch, per_batch]
        out_specs.append(seq)
        out_shape.append(jax.ShapeDtypeStruct((bsz, t, d), _BF16))
    out = pl.pallas_call(
        functools.partial(_residual_kernel, with_next=next_mod is not None),
        grid=(bsz, t // tm),
        in_specs=in_specs,
        out_specs=out_specs,
        out_shape=out_shape,
        compiler_params=pltpu.CompilerParams(dimension_semantics=("parallel", "parallel")),
        name="residual",
    )(*args)
    return out if next_mod is not None else out[0]


def _in_proj_weights(w):
    rw_end, z_end, xbc_end, dt_end = IN_SPLITS
    lora_dt = jnp.concatenate([w[:, RW_SPLITS[2]:rw_end], w[:, xbc_end:dt_end]], axis=1)
    lora_dt = jnp.pad(lora_dt, ((0, 0), (0, SMALL_COLS - lora_dt.shape[1])))
    parts = [w[:, :RW_SPLITS[0]], w[:, RW_SPLITS[0]:RW_SPLITS[1]], w[:, RW_SPLITS[1]:RW_SPLITS[2]],
             w[:, rw_end:z_end], w[:, z_end:xbc_end], w[:, dt_end:], lora_dt]
    return [p.astype(_BF16) for p in parts]


def _in_proj(h, weights, h_ssm=None):
    bsz, t, d = h.shape
    mm = lambda u, w: _matmul(u.reshape(bsz * t, d), w).reshape(bsz, t, -1)
    w_r, w_k, w_v, w_z, w_xbc, w_gate, w_lora_dt = weights
    lora_dt = mm(h, w_lora_dt)
    dt_src = lora_dt if h_ssm is None else mm(h_ssm, w_lora_dt)
    h_ssm = h if h_ssm is None else h_ssm
    n_lora = RW_COLS - RW_SPLITS[2]
    return (mm(h, w_r), mm(h, w_k), mm(h, w_v), lora_dt, mm(h_ssm, w_z), mm(h_ssm, w_xbc),
            dt_src[..., n_lora:n_lora + 2 * SSM_HEADS], mm(h, w_gate))


def _raster_to_column(u, rows):
    b, s, ch = u.shape
    return u.reshape(b, rows, GRID_W, ch).transpose(0, 2, 1, 3).reshape(b, s, ch)


def _column_to_raster(u, rows):
    b, s, ch = u.shape
    return u.reshape(b, GRID_W, rows, ch).transpose(0, 2, 1, 3).reshape(b, s, ch)


RW_TILE_T = 128
LORA_COLS = RW_COLS - 3 * RW_WIDTH


def _head_matrices(width=RW_WIDTH, group=RW_HEAD):
    group_of_lane = jnp.arange(width) // group
    onehot = (group_of_lane[:, None] == jnp.arange(width // group)[None, :]).astype(_BF16)
    return onehot, onehot.T


def _head_sum(x, hsum):
    return sum(jnp.dot(part, hsum, preferred_element_type=_F32) for part in _split3(x))


def _head_widen(cols, hwide):
    return sum(jnp.dot(part, hwide, preferred_element_type=_F32) for part in _split3(cols))


def _token_shift_block(c_ref, p_ref, n_ref, mu, first, last):
    x = c_ref[0]
    tm = x.shape[0]
    row = lax.broadcasted_iota(jnp.int32, x.shape, 0)
    prev_row = jnp.where(first, 0.0, p_ref[0, SUBLANES - 1:SUBLANES, :])
    next_row = jnp.where(last, 0.0, n_ref[0, 0:1, :])
    prev = jnp.where(row == 0, prev_row, pltpu.roll(x, 1, 0))
    nxt = jnp.where(row == tm - 1, next_row, pltpu.roll(x, tm - 1, 0))
    return x + mu * (0.5 * (prev + nxt) - x)


def _rw_pre_kernel(rc, rp, rn, kc, kp, kn, vc, vp, vn, lc, lp, ln,
                   mu_r, mu_k, mu_v, mu_l, w0, a0, k_k, k_a, r_k, ln_b, w2, a2, g2, hsum, hwide,
                   r_o, v_o, kk_o, lwf_o, lwb_o, kf_o, kb_o, kkaf_o, kkab_o, gate_o, bonus_o):
    first = pl.program_id(1) == 0
    last = pl.program_id(1) == pl.num_programs(1) - 1
    r = _token_shift_block(rc, rp, rn, mu_r[...], first, last)
    k = _token_shift_block(kc, kp, kn, mu_k[...], first, last)
    v = _token_shift_block(vc, vp, vn, mu_v[...], first, last)
    lora = _token_shift_block(lc, lp, ln, mu_l[...], first, last)
    r_o[0] = r
    v_o[0] = v

    kk = k * k_k[...]
    norm = jnp.maximum(jnp.sqrt(_head_sum(kk * kk, hsum[...])), 1e-12)
    kk = kk * _head_widen(1.0 / norm, hwide[...])
    kk_o[0] = kk

    k_sum = None
    for d, (lw_o, kd_o, kka_o) in enumerate(((lwf_o, kf_o, kkaf_o), (lwb_o, kb_o, kkab_o))):
        wd = lora[:, d * LORA_W:(d + 1) * LORA_W]
        ad = lora[:, 2 * LORA_W + d * LORA_A:2 * LORA_W + (d + 1) * LORA_A]
        lw_o[0] = -math.exp(-0.5) * jax.nn.sigmoid(w0[d:d + 1, :] + _mm(jnp.tanh(wd), w2[d]))
        a = jax.nn.sigmoid(a0[d:d + 1, :] + _mm(ad, a2[d]))
        k_dir = k * (1.0 + (a - 1.0) * k_a[...])
        kd_o[0] = k_dir
        kka_o[0] = kk * a
        k_sum = k_dir if k_sum is None else k_sum + k_dir

    gd = lora[:, 2 * LORA_W + 2 * LORA_A:2 * LORA_W + 2 * LORA_A + LORA_G]
    gate = _mm(jax.nn.sigmoid(gd), g2[...])
    bonus = _head_widen(_head_sum(r * k_sum * r_k[...], hsum[...]), hwide[...]) * v
    gate_o[0] = gate
    bonus_o[0] = (ln_b[...] + bonus) * gate


def _rw_pre(r, k, v, lora_dt, mu, w0, w2, a0, a2, g2, k_k, k_a, r_k, ln_b):
    bsz, t, width = r.shape
    tm = RW_TILE_T
    nt = t // tm
    blocks_per_tile = tm // SUBLANES
    last_block = t // SUBLANES - 1
    lcols = lora_dt.shape[-1]

    def specs(cols):
        return [pl.BlockSpec((1, tm, cols), lambda b, i: (b, i, 0)),
                pl.BlockSpec((1, SUBLANES, cols), lambda b, i: (b, jnp.maximum(i * blocks_per_tile - 1, 0), 0)),
                pl.BlockSpec((1, SUBLANES, cols), lambda b, i: (b, jnp.minimum((i + 1) * blocks_per_tile, last_block), 0))]

    def whole(a):
        return pl.BlockSpec(a.shape, lambda b, i: (0,) * a.ndim)

    row = lambda a: a.reshape(1, -1).astype(_F32)
    mu_l = jnp.pad(mu[RW_SPLITS[2]:], (0, lcols - LORA_COLS))
    hsum, hwide = _head_matrices()
    params = [row(mu[:RW_SPLITS[0]]), row(mu[RW_SPLITS[0]:RW_SPLITS[1]]), row(mu[RW_SPLITS[1]:RW_SPLITS[2]]), row(mu_l),
              w0.astype(_F32), a0.astype(_F32), row(k_k), row(k_a), row(r_k), row(ln_b),
              w2.astype(_BF16), a2.astype(_BF16), g2.astype(_BF16), hsum, hwide]
    out_spec = pl.BlockSpec((1, tm, width), lambda b, i: (b, i, 0))
    out_shape = jax.ShapeDtypeStruct((bsz, t, width), _F32)
    return pl.pallas_call(
        _rw_pre_kernel,
        grid=(bsz, nt),
        in_specs=specs(width) * 3 + specs(lcols) + [whole(p) for p in params],
        out_specs=[out_spec] * 11,
        out_shape=[out_shape] * 11,
        compiler_params=pltpu.CompilerParams(dimension_semantics=("parallel", "parallel"), vmem_limit_bytes=56 * 2**20),
        name="rw_pre",
    )(r, r, r, k, k, k, v, v, v, lora_dt, lora_dt, lora_dt, *params)


def _rw_post_kernel(yf, yb, gate, bonus, ln_w, hsum, hwide, o_ref):
    y = yf[0] + yb[0]
    mean = _head_widen(_head_sum(y, hsum[...]), hwide[...]) * (1.0 / RW_HEAD)
    cen = y - mean
    var = _head_widen(_head_sum(cen * cen, hsum[...]), hwide[...]) * (1.0 / RW_HEAD)
    o_ref[0] = (cen * lax.rsqrt(var + LN_X_EPS) * ln_w[...] * gate[0] + bonus[0]).astype(o_ref.dtype)


def _rw_post(y_f, y_b, gate, bonus, ln_w):
    bsz, t, width = y_f.shape
    tm = RW_TILE_T
    hsum, hwide = _head_matrices()
    spec = pl.BlockSpec((1, tm, width), lambda b, i: (b, i, 0))
    whole = lambda a: pl.BlockSpec(a.shape, lambda b, i: (0,) * a.ndim)
    ln_w = ln_w.reshape(1, width).astype(_F32)
    return pl.pallas_call(
        _rw_post_kernel,
        grid=(bsz, t // tm),
        in_specs=[spec] * 4 + [whole(ln_w), whole(hsum), whole(hwide)],
        out_specs=spec,
        out_shape=jax.ShapeDtypeStruct((bsz, t, width), _BF16),
        compiler_params=pltpu.CompilerParams(dimension_semantics=("parallel", "parallel")),
        name="rw_post",
    )(y_f, y_b, gate, bonus, ln_w, hsum, hwide)


def _rwkv7_branch(r, k, v, lora_dt, mu, s_fwd, s_bwd, w0, w2, a0, a2, g2, k_k, k_a, r_k, ln_w, ln_b, want_output):
    r, v, kk, lw_f, lw_b, k_f, k_b, kka_f, kka_b, gate, bonus = _rw_pre(
        r, k, v, lora_dt, mu, w0, w2, a0, a2, g2, k_k, k_a, r_k, ln_b)
    y_f, s_fwd = _wkv_scan(r, lw_f, k_f, v, kk, kka_f, s_fwd, reverse=False)
    y_b, s_bwd = _wkv_scan(r, lw_b, k_b, v, kk, kka_b, s_bwd, reverse=True)
    out = _rw_post(y_f, y_b, gate, bonus, ln_w) if want_output else None
    return out, s_fwd, s_bwd


def _split3(x):
    hi = x.astype(_BF16)
    rem = x - hi.astype(_F32)
    mid = rem.astype(_BF16)
    lo = (rem - mid.astype(_F32)).astype(_BF16)
    return hi, mid, lo


def _ssd_kernel(xs_ref, bm_ref, cm_ref, la_ref, lat_ref, dt_ref, h0_ref, y_ref, h_ref, *, reverse):
    L = SSM_CHUNK
    P = SSM_HEADDIM

    @pl.when(pl.program_id(1) == 0)
    def _():
        h_ref[...] = h0_ref[...]

    ti = lax.broadcasted_iota(jnp.int32, (L, L), 0)
    tj = lax.broadcasted_iota(jnp.int32, (L, L), 1)
    before_incl = (ti <= tj) if reverse else (ti >= tj)
    cum = jnp.where(before_incl, 1.0, 0.0).astype(_BF16)
    cum_t = jnp.where(before_incl, 0.0, 1.0).astype(_BF16) + jnp.where(ti == tj, 1.0, 0.0).astype(_BF16)

    la = la_ref[0]
    cs = sum(jnp.dot(cum, part, preferred_element_type=_F32) for part in _split3(la))
    cs_t = sum(jnp.dot(part, cum_t, preferred_element_type=_F32) for part in _split3(lat_ref[0]))
    tot = cs[0:1] if reverse else cs[L - 1:L]
    dt = dt_ref[0]

    hi = lax.broadcasted_iota(jnp.int32, (SSM_HEADS, SSM_WIDTH), 0)
    hj = lax.broadcasted_iota(jnp.int32, (SSM_HEADS, SSM_WIDTH), 1)
    lo_edge = hi * P
    widen = jnp.where((hj >= lo_edge) & (hj < lo_edge + P), 1.0, 0.0).astype(_BF16)

    def wide(cols):
        return sum(jnp.dot(part, widen, preferred_element_type=_F32) for part in _split3(cols))

    xs = xs_ref[0]
    xdt = xs * wide(dt)
    xdt_end = (xs * wide(dt * jnp.exp(tot - cs))).astype(_BF16)
    xdt = xdt.astype(_BF16)
    decay_in = wide(jnp.exp(cs))
    e_tot = jnp.exp(tot)

    gs, hs = range(SSM_GROUPS), range(SSM_HEADS)
    cols = [slice(h * P, (h + 1) * P) for h in hs]
    bm = [bm_ref[0, :, g * SSM_STATE:(g + 1) * SSM_STATE].astype(_BF16) for g in gs]
    cm = [cm_ref[0, :, g * SSM_STATE:(g + 1) * SSM_STATE].astype(_BF16) for g in gs]
    cb = [lax.dot_general(cm[g], bm[g], _NT, preferred_element_type=_F32) for g in gs]
    state = [h_ref[0, h] for h in hs]
    from_state = [lax.dot_general(cm[h // SSM_HPG], state[h].astype(_BF16), _NT, preferred_element_type=_F32) for h in hs]
    new = [lax.dot_general(xdt_end[:, cols[h]], bm[h // SSM_HPG], _TN, preferred_element_type=_F32) for h in hs]
    pieces = []
    for h in hs:
        seg = jnp.where(before_incl, jnp.exp(cs[:, h:h + 1] - cs_t[h:h + 1, :]), 0.0)
        y_h = jnp.dot((cb[h // SSM_HPG] * seg).astype(_BF16), xdt[:, cols[h]], preferred_element_type=_F32)
        pieces.append(y_h + from_state[h] * decay_in[:, cols[h]])
    for h in hs:
        h_ref[0, h] = state[h] * e_tot[:, h:h + 1] + new[h]
    y_ref[0] = jnp.concatenate(pieces, axis=1)


def _ssd_scan(xs, bm, cm, log_a, dt, h0, reverse):
    bsz, t, width = xs.shape
    nc = t // SSM_CHUNK
    cidx = (lambda c: nc - 1 - c) if reverse else (lambda c: c)
    seq = lambda w: pl.BlockSpec((1, SSM_CHUNK, w), lambda b, c: (b, cidx(c), 0))
    st_spec = pl.BlockSpec((1,) + h0.shape[1:], lambda b, c: (b, 0, 0, 0))
    gn = SSM_GROUPS * SSM_STATE
    return pl.pallas_call(
        functools.partial(_ssd_kernel, reverse=reverse),
        grid=(bsz, nc),
        in_specs=[seq(width), seq(gn), seq(gn), seq(SSM_HEADS),
                  pl.BlockSpec((1, SSM_HEADS, SSM_CHUNK), lambda b, c: (b, 0, cidx(c))), seq(SSM_HEADS), st_spec],
        out_specs=[seq(width), st_spec],
        out_shape=[jax.ShapeDtypeStruct((bsz, t, width), _F32), jax.ShapeDtypeStruct(h0.shape, _F32)],
        compiler_params=pltpu.CompilerParams(dimension_semantics=("parallel", "arbitrary")),
        name="ssd_rev" if reverse else "ssd_fwd",
    )(xs, bm, cm, log_a, jnp.swapaxes(log_a, 1, 2), dt, h0)


SSM_TILE_T = 128


def _ssm_pre_kernel(xc, xp, xn, w_ref, b_ref, xs_o, bm_o, cm_o):
    first = pl.program_id(1) == 0
    last = pl.program_id(1) == pl.num_programs(1) - 1
    x = xc[0]
    tm = x.shape[0]
    row = lax.broadcasted_iota(jnp.int32, x.shape, 0)
    prev = jnp.where(first, 0.0, xp[0])
    nxt = jnp.where(last, 0.0, xn[0])
    half = SSM_CONV // 2
    acc = w_ref[half:half + 1, :] * x
    for off in range(1, half + 1):
        back = pltpu.roll(x, off, 0)
        fwd = pltpu.roll(x, tm - off, 0)
        for i in range(off):
            back = jnp.where(row == i, prev[SUBLANES - off + i:SUBLANES - off + i + 1, :], back)
            fwd = jnp.where(row == tm - off + i, nxt[i:i + 1, :], fwd)
        acc = acc + w_ref[half - off:half - off + 1, :] * back + w_ref[half + off:half + off + 1, :] * fwd
    u = acc + b_ref[...]
    u = u * jax.nn.sigmoid(u)
    gn = SSM_GROUPS * SSM_STATE
    xs_o[0] = u[:, :SSM_WIDTH]
    bm_o[0] = u[:, SSM_WIDTH:SSM_WIDTH + gn]
    cm_o[0] = u[:, SSM_WIDTH + gn:]


def _ssm_pre(xbc, conv_w, conv_b):
    bsz, t, cols = xbc.shape
    tm = SSM_TILE_T
    blocks_per_tile = tm // SUBLANES
    last_block = t // SUBLANES - 1
    gn = SSM_GROUPS * SSM_STATE
    seq = lambda c: pl.BlockSpec((1, tm, c), lambda b, i: (b, i, 0))
    whole = lambda a: pl.BlockSpec(a.shape, lambda b, i: (0,) * a.ndim)
    conv_w = conv_w.astype(_F32)
    conv_b = conv_b.reshape(1, cols).astype(_F32)
    return pl.pallas_call(
        _ssm_pre_kernel,
        grid=(bsz, t // tm),
        in_specs=[seq(cols),
                  pl.BlockSpec((1, SUBLANES, cols), lambda b, i: (b, jnp.maximum(i * blocks_per_tile - 1, 0), 0)),
                  pl.BlockSpec((1, SUBLANES, cols), lambda b, i: (b, jnp.minimum((i + 1) * blocks_per_tile, last_block), 0)),
                  whole(conv_w), whole(conv_b)],
        out_specs=[seq(SSM_WIDTH), seq(gn), seq(gn)],
        out_shape=[jax.ShapeDtypeStruct((bsz, t, c), _F32) for c in (SSM_WIDTH, gn, gn)],
        compiler_params=pltpu.CompilerParams(dimension_semantics=("parallel", "parallel"), vmem_limit_bytes=48 * 2**20),
        name="ssm_pre",
    )(xbc, xbc, xbc, conv_w, conv_b)


def _ssm_post_kernel(yf, yb, xs, z, d_ref, gain_ref, gsum, gwide, o_ref):
    zz = z[0]
    u = (yf[0] + yb[0] + d_ref[...] * xs[0]) * (zz * jax.nn.sigmoid(zz))
    ms = _head_widen(_head_sum(u * u, gsum[...]), gwide[...]) * (SSM_GROUPS / SSM_WIDTH)
    o_ref[0] = (u * lax.rsqrt(ms + NORM_EPS) * gain_ref[...]).astype(o_ref.dtype)


def _ssm_post(y_f, y_b, xs, z, d_skip, gain):
    bsz, t, width = y_f.shape
    tm = SSM_TILE_T
    gsum, gwide = _head_matrices(width, width // SSM_GROUPS)
    seq = pl.BlockSpec((1, tm, width), lambda b, i: (b, i, 0))
    whole = lambda a: pl.BlockSpec(a.shape, lambda b, i: (0,) * a.ndim)
    d_wide = jnp.repeat(d_skip, SSM_HEADDIM).reshape(1, width).astype(_F32)
    gain = gain.reshape(1, width).astype(_F32)
    return pl.pallas_call(
        _ssm_post_kernel,
        grid=(bsz, t // tm),
        in_specs=[seq] * 4 + [whole(d_wide), whole(gain), whole(gsum), whole(gwide)],
        out_specs=seq,
        out_shape=jax.ShapeDtypeStruct((bsz, t, width), _BF16),
        compiler_params=pltpu.CompilerParams(dimension_semantics=("parallel", "parallel")),
        name="ssm_post",
    )(y_f, y_b, xs, z, d_wide, gain, gsum, gwide)


def _mamba2_branch(xbc, dt_raw, z, h0_fwd, h0_bwd, conv_w, conv_b, dt_bias, a_log, d_skip, norm_w, want_output):
    bsz, t, _ = xbc.shape
    xs, bm, cm = _ssm_pre(xbc, conv_w, conv_b)
    dt = jax.nn.softplus(dt_raw.astype(_F32).reshape(bsz, t, 2, SSM_HEADS) + dt_bias)
    log_a = -jnp.exp(a_log) * dt
    heads = lambda h: h.reshape(bsz, SSM_HEADS, SSM_HEADDIM, SSM_STATE)
    y_f, h_f = _ssd_scan(xs, bm, cm, log_a[:, :, 0], dt[:, :, 0], heads(h0_fwd), reverse=False)
    y_b, h_b = _ssd_scan(xs, bm, cm, log_a[:, :, 1], dt[:, :, 1], heads(h0_bwd), reverse=True)
    out = _ssm_post(y_f, y_b, xs, z, d_skip, norm_w) if want_output else None
    return out, h_f, h_b


def _merge_branches(y_a, y_b, gates, w_a, w_b, w_o):
    bsz, t, d = y_a.shape
    mm = lambda u, w: _matmul(u.reshape(bsz * t, -1).astype(_BF16), w.astype(_BF16)).reshape(bsz, t, -1)
    g_a, g_b = jnp.split(gates, 2, axis=-1)
    m = jax.nn.sigmoid(g_a) * mm(y_a, w_a) + jax.nn.sigmoid(g_b) * mm(y_b, w_b)
    return mm(m, w_o)


PEER_ROUTE_TOKENS = 256
PEER_TOKEN_BLOCK = 512
PEER_EXPERT_BLOCK = 1024
PEER_KEY_GROUP = 4
SUBLANES = 8


def _top_values(x, count):
    rows = lax.broadcasted_iota(jnp.int32, x.shape, 0)
    rank = jnp.full(x.shape, float(count), _F32)
    vals = []
    for it in range(count):
        m = jnp.max(x, axis=0, keepdims=True)
        vals.append(m)
        first = jnp.min(jnp.where(x == m, rows, x.shape[0]), axis=0, keepdims=True)
        hit = rows == first
        rank = jnp.where(hit, float(it), rank)
        x = jnp.where(hit, -jnp.inf, x)
    return vals, rank, x


def _peer_route_kernel(h_ref, wqt_ref, sub_ref, cnt1_ref, e1_ref, rank2_ref, e2_ref):
    qt = lax.dot_general(wqt_ref[...], h_ref[...].astype(_BF16), _NT, preferred_element_type=_F32)
    for h in range(PEER_HEADS):
        sc = []
        for s in range(2):
            lo = (2 * h + s) * PEER_HALF
            sc.append(jnp.dot(sub_ref[2 * h + s], qt[lo:lo + PEER_HALF].astype(_BF16), preferred_element_type=_F32))
        top_a, rank_a, _ = _top_values(sc[0], PEER_TOPK)
        top_b, rank_b, _ = _top_values(sc[1], PEER_TOPK)
        top_b = jnp.concatenate(top_b, axis=0)
        width = [PEER_TOPK // (i + 1) for i in range(PEER_TOPK)]
        cand = jnp.concatenate([top_a[i] + top_b[:width[i]] for i in range(PEER_TOPK)], axis=0)
        best, _, left = _top_values(cand, PEER_TOPK)
        taken = jnp.where(left == cand, 0.0, 1.0)
        norm = best[0] * 0.0
        for val in best:
            norm = norm + jnp.exp(val - best[0])
        cnt1 = jnp.zeros_like(rank_a)
        start = 0
        for i in range(PEER_TOPK):
            used = jnp.sum(taken[start:start + width[i]], axis=0, keepdims=True)
            cnt1 = jnp.where(rank_a == float(i), used, cnt1)
            start += width[i]
        cnt1_ref[h] = cnt1.astype(cnt1_ref.dtype)
        rank2_ref[h] = rank_b.astype(rank2_ref.dtype)
        e1_ref[h] = (jnp.exp(sc[0] - top_a[0]) / norm).astype(e1_ref.dtype)
        e2_ref[h] = jnp.exp(sc[1] - top_b[0:1]).astype(e2_ref.dtype)


def _peer_route(tok, wq, subkeys):
    n, d = tok.shape
    tr = PEER_ROUTE_TOKENS
    wqt = wq.T.astype(_BF16)
    sub = subkeys.reshape(PEER_HEADS * 2, PEER_KEYS, PEER_HALF).astype(_BF16)
    key_spec = pl.BlockSpec((PEER_HEADS, PEER_KEYS, tr), lambda i: (0, 0, i))
    row_shape = jax.ShapeDtypeStruct((PEER_HEADS, PEER_KEYS, n), _F32)
    tile_shape = jax.ShapeDtypeStruct((PEER_HEADS, PEER_KEYS, n), _BF16)
    return pl.pallas_call(
        _peer_route_kernel,
        grid=(n // tr,),
        in_specs=[pl.BlockSpec((tr, d), lambda i: (i, 0)),
                  pl.BlockSpec(wqt.shape, lambda i: (0, 0)),
                  pl.BlockSpec(sub.shape, lambda i: (0, 0, 0))],
        out_specs=[key_spec] * 4,
        out_shape=[row_shape, row_shape, tile_shape, tile_shape],
        compiler_params=pltpu.CompilerParams(dimension_semantics=("parallel",), vmem_limit_bytes=48 * 2**20),
        name="peer_route",
    )(tok, wqt, sub)


def _gelu_exact(x):
    return 0.5 * x * (1.0 + lax.erf(x * (1.0 / math.sqrt(2.0))))


def _peer_gate_block(a_ref, w_ref, b_ref, cnt1_ref, e1_ref, rank2_ref, e2_ref, block):
    keys_per_block = PEER_EXPERT_BLOCK // PEER_KEYS
    rt = 2 * SUBLANES
    tb = a_ref.shape[1]
    for h in range(PEER_HEADS):
        for j in range(keys_per_block):
            i1 = block * keys_per_block + j
            b_ref[0, j * PEER_HEADS + h] = jnp.broadcast_to(cnt1_ref[h, pl.ds(i1, 1), :], (rt, tb)).astype(_BF16)
            b_ref[1, j * PEER_HEADS + h] = jnp.broadcast_to(e1_ref[h, pl.ds(i1, 1), :], (rt, tb)).astype(_BF16)

    def row_tile(tile, carry):
        r0 = pl.multiple_of(tile * rt, rt)
        for j0 in range(0, keys_per_block, PEER_KEY_GROUP):
            group = range(j0, j0 + PEER_KEY_GROUP)
            gates = {j: None for j in group}
            for h in range(PEER_HEADS):
                rank2 = rank2_ref[h, pl.ds(r0, rt), :]
                e2 = e2_ref[h, pl.ds(r0, rt), :]
                for j in group:
                    term = jnp.where(rank2 < b_ref[0, j * PEER_HEADS + h], b_ref[1, j * PEER_HEADS + h] * e2, jnp.zeros_like(e2))
                    gates[j] = term if gates[j] is None else gates[j] + term
            for j in group:
                rows = pl.ds(pl.multiple_of(j * PEER_KEYS + r0, rt), rt)
                w_ref[rows, :] = gates[j] * _gelu_exact(a_ref[rows, :]).astype(_BF16)
        return carry

    lax.fori_loop(0, PEER_KEYS // rt, row_tile, 0)


def _peer_expert_kernel(x_ref, u_ref, v_ref, cnt1_ref, e1_ref, rank2_ref, e2_ref, o_ref, a_scr, w_scr, b_scr):
    eb = pl.program_id(1)

    @pl.when(eb == 0)
    def _():
        o_ref[...] = jnp.zeros_like(o_ref)

    a_scr[...] = lax.dot_general(u_ref[...], x_ref[...], _NT, preferred_element_type=_F32)
    _peer_gate_block(a_scr, w_scr, b_scr, cnt1_ref, e1_ref, rank2_ref, e2_ref, eb)
    o_ref[...] += lax.dot_general(w_scr[...], v_ref[...], _TN, preferred_element_type=_F32)


def _peer_ffn(h, wq, subkeys, u_tab, v_tab):
    bsz, t, d = h.shape
    n = bsz * t
    tok = h.reshape(n, d)
    routing = _peer_route(tok, wq, subkeys)
    tb, eb = PEER_TOKEN_BLOCK, PEER_EXPERT_BLOCK
    n_experts = u_tab.shape[0]
    key_spec = pl.BlockSpec((PEER_HEADS, PEER_KEYS, tb), lambda i, e: (0, 0, i))
    tab_spec = pl.BlockSpec((eb, d), lambda i, e: (e, 0))
    out = pl.pallas_call(
        _peer_expert_kernel,
        grid=(n // tb, n_experts // eb),
        in_specs=[pl.BlockSpec((tb, d), lambda i, e: (i, 0)), tab_spec, tab_spec] + [key_spec] * 4,
        out_specs=pl.BlockSpec((tb, d), lambda i, e: (i, 0)),
        out_shape=jax.ShapeDtypeStruct((n, d), _F32),
        scratch_shapes=[pltpu.VMEM((eb, tb), _F32), pltpu.VMEM((eb, tb), _BF16),
                        pltpu.VMEM((2, (eb // PEER_KEYS) * PEER_HEADS, 2 * SUBLANES, tb), _BF16)],
        compiler_params=pltpu.CompilerParams(dimension_semantics=("parallel", "arbitrary"), vmem_limit_bytes=56 * 2**20),
        name="peer_experts",
    )(tok, u_tab.astype(_BF16), v_tab.astype(_BF16), *routing)
    return out.reshape(bsz, t, d)


def kernel(x, c, ctx, c_ctx, w_mod, b_mod, norm_pre1, norm_post1, norm_pre2, norm_post2, w_in, rw_mu, rw_w0, rw_w2, rw_a0, rw_a2, rw_g2, rw_k_k, rw_k_a, rw_r_k, rw_ln_w, rw_ln_b, ssm_conv_w, ssm_conv_b, ssm_dt_bias, ssm_a_log, ssm_d, ssm_norm_w, w_branch_a, w_branch_b, w_out, peer_wq, peer_subkeys, peer_u, peer_v):
    bsz, seq, _ = x.shape
    rows = seq // GRID_W
    depth = w_mod.shape[0]
    assert depth == 1
    rw_zero = jnp.zeros((bsz, RW_WIDTH // LANES, LANES, LANES), _F32)
    ssm_zero = jnp.zeros((bsz, SSM_GROUPS, SSM_HPG, SSM_HEADDIM, SSM_STATE), _F32)
    l = 0
    mod_x = (jax.nn.silu(c) @ w_mod[l] + b_mod[l])[:, None, :]
    mod_c = (jax.nn.silu(c_ctx) @ w_mod[l] + b_mod[l])[None, None, :]
    sh1x, sc1x, g1x, sh2x, sc2x, g2x = jnp.split(mod_x, N_MOD, axis=-1)
    sh1c, sc1c, g1c, sh2c, sc2c, g2c = jnp.split(mod_c, N_MOD, axis=-1)

    in_w = _in_proj_weights(w_in[l])
    hx = _norm_mod(x, norm_pre1[l], sc1x, sh1x)
    r_x, k_x, v_x, lora_x, z_x, xbc_x, dt_x, gate_x = _in_proj(hx, in_w, h_ssm=_raster_to_column(hx, rows))
    r_c, k_c, v_c, lora_c, z_c, xbc_c, dt_c, gate_c = _in_proj(_norm_mod(ctx, norm_pre1[l], sc1c, sh1c), in_w)

    rw_params = (rw_w0[l], rw_w2[l], rw_a0[l], rw_a2[l], rw_g2[l], rw_k_k[l], rw_k_a[l], rw_r_k[l], rw_ln_w[l], rw_ln_b[l])
    _, s_fwd, s_bwd = _rwkv7_branch(r_c, k_c, v_c, lora_c, rw_mu[l], rw_zero, rw_zero, *rw_params, want_output=False)
    ya_x, _, _ = _rwkv7_branch(r_x, k_x, v_x, lora_x, rw_mu[l], s_fwd, s_bwd, *rw_params, want_output=True)

    ssm_params = (ssm_conv_w[l], ssm_conv_b[l], ssm_dt_bias[l], ssm_a_log[l], ssm_d[l], ssm_norm_w[l])
    _, h_fwd, h_bwd = _mamba2_branch(xbc_c, dt_c, z_c, ssm_zero, ssm_zero, *ssm_params, want_output=False)
    yb_x, _, _ = _mamba2_branch(xbc_x, dt_x, z_x, h_fwd, h_bwd, *ssm_params, want_output=True)
    yb_x = _column_to_raster(yb_x, rows)

    mix_x = _merge_branches(ya_x, yb_x, gate_x, w_branch_a[l], w_branch_b[l], w_out[l])
    x, h2x = _residual(x, mix_x, g1x, norm_post1[l], next_mod=(norm_pre2[l], sc2x, sh2x))
    return _residual(x, _peer_ffn(h2x, peer_wq[l], peer_subkeys[l], peer_u[l], peer_v[l]), g2x, norm_post2[l])
```

```python
import functools
import math

import jax
import jax.numpy as jnp
from jax import lax
from jax.experimental import pallas as pl
from jax.experimental.pallas import tpu as pltpu

D_MODEL = 2048
GRID_W = 64
N_MOD = 6
NORM_EPS = 1e-6
RW_HEAD = 64
RW_WIDTH = D_MODEL
RW_HEADS = RW_WIDTH // RW_HEAD
LORA_W = 96
LORA_A = 96
LORA_G = 256
LN_X_EPS = 64e-5
SSM_WIDTH = D_MODEL
SSM_HEADDIM = 64
SSM_HEADS = SSM_WIDTH // SSM_HEADDIM
SSM_GROUPS = 8
SSM_HPG = SSM_HEADS // SSM_GROUPS
SSM_STATE = 128
SSM_CONV = 5
SSM_CHUNK = 128
PEER_HEADS = 8
PEER_KEYS = 128
PEER_TOPK = 16
PEER_QDIM = 256
PEER_HALF = PEER_QDIM // 2
PEER_BLOCK = 128
RW_COLS = 3 * RW_WIDTH + 2 * LORA_W + 2 * LORA_A + LORA_G
XBC_COLS = SSM_WIDTH + 2 * SSM_GROUPS * SSM_STATE
IN_SPLITS = (RW_COLS, RW_COLS + SSM_WIDTH, RW_COLS + SSM_WIDTH + XBC_COLS, RW_COLS + SSM_WIDTH + XBC_COLS + 2 * SSM_HEADS)
RW_SPLITS = (RW_WIDTH, 2 * RW_WIDTH, 3 * RW_WIDTH, 3 * RW_WIDTH + LORA_W, 3 * RW_WIDTH + 2 * LORA_W, 3 * RW_WIDTH + 2 * LORA_W + LORA_A, 3 * RW_WIDTH + 2 * LORA_W + 2 * LORA_A)

LANES = 128
WKV_CHUNK = 64
WKV_PAIR = LANES // RW_HEAD
WKV_GROUPS_PER_STEP = 16

_F32 = jnp.float32
_BF16 = jnp.bfloat16
_NT = (((1,), (1,)), ((), ()))
_TN = (((0,), (0,)), ((), ()))


def _mm(a, b, dims=None):
    a = a.astype(_BF16)
    b = b.astype(_BF16)
    if dims is None:
        return jnp.dot(a, b, preferred_element_type=_F32)
    return lax.dot_general(a, b, dims, preferred_element_type=_F32)


def _wkv_kernel(r_ref, lw_ref, k_ref, v_ref, a_ref, b_ref, s0_ref, y_ref, s_ref, *, reverse, chunk, groups):
    L = chunk
    L2 = WKV_PAIR * L

    @pl.when(pl.program_id(2) == 0)
    def _():
        s_ref[...] = s0_ref[...]

    ti = lax.broadcasted_iota(jnp.int32, (L, L), 0)
    tj = lax.broadcasted_iota(jnp.int32, (L, L), 1)
    before_incl = (ti <= tj) if reverse else (ti >= tj)
    cum = jnp.where(before_incl, 1.0, 0.0).astype(_BF16)

    lw = lw_ref[0]
    hi = lw.astype(_BF16)
    rem = lw - hi.astype(_F32)
    mid = rem.astype(_BF16)
    lo = (rem - mid.astype(_F32)).astype(_BF16)
    cs = (jnp.dot(cum, hi, preferred_element_type=_F32)
          + jnp.dot(cum, mid, preferred_element_type=_F32)
          + jnp.dot(cum, lo, preferred_element_type=_F32))
    tot = cs[0:1] if reverse else cs[L - 1:L]
    half = 0.5 * tot
    e_pos = jnp.exp(cs - half)
    e_neg = jnp.exp(half - cs)
    e_prev = jnp.exp(cs - lw - half)
    e_end = jnp.exp(tot - cs)
    e_half = jnp.exp(half)
    e_tot = jnp.exp(tot)

    rt = r_ref[0] * e_pos
    at = a_ref[0] * e_prev
    kt = k_ref[0] * e_neg
    bt = b_ref[0] * e_neg
    ke = k_ref[0] * e_end
    be = b_ref[0] * e_end
    vv = v_ref[0]

    lane = lax.broadcasted_iota(jnp.int32, (L, LANES), 1)
    first_head = lane < RW_HEAD

    def stack(x):
        return jnp.concatenate([jnp.where(first_head, x, 0.0), jnp.where(first_head, 0.0, x)], axis=0)

    si = lax.broadcasted_iota(jnp.int32, (L2, L2), 0)
    sj = lax.broadcasted_iota(jnp.int32, (L2, L2), 1)
    same_head = (si < L) == (sj < L)
    pi = jnp.where(si < L, si, si - L)
    pj = jnp.where(sj < L, sj, sj - L)
    incl2 = same_head & ((pi <= pj) if reverse else (pi >= pj))
    strict2 = same_head & ((pi < pj) if reverse else (pi > pj))
    eye2 = jnp.where(si == sj, 1.0, 0.0)

    gs = range(groups)
    sls = [slice(g * LANES, (g + 1) * LANES) for g in gs]
    v_s = [stack(vv[:, sl]).astype(_BF16) for sl in sls]
    lhs = [jnp.concatenate([stack(at[:, sl]), stack(rt[:, sl])], axis=0).astype(_BF16) for sl in sls]
    rhs = [jnp.concatenate([stack(kt[:, sl]), stack(bt[:, sl])], axis=0).astype(_BF16) for sl in sls]
    state = [s_ref[0, g] for g in gs]
    scores = [_mm(lhs[g], rhs[g], _NT) for g in gs]
    from_state = [_mm(lhs[g], state[g] * e_half[:, sls[g]], _NT) for g in gs]

    power = [jnp.where(strict2, -scores[g][:L2, L2:], 0.0).astype(_BF16) for g in gs]
    inv = [eye2 + power[g] for g in gs]
    ak_v = [_mm(jnp.where(strict2, scores[g][:L2, :L2], 0.0), v_s[g]) for g in gs]
    for _ in range(int(math.log2(L)) - 1):
        power = [_mm(power[g], power[g]).astype(_BF16) for g in gs]
        inv = [inv[g] + _mm(inv[g], power[g]) for g in gs]
    u_s = [_mm(inv[g], from_state[g][:L2] + ak_v[g]) for g in gs]

    vu = [jnp.concatenate([v_s[g], u_s[g].astype(_BF16)], axis=0) for g in gs]
    for g in gs:
        r_kb = jnp.concatenate([jnp.where(incl2, scores[g][L2:, :L2], 0.0),
                                jnp.where(incl2, -scores[g][L2:, L2:], 0.0)], axis=1)
        y_s = from_state[g][L2:] + _mm(r_kb, vu[g])
        y_ref[0, :, sls[g]] = y_s[:L] + y_s[L:]
    for g in gs:
        kb = jnp.concatenate([stack(ke[:, sls[g]]), -stack(be[:, sls[g]])], axis=0)
        s_ref[0, g] = state[g] * e_tot[:, sls[g]] + _mm(vu[g], kb, _TN)


def _wkv_scan(r, lw, k, v, a, b, s0, reverse):
    bsz, t, width = r.shape
    groups = WKV_GROUPS_PER_STEP
    assert t % WKV_CHUNK == 0 and width % (groups * LANES) == 0
    nc = t // WKV_CHUNK
    ngroup_steps = width // (groups * LANES)
    cidx = (lambda c: nc - 1 - c) if reverse else (lambda c: c)
    seq_spec = pl.BlockSpec((1, WKV_CHUNK, groups * LANES), lambda bi, gi, c: (bi, cidx(c), gi))
    st_spec = pl.BlockSpec((1, groups, LANES, LANES), lambda bi, gi, c: (bi, gi, 0, 0))
    return pl.pallas_call(
        functools.partial(_wkv_kernel, reverse=reverse, chunk=WKV_CHUNK, groups=groups),
        grid=(bsz, ngroup_steps, nc),
        in_specs=[seq_spec] * 6 + [st_spec],
        out_specs=[seq_spec, st_spec],
        out_shape=[jax.ShapeDtypeStruct((bsz, t, width), _F32), jax.ShapeDtypeStruct(s0.shape, _F32)],
        compiler_params=pltpu.CompilerParams(dimension_semantics=("parallel", "parallel", "arbitrary")),
        name="wkv7_rev" if reverse else "wkv7_fwd",
    )(r, lw, k, v, a, b, s0)


MM_TILE_M = 1024
MM_TILE_N = 1024
NORM_TILE_M = 512
SMALL_COLS = 768


def _matmul_kernel(a_ref, w_ref, o_ref):
    o_ref[...] = jnp.dot(a_ref[...], w_ref[...], preferred_element_type=_F32)


def _matmul(a, w):
    m, k = a.shape
    n = w.shape[1]
    tm, tn = min(m, MM_TILE_M), min(n, MM_TILE_N)
    assert m % tm == 0 and n % tn == 0 and a.dtype == _BF16 and w.dtype == _BF16
    return pl.pallas_call(
        _matmul_kernel,
        grid=(n // tn, m // tm),
        in_specs=[pl.BlockSpec((tm, k), lambda j, i: (i, 0)), pl.BlockSpec((k, tn), lambda j, i: (0, j))],
        out_specs=pl.BlockSpec((tm, tn), lambda j, i: (i, j)),
        out_shape=jax.ShapeDtypeStruct((m, n), _F32),
        compiler_params=pltpu.CompilerParams(dimension_semantics=("parallel", "parallel"), vmem_limit_bytes=48 * 2**20),
        name="matmul",
    )(a, w)


def _norm_mod_kernel(x_ref, gain_ref, sc_ref, sh_ref, o_ref):
    x = x_ref[0]
    inv = lax.rsqrt(jnp.mean(x * x, axis=-1, keepdims=True) + NORM_EPS)
    o_ref[0] = ((x * inv * gain_ref[...]) * (1.0 + sc_ref[0]) + sh_ref[0]).astype(o_ref.dtype)


def _norm_mod(x, gain, scale, shift):
    bsz, t, d = x.shape
    tm = min(t, NORM_TILE_M)
    mod_spec = pl.BlockSpec((1, 1, d), (lambda b, i: (b, 0, 0)) if scale.shape[0] == bsz else (lambda b, i: (0, 0, 0)))
    return pl.pallas_call(
        _norm_mod_kernel,
        grid=(bsz, t // tm),
        in_specs=[pl.BlockSpec((1, tm, d), lambda b, i: (b, i, 0)), pl.BlockSpec((1, d), lambda b, i: (0, 0)), mod_spec, mod_spec],
        out_specs=pl.BlockSpec((1, tm, d), lambda b, i: (b, i, 0)),
        out_shape=jax.ShapeDtypeStruct((bsz, t, d), _BF16),
        compiler_params=pltpu.CompilerParams(dimension_semantics=("parallel", "parallel")),
        name="norm_mod",
    )(x, gain.reshape(1, d), scale, shift)


def _residual_kernel(x_ref, u_ref, g_ref, gain_ref, *rest, with_next):
    u = u_ref[0]
    inv = lax.rsqrt(jnp.mean(u * u, axis=-1, keepdims=True) + NORM_EPS)
    y = x_ref[0] + g_ref[0] * (u * inv * gain_ref[...])
    if not with_next:
        rest[0][0] = y
        return
    gain2_ref, sc_ref, sh_ref, y_ref, h_ref = rest
    y_ref[0] = y
    inv2 = lax.rsqrt(jnp.mean(y * y, axis=-1, keepdims=True) + NORM_EPS)
    h_ref[0] = ((y * inv2 * gain2_ref[...]) * (1.0 + sc_ref[0]) + sh_ref[0]).astype(h_ref.dtype)


def _residual(x, u, g, gain, next_mod=None):
    bsz, t, d = x.shape
    tm = min(t, NORM_TILE_M)
    seq = pl.BlockSpec((1, tm, d), lambda b, i: (b, i, 0))
    per_batch = pl.BlockSpec((1, 1, d), lambda b, i: (b, 0, 0))
    row = pl.BlockSpec((1, d), lambda b, i: (0, 0))
    args, in_specs = [x, u, g, gain.reshape(1, d)], [seq, seq, per_batch, row]
    out_specs, out_shape = [seq], [jax.ShapeDtypeStruct((bsz, t, d), _F32)]
    if next_mod is not None:
        gain2, scale, shift = next_mod
        args += [gain2.reshape(1, d), scale, shift]
        in_specs += [row, per_batch, per_batch]
        out_specs.append(seq)
        out_shape.append(jax.ShapeDtypeStruct((bsz, t, d), _BF16))
    out = pl.pallas_call(
        functools.partial(_residual_kernel, with_next=next_mod is not None),
        grid=(bsz, t // tm),
        in_specs=in_specs,
        out_specs=out_specs,
        out_shape=out_shape,
        compiler_params=pltpu.CompilerParams(dimension_semantics=("parallel", "parallel")),
        name="residual",
    )(*args)
    return out if next_mod is not None else out[0]


def _in_proj_weights(w):
    rw_end, z_end, xbc_end, dt_end = IN_SPLITS
    lora_dt = jnp.concatenate([w[:, RW_SPLITS[2]:rw_end], w[:, xbc_end:dt_end]], axis=1)
    lora_dt = jnp.pad(lora_dt, ((0, 0), (0, SMALL_COLS - lora_dt.shape[1])))
    parts = [w[:, :RW_SPLITS[0]], w[:, RW_SPLITS[0]:RW_SPLITS[1]], w[:, RW_SPLITS[1]:RW_SPLITS[2]],
             w[:, rw_end:z_end], w[:, z_end:xbc_end], w[:, dt_end:], lora_dt]
    return [p.astype(_BF16) for p in parts]


def _in_proj(h, weights, h_ssm=None):
    bsz, t, d = h.shape
    mm = lambda u, w: _matmul(u.reshape(bsz * t, d), w).reshape(bsz, t, -1)
    w_r, w_k, w_v, w_z, w_xbc, w_gate, w_lora_dt = weights
    lora_dt = mm(h, w_lora_dt)
    dt_src = lora_dt if h_ssm is None else mm(h_ssm, w_lora_dt)
    h_ssm = h if h_ssm is None else h_ssm
    n_lora = RW_COLS - RW_SPLITS[2]
    return (mm(h, w_r), mm(h, w_k), mm(h, w_v), lora_dt, mm(h_ssm, w_z), mm(h_ssm, w_xbc),
            dt_src[..., n_lora:n_lora + 2 * SSM_HEADS], mm(h, w_gate))


def _raster_to_column(u, rows):
    b, s, ch = u.shape
    return u.reshape(b, rows, GRID_W, ch).transpose(0, 2, 1, 3).reshape(b, s, ch)


def _column_to_raster(u, rows):
    b, s, ch = u.shape
    return u.reshape(b, GRID_W, rows, ch).transpose(0, 2, 1, 3).reshape(b, s, ch)


RW_TILE_T = 128
LORA_COLS = RW_COLS - 3 * RW_WIDTH


def _head_matrices(width=RW_WIDTH, group=RW_HEAD):
    group_of_lane = jnp.arange(width) // group
    onehot = (group_of_lane[:, None] == jnp.arange(width // group)[None, :]).astype(_BF16)
    return onehot, onehot.T


def _head_sum(x, hsum):
    return sum(jnp.dot(part, hsum, preferred_element_type=_F32) for part in _split3(x))


def _head_widen(cols, hwide):
    return sum(jnp.dot(part, hwide, preferred_element_type=_F32) for part in _split3(cols))


def _token_shift_block(c_ref, p_ref, n_ref, mu, first, last):
    x = c_ref[0]
    tm = x.shape[0]
    row = lax.broadcasted_iota(jnp.int32, x.shape, 0)
    prev_row = jnp.where(first, 0.0, p_ref[0, SUBLANES - 1:SUBLANES, :])
    next_row = jnp.where(last, 0.0, n_ref[0, 0:1, :])
    prev = jnp.where(row == 0, prev_row, pltpu.roll(x, 1, 0))
    nxt = jnp.where(row == tm - 1, next_row, pltpu.roll(x, tm - 1, 0))
    return x + mu * (0.5 * (prev + nxt) - x)


def _rw_pre_kernel(rc, rp, rn, kc, kp, kn, vc, vp, vn, lc, lp, ln,
                   mu_r, mu_k, mu_v, mu_l, w0, a0, k_k, k_a, r_k, ln_b, w2, a2, g2, hsum, hwide,
                   r_o, v_o, kk_o, lwf_o, lwb_o, kf_o, kb_o, kkaf_o, kkab_o, gate_o, bonus_o):
    first = pl.program_id(1) == 0
    last = pl.program_id(1) == pl.num_programs(1) - 1
    r = _token_shift_block(rc, rp, rn, mu_r[...], first, last)
    k = _token_shift_block(kc, kp, kn, mu_k[...], first, last)
    v = _token_shift_block(vc, vp, vn, mu_v[...], first, last)
    lora = _token_shift_block(lc, lp, ln, mu_l[...], first, last)
    r_o[0] = r
    v_o[0] = v

    kk = k * k_k[...]
    norm = jnp.maximum(jnp.sqrt(_head_sum(kk * kk, hsum[...])), 1e-12)
    kk = kk * _head_widen(1.0 / norm, hwide[...])
    kk_o[0] = kk

    k_sum = None
    for d, (lw_o, kd_o, kka_o) in enumerate(((lwf_o, kf_o, kkaf_o), (lwb_o, kb_o, kkab_o))):
        wd = lora[:, d * LORA_W:(d + 1) * LORA_W]
        ad = lora[:, 2 * LORA_W + d * LORA_A:2 * LORA_W + (d + 1) * LORA_A]
        lw_o[0] = -math.exp(-0.5) * jax.nn.sigmoid(w0[d:d + 1, :] + _mm(jnp.tanh(wd), w2[d]))
        a = jax.nn.sigmoid(a0[d:d + 1, :] + _mm(ad, a2[d]))
        k_dir = k * (1.0 + (a - 1.0) * k_a[...])
        kd_o[0] = k_dir
        kka_o[0] = kk * a
        k_sum = k_dir if k_sum is None else k_sum + k_dir

    gd = lora[:, 2 * LORA_W + 2 * LORA_A:2 * LORA_W + 2 * LORA_A + LORA_G]
    gate = _mm(jax.nn.sigmoid(gd), g2[...])
    bonus = _head_widen(_head_sum(r * k_sum * r_k[...], hsum[...]), hwide[...]) * v
    gate_o[0] = gate
    bonus_o[0] = (ln_b[...] + bonus) * gate


def _rw_pre(r, k, v, lora_dt, mu, w0, w2, a0, a2, g2, k_k, k_a, r_k, ln_b):
    bsz, t, width = r.shape
    tm = RW_TILE_T
    nt = t // tm
    blocks_per_tile = tm // SUBLANES
    last_block = t // SUBLANES - 1
    lcols = lora_dt.shape[-1]

    def specs(cols):
        return [pl.BlockSpec((1, tm, cols), lambda b, i: (b, i, 0)),
                pl.BlockSpec((1, SUBLANES, cols), lambda b, i: (b, jnp.maximum(i * blocks_per_tile - 1, 0), 0)),
                pl.BlockSpec((1, SUBLANES, cols), lambda b, i: (b, jnp.minimum((i + 1) * blocks_per_tile, last_block), 0))]

    def whole(a):
        return pl.BlockSpec(a.shape, lambda b, i: (0,) * a.ndim)

    row = lambda a: a.reshape(1, -1).astype(_F32)
    mu_l = jnp.pad(mu[RW_SPLITS[2]:], (0, lcols - LORA_COLS))
    hsum, hwide = _head_matrices()
    params = [row(mu[:RW_SPLITS[0]]), row(mu[RW_SPLITS[0]:RW_SPLITS[1]]), row(mu[RW_SPLITS[1]:RW_SPLITS[2]]), row(mu_l),
              w0.astype(_F32), a0.astype(_F32), row(k_k), row(k_a), row(r_k), row(ln_b),
              w2.astype(_BF16), a2.astype(_BF16), g2.astype(_BF16), hsum, hwide]
    out_spec = pl.BlockSpec((1, tm, width), lambda b, i: (b, i, 0))
    out_shape = jax.ShapeDtypeStruct((bsz, t, width), _F32)
    return pl.pallas_call(
        _rw_pre_kernel,
        grid=(bsz, nt),
        in_specs=specs(width) * 3 + specs(lcols) + [whole(p) for p in params],
        out_specs=[out_spec] * 11,
        out_shape=[out_shape] * 11,
        compiler_params=pltpu.CompilerParams(dimension_semantics=("parallel", "parallel"), vmem_limit_bytes=56 * 2**20),
        name="rw_pre",
    )(r, r, r, k, k, k, v, v, v, lora_dt, lora_dt, lora_dt, *params)


def _rw_post_kernel(yf, yb, gate, bonus, ln_w, hsum, hwide, o_ref):
    y = yf[0] + yb[0]
    mean = _head_widen(_head_sum(y, hsum[...]), hwide[...]) * (1.0 / RW_HEAD)
    cen = y - mean
    var = _head_widen(_head_sum(cen * cen, hsum[...]), hwide[...]) * (1.0 / RW_HEAD)
    o_ref[0] = (cen * lax.rsqrt(var + LN_X_EPS) * ln_w[...] * gate[0] + bonus[0]).astype(o_ref.dtype)


def _rw_post(y_f, y_b, gate, bonus, ln_w):
    bsz, t, width = y_f.shape
    tm = RW_TILE_T
    hsum, hwide = _head_matrices()
    spec = pl.BlockSpec((1, tm, width), lambda b, i: (b, i, 0))
    whole = lambda a: pl.BlockSpec(a.shape, lambda b, i: (0,) * a.ndim)
    ln_w = ln_w.reshape(1, width).astype(_F32)
    return pl.pallas_call(
        _rw_post_kernel,
        grid=(bsz, t // tm),
        in_specs=[spec] * 4 + [whole(ln_w), whole(hsum), whole(hwide)],
        out_specs=spec,
        out_shape=jax.ShapeDtypeStruct((bsz, t, width), _BF16),
        compiler_params=pltpu.CompilerParams(dimension_semantics=("parallel", "parallel")),
        name="rw_post",
    )(y_f, y_b, gate, bonus, ln_w, hsum, hwide)


def _rwkv7_branch(r, k, v, lora_dt, mu, s_fwd, s_bwd, w0, w2, a0, a2, g2, k_k, k_a, r_k, ln_w, ln_b, want_output):
    r, v, kk, lw_f, lw_b, k_f, k_b, kka_f, kka_b, gate, bonus = _rw_pre(
        r, k, v, lora_dt, mu, w0, w2, a0, a2, g2, k_k, k_a, r_k, ln_b)
    y_f, s_fwd = _wkv_scan(r, lw_f, k_f, v, kk, kka_f, s_fwd, reverse=False)
    y_b, s_bwd = _wkv_scan(r, lw_b, k_b, v, kk, kka_b, s_bwd, reverse=True)
    out = _rw_post(y_f, y_b, gate, bonus, ln_w) if want_output else None
    return out, s_fwd, s_bwd


def _split3(x):
    hi = x.astype(_BF16)
    rem = x - hi.astype(_F32)
    mid = rem.astype(_BF16)
    lo = (rem - mid.astype(_F32)).astype(_BF16)
    return hi, mid, lo


def _ssd_kernel(xs_ref, bm_ref, cm_ref, la_ref, lat_ref, dt_ref, h0_ref, y_ref, h_ref, *, reverse):
    L = SSM_CHUNK
    P = SSM_HEADDIM

    @pl.when(pl.program_id(1) == 0)
    def _():
        h_ref[...] = h0_ref[...]

    ti = lax.broadcasted_iota(jnp.int32, (L, L), 0)
    tj = lax.broadcasted_iota(jnp.int32, (L, L), 1)
    before_incl = (ti <= tj) if reverse else (ti >= tj)
    cum = jnp.where(before_incl, 1.0, 0.0).astype(_BF16)
    cum_t = jnp.where(before_incl, 0.0, 1.0).astype(_BF16) + jnp.where(ti == tj, 1.0, 0.0).astype(_BF16)

    la = la_ref[0]
    cs = sum(jnp.dot(cum, part, preferred_element_type=_F32) for part in _split3(la))
    cs_t = sum(jnp.dot(part, cum_t, preferred_element_type=_F32) for part in _split3(lat_ref[0]))
    tot = cs[0:1] if reverse else cs[L - 1:L]
    dt = dt_ref[0]

    hi = lax.broadcasted_iota(jnp.int32, (SSM_HEADS, SSM_WIDTH), 0)
    hj = lax.broadcasted_iota(jnp.int32, (SSM_HEADS, SSM_WIDTH), 1)
    lo_edge = hi * P
    widen = jnp.where((hj >= lo_edge) & (hj < lo_edge + P), 1.0, 0.0).astype(_BF16)

    def wide(cols):
        return sum(jnp.dot(part, widen, preferred_element_type=_F32) for part in _split3(cols))

    xs = xs_ref[0]
    xdt = xs * wide(dt)
    xdt_end = (xs * wide(dt * jnp.exp(tot - cs))).astype(_BF16)
    xdt = xdt.astype(_BF16)
    decay_in = wide(jnp.exp(cs))
    e_tot = jnp.exp(tot)

    gs, hs = range(SSM_GROUPS), range(SSM_HEADS)
    cols = [slice(h * P, (h + 1) * P) for h in hs]
    bm = [bm_ref[0, :, g * SSM_STATE:(g + 1) * SSM_STATE].astype(_BF16) for g in gs]
    cm = [cm_ref[0, :, g * SSM_STATE:(g + 1) * SSM_STATE].astype(_BF16) for g in gs]
    cb = [lax.dot_general(cm[g], bm[g], _NT, preferred_element_type=_F32) for g in gs]
    state = [h_ref[0, h] for h in hs]
    from_state = [lax.dot_general(cm[h // SSM_HPG], state[h].astype(_BF16), _NT, preferred_element_type=_F32) for h in hs]
    new = [lax.dot_general(xdt_end[:, cols[h]], bm[h // SSM_HPG], _TN, preferred_element_type=_F32) for h in hs]
    pieces = []
    for h in hs:
        seg = jnp.where(before_incl, jnp.exp(cs[:, h:h + 1] - cs_t[h:h + 1, :]), 0.0)
        y_h = jnp.dot((cb[h // SSM_HPG] * seg).astype(_BF16), xdt[:, cols[h]], preferred_element_type=_F32)
        pieces.append(y_h + from_state[h] * decay_in[:, cols[h]])
    for h in hs:
        h_ref[0, h] = state[h] * e_tot[:, h:h + 1] + new[h]
    y_ref[0] = jnp.concatenate(pieces, axis=1)


def _ssd_scan(xs, bm, cm, log_a, dt, h0, reverse):
    bsz, t, width = xs.shape
    nc = t // SSM_CHUNK
    cidx = (lambda c: nc - 1 - c) if reverse else (lambda c: c)
    seq = lambda w: pl.BlockSpec((1, SSM_CHUNK, w), lambda b, c: (b, cidx(c), 0))
    st_spec = pl.BlockSpec((1,) + h0.shape[1:], lambda b, c: (b, 0, 0, 0))
    gn = SSM_GROUPS * SSM_STATE
    return pl.pallas_call(
        functools.partial(_ssd_kernel, reverse=reverse),
        grid=(bsz, nc),
        in_specs=[seq(width), seq(gn), seq(gn), seq(SSM_HEADS),
                  pl.BlockSpec((1, SSM_HEADS, SSM_CHUNK), lambda b, c: (b, 0, cidx(c))), seq(SSM_HEADS), st_spec],
        out_specs=[seq(width), st_spec],
        out_shape=[jax.ShapeDtypeStruct((bsz, t, width), _F32), jax.ShapeDtypeStruct(h0.shape, _F32)],
        compiler_params=pltpu.CompilerParams(dimension_semantics=("parallel", "arbitrary")),
        name="ssd_rev" if reverse else "ssd_fwd",
    )(xs, bm, cm, log_a, jnp.swapaxes(log_a, 1, 2), dt, h0)


SSM_TILE_T = 128


def _ssm_pre_kernel(xc, xp, xn, w_ref, b_ref, xs_o, bm_o, cm_o):
    first = pl.program_id(1) == 0
    last = pl.program_id(1) == pl.num_programs(1) - 1
    x = xc[0]
    tm = x.shape[0]
    row = lax.broadcasted_iota(jnp.int32, x.shape, 0)
    prev = jnp.where(first, 0.0, xp[0])
    nxt = jnp.where(last, 0.0, xn[0])
    half = SSM_CONV // 2
    acc = w_ref[half:half + 1, :] * x
    for off in range(1, half + 1):
        back = pltpu.roll(x, off, 0)
        fwd = pltpu.roll(x, tm - off, 0)
        for i in range(off):
            back = jnp.where(row == i, prev[SUBLANES - off + i:SUBLANES - off + i + 1, :], back)
            fwd = jnp.where(row == tm - off + i, nxt[i:i + 1, :], fwd)
        acc = acc + w_ref[half - off:half - off + 1, :] * back + w_ref[half + off:half + off + 1, :] * fwd
    u = acc + b_ref[...]
    u = u * jax.nn.sigmoid(u)
    gn = SSM_GROUPS * SSM_STATE
    xs_o[0] = u[:, :SSM_WIDTH]
    bm_o[0] = u[:, SSM_WIDTH:SSM_WIDTH + gn]
    cm_o[0] = u[:, SSM_WIDTH + gn:]


def _ssm_pre(xbc, conv_w, conv_b):
    bsz, t, cols = xbc.shape
    tm = SSM_TILE_T
    blocks_per_tile = tm // SUBLANES
    last_block = t // SUBLANES - 1
    gn = SSM_GROUPS * SSM_STATE
    seq = lambda c: pl.BlockSpec((1, tm, c), lambda b, i: (b, i, 0))
    whole = lambda a: pl.BlockSpec(a.shape, lambda b, i: (0,) * a.ndim)
    conv_w = conv_w.astype(_F32)
    conv_b = conv_b.reshape(1, cols).astype(_F32)
    return pl.pallas_call(
        _ssm_pre_kernel,
        grid=(bsz, t // tm),
        in_specs=[seq(cols),
                  pl.BlockSpec((1, SUBLANES, cols), lambda b, i: (b, jnp.maximum(i * blocks_per_tile - 1, 0), 0)),
                  pl.BlockSpec((1, SUBLANES, cols), lambda b, i: (b, jnp.minimum((i + 1) * blocks_per_tile, last_block), 0)),
                  whole(conv_w), whole(conv_b)],
        out_specs=[seq(SSM_WIDTH), seq(gn), seq(gn)],
        out_shape=[jax.ShapeDtypeStruct((bsz, t, c), _F32) for c in (SSM_WIDTH, gn, gn)],
        compiler_params=pltpu.CompilerParams(dimension_semantics=("parallel", "parallel"), vmem_limit_bytes=48 * 2**20),
        name="ssm_pre",
    )(xbc, xbc, xbc, conv_w, conv_b)


def _ssm_post_kernel(yf, yb, xs, z, d_ref, gain_ref, gsum, gwide, o_ref):
    zz = z[0]
    u = (yf[0] + yb[0] + d_ref[...] * xs[0]) * (zz * jax.nn.sigmoid(zz))
    ms = _head_widen(_head_sum(u * u, gsum[...]), gwide[...]) * (SSM_GROUPS / SSM_WIDTH)
    o_ref[0] = (u * lax.rsqrt(ms + NORM_EPS) * gain_ref[...]).astype(o_ref.dtype)


def _ssm_post(y_f, y_b, xs, z, d_skip, gain):
    bsz, t, width = y_f.shape
    tm = SSM_TILE_T
    gsum, gwide = _head_matrices(width, width // SSM_GROUPS)
    seq = pl.BlockSpec((1, tm, width), lambda b, i: (b, i, 0))
    whole = lambda a: pl.BlockSpec(a.shape, lambda b, i: (0,) * a.ndim)
    d_wide = jnp.repeat(d_skip, SSM_HEADDIM).reshape(1, width).astype(_F32)
    gain = gain.reshape(1, width).astype(_F32)
    return pl.pallas_call(
        _ssm_post_kernel,
        grid=(bsz, t // tm),
        in_specs=[seq] * 4 + [whole(d_wide), whole(gain), whole(gsum), whole(gwide)],
        out_specs=seq,
        out_shape=jax.ShapeDtypeStruct((bsz, t, width), _BF16),
        compiler_params=pltpu.CompilerParams(dimension_semantics=("parallel", "parallel")),
        name="ssm_post",
    )(y_f, y_b, xs, z, d_wide, gain, gsum, gwide)


def _mamba2_branch(xbc, dt_raw, z, h0_fwd, h0_bwd, conv_w, conv_b, dt_bias, a_log, d_skip, norm_w, want_output):
    bsz, t, _ = xbc.shape
    xs, bm, cm = _ssm_pre(xbc, conv_w, conv_b)
    dt = jax.nn.softplus(dt_raw.astype(_F32).reshape(bsz, t, 2, SSM_HEADS) + dt_bias)
    log_a = -jnp.exp(a_log) * dt
    heads = lambda h: h.reshape(bsz, SSM_HEADS, SSM_HEADDIM, SSM_STATE)
    y_f, h_f = _ssd_scan(xs, bm, cm, log_a[:, :, 0], dt[:, :, 0], heads(h0_fwd), reverse=False)
    y_b, h_b = _ssd_scan(xs, bm, cm, log_a[:, :, 1], dt[:, :, 1], heads(h0_bwd), reverse=True)
    out = _ssm_post(y_f, y_b, xs, z, d_skip, norm_w) if want_output else None
    return out, h_f, h_b


def _gated_pair_kernel(ya_ref, yb_ref, wa_ref, wb_ref, ga_ref, gb_ref, o_ref):
    pa = jnp.dot(ya_ref[...], wa_ref[...], preferred_element_type=_F32)
    pb = jnp.dot(yb_ref[...], wb_ref[...], preferred_element_type=_F32)
    o_ref[...] = (jax.nn.sigmoid(ga_ref[...]) * pa + jax.nn.sigmoid(gb_ref[...]) * pb).astype(o_ref.dtype)


def _merge_branches(y_a, y_b, gates, w_a, w_b, w_o):
    bsz, t, d = y_a.shape
    m, n = bsz * t, w_a.shape[1]
    tm, tn = min(m, NORM_TILE_M), min(n, MM_TILE_N)
    assert m % tm == 0 and n % tn == 0 and gates.shape[-1] == 2 * n
    rows = pl.BlockSpec((tm, d), lambda j, i: (i, 0))
    cols = pl.BlockSpec((d, tn), lambda j, i: (0, j))
    mixed = pl.pallas_call(
        _gated_pair_kernel,
        grid=(n // tn, m // tm),
        in_specs=[rows, rows, cols, cols,
                  pl.BlockSpec((tm, tn), lambda j, i: (i, j)),
                  pl.BlockSpec((tm, tn), lambda j, i: (i, n // tn + j))],
        out_specs=pl.BlockSpec((tm, tn), lambda j, i: (i, j)),
        out_shape=jax.ShapeDtypeStruct((m, n), _BF16),
        compiler_params=pltpu.CompilerParams(dimension_semantics=("parallel", "parallel"), vmem_limit_bytes=56 * 2**20),
        name="gated_pair",
    )(y_a.reshape(m, d), y_b.reshape(m, d), w_a.astype(_BF16), w_b.astype(_BF16), gates.reshape(m, 2 * n), gates.reshape(m, 2 * n))
    return _matmul(mixed, w_o.astype(_BF16)).reshape(bsz, t, -1)


PEER_ROUTE_TOKENS = 256
PEER_TOKEN_BLOCK = 512
PEER_EXPERT_BLOCK = 1024
PEER_KEY_GROUP = 4
SUBLANES = 8


def _top_values(x, count):
    rows = lax.broadcasted_iota(jnp.int32, x.shape, 0)
    rank = jnp.full(x.shape, float(count), _F32)
    vals = []
    for it in range(count):
        m = jnp.max(x, axis=0, keepdims=True)
        vals.append(m)
        first = jnp.min(jnp.where(x == m, rows, x.shape[0]), axis=0, keepdims=True)
        hit = rows == first
        rank = jnp.where(hit, float(it), rank)
        x = jnp.where(hit, -jnp.inf, x)
    return vals, rank, x


def _peer_route_kernel(h_ref, wqt_ref, sub_ref, cnt1_ref, e1_ref, rank2_ref, e2_ref):
    qt = lax.dot_general(wqt_ref[...], h_ref[...].astype(_BF16), _NT, preferred_element_type=_F32)
    for h in range(PEER_HEADS):
        sc = []
        for s in range(2):
            lo = (2 * h + s) * PEER_HALF
            sc.append(jnp.dot(sub_ref[2 * h + s], qt[lo:lo + PEER_HALF].astype(_BF16), preferred_element_type=_F32))
        top_a, rank_a, _ = _top_values(sc[0], PEER_TOPK)
        top_b, rank_b, _ = _top_values(sc[1], PEER_TOPK)
        top_b = jnp.concatenate(top_b, axis=0)
        width = [PEER_TOPK // (i + 1) for i in range(PEER_TOPK)]
        cand = jnp.concatenate([top_a[i] + top_b[:width[i]] for i in range(PEER_TOPK)], axis=0)
        best, _, left = _top_values(cand, PEER_TOPK)
        taken = jnp.where(left == cand, 0.0, 1.0)
        norm = best[0] * 0.0
        for val in best:
            norm = norm + jnp.exp(val - best[0])
        cnt1 = jnp.zeros_like(rank_a)
        start = 0
        for i in range(PEER_TOPK):
            used = jnp.sum(taken[start:start + width[i]], axis=0, keepdims=True)
            cnt1 = jnp.where(rank_a == float(i), used, cnt1)
            start += width[i]
        cnt1_ref[h] = cnt1.astype(cnt1_ref.dtype)
        rank2_ref[h] = rank_b.astype(rank2_ref.dtype)
        e1_ref[h] = (jnp.exp(sc[0] - top_a[0]) / norm).astype(e1_ref.dtype)
        e2_ref[h] = jnp.exp(sc[1] - top_b[0:1]).astype(e2_ref.dtype)


def _peer_route(tok, wq, subkeys):
    n, d = tok.shape
    tr = PEER_ROUTE_TOKENS
    wqt = wq.T.astype(_BF16)
    sub = subkeys.reshape(PEER_HEADS * 2, PEER_KEYS, PEER_HALF).astype(_BF16)
    key_spec = pl.BlockSpec((PEER_HEADS, PEER_KEYS, tr), lambda i: (0, 0, i))
    row_shape = jax.ShapeDtypeStruct((PEER_HEADS, PEER_KEYS, n), _F32)
    tile_shape = jax.ShapeDtypeStruct((PEER_HEADS, PEER_KEYS, n), _BF16)
    return pl.pallas_call(
        _peer_route_kernel,
        grid=(n // tr,),
        in_specs=[pl.BlockSpec((tr, d), lambda i: (i, 0)),
                  pl.BlockSpec(wqt.shape, lambda i: (0, 0)),
                  pl.BlockSpec(sub.shape, lambda i: (0, 0, 0))],
        out_specs=[key_spec] * 4,
        out_shape=[row_shape, row_shape, tile_shape, tile_shape],
        compiler_params=pltpu.CompilerParams(dimension_semantics=("parallel",), vmem_limit_bytes=48 * 2**20),
        name="peer_route",
    )(tok, wqt, sub)


def _gelu_exact(x):
    return 0.5 * x * (1.0 + lax.erf(x * (1.0 / math.sqrt(2.0))))


def _peer_gate_block(a_ref, w_ref, b_ref, cnt1_ref, e1_ref, rank2_ref, e2_ref, block):
    keys_per_block = PEER_EXPERT_BLOCK // PEER_KEYS
    rt = 2 * SUBLANES
    tb = a_ref.shape[1]
    for h in range(PEER_HEADS):
        for j in range(keys_per_block):
            i1 = block * keys_per_block + j
            b_ref[0, j * PEER_HEADS + h] = jnp.broadcast_to(cnt1_ref[h, pl.ds(i1, 1), :], (rt, tb)).astype(_BF16)
            b_ref[1, j * PEER_HEADS + h] = jnp.broadcast_to(e1_ref[h, pl.ds(i1, 1), :], (rt, tb)).astype(_BF16)

    def row_tile(tile, carry):
        r0 = pl.multiple_of(tile * rt, rt)
        for j0 in range(0, keys_per_block, PEER_KEY_GROUP):
            group = range(j0, j0 + PEER_KEY_GROUP)
            gates = {j: None for j in group}
            for h in range(PEER_HEADS):
                rank2 = rank2_ref[h, pl.ds(r0, rt), :]
                e2 = e2_ref[h, pl.ds(r0, rt), :]
                for j in group:
                    term = jnp.where(rank2 < b_ref[0, j * PEER_HEADS + h], b_ref[1, j * PEER_HEADS + h] * e2, jnp.zeros_like(e2))
                    gates[j] = term if gates[j] is None else gates[j] + term
            for j in group:
                rows = pl.ds(pl.multiple_of(j * PEER_KEYS + r0, rt), rt)
                w_ref[rows, :] = gates[j] * _gelu_exact(a_ref[rows, :]).astype(_BF16)
        return carry

    lax.fori_loop(0, PEER_KEYS // rt, row_tile, 0)


def _peer_expert_kernel(x_ref, u_ref, v_ref, cnt1_ref, e1_ref, rank2_ref, e2_ref, o_ref, a_scr, w_scr, b_scr):
    eb = pl.program_id(1)

    @pl.when(eb == 0)
    def _():
        o_ref[...] = jnp.zeros_like(o_ref)

    a_scr[...] = lax.dot_general(u_ref[...], x_ref[...], _NT, preferred_element_type=_F32)
    _peer_gate_block(a_scr, w_scr, b_scr, cnt1_ref, e1_ref, rank2_ref, e2_ref, eb)
    o_ref[...] += lax.dot_general(w_scr[...], v_ref[...], _TN, preferred_element_type=_F32)


def _peer_ffn(h, wq, subkeys, u_tab, v_tab):
    bsz, t, d = h.shape
    n = bsz * t
    tok = h.reshape(n, d)
    routing = _peer_route(tok, wq, subkeys)
    tb, eb = PEER_TOKEN_BLOCK, PEER_EXPERT_BLOCK
    n_experts = u_tab.shape[0]
    key_spec = pl.BlockSpec((PEER_HEADS, PEER_KEYS, tb), lambda i, e: (0, 0, i))
    tab_spec = pl.BlockSpec((eb, d), lambda i, e: (e, 0))
    out = pl.pallas_call(
        _peer_expert_kernel,
        grid=(n // tb, n_experts // eb),
        in_specs=[pl.BlockSpec((tb, d), lambda i, e: (i, 0)), tab_spec, tab_spec] + [key_spec] * 4,
        out_specs=pl.BlockSpec((tb, d), lambda i, e: (i, 0)),
        out_shape=jax.ShapeDtypeStruct((n, d), _F32),
        scratch_shapes=[pltpu.VMEM((eb, tb), _F32), pltpu.VMEM((eb, tb), _BF16),
                        pltpu.VMEM((2, (eb // PEER_KEYS) * PEER_HEADS, 2 * SUBLANES, tb), _BF16)],
        compiler_params=pltpu.CompilerParams(dimension_semantics=("parallel", "arbitrary"), vmem_limit_bytes=56 * 2**20),
        name="peer_experts",
    )(tok, u_tab.astype(_BF16), v_tab.astype(_BF16), *routing)
    return out.reshape(bsz, t, d)


def kernel(x, c, ctx, c_ctx, w_mod, b_mod, norm_pre1, norm_post1, norm_pre2, norm_post2, w_in, rw_mu, rw_w0, rw_w2, rw_a0, rw_a2, rw_g2, rw_k_k, rw_k_a, rw_r_k, rw_ln_w, rw_ln_b, ssm_conv_w, ssm_conv_b, ssm_dt_bias, ssm_a_log, ssm_d, ssm_norm_w, w_branch_a, w_branch_b, w_out, peer_wq, peer_subkeys, peer_u, peer_v):
    bsz, seq, _ = x.shape
    rows = seq // GRID_W
    depth = w_mod.shape[0]
    assert depth == 1
    rw_zero = jnp.zeros((bsz, RW_WIDTH // LANES, LANES, LANES), _F32)
    ssm_zero = jnp.zeros((bsz, SSM_GROUPS, SSM_HPG, SSM_HEADDIM, SSM_STATE), _F32)
    l = 0
    mod_x = (jax.nn.silu(c) @ w_mod[l] + b_mod[l])[:, None, :]
    mod_c = (jax.nn.silu(c_ctx) @ w_mod[l] + b_mod[l])[None, None, :]
    sh1x, sc1x, g1x, sh2x, sc2x, g2x = jnp.split(mod_x, N_MOD, axis=-1)
    sh1c, sc1c, g1c, sh2c, sc2c, g2c = jnp.split(mod_c, N_MOD, axis=-1)

    in_w = _in_proj_weights(w_in[l])
    hx = _norm_mod(x, norm_pre1[l], sc1x, sh1x)
    r_x, k_x, v_x, lora_x, z_x, xbc_x, dt_x, gate_x = _in_proj(hx, in_w, h_ssm=_raster_to_column(hx, rows))
    r_c, k_c, v_c, lora_c, z_c, xbc_c, dt_c, gate_c = _in_proj(_norm_mod(ctx, norm_pre1[l], sc1c, sh1c), in_w)

    rw_params = (rw_w0[l], rw_w2[l], rw_a0[l], rw_a2[l], rw_g2[l], rw_k_k[l], rw_k_a[l], rw_r_k[l], rw_ln_w[l], rw_ln_b[l])
    _, s_fwd, s_bwd = _rwkv7_branch(r_c, k_c, v_c, lora_c, rw_mu[l], rw_zero, rw_zero, *rw_params, want_output=False)
    ya_x, _, _ = _rwkv7_branch(r_x, k_x, v_x, lora_x, rw_mu[l], s_fwd, s_bwd, *rw_params, want_output=True)

    ssm_params = (ssm_conv_w[l], ssm_conv_b[l], ssm_dt_bias[l], ssm_a_log[l], ssm_d[l], ssm_norm_w[l])
    _, h_fwd, h_bwd = _mamba2_branch(xbc_c, dt_c, z_c, ssm_zero, ssm_zero, *ssm_params, want_output=False)
    yb_x, _, _ = _mamba2_branch(xbc_x, dt_x, z_x, h_fwd, h_bwd, *ssm_params, want_output=True)
    yb_x = _column_to_raster(yb_x, rows)

    mix_x = _merge_branches(ya_x, yb_x, gate_x, w_branch_a[l], w_branch_b[l], w_out[l])
    x, h2x = _residual(x, mix_x, g1x, norm_post1[l], next_mod=(norm_pre2[l], sc2x, sh2x))
    return _residual(x, _peer_ffn(h2x, peer_wq[l], peer_subkeys[l], peer_u[l], peer_v[l]), g2x, norm_post2[l])
```

```python
import functools
import math

import jax
import jax.numpy as jnp
from jax import lax
from jax.experimental import pallas as pl
from jax.experimental.pallas import tpu as pltpu

D_MODEL = 2048
GRID_W = 64
N_MOD = 6
NORM_EPS = 1e-6
RW_HEAD = 64
RW_WIDTH = D_MODEL
RW_HEADS = RW_WIDTH // RW_HEAD
LORA_W = 96
LORA_A = 96
LORA_G = 256
LN_X_EPS = 64e-5
SSM_WIDTH = D_MODEL
SSM_HEADDIM = 64
SSM_HEADS = SSM_WIDTH // SSM_HEADDIM
SSM_GROUPS = 8
SSM_HPG = SSM_HEADS // SSM_GROUPS
SSM_STATE = 128
SSM_CONV = 5
SSM_CHUNK = 128
PEER_HEADS = 8
PEER_KEYS = 128
PEER_TOPK = 16
PEER_QDIM = 256
PEER_HALF = PEER_QDIM // 2
RW_COLS = 3 * RW_WIDTH + 2 * LORA_W + 2 * LORA_A + LORA_G
XBC_COLS = SSM_WIDTH + 2 * SSM_GROUPS * SSM_STATE
IN_SPLITS = (RW_COLS, RW_COLS + SSM_WIDTH, RW_COLS + SSM_WIDTH + XBC_COLS, RW_COLS + SSM_WIDTH + XBC_COLS + 2 * SSM_HEADS)
RW_SPLITS = (RW_WIDTH, 2 * RW_WIDTH, 3 * RW_WIDTH, 3 * RW_WIDTH + LORA_W, 3 * RW_WIDTH + 2 * LORA_W, 3 * RW_WIDTH + 2 * LORA_W + LORA_A, 3 * RW_WIDTH + 2 * LORA_W + 2 * LORA_A)

LANES = 128
SUBLANES = 8
V7X_VMEM_BYTES = 64 * 2**20
VMEM_LIMIT_BYTES = V7X_VMEM_BYTES * 7 // 8
WKV_CHUNK = 64
WKV_PAIR = LANES // RW_HEAD
WKV_GROUPS_PER_STEP = 16

_F32 = jnp.float32
_BF16 = jnp.bfloat16
_NT = (((1,), (1,)), ((), ()))
_TN = (((0,), (0,)), ((), ()))


def _mm(a, b, dims=None):
    a = a.astype(_BF16)
    b = b.astype(_BF16)
    if dims is None:
        return jnp.dot(a, b, preferred_element_type=_F32)
    return lax.dot_general(a, b, dims, preferred_element_type=_F32)


def _wkv_kernel(r_ref, lw_ref, k_ref, v_ref, a_ref, b_ref, s0_ref, y_ref, s_ref, *, reverse, chunk, groups):
    L = chunk
    L2 = WKV_PAIR * L

    @pl.when(pl.program_id(2) == 0)
    def _():
        s_ref[...] = s0_ref[...]

    ti = lax.broadcasted_iota(jnp.int32, (L, L), 0)
    tj = lax.broadcasted_iota(jnp.int32, (L, L), 1)
    before_incl = (ti <= tj) if reverse else (ti >= tj)
    cum = jnp.where(before_incl, 1.0, 0.0).astype(_BF16)

    lw = lw_ref[0]
    hi = lw.astype(_BF16)
    rem = lw - hi.astype(_F32)
    mid = rem.astype(_BF16)
    lo = (rem - mid.astype(_F32)).astype(_BF16)
    cs = (jnp.dot(cum, hi, preferred_element_type=_F32)
          + jnp.dot(cum, mid, preferred_element_type=_F32)
          + jnp.dot(cum, lo, preferred_element_type=_F32))
    tot = cs[0:1] if reverse else cs[L - 1:L]
    half = 0.5 * tot
    e_pos = jnp.exp(cs - half)
    e_neg = jnp.exp(half - cs)
    e_prev = jnp.exp(cs - lw - half)
    e_end = jnp.exp(tot - cs)
    e_half = jnp.exp(half)
    e_tot = jnp.exp(tot)

    rt = r_ref[0] * e_pos
    at = a_ref[0] * e_prev
    kt = k_ref[0] * e_neg
    bt = b_ref[0] * e_neg
    ke = k_ref[0] * e_end
    be = b_ref[0] * e_end
    vv = v_ref[0]

    lane = lax.broadcasted_iota(jnp.int32, (L, LANES), 1)
    first_head = lane < RW_HEAD

    def stack(x):
        return jnp.concatenate([jnp.where(first_head, x, 0.0), jnp.where(first_head, 0.0, x)], axis=0)

    si = lax.broadcasted_iota(jnp.int32, (L2, L2), 0)
    sj = lax.broadcasted_iota(jnp.int32, (L2, L2), 1)
    same_head = (si < L) == (sj < L)
    pi = jnp.where(si < L, si, si - L)
    pj = jnp.where(sj < L, sj, sj - L)
    incl2 = same_head & ((pi <= pj) if reverse else (pi >= pj))
    strict2 = same_head & ((pi < pj) if reverse else (pi > pj))
    eye2 = jnp.where(si == sj, 1.0, 0.0)

    gs = range(groups)
    sls = [slice(g * LANES, (g + 1) * LANES) for g in gs]
    v_s = [stack(vv[:, sl]).astype(_BF16) for sl in sls]
    lhs = [jnp.concatenate([stack(at[:, sl]), stack(rt[:, sl])], axis=0).astype(_BF16) for sl in sls]
    rhs = [jnp.concatenate([stack(kt[:, sl]), stack(bt[:, sl])], axis=0).astype(_BF16) for sl in sls]
    state = [s_ref[0, g] for g in gs]
    scores = [_mm(lhs[g], rhs[g], _NT) for g in gs]
    from_state = [_mm(lhs[g], state[g] * e_half[:, sls[g]], _NT) for g in gs]

    power = [jnp.where(strict2, -scores[g][:L2, L2:], 0.0).astype(_BF16) for g in gs]
    inv = [eye2 + power[g] for g in gs]
    ak_v = [_mm(jnp.where(strict2, scores[g][:L2, :L2], 0.0), v_s[g]) for g in gs]
    for _ in range(int(math.log2(L)) - 1):
        power = [_mm(power[g], power[g]).astype(_BF16) for g in gs]
        inv = [inv[g] + _mm(inv[g], power[g]) for g in gs]
    u_s = [_mm(inv[g], from_state[g][:L2] + ak_v[g]) for g in gs]

    vu = [jnp.concatenate([v_s[g], u_s[g].astype(_BF16)], axis=0) for g in gs]
    for g in gs:
        r_kb = jnp.concatenate([jnp.where(incl2, scores[g][L2:, :L2], 0.0),
                                jnp.where(incl2, -scores[g][L2:, L2:], 0.0)], axis=1)
        y_s = from_state[g][L2:] + _mm(r_kb, vu[g])
        y_ref[0, :, sls[g]] = y_s[:L] + y_s[L:]
    for g in gs:
        kb = jnp.concatenate([stack(ke[:, sls[g]]), -stack(be[:, sls[g]])], axis=0)
        s_ref[0, g] = state[g] * e_tot[:, sls[g]] + _mm(vu[g], kb, _TN)


def _wkv_scan(r, lw, k, v, a, b, s0, reverse):
    bsz, t, width = r.shape
    groups = WKV_GROUPS_PER_STEP
    assert t % WKV_CHUNK == 0 and width % (groups * LANES) == 0
    nc = t // WKV_CHUNK
    ngroup_steps = width // (groups * LANES)
    cidx = (lambda c: nc - 1 - c) if reverse else (lambda c: c)
    seq_spec = pl.BlockSpec((1, WKV_CHUNK, groups * LANES), lambda bi, gi, c: (bi, cidx(c), gi))
    st_spec = pl.BlockSpec((1, groups, LANES, LANES), lambda bi, gi, c: (bi, gi, 0, 0))
    return pl.pallas_call(
        functools.partial(_wkv_kernel, reverse=reverse, chunk=WKV_CHUNK, groups=groups),
        grid=(bsz, ngroup_steps, nc),
        in_specs=[seq_spec] * 6 + [st_spec],
        out_specs=[seq_spec, st_spec],
        out_shape=[jax.ShapeDtypeStruct((bsz, t, width), _F32), jax.ShapeDtypeStruct(s0.shape, _F32)],
        compiler_params=pltpu.CompilerParams(dimension_semantics=("parallel", "parallel", "arbitrary")),
        name="wkv7_rev" if reverse else "wkv7_fwd",
    )(r, lw, k, v, a, b, s0)


MM_TILE_M = 1024
MM_TILE_N = 1024
NORM_TILE_M = 512
SMALL_COLS = 768


def _matmul_kernel(a_ref, w_ref, o_ref):
    o_ref[...] = jnp.dot(a_ref[...], w_ref[...], preferred_element_type=_F32)


def _matmul(a, w):
    m, k = a.shape
    n = w.shape[1]
    tm, tn = min(m, MM_TILE_M), min(n, MM_TILE_N)
    assert m % tm == 0 and n % tn == 0 and a.dtype == _BF16 and w.dtype == _BF16
    return pl.pallas_call(
        _matmul_kernel,
        grid=(n // tn, m // tm),
        in_specs=[pl.BlockSpec((tm, k), lambda j, i: (i, 0)), pl.BlockSpec((k, tn), lambda j, i: (0, j))],
        out_specs=pl.BlockSpec((tm, tn), lambda j, i: (i, j)),
        out_shape=jax.ShapeDtypeStruct((m, n), _F32),
        compiler_params=pltpu.CompilerParams(dimension_semantics=("parallel", "parallel"), vmem_limit_bytes=VMEM_LIMIT_BYTES),
        name="matmul",
    )(a, w)


def _norm_mod_kernel(x_ref, gain_ref, sc_ref, sh_ref, o_ref):
    x = x_ref[0]
    inv = lax.rsqrt(jnp.mean(x * x, axis=-1, keepdims=True) + NORM_EPS)
    o_ref[0] = ((x * inv * gain_ref[...]) * (1.0 + sc_ref[0]) + sh_ref[0]).astype(o_ref.dtype)


def _norm_mod(x, gain, scale, shift):
    bsz, t, d = x.shape
    tm = min(t, NORM_TILE_M)
    mod_spec = pl.BlockSpec((1, 1, d), (lambda b, i: (b, 0, 0)) if scale.shape[0] == bsz else (lambda b, i: (0, 0, 0)))
    return pl.pallas_call(
        _norm_mod_kernel,
        grid=(bsz, t // tm),
        in_specs=[pl.BlockSpec((1, tm, d), lambda b, i: (b, i, 0)), pl.BlockSpec((1, d), lambda b, i: (0, 0)), mod_spec, mod_spec],
        out_specs=pl.BlockSpec((1, tm, d), lambda b, i: (b, i, 0)),
        out_shape=jax.ShapeDtypeStruct((bsz, t, d), _BF16),
        compiler_params=pltpu.CompilerParams(dimension_semantics=("parallel", "parallel")),
        name="norm_mod",
    )(x, gain.reshape(1, d), scale, shift)


def _residual_kernel(x_ref, u_ref, g_ref, gain_ref, *rest, with_next):
    u = u_ref[0]
    inv = lax.rsqrt(jnp.mean(u * u, axis=-1, keepdims=True) + NORM_EPS)
    y = x_ref[0] + g_ref[0] * (u * inv * gain_ref[...])
    if not with_next:
        rest[0][0] = y
        return
    gain2_ref, sc_ref, sh_ref, y_ref, h_ref = rest
    y_ref[0] = y
    inv2 = lax.rsqrt(jnp.mean(y * y, axis=-1, keepdims=True) + NORM_EPS)
    h_ref[0] = ((y * inv2 * gain2_ref[...]) * (1.0 + sc_ref[0]) + sh_ref[0]).astype(h_ref.dtype)


def _residual(x, u, g, gain, next_mod=None):
    bsz, t, d = x.shape
    tm = min(t, NORM_TILE_M)
    seq = pl.BlockSpec((1, tm, d), lambda b, i: (b, i, 0))
    per_batch = pl.BlockSpec((1, 1, d), lambda b, i: (b, 0, 0))
    row = pl.BlockSpec((1, d), lambda b, i: (0, 0))
    args, in_specs = [x, u, g, gain.reshape(1, d)], [seq, seq, per_batch, row]
    out_specs, out_shape = [seq], [jax.ShapeDtypeStruct((bsz, t, d), _F32)]
    if next_mod is not None:
        gain2, scale, shift = next_mod
        args += [gain2.reshape(1, d), scale, shift]
        in_specs += [row, per_batch, per_batch]
        out_specs.append(seq)
        out_shape.append(jax.ShapeDtypeStruct((bsz, t, d), _BF16))
    out = pl.pallas_call(
        functools.partial(_residual_kernel, with_next=next_mod is not None),
        grid=(bsz, t // tm),
        in_specs=in_specs,
        out_specs=out_specs,
        out_shape=out_shape,
        compiler_params=pltpu.CompilerParams(dimension_semantics=("parallel", "parallel")),
        name="residual",
    )(*args)
    return out if next_mod is not None else out[0]


def _in_proj_weights(w):
    rw_end, z_end, xbc_end, dt_end = IN_SPLITS
    lora_dt = jnp.concatenate([w[:, RW_SPLITS[2]:rw_end], w[:, xbc_end:dt_end]], axis=1)
    lora_dt = jnp.pad(lora_dt, ((0, 0), (0, SMALL_COLS - lora_dt.shape[1])))
    parts = [w[:, :RW_SPLITS[0]], w[:, RW_SPLITS[0]:RW_SPLITS[1]], w[:, RW_SPLITS[1]:RW_SPLITS[2]],
             w[:, rw_end:z_end], w[:, z_end:xbc_end], w[:, dt_end:], lora_dt]
    return [p.astype(_BF16) for p in parts]


def _in_proj(h, weights, h_ssm=None):
    bsz, t, d = h.shape
    mm = lambda u, w: _matmul(u.reshape(bsz * t, d), w).reshape(bsz, t, -1)
    w_r, w_k, w_v, w_z, w_xbc, w_gate, w_lora_dt = weights
    lora_dt = mm(h, w_lora_dt)
    dt_src = lora_dt if h_ssm is None else mm(h_ssm, w_lora_dt)
    h_ssm = h if h_ssm is None else h_ssm
    n_lora = RW_COLS - RW_SPLITS[2]
    return (mm(h, w_r), mm(h, w_k), mm(h, w_v), lora_dt, mm(h_ssm, w_z), mm(h_ssm, w_xbc),
            dt_src[..., n_lora:n_lora + 2 * SSM_HEADS], mm(h, w_gate))


def _raster_to_column(u, rows):
    b, s, ch = u.shape
    return u.reshape(b, rows, GRID_W, ch).transpose(0, 2, 1, 3).reshape(b, s, ch)


def _column_to_raster(u, rows):
    b, s, ch = u.shape
    return u.reshape(b, GRID_W, rows, ch).transpose(0, 2, 1, 3).reshape(b, s, ch)


RW_TILE_T = 128
LORA_COLS = RW_COLS - 3 * RW_WIDTH


def _head_matrices(width=RW_WIDTH, group=RW_HEAD):
    group_of_lane = jnp.arange(width) // group
    onehot = (group_of_lane[:, None] == jnp.arange(width // group)[None, :]).astype(_BF16)
    return onehot, onehot.T


def _head_sum(x, hsum):
    return sum(jnp.dot(part, hsum, preferred_element_type=_F32) for part in _split3(x))


def _head_widen(cols, hwide):
    return sum(jnp.dot(part, hwide, preferred_element_type=_F32) for part in _split3(cols))


def _token_shift_block(c_ref, p_ref, n_ref, mu, first, last):
    x = c_ref[0]
    tm = x.shape[0]
    row = lax.broadcasted_iota(jnp.int32, x.shape, 0)
    prev_row = jnp.where(first, 0.0, p_ref[0, SUBLANES - 1:SUBLANES, :])
    next_row = jnp.where(last, 0.0, n_ref[0, 0:1, :])
    prev = jnp.where(row == 0, prev_row, pltpu.roll(x, 1, 0))
    nxt = jnp.where(row == tm - 1, next_row, pltpu.roll(x, tm - 1, 0))
    return x + mu * (0.5 * (prev + nxt) - x)


def _rw_pre_kernel(rc, rp, rn, kc, kp, kn, vc, vp, vn, lc, lp, ln,
                   mu_r, mu_k, mu_v, mu_l, w0, a0, k_k, k_a, r_k, ln_b, w2, a2, g2, hsum, hwide,
                   r_o, v_o, kk_o, lwf_o, lwb_o, kf_o, kb_o, kkaf_o, kkab_o, gate_o, bonus_o):
    first = pl.program_id(1) == 0
    last = pl.program_id(1) == pl.num_programs(1) - 1
    r = _token_shift_block(rc, rp, rn, mu_r[...], first, last)
    k = _token_shift_block(kc, kp, kn, mu_k[...], first, last)
    v = _token_shift_block(vc, vp, vn, mu_v[...], first, last)
    lora = _token_shift_block(lc, lp, ln, mu_l[...], first, last)
    r_o[0] = r
    v_o[0] = v

    kk = k * k_k[...]
    norm = jnp.maximum(jnp.sqrt(_head_sum(kk * kk, hsum[...])), 1e-12)
    kk = kk * _head_widen(1.0 / norm, hwide[...])
    kk_o[0] = kk

    k_sum = None
    for d, (lw_o, kd_o, kka_o) in enumerate(((lwf_o, kf_o, kkaf_o), (lwb_o, kb_o, kkab_o))):
        wd = lora[:, d * LORA_W:(d + 1) * LORA_W]
        ad = lora[:, 2 * LORA_W + d * LORA_A:2 * LORA_W + (d + 1) * LORA_A]
        lw_o[0] = -math.exp(-0.5) * jax.nn.sigmoid(w0[d:d + 1, :] + _mm(jnp.tanh(wd), w2[d]))
        a = jax.nn.sigmoid(a0[d:d + 1, :] + _mm(ad, a2[d]))
        k_dir = k * (1.0 + (a - 1.0) * k_a[...])
        kd_o[0] = k_dir
        kka_o[0] = kk * a
        k_sum = k_dir if k_sum is None else k_sum + k_dir

    gd = lora[:, 2 * LORA_W + 2 * LORA_A:2 * LORA_W + 2 * LORA_A + LORA_G]
    gate = _mm(jax.nn.sigmoid(gd), g2[...])
    bonus = _head_widen(_head_sum(r * k_sum * r_k[...], hsum[...]), hwide[...]) * v
    gate_o[0] = gate
    bonus_o[0] = (ln_b[...] + bonus) * gate


def _rw_pre(r, k, v, lora_dt, mu, w0, w2, a0, a2, g2, k_k, k_a, r_k, ln_b):
    bsz, t, width = r.shape
    tm = RW_TILE_T
    nt = t // tm
    blocks_per_tile = tm // SUBLANES
    last_block = t // SUBLANES - 1
    lcols = lora_dt.shape[-1]

    def specs(cols):
        return [pl.BlockSpec((1, tm, cols), lambda b, i: (b, i, 0)),
                pl.BlockSpec((1, SUBLANES, cols), lambda b, i: (b, jnp.maximum(i * blocks_per_tile - 1, 0), 0)),
                pl.BlockSpec((1, SUBLANES, cols), lambda b, i: (b, jnp.minimum((i + 1) * blocks_per_tile, last_block), 0))]

    def whole(a):
        return pl.BlockSpec(a.shape, lambda b, i: (0,) * a.ndim)

    row = lambda a: a.reshape(1, -1).astype(_F32)
    mu_l = jnp.pad(mu[RW_SPLITS[2]:], (0, lcols - LORA_COLS))
    hsum, hwide = _head_matrices()
    params = [row(mu[:RW_SPLITS[0]]), row(mu[RW_SPLITS[0]:RW_SPLITS[1]]), row(mu[RW_SPLITS[1]:RW_SPLITS[2]]), row(mu_l),
              w0.astype(_F32), a0.astype(_F32), row(k_k), row(k_a), row(r_k), row(ln_b),
              w2.astype(_BF16), a2.astype(_BF16), g2.astype(_BF16), hsum, hwide]
    out_spec = pl.BlockSpec((1, tm, width), lambda b, i: (b, i, 0))
    out_shape = jax.ShapeDtypeStruct((bsz, t, width), _F32)
    return pl.pallas_call(
        _rw_pre_kernel,
        grid=(bsz, nt),
        in_specs=specs(width) * 3 + specs(lcols) + [whole(p) for p in params],
        out_specs=[out_spec] * 11,
        out_shape=[out_shape] * 11,
        compiler_params=pltpu.CompilerParams(dimension_semantics=("parallel", "parallel"), vmem_limit_bytes=VMEM_LIMIT_BYTES),
        name="rw_pre",
    )(r, r, r, k, k, k, v, v, v, lora_dt, lora_dt, lora_dt, *params)


def _rw_post_kernel(yf, yb, gate, bonus, ln_w, hsum, hwide, o_ref):
    y = yf[0] + yb[0]
    mean = _head_widen(_head_sum(y, hsum[...]), hwide[...]) * (1.0 / RW_HEAD)
    cen = y - mean
    var = _head_widen(_head_sum(cen * cen, hsum[...]), hwide[...]) * (1.0 / RW_HEAD)
    o_ref[0] = (cen * lax.rsqrt(var + LN_X_EPS) * ln_w[...] * gate[0] + bonus[0]).astype(o_ref.dtype)


def _rw_post(y_f, y_b, gate, bonus, ln_w):
    bsz, t, width = y_f.shape
    tm = RW_TILE_T
    hsum, hwide = _head_matrices()
    spec = pl.BlockSpec((1, tm, width), lambda b, i: (b, i, 0))
    whole = lambda a: pl.BlockSpec(a.shape, lambda b, i: (0,) * a.ndim)
    ln_w = ln_w.reshape(1, width).astype(_F32)
    return pl.pallas_call(
        _rw_post_kernel,
        grid=(bsz, t // tm),
        in_specs=[spec] * 4 + [whole(ln_w), whole(hsum), whole(hwide)],
        out_specs=spec,
        out_shape=jax.ShapeDtypeStruct((bsz, t, width), _BF16),
        compiler_params=pltpu.CompilerParams(dimension_semantics=("parallel", "parallel")),
        name="rw_post",
    )(y_f, y_b, gate, bonus, ln_w, hsum, hwide)


def _rwkv7_branch(r, k, v, lora_dt, mu, s_fwd, s_bwd, w0, w2, a0, a2, g2, k_k, k_a, r_k, ln_w, ln_b, want_output):
    r, v, kk, lw_f, lw_b, k_f, k_b, kka_f, kka_b, gate, bonus = _rw_pre(
        r, k, v, lora_dt, mu, w0, w2, a0, a2, g2, k_k, k_a, r_k, ln_b)
    y_f, s_fwd = _wkv_scan(r, lw_f, k_f, v, kk, kka_f, s_fwd, reverse=False)
    y_b, s_bwd = _wkv_scan(r, lw_b, k_b, v, kk, kka_b, s_bwd, reverse=True)
    out = _rw_post(y_f, y_b, gate, bonus, ln_w) if want_output else None
    return out, s_fwd, s_bwd


def _split3(x):
    hi = x.astype(_BF16)
    rem = x - hi.astype(_F32)
    mid = rem.astype(_BF16)
    lo = (rem - mid.astype(_F32)).astype(_BF16)
    return hi, mid, lo


def _ssd_kernel(xs_ref, bm_ref, cm_ref, la_ref, lat_ref, dt_ref, h0_ref, y_ref, h_ref, *, reverse):
    L = SSM_CHUNK
    P = SSM_HEADDIM

    @pl.when(pl.program_id(1) == 0)
    def _():
        h_ref[...] = h0_ref[...]

    ti = lax.broadcasted_iota(jnp.int32, (L, L), 0)
    tj = lax.broadcasted_iota(jnp.int32, (L, L), 1)
    before_incl = (ti <= tj) if reverse else (ti >= tj)
    cum = jnp.where(before_incl, 1.0, 0.0).astype(_BF16)
    cum_t = jnp.where(before_incl, 0.0, 1.0).astype(_BF16) + jnp.where(ti == tj, 1.0, 0.0).astype(_BF16)

    la = la_ref[0]
    cs = sum(jnp.dot(cum, part, preferred_element_type=_F32) for part in _split3(la))
    cs_t = sum(jnp.dot(part, cum_t, preferred_element_type=_F32) for part in _split3(lat_ref[0]))
    tot = cs[0:1] if reverse else cs[L - 1:L]
    dt = dt_ref[0]

    hi = lax.broadcasted_iota(jnp.int32, (SSM_HEADS, SSM_WIDTH), 0)
    hj = lax.broadcasted_iota(jnp.int32, (SSM_HEADS, SSM_WIDTH), 1)
    lo_edge = hi * P
    widen = jnp.where((hj >= lo_edge) & (hj < lo_edge + P), 1.0, 0.0).astype(_BF16)

    def wide(cols):
        return sum(jnp.dot(part, widen, preferred_element_type=_F32) for part in _split3(cols))

    xs = xs_ref[0]
    xdt = xs * wide(dt)
    xdt_end = (xs * wide(dt * jnp.exp(tot - cs))).astype(_BF16)
    xdt = xdt.astype(_BF16)
    decay_in = wide(jnp.exp(cs))
    e_tot = jnp.exp(tot)

    gs, hs = range(SSM_GROUPS), range(SSM_HEADS)
    cols = [slice(h * P, (h + 1) * P) for h in hs]
    bm = [bm_ref[0, :, g * SSM_STATE:(g + 1) * SSM_STATE].astype(_BF16) for g in gs]
    cm = [cm_ref[0, :, g * SSM_STATE:(g + 1) * SSM_STATE].astype(_BF16) for g in gs]
    cb = [lax.dot_general(cm[g], bm[g], _NT, preferred_element_type=_F32) for g in gs]
    state = [h_ref[0, h] for h in hs]
    from_state = [lax.dot_general(cm[h // SSM_HPG], state[h].astype(_BF16), _NT, preferred_element_type=_F32) for h in hs]
    new = [lax.dot_general(xdt_end[:, cols[h]], bm[h // SSM_HPG], _TN, preferred_element_type=_F32) for h in hs]
    pieces = []
    for h in hs:
        seg = jnp.where(before_incl, jnp.exp(cs[:, h:h + 1] - cs_t[h:h + 1, :]), 0.0)
        y_h = jnp.dot((cb[h // SSM_HPG] * seg).astype(_BF16), xdt[:, cols[h]], preferred_element_type=_F32)
        pieces.append(y_h + from_state[h] * decay_in[:, cols[h]])
    for h in hs:
        h_ref[0, h] = state[h] * e_tot[:, h:h + 1] + new[h]
    y_ref[0] = jnp.concatenate(pieces, axis=1)


def _ssd_scan(xs, bm, cm, log_a, dt, h0, reverse):
    bsz, t, width = xs.shape
    nc = t // SSM_CHUNK
    cidx = (lambda c: nc - 1 - c) if reverse else (lambda c: c)
    seq = lambda w: pl.BlockSpec((1, SSM_CHUNK, w), lambda b, c: (b, cidx(c), 0))
    st_spec = pl.BlockSpec((1,) + h0.shape[1:], lambda b, c: (b, 0, 0, 0))
    gn = SSM_GROUPS * SSM_STATE
    return pl.pallas_call(
        functools.partial(_ssd_kernel, reverse=reverse),
        grid=(bsz, nc),
        in_specs=[seq(width), seq(gn), seq(gn), seq(SSM_HEADS),
                  pl.BlockSpec((1, SSM_HEADS, SSM_CHUNK), lambda b, c: (b, 0, cidx(c))), seq(SSM_HEADS), st_spec],
        out_specs=[seq(width), st_spec],
        out_shape=[jax.ShapeDtypeStruct((bsz, t, width), _F32), jax.ShapeDtypeStruct(h0.shape, _F32)],
        compiler_params=pltpu.CompilerParams(dimension_semantics=("parallel", "arbitrary")),
        name="ssd_rev" if reverse else "ssd_fwd",
    )(xs, bm, cm, log_a, jnp.swapaxes(log_a, 1, 2), dt, h0)


SSM_TILE_T = 128


def _ssm_pre_kernel(xc, xp, xn, w_ref, b_ref, xs_o, bm_o, cm_o):
    first = pl.program_id(1) == 0
    last = pl.program_id(1) == pl.num_programs(1) - 1
    x = xc[0]
    tm = x.shape[0]
    row = lax.broadcasted_iota(jnp.int32, x.shape, 0)
    prev = jnp.where(first, 0.0, xp[0])
    nxt = jnp.where(last, 0.0, xn[0])
    half = SSM_CONV // 2
    acc = w_ref[half:half + 1, :] * x
    for off in range(1, half + 1):
        back = pltpu.roll(x, off, 0)
        fwd = pltpu.roll(x, tm - off, 0)
        for i in range(off):
            back = jnp.where(row == i, prev[SUBLANES - off + i:SUBLANES - off + i + 1, :], back)
            fwd = jnp.where(row == tm - off + i, nxt[i:i + 1, :], fwd)
        acc = acc + w_ref[half - off:half - off + 1, :] * back + w_ref[half + off:half + off + 1, :] * fwd
    u = acc + b_ref[...]
    u = u * jax.nn.sigmoid(u)
    gn = SSM_GROUPS * SSM_STATE
    xs_o[0] = u[:, :SSM_WIDTH]
    bm_o[0] = u[:, SSM_WIDTH:SSM_WIDTH + gn]
    cm_o[0] = u[:, SSM_WIDTH + gn:]


def _ssm_pre(xbc, conv_w, conv_b):
    bsz, t, cols = xbc.shape
    tm = SSM_TILE_T
    blocks_per_tile = tm // SUBLANES
    last_block = t // SUBLANES - 1
    gn = SSM_GROUPS * SSM_STATE
    seq = lambda c: pl.BlockSpec((1, tm, c), lambda b, i: (b, i, 0))
    whole = lambda a: pl.BlockSpec(a.shape, lambda b, i: (0,) * a.ndim)
    conv_w = conv_w.astype(_F32)
    conv_b = conv_b.reshape(1, cols).astype(_F32)
    return pl.pallas_call(
        _ssm_pre_kernel,
        grid=(bsz, t // tm),
        in_specs=[seq(cols),
                  pl.BlockSpec((1, SUBLANES, cols), lambda b, i: (b, jnp.maximum(i * blocks_per_tile - 1, 0), 0)),
                  pl.BlockSpec((1, SUBLANES, cols), lambda b, i: (b, jnp.minimum((i + 1) * blocks_per_tile, last_block), 0)),
                  whole(conv_w), whole(conv_b)],
        out_specs=[seq(SSM_WIDTH), seq(gn), seq(gn)],
        out_shape=[jax.ShapeDtypeStruct((bsz, t, c), _F32) for c in (SSM_WIDTH, gn, gn)],
        compiler_params=pltpu.CompilerParams(dimension_semantics=("parallel", "parallel"), vmem_limit_bytes=VMEM_LIMIT_BYTES),
        name="ssm_pre",
    )(xbc, xbc, xbc, conv_w, conv_b)


def _ssm_post_kernel(yf, yb, xs, z, d_ref, gain_ref, gsum, gwide, o_ref):
    zz = z[0]
    u = (yf[0] + yb[0] + d_ref[...] * xs[0]) * (zz * jax.nn.sigmoid(zz))
    ms = _head_widen(_head_sum(u * u, gsum[...]), gwide[...]) * (SSM_GROUPS / SSM_WIDTH)
    o_ref[0] = (u * lax.rsqrt(ms + NORM_EPS) * gain_ref[...]).astype(o_ref.dtype)


def _ssm_post(y_f, y_b, xs, z, d_skip, gain):
    bsz, t, width = y_f.shape
    tm = SSM_TILE_T
    gsum, gwide = _head_matrices(width, width // SSM_GROUPS)
    seq = pl.BlockSpec((1, tm, width), lambda b, i: (b, i, 0))
    whole = lambda a: pl.BlockSpec(a.shape, lambda b, i: (0,) * a.ndim)
    d_wide = jnp.repeat(d_skip, SSM_HEADDIM).reshape(1, width).astype(_F32)
    gain = gain.reshape(1, width).astype(_F32)
    return pl.pallas_call(
        _ssm_post_kernel,
        grid=(bsz, t // tm),
        in_specs=[seq] * 4 + [whole(d_wide), whole(gain), whole(gsum), whole(gwide)],
        out_specs=seq,
        out_shape=jax.ShapeDtypeStruct((bsz, t, width), _BF16),
        compiler_params=pltpu.CompilerParams(dimension_semantics=("parallel", "parallel")),
        name="ssm_post",
    )(y_f, y_b, xs, z, d_wide, gain, gsum, gwide)


def _mamba2_branch(xbc, dt_raw, z, h0_fwd, h0_bwd, conv_w, conv_b, dt_bias, a_log, d_skip, norm_w, want_output):
    bsz, t, _ = xbc.shape
    xs, bm, cm = _ssm_pre(xbc, conv_w, conv_b)
    dt = jax.nn.softplus(dt_raw.astype(_F32).reshape(bsz, t, 2, SSM_HEADS) + dt_bias)
    log_a = -jnp.exp(a_log) * dt
    heads = lambda h: h.reshape(bsz, SSM_HEADS, SSM_HEADDIM, SSM_STATE)
    y_f, h_f = _ssd_scan(xs, bm, cm, log_a[:, :, 0], dt[:, :, 0], heads(h0_fwd), reverse=False)
    y_b, h_b = _ssd_scan(xs, bm, cm, log_a[:, :, 1], dt[:, :, 1], heads(h0_bwd), reverse=True)
    out = _ssm_post(y_f, y_b, xs, z, d_skip, norm_w) if want_output else None
    return out, h_f, h_b


def _gated_pair_kernel(ya_ref, yb_ref, wa_ref, wb_ref, ga_ref, gb_ref, o_ref):
    pa = jnp.dot(ya_ref[...], wa_ref[...], preferred_element_type=_F32)
    pb = jnp.dot(yb_ref[...], wb_ref[...], preferred_element_type=_F32)
    o_ref[...] = (jax.nn.sigmoid(ga_ref[...]) * pa + jax.nn.sigmoid(gb_ref[...]) * pb).astype(o_ref.dtype)


def _merge_branches(y_a, y_b, gates, w_a, w_b, w_o):
    bsz, t, d = y_a.shape
    m, n = bsz * t, w_a.shape[1]
    tm, tn = min(m, NORM_TILE_M), min(n, MM_TILE_N)
    assert m % tm == 0 and n % tn == 0 and gates.shape[-1] == 2 * n
    rows = pl.BlockSpec((tm, d), lambda j, i: (i, 0))
    cols = pl.BlockSpec((d, tn), lambda j, i: (0, j))
    mixed = pl.pallas_call(
        _gated_pair_kernel,
        grid=(n // tn, m // tm),
        in_specs=[rows, rows, cols, cols,
                  pl.BlockSpec((tm, tn), lambda j, i: (i, j)),
                  pl.BlockSpec((tm, tn), lambda j, i: (i, n // tn + j))],
        out_specs=pl.BlockSpec((tm, tn), lambda j, i: (i, j)),
        out_shape=jax.ShapeDtypeStruct((m, n), _BF16),
        compiler_params=pltpu.CompilerParams(dimension_semantics=("parallel", "parallel"), vmem_limit_bytes=VMEM_LIMIT_BYTES),
        name="gated_pair",
    )(y_a.reshape(m, d), y_b.reshape(m, d), w_a.astype(_BF16), w_b.astype(_BF16), gates.reshape(m, 2 * n), gates.reshape(m, 2 * n))
    return _matmul(mixed, w_o.astype(_BF16)).reshape(bsz, t, -1)


PEER_ROUTE_TOKENS = 256
PEER_TOKEN_BLOCK = 512
PEER_EXPERT_BLOCK = 1024
PEER_KEY_GROUP = 4


def _top_values(x, count):
    rows = lax.broadcasted_iota(jnp.int32, x.shape, 0)
    rank = jnp.full(x.shape, float(count), _F32)
    vals = []
    for it in range(count):
        m = jnp.max(x, axis=0, keepdims=True)
        vals.append(m)
        first = jnp.min(jnp.where(x == m, rows, x.shape[0]), axis=0, keepdims=True)
        hit = rows == first
        rank = jnp.where(hit, float(it), rank)
        x = jnp.where(hit, -jnp.inf, x)
    return vals, rank, x


def _peer_route_kernel(h_ref, wqt_ref, sub_ref, cnt1_ref, e1_ref, rank2_ref, e2_ref):
    qt = lax.dot_general(wqt_ref[...], h_ref[...].astype(_BF16), _NT, preferred_element_type=_F32)
    for h in range(PEER_HEADS):
        sc = []
        for s in range(2):
            lo = (2 * h + s) * PEER_HALF
            sc.append(jnp.dot(sub_ref[2 * h + s], qt[lo:lo + PEER_HALF].astype(_BF16), preferred_element_type=_F32))
        top_a, rank_a, _ = _top_values(sc[0], PEER_TOPK)
        top_b, rank_b, _ = _top_values(sc[1], PEER_TOPK)
        top_b = jnp.concatenate(top_b, axis=0)
        width = [PEER_TOPK // (i + 1) for i in range(PEER_TOPK)]
        cand = jnp.concatenate([top_a[i] + top_b[:width[i]] for i in range(PEER_TOPK)], axis=0)
        best, _, left = _top_values(cand, PEER_TOPK)
        taken = jnp.where(left == cand, 0.0, 1.0)
        norm = best[0] * 0.0
        for val in best:
            norm = norm + jnp.exp(val - best[0])
        cnt1 = jnp.zeros_like(rank_a)
        start = 0
        for i in range(PEER_TOPK):
            used = jnp.sum(taken[start:start + width[i]], axis=0, keepdims=True)
            cnt1 = jnp.where(rank_a == float(i), used, cnt1)
            start += width[i]
        cnt1_ref[h] = cnt1.astype(cnt1_ref.dtype)
        rank2_ref[h] = rank_b.astype(rank2_ref.dtype)
        e1_ref[h] = (jnp.exp(sc[0] - top_a[0]) / norm).astype(e1_ref.dtype)
        e2_ref[h] = jnp.exp(sc[1] - top_b[0:1]).astype(e2_ref.dtype)


def _peer_route(tok, wq, subkeys):
    n, d = tok.shape
    tr = PEER_ROUTE_TOKENS
    wqt = wq.T.astype(_BF16)
    sub = subkeys.reshape(PEER_HEADS * 2, PEER_KEYS, PEER_HALF).astype(_BF16)
    key_spec = pl.BlockSpec((PEER_HEADS, PEER_KEYS, tr), lambda i: (0, 0, i))
    row_shape = jax.ShapeDtypeStruct((PEER_HEADS, PEER_KEYS, n), _F32)
    tile_shape = jax.ShapeDtypeStruct((PEER_HEADS, PEER_KEYS, n), _BF16)
    return pl.pallas_call(
        _peer_route_kernel,
        grid=(n // tr,),
        in_specs=[pl.BlockSpec((tr, d), lambda i: (i, 0)),
                  pl.BlockSpec(wqt.shape, lambda i: (0, 0)),
                  pl.BlockSpec(sub.shape, lambda i: (0, 0, 0))],
        out_specs=[key_spec] * 4,
        out_shape=[row_shape, row_shape, tile_shape, tile_shape],
        compiler_params=pltpu.CompilerParams(dimension_semantics=("parallel",), vmem_limit_bytes=VMEM_LIMIT_BYTES),
        name="peer_route",
    )(tok, wqt, sub)


def _gelu_exact(x):
    return 0.5 * x * (1.0 + lax.erf(x * (1.0 / math.sqrt(2.0))))


def _peer_gate_block(a_ref, w_ref, b_ref, cnt1_ref, e1_ref, rank2_ref, e2_ref, block):
    keys_per_block = PEER_EXPERT_BLOCK // PEER_KEYS
    rt = 2 * SUBLANES
    tb = a_ref.shape[1]
    for h in range(PEER_HEADS):
        for j in range(keys_per_block):
            i1 = block * keys_per_block + j
            b_ref[0, j * PEER_HEADS + h] = jnp.broadcast_to(cnt1_ref[h, pl.ds(i1, 1), :], (rt, tb)).astype(_BF16)
            b_ref[1, j * PEER_HEADS + h] = jnp.broadcast_to(e1_ref[h, pl.ds(i1, 1), :], (rt, tb)).astype(_BF16)

    def row_tile(tile, carry):
        r0 = pl.multiple_of(tile * rt, rt)
        for j0 in range(0, keys_per_block, PEER_KEY_GROUP):
            group = range(j0, j0 + PEER_KEY_GROUP)
            gates = {j: None for j in group}
            for h in range(PEER_HEADS):
                rank2 = rank2_ref[h, pl.ds(r0, rt), :]
                e2 = e2_ref[h, pl.ds(r0, rt), :]
                for j in group:
                    term = jnp.where(rank2 < b_ref[0, j * PEER_HEADS + h], b_ref[1, j * PEER_HEADS + h] * e2, jnp.zeros_like(e2))
                    gates[j] = term if gates[j] is None else gates[j] + term
            for j in group:
                rows = pl.ds(pl.multiple_of(j * PEER_KEYS + r0, rt), rt)
                w_ref[rows, :] = gates[j] * _gelu_exact(a_ref[rows, :]).astype(_BF16)
        return carry

    lax.fori_loop(0, PEER_KEYS // rt, row_tile, 0)


def _peer_expert_kernel(x_ref, u_ref, v_ref, cnt1_ref, e1_ref, rank2_ref, e2_ref, o_ref, a_scr, w_scr, b_scr):
    eb = pl.program_id(1)

    @pl.when(eb == 0)
    def _():
        o_ref[...] = jnp.zeros_like(o_ref)

    a_scr[...] = lax.dot_general(u_ref[...], x_ref[...], _NT, preferred_element_type=_F32)
    _peer_gate_block(a_scr, w_scr, b_scr, cnt1_ref, e1_ref, rank2_ref, e2_ref, eb)
    o_ref[...] += lax.dot_general(w_scr[...], v_ref[...], _TN, preferred_element_type=_F32)


def _peer_ffn(h, wq, subkeys, u_tab, v_tab):
    bsz, t, d = h.shape
    n = bsz * t
    tok = h.reshape(n, d)
    routing = _peer_route(tok, wq, subkeys)
    tb, eb = PEER_TOKEN_BLOCK, PEER_EXPERT_BLOCK
    n_experts = u_tab.shape[0]
    key_spec = pl.BlockSpec((PEER_HEADS, PEER_KEYS, tb), lambda i, e: (0, 0, i))
    tab_spec = pl.BlockSpec((eb, d), lambda i, e: (e, 0))
    out = pl.pallas_call(
        _peer_expert_kernel,
        grid=(n // tb, n_experts // eb),
        in_specs=[pl.BlockSpec((tb, d), lambda i, e: (i, 0)), tab_spec, tab_spec] + [key_spec] * 4,
        out_specs=pl.BlockSpec((tb, d), lambda i, e: (i, 0)),
        out_shape=jax.ShapeDtypeStruct((n, d), _F32),
        scratch_shapes=[pltpu.VMEM((eb, tb), _F32), pltpu.VMEM((eb, tb), _BF16),
                        pltpu.VMEM((2, (eb // PEER_KEYS) * PEER_HEADS, 2 * SUBLANES, tb), _BF16)],
        compiler_params=pltpu.CompilerParams(dimension_semantics=("parallel", "arbitrary"), vmem_limit_bytes=VMEM_LIMIT_BYTES),
        name="peer_experts",
    )(tok, u_tab.astype(_BF16), v_tab.astype(_BF16), *routing)
    return out.reshape(bsz, t, d)


def kernel(x, c, ctx, c_ctx, w_mod, b_mod, norm_pre1, norm_post1, norm_pre2, norm_post2, w_in, rw_mu, rw_w0, rw_w2, rw_a0, rw_a2, rw_g2, rw_k_k, rw_k_a, rw_r_k, rw_ln_w, rw_ln_b, ssm_conv_w, ssm_conv_b, ssm_dt_bias, ssm_a_log, ssm_d, ssm_norm_w, w_branch_a, w_branch_b, w_out, peer_wq, peer_subkeys, peer_u, peer_v):
    bsz, seq, _ = x.shape
    rows = seq // GRID_W
    depth = w_mod.shape[0]
    assert depth == 1
    rw_zero = jnp.zeros((bsz, RW_WIDTH // LANES, LANES, LANES), _F32)
    ssm_zero = jnp.zeros((bsz, SSM_GROUPS, SSM_HPG, SSM_HEADDIM, SSM_STATE), _F32)
    l = 0
    mod_x = (jax.nn.silu(c) @ w_mod[l] + b_mod[l])[:, None, :]
    mod_c = (jax.nn.silu(c_ctx) @ w_mod[l] + b_mod[l])[None, None, :]
    sh1x, sc1x, g1x, sh2x, sc2x, g2x = jnp.split(mod_x, N_MOD, axis=-1)
    sh1c, sc1c, g1c, sh2c, sc2c, g2c = jnp.split(mod_c, N_MOD, axis=-1)

    in_w = _in_proj_weights(w_in[l])
    hx = _norm_mod(x, norm_pre1[l], sc1x, sh1x)
    r_x, k_x, v_x, lora_x, z_x, xbc_x, dt_x, gate_x = _in_proj(hx, in_w, h_ssm=_raster_to_column(hx, rows))
    r_c, k_c, v_c, lora_c, z_c, xbc_c, dt_c, gate_c = _in_proj(_norm_mod(ctx, norm_pre1[l], sc1c, sh1c), in_w)

    rw_params = (rw_w0[l], rw_w2[l], rw_a0[l], rw_a2[l], rw_g2[l], rw_k_k[l], rw_k_a[l], rw_r_k[l], rw_ln_w[l], rw_ln_b[l])
    _, s_fwd, s_bwd = _rwkv7_branch(r_c, k_c, v_c, lora_c, rw_mu[l], rw_zero, rw_zero, *rw_params, want_output=False)
    ya_x, _, _ = _rwkv7_branch(r_x, k_x, v_x, lora_x, rw_mu[l], s_fwd, s_bwd, *rw_params, want_output=True)

    ssm_params = (ssm_conv_w[l], ssm_conv_b[l], ssm_dt_bias[l], ssm_a_log[l], ssm_d[l], ssm_norm_w[l])
    _, h_fwd, h_bwd = _mamba2_branch(xbc_c, dt_c, z_c, ssm_zero, ssm_zero, *ssm_params, want_output=False)
    yb_x, _, _ = _mamba2_branch(xbc_x, dt_x, z_x, h_fwd, h_bwd, *ssm_params, want_output=True)
    yb_x = _column_to_raster(yb_x, rows)

    mix_x = _merge_branches(ya_x, yb_x, gate_x, w_branch_a[l], w_branch_b[l], w_out[l])
    x, h2x = _residual(x, mix_x, g1x, norm_post1[l], next_mod=(norm_pre2[l], sc2x, sh2x))
    return _residual(x, _peer_ffn(h2x, peer_wq[l], peer_subkeys[l], peer_u[l], peer_v[l]), g2x, norm_post2[l])
```

```python
import functools
import math

import jax
import jax.numpy as jnp
from jax import lax
from jax.experimental import pallas as pl
from jax.experimental.pallas import tpu as pltpu

D_MODEL = 2048
GRID_W = 64
N_MOD = 6
NORM_EPS = 1e-6
RW_HEAD = 64
RW_WIDTH = D_MODEL
RW_HEADS = RW_WIDTH // RW_HEAD
LORA_W = 96
LORA_A = 96
LORA_G = 256
LN_X_EPS = 64e-5
SSM_WIDTH = D_MODEL
SSM_HEADDIM = 64
SSM_HEADS = SSM_WIDTH // SSM_HEADDIM
SSM_GROUPS = 8
SSM_HPG = SSM_HEADS // SSM_GROUPS
SSM_STATE = 128
SSM_CONV = 5
SSM_CHUNK = 128
PEER_HEADS = 8
PEER_KEYS = 128
PEER_TOPK = 16
PEER_QDIM = 256
PEER_HALF = PEER_QDIM // 2
RW_COLS = 3 * RW_WIDTH + 2 * LORA_W + 2 * LORA_A + LORA_G
XBC_COLS = SSM_WIDTH + 2 * SSM_GROUPS * SSM_STATE
IN_SPLITS = (RW_COLS, RW_COLS + SSM_WIDTH, RW_COLS + SSM_WIDTH + XBC_COLS, RW_COLS + SSM_WIDTH + XBC_COLS + 2 * SSM_HEADS)
RW_SPLITS = (RW_WIDTH, 2 * RW_WIDTH, 3 * RW_WIDTH, 3 * RW_WIDTH + LORA_W, 3 * RW_WIDTH + 2 * LORA_W, 3 * RW_WIDTH + 2 * LORA_W + LORA_A, 3 * RW_WIDTH + 2 * LORA_W + 2 * LORA_A)

LANES = 128
SUBLANES = 8
V7X_VMEM_BYTES = 64 * 2**20
VMEM_LIMIT_BYTES = V7X_VMEM_BYTES * 7 // 8
WKV_CHUNK = 64
WKV_PAIR = LANES // RW_HEAD
WKV_GROUPS_PER_STEP = 16

_F32 = jnp.float32
_BF16 = jnp.bfloat16
_NT = (((1,), (1,)), ((), ()))
_TN = (((0,), (0,)), ((), ()))


def _mm(a, b, dims=None):
    a = a.astype(_BF16)
    b = b.astype(_BF16)
    if dims is None:
        return jnp.dot(a, b, preferred_element_type=_F32)
    return lax.dot_general(a, b, dims, preferred_element_type=_F32)


def _wkv_kernel(r_ref, lw_ref, k_ref, v_ref, a_ref, b_ref, s0_ref, y_ref, s_ref, *, reverse, chunk, groups):
    L = chunk
    L2 = WKV_PAIR * L

    @pl.when(pl.program_id(2) == 0)
    def _():
        s_ref[...] = s0_ref[...]

    ti = lax.broadcasted_iota(jnp.int32, (L, L), 0)
    tj = lax.broadcasted_iota(jnp.int32, (L, L), 1)
    before_incl = (ti <= tj) if reverse else (ti >= tj)
    cum = jnp.where(before_incl, 1.0, 0.0).astype(_BF16)

    lw = lw_ref[0]
    hi = lw.astype(_BF16)
    rem = lw - hi.astype(_F32)
    mid = rem.astype(_BF16)
    lo = (rem - mid.astype(_F32)).astype(_BF16)
    cs = (jnp.dot(cum, hi, preferred_element_type=_F32)
          + jnp.dot(cum, mid, preferred_element_type=_F32)
          + jnp.dot(cum, lo, preferred_element_type=_F32))
    tot = cs[0:1] if reverse else cs[L - 1:L]
    half = 0.5 * tot
    e_pos = jnp.exp(cs - half)
    e_neg = jnp.exp(half - cs)
    e_prev = jnp.exp(cs - lw - half)
    e_end = jnp.exp(tot - cs)
    e_half = jnp.exp(half)
    e_tot = jnp.exp(tot)

    rt = r_ref[0] * e_pos
    at = a_ref[0] * e_prev
    kt = k_ref[0] * e_neg
    bt = b_ref[0] * e_neg
    ke = k_ref[0] * e_end
    be = b_ref[0] * e_end
    vv = v_ref[0]

    lane = lax.broadcasted_iota(jnp.int32, (L, LANES), 1)
    first_head = lane < RW_HEAD

    def stack(x):
        return jnp.concatenate([jnp.where(first_head, x, 0.0), jnp.where(first_head, 0.0, x)], axis=0)

    si = lax.broadcasted_iota(jnp.int32, (L2, L2), 0)
    sj = lax.broadcasted_iota(jnp.int32, (L2, L2), 1)
    same_head = (si < L) == (sj < L)
    pi = jnp.where(si < L, si, si - L)
    pj = jnp.where(sj < L, sj, sj - L)
    incl2 = same_head & ((pi <= pj) if reverse else (pi >= pj))
    strict2 = same_head & ((pi < pj) if reverse else (pi > pj))
    eye2 = jnp.where(si == sj, 1.0, 0.0)

    gs = range(groups)
    sls = [slice(g * LANES, (g + 1) * LANES) for g in gs]
    v_s = [stack(vv[:, sl]).astype(_BF16) for sl in sls]
    lhs = [jnp.concatenate([stack(at[:, sl]), stack(rt[:, sl])], axis=0).astype(_BF16) for sl in sls]
    rhs = [jnp.concatenate([stack(kt[:, sl]), stack(bt[:, sl])], axis=0).astype(_BF16) for sl in sls]
    state = [s_ref[0, g] for g in gs]
    scores = [_mm(lhs[g], rhs[g], _NT) for g in gs]
    from_state = [_mm(lhs[g], state[g] * e_half[:, sls[g]], _NT) for g in gs]

    power = [jnp.where(strict2, -scores[g][:L2, L2:], 0.0).astype(_BF16) for g in gs]
    inv = [eye2 + power[g] for g in gs]
    ak_v = [_mm(jnp.where(strict2, scores[g][:L2, :L2], 0.0), v_s[g]) for g in gs]
    for _ in range(int(math.log2(L)) - 1):
        power = [_mm(power[g], power[g]).astype(_BF16) for g in gs]
        inv = [inv[g] + _mm(inv[g], power[g]) for g in gs]
    u_s = [_mm(inv[g], from_state[g][:L2] + ak_v[g]) for g in gs]

    vu = [jnp.concatenate([v_s[g], u_s[g].astype(_BF16)], axis=0) for g in gs]
    for g in gs:
        r_kb = jnp.concatenate([jnp.where(incl2, scores[g][L2:, :L2], 0.0),
                                jnp.where(incl2, -scores[g][L2:, L2:], 0.0)], axis=1)
        y_s = from_state[g][L2:] + _mm(r_kb, vu[g])
        y_ref[0, :, sls[g]] = y_s[:L] + y_s[L:]
    for g in gs:
        kb = jnp.concatenate([stack(ke[:, sls[g]]), -stack(be[:, sls[g]])], axis=0)
        s_ref[0, g] = state[g] * e_tot[:, sls[g]] + _mm(vu[g], kb, _TN)


def _wkv_scan(r, lw, k, v, a, b, s0, reverse):
    bsz, t, width = r.shape
    groups = WKV_GROUPS_PER_STEP
    assert t % WKV_CHUNK == 0 and width % (groups * LANES) == 0
    nc = t // WKV_CHUNK
    ngroup_steps = width // (groups * LANES)
    cidx = (lambda c: nc - 1 - c) if reverse else (lambda c: c)
    seq_spec = pl.BlockSpec((1, WKV_CHUNK, groups * LANES), lambda bi, gi, c: (bi, cidx(c), gi))
    st_spec = pl.BlockSpec((1, groups, LANES, LANES), lambda bi, gi, c: (bi, gi, 0, 0))
    return pl.pallas_call(
        functools.partial(_wkv_kernel, reverse=reverse, chunk=WKV_CHUNK, groups=groups),
        grid=(bsz, ngroup_steps, nc),
        in_specs=[seq_spec] * 6 + [st_spec],
        out_specs=[seq_spec, st_spec],
        out_shape=[jax.ShapeDtypeStruct((bsz, t, width), _F32), jax.ShapeDtypeStruct(s0.shape, _F32)],
        compiler_params=pltpu.CompilerParams(dimension_semantics=("parallel", "parallel", "arbitrary")),
        name="wkv7_rev" if reverse else "wkv7_fwd",
    )(r, lw, k, v, a, b, s0)


MM_TILE_M = 1024
MM_TILE_N = 1024
NORM_TILE_M = 512
SMALL_COLS = 768


def _matmul_kernel(a_ref, w_ref, o_ref):
    o_ref[...] = jnp.dot(a_ref[...], w_ref[...], preferred_element_type=_F32)


def _matmul(a, w):
    m, k = a.shape
    n = w.shape[1]
    tm, tn = min(m, MM_TILE_M), min(n, MM_TILE_N)
    assert m % tm == 0 and n % tn == 0 and a.dtype == _BF16 and w.dtype == _BF16
    return pl.pallas_call(
        _matmul_kernel,
        grid=(n // tn, m // tm),
        in_specs=[pl.BlockSpec((tm, k), lambda j, i: (i, 0)), pl.BlockSpec((k, tn), lambda j, i: (0, j))],
        out_specs=pl.BlockSpec((tm, tn), lambda j, i: (i, j)),
        out_shape=jax.ShapeDtypeStruct((m, n), _F32),
        compiler_params=pltpu.CompilerParams(dimension_semantics=("parallel", "parallel"), vmem_limit_bytes=VMEM_LIMIT_BYTES),
        name="matmul",
    )(a, w)


def _norm_mod_kernel(x_ref, gain_ref, sc_ref, sh_ref, o_ref):
    x = x_ref[0]
    inv = lax.rsqrt(jnp.mean(x * x, axis=-1, keepdims=True) + NORM_EPS)
    o_ref[0] = ((x * inv * gain_ref[...]) * (1.0 + sc_ref[0]) + sh_ref[0]).astype(o_ref.dtype)


def _norm_mod(x, gain, scale, shift):
    bsz, t, d = x.shape
    tm = min(t, NORM_TILE_M)
    mod_spec = pl.BlockSpec((1, 1, d), (lambda b, i: (b, 0, 0)) if scale.shape[0] == bsz else (lambda b, i: (0, 0, 0)))
    return pl.pallas_call(
        _norm_mod_kernel,
        grid=(bsz, t // tm),
        in_specs=[pl.BlockSpec((1, tm, d), lambda b, i: (b, i, 0)), pl.BlockSpec((1, d), lambda b, i: (0, 0)), mod_spec, mod_spec],
        out_specs=pl.BlockSpec((1, tm, d), lambda b, i: (b, i, 0)),
        out_shape=jax.ShapeDtypeStruct((bsz, t, d), _BF16),
        compiler_params=pltpu.CompilerParams(dimension_semantics=("parallel", "parallel")),
        name="norm_mod",
    )(x, gain.reshape(1, d), scale, shift)


def _residual_kernel(x_ref, u_ref, g_ref, gain_ref, *rest, with_next):
    u = u_ref[0]
    inv = lax.rsqrt(jnp.mean(u * u, axis=-1, keepdims=True) + NORM_EPS)
    y = x_ref[0] + g_ref[0] * (u * inv * gain_ref[...])
    if not with_next:
        rest[0][0] = y
        return
    gain2_ref, sc_ref, sh_ref, y_ref, h_ref = rest
    y_ref[0] = y
    inv2 = lax.rsqrt(jnp.mean(y * y, axis=-1, keepdims=True) + NORM_EPS)
    h_ref[0] = ((y * inv2 * gain2_ref[...]) * (1.0 + sc_ref[0]) + sh_ref[0]).astype(h_ref.dtype)


def _residual(x, u, g, gain, next_mod=None):
    bsz, t, d = x.shape
    tm = min(t, NORM_TILE_M)
    seq = pl.BlockSpec((1, tm, d), lambda b, i: (b, i, 0))
    per_batch = pl.BlockSpec((1, 1, d), lambda b, i: (b, 0, 0))
    row = pl.BlockSpec((1, d), lambda b, i: (0, 0))
    args, in_specs = [x, u, g, gain.reshape(1, d)], [seq, seq, per_batch, row]
    out_specs, out_shape = [seq], [jax.ShapeDtypeStruct((bsz, t, d), _F32)]
    if next_mod is not None:
        gain2, scale, shift = next_mod
        args += [gain2.reshape(1, d), scale, shift]
        in_specs += [row, per_batch, per_batch]
        out_specs.append(seq)
        out_shape.append(jax.ShapeDtypeStruct((bsz, t, d), _BF16))
    out = pl.pallas_call(
        functools.partial(_residual_kernel, with_next=next_mod is not None),
        grid=(bsz, t // tm),
        in_specs=in_specs,
        out_specs=out_specs,
        out_shape=out_shape,
        compiler_params=pltpu.CompilerParams(dimension_semantics=("parallel", "parallel")),
        name="residual",
    )(*args)
    return out if next_mod is not None else out[0]


def _in_proj_weights(w):
    rw_end, z_end, xbc_end, dt_end = IN_SPLITS
    lora_dt = jnp.concatenate([w[:, RW_SPLITS[2]:rw_end], w[:, xbc_end:dt_end]], axis=1)
    lora_dt = jnp.pad(lora_dt, ((0, 0), (0, SMALL_COLS - lora_dt.shape[1])))
    parts = [w[:, :RW_SPLITS[0]], w[:, RW_SPLITS[0]:RW_SPLITS[1]], w[:, RW_SPLITS[1]:RW_SPLITS[2]],
             w[:, rw_end:z_end], w[:, z_end:xbc_end], w[:, dt_end:], lora_dt]
    return [p.astype(_BF16) for p in parts]


def _in_proj(h, weights, h_ssm=None):
    bsz, t, d = h.shape
    mm = lambda u, w: _matmul(u.reshape(bsz * t, d), w).reshape(bsz, t, -1)
    w_r, w_k, w_v, w_z, w_xbc, w_gate, w_lora_dt = weights
    lora_dt = mm(h, w_lora_dt)
    dt_src = lora_dt if h_ssm is None else mm(h_ssm, w_lora_dt)
    h_ssm = h if h_ssm is None else h_ssm
    n_lora = RW_COLS - RW_SPLITS[2]
    return (mm(h, w_r), mm(h, w_k), mm(h, w_v), lora_dt, mm(h_ssm, w_z), mm(h_ssm, w_xbc),
            dt_src[..., n_lora:n_lora + 2 * SSM_HEADS], mm(h, w_gate))


def _raster_to_column(u, rows):
    b, s, ch = u.shape
    return u.reshape(b, rows, GRID_W, ch).transpose(0, 2, 1, 3).reshape(b, s, ch)


def _column_to_raster(u, rows):
    b, s, ch = u.shape
    return u.reshape(b, GRID_W, rows, ch).transpose(0, 2, 1, 3).reshape(b, s, ch)


RW_TILE_T = 128
LORA_COLS = RW_COLS - 3 * RW_WIDTH


def _head_matrices(width=RW_WIDTH, group=RW_HEAD):
    group_of_lane = jnp.arange(width) // group
    onehot = (group_of_lane[:, None] == jnp.arange(width // group)[None, :]).astype(_BF16)
    return onehot, onehot.T


def _head_sum(x, hsum):
    return sum(jnp.dot(part, hsum, preferred_element_type=_F32) for part in _split3(x))


def _head_widen(cols, hwide):
    return sum(jnp.dot(part, hwide, preferred_element_type=_F32) for part in _split3(cols))


def _token_shift_block(c_ref, p_ref, n_ref, mu, first, last):
    x = c_ref[0]
    tm = x.shape[0]
    row = lax.broadcasted_iota(jnp.int32, x.shape, 0)
    prev_row = jnp.where(first, 0.0, p_ref[0, SUBLANES - 1:SUBLANES, :])
    next_row = jnp.where(last, 0.0, n_ref[0, 0:1, :])
    prev = jnp.where(row == 0, prev_row, pltpu.roll(x, 1, 0))
    nxt = jnp.where(row == tm - 1, next_row, pltpu.roll(x, tm - 1, 0))
    return x + mu * (0.5 * (prev + nxt) - x)


def _rw_pre_kernel(rc, rp, rn, kc, kp, kn, vc, vp, vn, lc, lp, ln,
                   mu_r, mu_k, mu_v, mu_l, w0, a0, k_k, k_a, r_k, ln_b, w2, a2, g2, hsum, hwide,
                   r_o, v_o, kk_o, lwf_o, lwb_o, kf_o, kb_o, kkaf_o, kkab_o, gate_o, bonus_o):
    first = pl.program_id(1) == 0
    last = pl.program_id(1) == pl.num_programs(1) - 1
    r = _token_shift_block(rc, rp, rn, mu_r[...], first, last)
    k = _token_shift_block(kc, kp, kn, mu_k[...], first, last)
    v = _token_shift_block(vc, vp, vn, mu_v[...], first, last)
    lora = _token_shift_block(lc, lp, ln, mu_l[...], first, last)
    r_o[0] = r
    v_o[0] = v

    kk = k * k_k[...]
    norm = jnp.maximum(jnp.sqrt(_head_sum(kk * kk, hsum[...])), 1e-12)
    kk = kk * _head_widen(1.0 / norm, hwide[...])
    kk_o[0] = kk

    k_sum = None
    for d, (lw_o, kd_o, kka_o) in enumerate(((lwf_o, kf_o, kkaf_o), (lwb_o, kb_o, kkab_o))):
        wd = lora[:, d * LORA_W:(d + 1) * LORA_W]
        ad = lora[:, 2 * LORA_W + d * LORA_A:2 * LORA_W + (d + 1) * LORA_A]
        lw_o[0] = -math.exp(-0.5) * jax.nn.sigmoid(w0[d:d + 1, :] + _mm(jnp.tanh(wd), w2[d]))
        a = jax.nn.sigmoid(a0[d:d + 1, :] + _mm(ad, a2[d]))
        k_dir = k * (1.0 + (a - 1.0) * k_a[...])
        kd_o[0] = k_dir
        kka_o[0] = kk * a
        k_sum = k_dir if k_sum is None else k_sum + k_dir

    gd = lora[:, 2 * LORA_W + 2 * LORA_A:2 * LORA_W + 2 * LORA_A + LORA_G]
    gate = _mm(jax.nn.sigmoid(gd), g2[...])
    bonus = _head_widen(_head_sum(r * k_sum * r_k[...], hsum[...]), hwide[...]) * v
    gate_o[0] = gate
    bonus_o[0] = (ln_b[...] + bonus) * gate


def _rw_pre(r, k, v, lora_dt, mu, w0, w2, a0, a2, g2, k_k, k_a, r_k, ln_b):
    bsz, t, width = r.shape
    tm = RW_TILE_T
    nt = t // tm
    blocks_per_tile = tm // SUBLANES
    last_block = t // SUBLANES - 1
    lcols = lora_dt.shape[-1]

    def specs(cols):
        return [pl.BlockSpec((1, tm, cols), lambda b, i: (b, i, 0)),
                pl.BlockSpec((1, SUBLANES, cols), lambda b, i: (b, jnp.maximum(i * blocks_per_tile - 1, 0), 0)),
                pl.BlockSpec((1, SUBLANES, cols), lambda b, i: (b, jnp.minimum((i + 1) * blocks_per_tile, last_block), 0))]

    def whole(a):
        return pl.BlockSpec(a.shape, lambda b, i: (0,) * a.ndim)

    row = lambda a: a.reshape(1, -1).astype(_F32)
    mu_l = jnp.pad(mu[RW_SPLITS[2]:], (0, lcols - LORA_COLS))
    hsum, hwide = _head_matrices()
    params = [row(mu[:RW_SPLITS[0]]), row(mu[RW_SPLITS[0]:RW_SPLITS[1]]), row(mu[RW_SPLITS[1]:RW_SPLITS[2]]), row(mu_l),
              w0.astype(_F32), a0.astype(_F32), row(k_k), row(k_a), row(r_k), row(ln_b),
              w2.astype(_BF16), a2.astype(_BF16), g2.astype(_BF16), hsum, hwide]
    out_spec = pl.BlockSpec((1, tm, width), lambda b, i: (b, i, 0))
    out_shape = jax.ShapeDtypeStruct((bsz, t, width), _F32)
    return pl.pallas_call(
        _rw_pre_kernel,
        grid=(bsz, nt),
        in_specs=specs(width) * 3 + specs(lcols) + [whole(p) for p in params],
        out_specs=[out_spec] * 11,
        out_shape=[out_shape] * 11,
        compiler_params=pltpu.CompilerParams(dimension_semantics=("parallel", "parallel"), vmem_limit_bytes=VMEM_LIMIT_BYTES),
        name="rw_pre",
    )(r, r, r, k, k, k, v, v, v, lora_dt, lora_dt, lora_dt, *params)


def _rw_post_kernel(yf, yb, gate, bonus, ln_w, hsum, hwide, o_ref):
    y = yf[0] + yb[0]
    mean = _head_widen(_head_sum(y, hsum[...]), hwide[...]) * (1.0 / RW_HEAD)
    cen = y - mean
    var = _head_widen(_head_sum(cen * cen, hsum[...]), hwide[...]) * (1.0 / RW_HEAD)
    o_ref[0] = (cen * lax.rsqrt(var + LN_X_EPS) * ln_w[...] * gate[0] + bonus[0]).astype(o_ref.dtype)


def _rw_post(y_f, y_b, gate, bonus, ln_w):
    bsz, t, width = y_f.shape
    tm = RW_TILE_T
    hsum, hwide = _head_matrices()
    spec = pl.BlockSpec((1, tm, width), lambda b, i: (b, i, 0))
    whole = lambda a: pl.BlockSpec(a.shape, lambda b, i: (0,) * a.ndim)
    ln_w = ln_w.reshape(1, width).astype(_F32)
    return pl.pallas_call(
        _rw_post_kernel,
        grid=(bsz, t // tm),
        in_specs=[spec] * 4 + [whole(ln_w), whole(hsum), whole(hwide)],
        out_specs=spec,
        out_shape=jax.ShapeDtypeStruct((bsz, t, width), _BF16),
        compiler_params=pltpu.CompilerParams(dimension_semantics=("parallel", "parallel")),
        name="rw_post",
    )(y_f, y_b, gate, bonus, ln_w, hsum, hwide)


def _rwkv7_branch(r, k, v, lora_dt, mu, s_fwd, s_bwd, w0, w2, a0, a2, g2, k_k, k_a, r_k, ln_w, ln_b, want_output):
    r, v, kk, lw_f, lw_b, k_f, k_b, kka_f, kka_b, gate, bonus = _rw_pre(
        r, k, v, lora_dt, mu, w0, w2, a0, a2, g2, k_k, k_a, r_k, ln_b)
    y_f, s_fwd = _wkv_scan(r, lw_f, k_f, v, kk, kka_f, s_fwd, reverse=False)
    y_b, s_bwd = _wkv_scan(r, lw_b, k_b, v, kk, kka_b, s_bwd, reverse=True)
    out = _rw_post(y_f, y_b, gate, bonus, ln_w) if want_output else None
    return out, s_fwd, s_bwd


def _split3(x):
    hi = x.astype(_BF16)
    rem = x - hi.astype(_F32)
    mid = rem.astype(_BF16)
    lo = (rem - mid.astype(_F32)).astype(_BF16)
    return hi, mid, lo


def _ssd_kernel(xs_ref, bm_ref, cm_ref, la_ref, lat_ref, dt_ref, h0_ref, y_ref, h_ref, *, reverse):
    L = SSM_CHUNK
    P = SSM_HEADDIM

    @pl.when(pl.program_id(1) == 0)
    def _():
        h_ref[...] = h0_ref[...]

    ti = lax.broadcasted_iota(jnp.int32, (L, L), 0)
    tj = lax.broadcasted_iota(jnp.int32, (L, L), 1)
    before_incl = (ti <= tj) if reverse else (ti >= tj)
    cum = jnp.where(before_incl, 1.0, 0.0).astype(_BF16)
    cum_t = jnp.where(before_incl, 0.0, 1.0).astype(_BF16) + jnp.where(ti == tj, 1.0, 0.0).astype(_BF16)

    la = la_ref[0]
    cs = sum(jnp.dot(cum, part, preferred_element_type=_F32) for part in _split3(la))
    cs_t = sum(jnp.dot(part, cum_t, preferred_element_type=_F32) for part in _split3(lat_ref[0]))
    tot = cs[0:1] if reverse else cs[L - 1:L]
    dt = dt_ref[0]

    hi = lax.broadcasted_iota(jnp.int32, (SSM_HEADS, SSM_WIDTH), 0)
    hj = lax.broadcasted_iota(jnp.int32, (SSM_HEADS, SSM_WIDTH), 1)
    lo_edge = hi * P
    widen = jnp.where((hj >= lo_edge) & (hj < lo_edge + P), 1.0, 0.0).astype(_BF16)

    def wide(cols):
        return sum(jnp.dot(part, widen, preferred_element_type=_F32) for part in _split3(cols))

    xs = xs_ref[0]
    xdt = xs * wide(dt)
    xdt_end = (xs * wide(dt * jnp.exp(tot - cs))).astype(_BF16)
    xdt = xdt.astype(_BF16)
    decay_in = wide(jnp.exp(cs))
    e_tot = jnp.exp(tot)

    gs, hs = range(SSM_GROUPS), range(SSM_HEADS)
    cols = [slice(h * P, (h + 1) * P) for h in hs]
    bm = [bm_ref[0, :, g * SSM_STATE:(g + 1) * SSM_STATE].astype(_BF16) for g in gs]
    cm = [cm_ref[0, :, g * SSM_STATE:(g + 1) * SSM_STATE].astype(_BF16) for g in gs]
    cb = [lax.dot_general(cm[g], bm[g], _NT, preferred_element_type=_F32) for g in gs]
    gp = SSM_HPG * P
    heads_of = [slice(g * SSM_HPG, (g + 1) * SSM_HPG) for g in gs]
    state = [h_ref[0, heads_of[g]].reshape(gp, SSM_STATE) for g in gs]
    from_state = [lax.dot_general(cm[g], state[g].astype(_BF16), _NT, preferred_element_type=_F32) for g in gs]
    new = [lax.dot_general(xdt_end[:, g * gp:(g + 1) * gp], bm[g], _TN, preferred_element_type=_F32) for g in gs]
    pieces = []
    for h in hs:
        seg = jnp.where(before_incl, jnp.exp(cs[:, h:h + 1] - cs_t[h:h + 1, :]), 0.0)
        pieces.append(jnp.dot((cb[h // SSM_HPG] * seg).astype(_BF16), xdt[:, cols[h]], preferred_element_type=_F32))
    for g in gs:
        keep = jnp.concatenate([jnp.broadcast_to(e_tot[:, h:h + 1], (P, SSM_STATE))
                                for h in range(g * SSM_HPG, (g + 1) * SSM_HPG)], axis=0)
        h_ref[0, heads_of[g]] = (state[g] * keep + new[g]).reshape(SSM_HPG, P, SSM_STATE)
    y_ref[0] = jnp.concatenate(pieces, axis=1) + jnp.concatenate(from_state, axis=1) * decay_in


def _ssd_scan(xs, bm, cm, log_a, dt, h0, reverse):
    bsz, t, width = xs.shape
    nc = t // SSM_CHUNK
    cidx = (lambda c: nc - 1 - c) if reverse else (lambda c: c)
    seq = lambda w: pl.BlockSpec((1, SSM_CHUNK, w), lambda b, c: (b, cidx(c), 0))
    st_spec = pl.BlockSpec((1,) + h0.shape[1:], lambda b, c: (b, 0, 0, 0))
    gn = SSM_GROUPS * SSM_STATE
    return pl.pallas_call(
        functools.partial(_ssd_kernel, reverse=reverse),
        grid=(bsz, nc),
        in_specs=[seq(width), seq(gn), seq(gn), seq(SSM_HEADS),
                  pl.BlockSpec((1, SSM_HEADS, SSM_CHUNK), lambda b, c: (b, 0, cidx(c))), seq(SSM_HEADS), st_spec],
        out_specs=[seq(width), st_spec],
        out_shape=[jax.ShapeDtypeStruct((bsz, t, width), _F32), jax.ShapeDtypeStruct(h0.shape, _F32)],
        compiler_params=pltpu.CompilerParams(dimension_semantics=("parallel", "arbitrary")),
        name="ssd_rev" if reverse else "ssd_fwd",
    )(xs, bm, cm, log_a, jnp.swapaxes(log_a, 1, 2), dt, h0)


SSM_TILE_T = 128


def _ssm_pre_kernel(xc, xp, xn, w_ref, b_ref, xs_o, bm_o, cm_o):
    first = pl.program_id(1) == 0
    last = pl.program_id(1) == pl.num_programs(1) - 1
    x = xc[0]
    tm = x.shape[0]
    row = lax.broadcasted_iota(jnp.int32, x.shape, 0)
    prev = jnp.where(first, 0.0, xp[0])
    nxt = jnp.where(last, 0.0, xn[0])
    half = SSM_CONV // 2
    acc = w_ref[half:half + 1, :] * x
    for off in range(1, half + 1):
        back = pltpu.roll(x, off, 0)
        fwd = pltpu.roll(x, tm - off, 0)
        for i in range(off):
            back = jnp.where(row == i, prev[SUBLANES - off + i:SUBLANES - off + i + 1, :], back)
            fwd = jnp.where(row == tm - off + i, nxt[i:i + 1, :], fwd)
        acc = acc + w_ref[half - off:half - off + 1, :] * back + w_ref[half + off:half + off + 1, :] * fwd
    u = acc + b_ref[...]
    u = u * jax.nn.sigmoid(u)
    gn = SSM_GROUPS * SSM_STATE
    xs_o[0] = u[:, :SSM_WIDTH]
    bm_o[0] = u[:, SSM_WIDTH:SSM_WIDTH + gn]
    cm_o[0] = u[:, SSM_WIDTH + gn:]


def _ssm_pre(xbc, conv_w, conv_b):
    bsz, t, cols = xbc.shape
    tm = SSM_TILE_T
    blocks_per_tile = tm // SUBLANES
    last_block = t // SUBLANES - 1
    gn = SSM_GROUPS * SSM_STATE
    seq = lambda c: pl.BlockSpec((1, tm, c), lambda b, i: (b, i, 0))
    whole = lambda a: pl.BlockSpec(a.shape, lambda b, i: (0,) * a.ndim)
    conv_w = conv_w.astype(_F32)
    conv_b = conv_b.reshape(1, cols).astype(_F32)
    return pl.pallas_call(
        _ssm_pre_kernel,
        grid=(bsz, t // tm),
        in_specs=[seq(cols),
                  pl.BlockSpec((1, SUBLANES, cols), lambda b, i: (b, jnp.maximum(i * blocks_per_tile - 1, 0), 0)),
                  pl.BlockSpec((1, SUBLANES, cols), lambda b, i: (b, jnp.minimum((i + 1) * blocks_per_tile, last_block), 0)),
                  whole(conv_w), whole(conv_b)],
        out_specs=[seq(SSM_WIDTH), seq(gn), seq(gn)],
        out_shape=[jax.ShapeDtypeStruct((bsz, t, c), _F32) for c in (SSM_WIDTH, gn, gn)],
        compiler_params=pltpu.CompilerParams(dimension_semantics=("parallel", "parallel"), vmem_limit_bytes=VMEM_LIMIT_BYTES),
        name="ssm_pre",
    )(xbc, xbc, xbc, conv_w, conv_b)


def _ssm_post_kernel(yf, yb, xs, z, d_ref, gain_ref, gsum, gwide, o_ref):
    zz = z[0]
    u = (yf[0] + yb[0] + d_ref[...] * xs[0]) * (zz * jax.nn.sigmoid(zz))
    ms = _head_widen(_head_sum(u * u, gsum[...]), gwide[...]) * (SSM_GROUPS / SSM_WIDTH)
    o_ref[0] = (u * lax.rsqrt(ms + NORM_EPS) * gain_ref[...]).astype(o_ref.dtype)


def _ssm_post(y_f, y_b, xs, z, d_skip, gain):
    bsz, t, width = y_f.shape
    tm = SSM_TILE_T
    gsum, gwide = _head_matrices(width, width // SSM_GROUPS)
    seq = pl.BlockSpec((1, tm, width), lambda b, i: (b, i, 0))
    whole = lambda a: pl.BlockSpec(a.shape, lambda b, i: (0,) * a.ndim)
    d_wide = jnp.repeat(d_skip, SSM_HEADDIM).reshape(1, width).astype(_F32)
    gain = gain.reshape(1, width).astype(_F32)
    return pl.pallas_call(
        _ssm_post_kernel,
        grid=(bsz, t // tm),
        in_specs=[seq] * 4 + [whole(d_wide), whole(gain), whole(gsum), whole(gwide)],
        out_specs=seq,
        out_shape=jax.ShapeDtypeStruct((bsz, t, width), _BF16),
        compiler_params=pltpu.CompilerParams(dimension_semantics=("parallel", "parallel")),
        name="ssm_post",
    )(y_f, y_b, xs, z, d_wide, gain, gsum, gwide)


def _mamba2_branch(xbc, dt_raw, z, h0_fwd, h0_bwd, conv_w, conv_b, dt_bias, a_log, d_skip, norm_w, want_output):
    bsz, t, _ = xbc.shape
    xs, bm, cm = _ssm_pre(xbc, conv_w, conv_b)
    dt = jax.nn.softplus(dt_raw.astype(_F32).reshape(bsz, t, 2, SSM_HEADS) + dt_bias)
    log_a = -jnp.exp(a_log) * dt
    heads = lambda h: h.reshape(bsz, SSM_HEADS, SSM_HEADDIM, SSM_STATE)
    y_f, h_f = _ssd_scan(xs, bm, cm, log_a[:, :, 0], dt[:, :, 0], heads(h0_fwd), reverse=False)
    y_b, h_b = _ssd_scan(xs, bm, cm, log_a[:, :, 1], dt[:, :, 1], heads(h0_bwd), reverse=True)
    out = _ssm_post(y_f, y_b, xs, z, d_skip, norm_w) if want_output else None
    return out, h_f, h_b


def _gated_pair_kernel(ya_ref, yb_ref, wa_ref, wb_ref, ga_ref, gb_ref, o_ref):
    pa = jnp.dot(ya_ref[...], wa_ref[...], preferred_element_type=_F32)
    pb = jnp.dot(yb_ref[...], wb_ref[...], preferred_element_type=_F32)
    o_ref[...] = (jax.nn.sigmoid(ga_ref[...]) * pa + jax.nn.sigmoid(gb_ref[...]) * pb).astype(o_ref.dtype)


def _merge_branches(y_a, y_b, gates, w_a, w_b, w_o):
    bsz, t, d = y_a.shape
    m, n = bsz * t, w_a.shape[1]
    tm, tn = min(m, NORM_TILE_M), min(n, MM_TILE_N)
    assert m % tm == 0 and n % tn == 0 and gates.shape[-1] == 2 * n
    rows = pl.BlockSpec((tm, d), lambda j, i: (i, 0))
    cols = pl.BlockSpec((d, tn), lambda j, i: (0, j))
    mixed = pl.pallas_call(
        _gated_pair_kernel,
        grid=(n // tn, m // tm),
        in_specs=[rows, rows, cols, cols,
                  pl.BlockSpec((tm, tn), lambda j, i: (i, j)),
                  pl.BlockSpec((tm, tn), lambda j, i: (i, n // tn + j))],
        out_specs=pl.BlockSpec((tm, tn), lambda j, i: (i, j)),
        out_shape=jax.ShapeDtypeStruct((m, n), _BF16),
        compiler_params=pltpu.CompilerParams(dimension_semantics=("parallel", "parallel"), vmem_limit_bytes=VMEM_LIMIT_BYTES),
        name="gated_pair",
    )(y_a.reshape(m, d), y_b.reshape(m, d), w_a.astype(_BF16), w_b.astype(_BF16), gates.reshape(m, 2 * n), gates.reshape(m, 2 * n))
    return _matmul(mixed, w_o.astype(_BF16)).reshape(bsz, t, -1)


PEER_ROUTE_TOKENS = 256
PEER_TOKEN_BLOCK = 512
PEER_EXPERT_BLOCK = 1024
PEER_KEY_GROUP = 4


def _top_values(x, count):
    rows = lax.broadcasted_iota(jnp.int32, x.shape, 0)
    rank = jnp.full(x.shape, float(count), _F32)
    vals = []
    for it in range(count):
        m = jnp.max(x, axis=0, keepdims=True)
        vals.append(m)
        first = jnp.min(jnp.where(x == m, rows, x.shape[0]), axis=0, keepdims=True)
        hit = rows == first
        rank = jnp.where(hit, float(it), rank)
        x = jnp.where(hit, -jnp.inf, x)
    return vals, rank, x


def _peer_route_kernel(h_ref, wqt_ref, sub_ref, cnt1_ref, e1_ref, rank2_ref, e2_ref):
    qt = lax.dot_general(wqt_ref[...], h_ref[...].astype(_BF16), _NT, preferred_element_type=_F32)
    for h in range(PEER_HEADS):
        sc = []
        for s in range(2):
            lo = (2 * h + s) * PEER_HALF
            sc.append(jnp.dot(sub_ref[2 * h + s], qt[lo:lo + PEER_HALF].astype(_BF16), preferred_element_type=_F32))
        top_a, rank_a, _ = _top_values(sc[0], PEER_TOPK)
        top_b, rank_b, _ = _top_values(sc[1], PEER_TOPK)
        top_b = jnp.concatenate(top_b, axis=0)
        width = [PEER_TOPK // (i + 1) for i in range(PEER_TOPK)]
        cand = jnp.concatenate([top_a[i] + top_b[:width[i]] for i in range(PEER_TOPK)], axis=0)
        best, _, left = _top_values(cand, PEER_TOPK)
        taken = jnp.where(left == cand, 0.0, 1.0)
        norm = best[0] * 0.0
        for val in best:
            norm = norm + jnp.exp(val - best[0])
        cnt1 = jnp.zeros_like(rank_a)
        start = 0
        for i in range(PEER_TOPK):
            used = jnp.sum(taken[start:start + width[i]], axis=0, keepdims=True)
            cnt1 = jnp.where(rank_a == float(i), used, cnt1)
            start += width[i]
        cnt1_ref[h] = cnt1.astype(cnt1_ref.dtype)
        rank2_ref[h] = rank_b.astype(rank2_ref.dtype)
        e1_ref[h] = (jnp.exp(sc[0] - top_a[0]) / norm).astype(e1_ref.dtype)
        e2_ref[h] = jnp.exp(sc[1] - top_b[0:1]).astype(e2_ref.dtype)


def _peer_route(tok, wq, subkeys):
    n, d = tok.shape
    tr = PEER_ROUTE_TOKENS
    wqt = wq.T.astype(_BF16)
    sub = subkeys.reshape(PEER_HEADS * 2, PEER_KEYS, PEER_HALF).astype(_BF16)
    key_spec = pl.BlockSpec((PEER_HEADS, PEER_KEYS, tr), lambda i: (0, 0, i))
    row_shape = jax.ShapeDtypeStruct((PEER_HEADS, PEER_KEYS, n), _F32)
    tile_shape = jax.ShapeDtypeStruct((PEER_HEADS, PEER_KEYS, n), _BF16)
    return pl.pallas_call(
        _peer_route_kernel,
        grid=(n // tr,),
        in_specs=[pl.BlockSpec((tr, d), lambda i: (i, 0)),
                  pl.BlockSpec(wqt.shape, lambda i: (0, 0)),
                  pl.BlockSpec(sub.shape, lambda i: (0, 0, 0))],
        out_specs=[key_spec] * 4,
        out_shape=[row_shape, row_shape, tile_shape, tile_shape],
        compiler_params=pltpu.CompilerParams(dimension_semantics=("parallel",), vmem_limit_bytes=VMEM_LIMIT_BYTES),
        name="peer_route",
    )(tok, wqt, sub)


def _gelu_exact(x):
    return 0.5 * x * (1.0 + lax.erf(x * (1.0 / math.sqrt(2.0))))


def _peer_gate_block(a_ref, w_ref, b_ref, cnt1_ref, e1_ref, rank2_ref, e2_ref, block):
    keys_per_block = PEER_EXPERT_BLOCK // PEER_KEYS
    rt = 2 * SUBLANES
    tb = a_ref.shape[1]
    for h in range(PEER_HEADS):
        for j in range(keys_per_block):
            i1 = block * keys_per_block + j
            b_ref[0, j * PEER_HEADS + h] = jnp.broadcast_to(cnt1_ref[h, pl.ds(i1, 1), :], (rt, tb)).astype(_BF16)
            b_ref[1, j * PEER_HEADS + h] = jnp.broadcast_to(e1_ref[h, pl.ds(i1, 1), :], (rt, tb)).astype(_BF16)

    def row_tile(tile, carry):
        r0 = pl.multiple_of(tile * rt, rt)
        for j0 in range(0, keys_per_block, PEER_KEY_GROUP):
            group = range(j0, j0 + PEER_KEY_GROUP)
            gates = {j: None for j in group}
            for h in range(PEER_HEADS):
                rank2 = rank2_ref[h, pl.ds(r0, rt), :]
                e2 = e2_ref[h, pl.ds(r0, rt), :]
                for j in group:
                    term = jnp.where(rank2 < b_ref[0, j * PEER_HEADS + h], b_ref[1, j * PEER_HEADS + h] * e2, jnp.zeros_like(e2))
                    gates[j] = term if gates[j] is None else gates[j] + term
            for j in group:
                rows = pl.ds(pl.multiple_of(j * PEER_KEYS + r0, rt), rt)
                w_ref[rows, :] = gates[j] * _gelu_exact(a_ref[rows, :]).astype(_BF16)
        return carry

    lax.fori_loop(0, PEER_KEYS // rt, row_tile, 0)


def _peer_expert_kernel(x_ref, u_ref, v_ref, cnt1_ref, e1_ref, rank2_ref, e2_ref, o_ref, a_scr, w_scr, b_scr):
    eb = pl.program_id(1)

    @pl.when(eb == 0)
    def _():
        o_ref[...] = jnp.zeros_like(o_ref)

    a_scr[...] = lax.dot_general(u_ref[...], x_ref[...], _NT, preferred_element_type=_F32)
    _peer_gate_block(a_scr, w_scr, b_scr, cnt1_ref, e1_ref, rank2_ref, e2_ref, eb)
    o_ref[...] += lax.dot_general(w_scr[...], v_ref[...], _TN, preferred_element_type=_F32)


def _peer_ffn(h, wq, subkeys, u_tab, v_tab):
    bsz, t, d = h.shape
    n = bsz * t
    tok = h.reshape(n, d)
    routing = _peer_route(tok, wq, subkeys)
    tb, eb = PEER_TOKEN_BLOCK, PEER_EXPERT_BLOCK
    n_experts = u_tab.shape[0]
    key_spec = pl.BlockSpec((PEER_HEADS, PEER_KEYS, tb), lambda i, e: (0, 0, i))
    tab_spec = pl.BlockSpec((eb, d), lambda i, e: (e, 0))
    out = pl.pallas_call(
        _peer_expert_kernel,
        grid=(n // tb, n_experts // eb),
        in_specs=[pl.BlockSpec((tb, d), lambda i, e: (i, 0)), tab_spec, tab_spec] + [key_spec] * 4,
        out_specs=pl.BlockSpec((tb, d), lambda i, e: (i, 0)),
        out_shape=jax.ShapeDtypeStruct((n, d), _F32),
        scratch_shapes=[pltpu.VMEM((eb, tb), _F32), pltpu.VMEM((eb, tb), _BF16),
                        pltpu.VMEM((2, (eb // PEER_KEYS) * PEER_HEADS, 2 * SUBLANES, tb), _BF16)],
        compiler_params=pltpu.CompilerParams(dimension_semantics=("parallel", "arbitrary"), vmem_limit_bytes=VMEM_LIMIT_BYTES),
        name="peer_experts",
    )(tok, u_tab.astype(_BF16), v_tab.astype(_BF16), *routing)
    return out.reshape(bsz, t, d)


def kernel(x, c, ctx, c_ctx, w_mod, b_mod, norm_pre1, norm_post1, norm_pre2, norm_post2, w_in, rw_mu, rw_w0, rw_w2, rw_a0, rw_a2, rw_g2, rw_k_k, rw_k_a, rw_r_k, rw_ln_w, rw_ln_b, ssm_conv_w, ssm_conv_b, ssm_dt_bias, ssm_a_log, ssm_d, ssm_norm_w, w_branch_a, w_branch_b, w_out, peer_wq, peer_subkeys, peer_u, peer_v):
    bsz, seq, _ = x.shape
    rows = seq // GRID_W
    depth = w_mod.shape[0]
    assert depth == 1
    rw_zero = jnp.zeros((bsz, RW_WIDTH // LANES, LANES, LANES), _F32)
    ssm_zero = jnp.zeros((bsz, SSM_GROUPS, SSM_HPG, SSM_HEADDIM, SSM_STATE), _F32)
    l = 0
    mod_x = (jax.nn.silu(c) @ w_mod[l] + b_mod[l])[:, None, :]
    mod_c = (jax.nn.silu(c_ctx) @ w_mod[l] + b_mod[l])[None, None, :]
    sh1x, sc1x, g1x, sh2x, sc2x, g2x = jnp.split(mod_x, N_MOD, axis=-1)
    sh1c, sc1c, g1c, sh2c, sc2c, g2c = jnp.split(mod_c, N_MOD, axis=-1)

    in_w = _in_proj_weights(w_in[l])
    hx = _norm_mod(x, norm_pre1[l], sc1x, sh1x)
    r_x, k_x, v_x, lora_x, z_x, xbc_x, dt_x, gate_x = _in_proj(hx, in_w, h_ssm=_raster_to_column(hx, rows))
    r_c, k_c, v_c, lora_c, z_c, xbc_c, dt_c, gate_c = _in_proj(_norm_mod(ctx, norm_pre1[l], sc1c, sh1c), in_w)

    rw_params = (rw_w0[l], rw_w2[l], rw_a0[l], rw_a2[l], rw_g2[l], rw_k_k[l], rw_k_a[l], rw_r_k[l], rw_ln_w[l], rw_ln_b[l])
    _, s_fwd, s_bwd = _rwkv7_branch(r_c, k_c, v_c, lora_c, rw_mu[l], rw_zero, rw_zero, *rw_params, want_output=False)
    ya_x, _, _ = _rwkv7_branch(r_x, k_x, v_x, lora_x, rw_mu[l], s_fwd, s_bwd, *rw_params, want_output=True)

    ssm_params = (ssm_conv_w[l], ssm_conv_b[l], ssm_dt_bias[l], ssm_a_log[l], ssm_d[l], ssm_norm_w[l])
    _, h_fwd, h_bwd = _mamba2_branch(xbc_c, dt_c, z_c, ssm_zero, ssm_zero, *ssm_params, want_output=False)
    yb_x, _, _ = _mamba2_branch(xbc_x, dt_x, z_x, h_fwd, h_bwd, *ssm_params, want_output=True)
    yb_x = _column_to_raster(yb_x, rows)

    mix_x = _merge_branches(ya_x, yb_x, gate_x, w_branch_a[l], w_branch_b[l], w_out[l])
    x, h2x = _residual(x, mix_x, g1x, norm_post1[l], next_mod=(norm_pre2[l], sc2x, sh2x))
    return _residual(x, _peer_ffn(h2x, peer_wq[l], peer_subkeys[l], peer_u[l], peer_v[l]), g2x, norm_post2[l])
```

```python
import functools
import math

import jax
import jax.numpy as jnp
from jax import lax
from jax.experimental import pallas as pl
from jax.experimental.pallas import tpu as pltpu

D_MODEL = 2048
GRID_W = 64
N_MOD = 6
NORM_EPS = 1e-6
RW_HEAD = 64
RW_WIDTH = D_MODEL
RW_HEADS = RW_WIDTH // RW_HEAD
LORA_W = 96
LORA_A = 96
LORA_G = 256
LN_X_EPS = 64e-5
SSM_WIDTH = D_MODEL
SSM_HEADDIM = 64
SSM_HEADS = SSM_WIDTH // SSM_HEADDIM
SSM_GROUPS = 8
SSM_HPG = SSM_HEADS // SSM_GROUPS
SSM_STATE = 128
SSM_CONV = 5
SSM_CHUNK = 128
PEER_HEADS = 8
PEER_KEYS = 128
PEER_TOPK = 16
PEER_QDIM = 256
PEER_HALF = PEER_QDIM // 2
RW_COLS = 3 * RW_WIDTH + 2 * LORA_W + 2 * LORA_A + LORA_G
XBC_COLS = SSM_WIDTH + 2 * SSM_GROUPS * SSM_STATE
IN_SPLITS = (RW_COLS, RW_COLS + SSM_WIDTH, RW_COLS + SSM_WIDTH + XBC_COLS, RW_COLS + SSM_WIDTH + XBC_COLS + 2 * SSM_HEADS)
RW_SPLITS = (RW_WIDTH, 2 * RW_WIDTH, 3 * RW_WIDTH, 3 * RW_WIDTH + LORA_W, 3 * RW_WIDTH + 2 * LORA_W, 3 * RW_WIDTH + 2 * LORA_W + LORA_A, 3 * RW_WIDTH + 2 * LORA_W + 2 * LORA_A)

LANES = 128
SUBLANES = 8
V7X_VMEM_BYTES = 64 * 2**20
VMEM_LIMIT_BYTES = V7X_VMEM_BYTES * 7 // 8
WKV_CHUNK = 64
WKV_PAIR = LANES // RW_HEAD
WKV_GROUPS_PER_STEP = 16

_F32 = jnp.float32
_BF16 = jnp.bfloat16
_NT = (((1,), (1,)), ((), ()))
_TN = (((0,), (0,)), ((), ()))


def _mm(a, b, dims=None):
    a = a.astype(_BF16)
    b = b.astype(_BF16)
    if dims is None:
        return jnp.dot(a, b, preferred_element_type=_F32)
    return lax.dot_general(a, b, dims, preferred_element_type=_F32)


def _wkv_kernel(r_ref, lw_ref, k_ref, v_ref, a_ref, b_ref, s0_ref, y_ref, s_ref, *, reverse, chunk, groups):
    L = chunk
    L2 = WKV_PAIR * L

    @pl.when(pl.program_id(2) == 0)
    def _():
        s_ref[...] = s0_ref[...]

    ti = lax.broadcasted_iota(jnp.int32, (L, L), 0)
    tj = lax.broadcasted_iota(jnp.int32, (L, L), 1)
    before_incl = (ti <= tj) if reverse else (ti >= tj)
    cum = jnp.where(before_incl, 1.0, 0.0).astype(_BF16)

    lw = lw_ref[0]
    hi = lw.astype(_BF16)
    rem = lw - hi.astype(_F32)
    mid = rem.astype(_BF16)
    lo = (rem - mid.astype(_F32)).astype(_BF16)
    cs = (jnp.dot(cum, hi, preferred_element_type=_F32)
          + jnp.dot(cum, mid, preferred_element_type=_F32)
          + jnp.dot(cum, lo, preferred_element_type=_F32))
    tot = cs[0:1] if reverse else cs[L - 1:L]
    half = 0.5 * tot
    e_pos = jnp.exp(cs - half)
    e_neg = jnp.exp(half - cs)
    e_prev = jnp.exp(cs - lw - half)
    e_end = jnp.exp(tot - cs)
    e_half = jnp.exp(half)
    e_tot = jnp.exp(tot)

    rt = r_ref[0] * e_pos
    at = a_ref[0] * e_prev
    kt = k_ref[0] * e_neg
    bt = b_ref[0] * e_neg
    ke = k_ref[0] * e_end
    be = b_ref[0] * e_end
    vv = v_ref[0]

    lane = lax.broadcasted_iota(jnp.int32, (L, LANES), 1)
    first_head = lane < RW_HEAD

    def stack(x):
        return jnp.concatenate([jnp.where(first_head, x, 0.0), jnp.where(first_head, 0.0, x)], axis=0)

    si = lax.broadcasted_iota(jnp.int32, (L2, L2), 0)
    sj = lax.broadcasted_iota(jnp.int32, (L2, L2), 1)
    same_head = (si < L) == (sj < L)
    pi = jnp.where(si < L, si, si - L)
    pj = jnp.where(sj < L, sj, sj - L)
    incl2 = same_head & ((pi <= pj) if reverse else (pi >= pj))
    strict2 = same_head & ((pi < pj) if reverse else (pi > pj))
    eye2 = jnp.where(si == sj, 1.0, 0.0)

    gs = range(groups)
    sls = [slice(g * LANES, (g + 1) * LANES) for g in gs]
    v_s = [stack(vv[:, sl]).astype(_BF16) for sl in sls]
    lhs = [jnp.concatenate([stack(at[:, sl]), stack(rt[:, sl])], axis=0).astype(_BF16) for sl in sls]
    rhs = [jnp.concatenate([stack(kt[:, sl]), stack(bt[:, sl])], axis=0).astype(_BF16) for sl in sls]
    state = [s_ref[0, g] for g in gs]
    scores = [_mm(lhs[g], rhs[g], _NT) for g in gs]
    from_state = [_mm(lhs[g], state[g] * e_half[:, sls[g]], _NT) for g in gs]

    power = [jnp.where(strict2, -scores[g][:L2, L2:], 0.0).astype(_BF16) for g in gs]
    inv = [eye2 + power[g] for g in gs]
    ak_v = [_mm(jnp.where(strict2, scores[g][:L2, :L2], 0.0), v_s[g]) for g in gs]
    for _ in range(int(math.log2(L)) - 1):
        power = [_mm(power[g], power[g]).astype(_BF16) for g in gs]
        inv = [inv[g] + _mm(inv[g], power[g]) for g in gs]
    u_s = [_mm(inv[g], from_state[g][:L2] + ak_v[g]) for g in gs]

    vu = [jnp.concatenate([v_s[g], u_s[g].astype(_BF16)], axis=0) for g in gs]
    for g in gs:
        r_kb = jnp.concatenate([jnp.where(incl2, scores[g][L2:, :L2], 0.0),
                                jnp.where(incl2, -scores[g][L2:, L2:], 0.0)], axis=1)
        y_s = from_state[g][L2:] + _mm(r_kb, vu[g])
        y_ref[0, :, sls[g]] = y_s[:L] + y_s[L:]
    for g in gs:
        kb = jnp.concatenate([stack(ke[:, sls[g]]), -stack(be[:, sls[g]])], axis=0)
        s_ref[0, g] = state[g] * e_tot[:, sls[g]] + _mm(vu[g], kb, _TN)


def _wkv_scan(r, lw, k, v, a, b, s0, reverse):
    bsz, t, width = r.shape
    groups = WKV_GROUPS_PER_STEP
    assert t % WKV_CHUNK == 0 and width % (groups * LANES) == 0
    nc = t // WKV_CHUNK
    ngroup_steps = width // (groups * LANES)
    cidx = (lambda c: nc - 1 - c) if reverse else (lambda c: c)
    seq_spec = pl.BlockSpec((1, WKV_CHUNK, groups * LANES), lambda bi, gi, c: (bi, cidx(c), gi))
    st_spec = pl.BlockSpec((1, groups, LANES, LANES), lambda bi, gi, c: (bi, gi, 0, 0))
    return pl.pallas_call(
        functools.partial(_wkv_kernel, reverse=reverse, chunk=WKV_CHUNK, groups=groups),
        grid=(bsz, ngroup_steps, nc),
        in_specs=[seq_spec] * 6 + [st_spec],
        out_specs=[seq_spec, st_spec],
        out_shape=[jax.ShapeDtypeStruct((bsz, t, width), _F32), jax.ShapeDtypeStruct(s0.shape, _F32)],
        compiler_params=pltpu.CompilerParams(dimension_semantics=("parallel", "parallel", "arbitrary")),
        name="wkv7_rev" if reverse else "wkv7_fwd",
    )(r, lw, k, v, a, b, s0)


MM_TILE_M = 1024
MM_TILE_N = 1024
NORM_TILE_M = 512
SMALL_COLS = 768


def _matmul_kernel(a_ref, w_ref, o_ref):
    o_ref[...] = jnp.dot(a_ref[...], w_ref[...], preferred_element_type=_F32)


def _matmul(a, w):
    m, k = a.shape
    n = w.shape[1]
    tm, tn = min(m, MM_TILE_M), min(n, MM_TILE_N)
    assert m % tm == 0 and n % tn == 0 and a.dtype == _BF16 and w.dtype == _BF16
    return pl.pallas_call(
        _matmul_kernel,
        grid=(n // tn, m // tm),
        in_specs=[pl.BlockSpec((tm, k), lambda j, i: (i, 0)), pl.BlockSpec((k, tn), lambda j, i: (0, j))],
        out_specs=pl.BlockSpec((tm, tn), lambda j, i: (i, j)),
        out_shape=jax.ShapeDtypeStruct((m, n), _F32),
        compiler_params=pltpu.CompilerParams(dimension_semantics=("parallel", "parallel"), vmem_limit_bytes=VMEM_LIMIT_BYTES),
        name="matmul",
    )(a, w)


def _norm_mod_kernel(x_ref, gain_ref, sc_ref, sh_ref, o_ref):
    x = x_ref[0]
    inv = lax.rsqrt(jnp.mean(x * x, axis=-1, keepdims=True) + NORM_EPS)
    o_ref[0] = ((x * inv * gain_ref[...]) * (1.0 + sc_ref[0]) + sh_ref[0]).astype(o_ref.dtype)


def _norm_mod(x, gain, scale, shift):
    bsz, t, d = x.shape
    tm = min(t, NORM_TILE_M)
    mod_spec = pl.BlockSpec((1, 1, d), (lambda b, i: (b, 0, 0)) if scale.shape[0] == bsz else (lambda b, i: (0, 0, 0)))
    return pl.pallas_call(
        _norm_mod_kernel,
        grid=(bsz, t // tm),
        in_specs=[pl.BlockSpec((1, tm, d), lambda b, i: (b, i, 0)), pl.BlockSpec((1, d), lambda b, i: (0, 0)), mod_spec, mod_spec],
        out_specs=pl.BlockSpec((1, tm, d), lambda b, i: (b, i, 0)),
        out_shape=jax.ShapeDtypeStruct((bsz, t, d), _BF16),
        compiler_params=pltpu.CompilerParams(dimension_semantics=("parallel", "parallel")),
        name="norm_mod",
    )(x, gain.reshape(1, d), scale, shift)


def _residual_kernel(x_ref, u_ref, g_ref, gain_ref, *rest, with_next):
    u = u_ref[0]
    inv = lax.rsqrt(jnp.mean(u * u, axis=-1, keepdims=True) + NORM_EPS)
    y = x_ref[0] + g_ref[0] * (u * inv * gain_ref[...])
    if not with_next:
        rest[0][0] = y
        return
    gain2_ref, sc_ref, sh_ref, y_ref, h_ref = rest
    y_ref[0] = y
    inv2 = lax.rsqrt(jnp.mean(y * y, axis=-1, keepdims=True) + NORM_EPS)
    h_ref[0] = ((y * inv2 * gain2_ref[...]) * (1.0 + sc_ref[0]) + sh_ref[0]).astype(h_ref.dtype)


def _residual(x, u, g, gain, next_mod=None):
    bsz, t, d = x.shape
    tm = min(t, NORM_TILE_M)
    seq = pl.BlockSpec((1, tm, d), lambda b, i: (b, i, 0))
    per_batch = pl.BlockSpec((1, 1, d), lambda b, i: (b, 0, 0))
    row = pl.BlockSpec((1, d), lambda b, i: (0, 0))
    args, in_specs = [x, u, g, gain.reshape(1, d)], [seq, seq, per_batch, row]
    out_specs, out_shape = [seq], [jax.ShapeDtypeStruct((bsz, t, d), _F32)]
    if next_mod is not None:
        gain2, scale, shift = next_mod
        args += [gain2.reshape(1, d), scale, shift]
        in_specs += [row, per_batch, per_batch]
        out_specs.append(seq)
        out_shape.append(jax.ShapeDtypeStruct((bsz, t, d), _BF16))
    out = pl.pallas_call(
        functools.partial(_residual_kernel, with_next=next_mod is not None),
        grid=(bsz, t // tm),
        in_specs=in_specs,
        out_specs=out_specs,
        out_shape=out_shape,
        compiler_params=pltpu.CompilerParams(dimension_semantics=("parallel", "parallel")),
        name="residual",
    )(*args)
    return out if next_mod is not None else out[0]


def _in_proj_weights(w):
    rw_end, z_end, xbc_end, dt_end = IN_SPLITS
    lora_dt = jnp.concatenate([w[:, RW_SPLITS[2]:rw_end], w[:, xbc_end:dt_end]], axis=1)
    lora_dt = jnp.pad(lora_dt, ((0, 0), (0, SMALL_COLS - lora_dt.shape[1])))
    parts = [w[:, :RW_SPLITS[0]], w[:, RW_SPLITS[0]:RW_SPLITS[1]], w[:, RW_SPLITS[1]:RW_SPLITS[2]],
             w[:, rw_end:z_end], w[:, z_end:xbc_end], w[:, dt_end:], lora_dt]
    return [p.astype(_BF16) for p in parts]


def _in_proj(h, weights, h_ssm=None):
    bsz, t, d = h.shape
    mm = lambda u, w: _matmul(u.reshape(bsz * t, d), w).reshape(bsz, t, -1)
    w_r, w_k, w_v, w_z, w_xbc, w_gate, w_lora_dt = weights
    lora_dt = mm(h, w_lora_dt)
    dt_src = lora_dt if h_ssm is None else mm(h_ssm, w_lora_dt)
    h_ssm = h if h_ssm is None else h_ssm
    n_lora = RW_COLS - RW_SPLITS[2]
    return (mm(h, w_r), mm(h, w_k), mm(h, w_v), lora_dt, mm(h_ssm, w_z), mm(h_ssm, w_xbc),
            dt_src[..., n_lora:n_lora + 2 * SSM_HEADS], mm(h, w_gate))


def _raster_to_column(u, rows):
    b, s, ch = u.shape
    return u.reshape(b, rows, GRID_W, ch).transpose(0, 2, 1, 3).reshape(b, s, ch)


def _column_to_raster(u, rows):
    b, s, ch = u.shape
    return u.reshape(b, GRID_W, rows, ch).transpose(0, 2, 1, 3).reshape(b, s, ch)


RW_TILE_T = 128
LORA_COLS = RW_COLS - 3 * RW_WIDTH


def _head_matrices(width=RW_WIDTH, group=RW_HEAD):
    group_of_lane = jnp.arange(width) // group
    onehot = (group_of_lane[:, None] == jnp.arange(width // group)[None, :]).astype(_BF16)
    return onehot, onehot.T


def _head_sum(x, hsum):
    return sum(jnp.dot(part, hsum, preferred_element_type=_F32) for part in _split3(x))


def _head_widen(cols, hwide):
    return sum(jnp.dot(part, hwide, preferred_element_type=_F32) for part in _split3(cols))


def _token_shift_block(c_ref, p_ref, n_ref, mu, first, last):
    x = c_ref[0]
    tm = x.shape[0]
    row = lax.broadcasted_iota(jnp.int32, x.shape, 0)
    prev_row = jnp.where(first, 0.0, p_ref[0, SUBLANES - 1:SUBLANES, :])
    next_row = jnp.where(last, 0.0, n_ref[0, 0:1, :])
    prev = jnp.where(row == 0, prev_row, pltpu.roll(x, 1, 0))
    nxt = jnp.where(row == tm - 1, next_row, pltpu.roll(x, tm - 1, 0))
    return x + mu * (0.5 * (prev + nxt) - x)


def _rw_pre_kernel(rc, rp, rn, kc, kp, kn, vc, vp, vn, lc, lp, ln,
                   mu_r, mu_k, mu_v, mu_l, w0, a0, k_k, k_a, r_k, ln_b, w2, a2, g2, hsum, hwide,
                   r_o, v_o, kk_o, lwf_o, lwb_o, kf_o, kb_o, kkaf_o, kkab_o, gate_o, bonus_o):
    first = pl.program_id(1) == 0
    last = pl.program_id(1) == pl.num_programs(1) - 1
    r = _token_shift_block(rc, rp, rn, mu_r[...], first, last)
    k = _token_shift_block(kc, kp, kn, mu_k[...], first, last)
    v = _token_shift_block(vc, vp, vn, mu_v[...], first, last)
    lora = _token_shift_block(lc, lp, ln, mu_l[...], first, last)
    r_o[0] = r
    v_o[0] = v

    kk = k * k_k[...]
    norm = jnp.maximum(jnp.sqrt(_head_sum(kk * kk, hsum[...])), 1e-12)
    kk = kk * _head_widen(1.0 / norm, hwide[...])
    kk_o[0] = kk

    k_sum = None
    for d, (lw_o, kd_o, kka_o) in enumerate(((lwf_o, kf_o, kkaf_o), (lwb_o, kb_o, kkab_o))):
        wd = lora[:, d * LORA_W:(d + 1) * LORA_W]
        ad = lora[:, 2 * LORA_W + d * LORA_A:2 * LORA_W + (d + 1) * LORA_A]
        lw_o[0] = -math.exp(-0.5) * jax.nn.sigmoid(w0[d:d + 1, :] + _mm(jnp.tanh(wd), w2[d]))
        a = jax.nn.sigmoid(a0[d:d + 1, :] + _mm(ad, a2[d]))
        k_dir = k * (1.0 + (a - 1.0) * k_a[...])
        kd_o[0] = k_dir
        kka_o[0] = kk * a
        k_sum = k_dir if k_sum is None else k_sum + k_dir

    gd = lora[:, 2 * LORA_W + 2 * LORA_A:2 * LORA_W + 2 * LORA_A + LORA_G]
    gate = _mm(jax.nn.sigmoid(gd), g2[...])
    bonus = _head_widen(_head_sum(r * k_sum * r_k[...], hsum[...]), hwide[...]) * v
    gate_o[0] = gate
    bonus_o[0] = (ln_b[...] + bonus) * gate


def _rw_pre(r, k, v, lora_dt, mu, w0, w2, a0, a2, g2, k_k, k_a, r_k, ln_b):
    bsz, t, width = r.shape
    tm = RW_TILE_T
    nt = t // tm
    blocks_per_tile = tm // SUBLANES
    last_block = t // SUBLANES - 1
    lcols = lora_dt.shape[-1]

    def specs(cols):
        return [pl.BlockSpec((1, tm, cols), lambda b, i: (b, i, 0)),
                pl.BlockSpec((1, SUBLANES, cols), lambda b, i: (b, jnp.maximum(i * blocks_per_tile - 1, 0), 0)),
                pl.BlockSpec((1, SUBLANES, cols), lambda b, i: (b, jnp.minimum((i + 1) * blocks_per_tile, last_block), 0))]

    def whole(a):
        return pl.BlockSpec(a.shape, lambda b, i: (0,) * a.ndim)

    row = lambda a: a.reshape(1, -1).astype(_F32)
    mu_l = jnp.pad(mu[RW_SPLITS[2]:], (0, lcols - LORA_COLS))
    hsum, hwide = _head_matrices()
    params = [row(mu[:RW_SPLITS[0]]), row(mu[RW_SPLITS[0]:RW_SPLITS[1]]), row(mu[RW_SPLITS[1]:RW_SPLITS[2]]), row(mu_l),
              w0.astype(_F32), a0.astype(_F32), row(k_k), row(k_a), row(r_k), row(ln_b),
              w2.astype(_BF16), a2.astype(_BF16), g2.astype(_BF16), hsum, hwide]
    out_spec = pl.BlockSpec((1, tm, width), lambda b, i: (b, i, 0))
    out_shape = jax.ShapeDtypeStruct((bsz, t, width), _F32)
    return pl.pallas_call(
        _rw_pre_kernel,
        grid=(bsz, nt),
        in_specs=specs(width) * 3 + specs(lcols) + [whole(p) for p in params],
        out_specs=[out_spec] * 11,
        out_shape=[out_shape] * 11,
        compiler_params=pltpu.CompilerParams(dimension_semantics=("parallel", "parallel"), vmem_limit_bytes=VMEM_LIMIT_BYTES),
        name="rw_pre",
    )(r, r, r, k, k, k, v, v, v, lora_dt, lora_dt, lora_dt, *params)


def _rw_post_kernel(yf, yb, gate, bonus, ln_w, hsum, hwide, o_ref):
    y = yf[0] + yb[0]
    mean = _head_widen(_head_sum(y, hsum[...]), hwide[...]) * (1.0 / RW_HEAD)
    cen = y - mean
    var = _head_widen(_head_sum(cen * cen, hsum[...]), hwide[...]) * (1.0 / RW_HEAD)
    o_ref[0] = (cen * lax.rsqrt(var + LN_X_EPS) * ln_w[...] * gate[0] + bonus[0]).astype(o_ref.dtype)


def _rw_post(y_f, y_b, gate, bonus, ln_w):
    bsz, t, width = y_f.shape
    tm = RW_TILE_T
    hsum, hwide = _head_matrices()
    spec = pl.BlockSpec((1, tm, width), lambda b, i: (b, i, 0))
    whole = lambda a: pl.BlockSpec(a.shape, lambda b, i: (0,) * a.ndim)
    ln_w = ln_w.reshape(1, width).astype(_F32)
    return pl.pallas_call(
        _rw_post_kernel,
        grid=(bsz, t // tm),
        in_specs=[spec] * 4 + [whole(ln_w), whole(hsum), whole(hwide)],
        out_specs=spec,
        out_shape=jax.ShapeDtypeStruct((bsz, t, width), _BF16),
        compiler_params=pltpu.CompilerParams(dimension_semantics=("parallel", "parallel")),
        name="rw_post",
    )(y_f, y_b, gate, bonus, ln_w, hsum, hwide)


def _rwkv7_branch(r, k, v, lora_dt, mu, s_fwd, s_bwd, w0, w2, a0, a2, g2, k_k, k_a, r_k, ln_w, ln_b, want_output):
    r, v, kk, lw_f, lw_b, k_f, k_b, kka_f, kka_b, gate, bonus = _rw_pre(
        r, k, v, lora_dt, mu, w0, w2, a0, a2, g2, k_k, k_a, r_k, ln_b)
    y_f, s_fwd = _wkv_scan(r, lw_f, k_f, v, kk, kka_f, s_fwd, reverse=False)
    y_b, s_bwd = _wkv_scan(r, lw_b, k_b, v, kk, kka_b, s_bwd, reverse=True)
    out = _rw_post(y_f, y_b, gate, bonus, ln_w) if want_output else None
    return out, s_fwd, s_bwd


def _split3(x):
    hi = x.astype(_BF16)
    rem = x - hi.astype(_F32)
    mid = rem.astype(_BF16)
    lo = (rem - mid.astype(_F32)).astype(_BF16)
    return hi, mid, lo


def _ssd_kernel(xs_ref, bm_ref, cm_ref, la_ref, lat_ref, dt_ref, h0_ref, y_ref, h_ref, *, reverse):
    L = SSM_CHUNK
    P = SSM_HEADDIM

    @pl.when(pl.program_id(1) == 0)
    def _():
        h_ref[...] = h0_ref[...]

    ti = lax.broadcasted_iota(jnp.int32, (L, L), 0)
    tj = lax.broadcasted_iota(jnp.int32, (L, L), 1)
    before_incl = (ti <= tj) if reverse else (ti >= tj)
    cum = jnp.where(before_incl, 1.0, 0.0).astype(_BF16)
    cum_t = jnp.where(before_incl, 0.0, 1.0).astype(_BF16) + jnp.where(ti == tj, 1.0, 0.0).astype(_BF16)

    la = la_ref[0]
    cs = sum(jnp.dot(cum, part, preferred_element_type=_F32) for part in _split3(la))
    cs_t = sum(jnp.dot(part, cum_t, preferred_element_type=_F32) for part in _split3(lat_ref[0]))
    tot = cs[0:1] if reverse else cs[L - 1:L]
    dt = dt_ref[0]

    hi = lax.broadcasted_iota(jnp.int32, (SSM_HEADS, SSM_WIDTH), 0)
    hj = lax.broadcasted_iota(jnp.int32, (SSM_HEADS, SSM_WIDTH), 1)
    lo_edge = hi * P
    widen = jnp.where((hj >= lo_edge) & (hj < lo_edge + P), 1.0, 0.0).astype(_BF16)

    widen3 = jnp.concatenate([widen, widen, widen], axis=0)
    stacked = jnp.concatenate([jnp.concatenate(_split3(c), axis=1)
                               for c in (dt, dt * jnp.exp(tot - cs), jnp.exp(cs))], axis=0)
    widened = jnp.dot(stacked, widen3, preferred_element_type=_F32)

    xs = xs_ref[0]
    xdt = xs * widened[:L]
    xdt_end = (xs * widened[L:2 * L]).astype(_BF16)
    xdt = xdt.astype(_BF16)
    decay_in = widened[2 * L:]
    e_tot = jnp.exp(tot)

    gs, hs = range(SSM_GROUPS), range(SSM_HEADS)
    cols = [slice(h * P, (h + 1) * P) for h in hs]
    bm = [bm_ref[0, :, g * SSM_STATE:(g + 1) * SSM_STATE].astype(_BF16) for g in gs]
    cm = [cm_ref[0, :, g * SSM_STATE:(g + 1) * SSM_STATE].astype(_BF16) for g in gs]
    cb = [lax.dot_general(cm[g], bm[g], _NT, preferred_element_type=_F32) for g in gs]
    gp = SSM_HPG * P
    heads_of = [slice(g * SSM_HPG, (g + 1) * SSM_HPG) for g in gs]
    state = [h_ref[0, heads_of[g]].reshape(gp, SSM_STATE) for g in gs]
    from_state = [lax.dot_general(cm[g], state[g].astype(_BF16), _NT, preferred_element_type=_F32) for g in gs]
    new = [lax.dot_general(xdt_end[:, g * gp:(g + 1) * gp], bm[g], _TN, preferred_element_type=_F32) for g in gs]
    pieces = []
    for h in hs:
        seg = jnp.where(before_incl, jnp.exp(cs[:, h:h + 1] - cs_t[h:h + 1, :]), 0.0)
        pieces.append(jnp.dot((cb[h // SSM_HPG] * seg).astype(_BF16), xdt[:, cols[h]], preferred_element_type=_F32))
    for g in gs:
        keep = jnp.concatenate([jnp.broadcast_to(e_tot[:, h:h + 1], (P, SSM_STATE))
                                for h in range(g * SSM_HPG, (g + 1) * SSM_HPG)], axis=0)
        h_ref[0, heads_of[g]] = (state[g] * keep + new[g]).reshape(SSM_HPG, P, SSM_STATE)
    y_ref[0] = jnp.concatenate(pieces, axis=1) + jnp.concatenate(from_state, axis=1) * decay_in


def _ssd_scan(xs, bm, cm, log_a, dt, h0, reverse):
    bsz, t, width = xs.shape
    nc = t // SSM_CHUNK
    cidx = (lambda c: nc - 1 - c) if reverse else (lambda c: c)
    seq = lambda w: pl.BlockSpec((1, SSM_CHUNK, w), lambda b, c: (b, cidx(c), 0))
    st_spec = pl.BlockSpec((1,) + h0.shape[1:], lambda b, c: (b, 0, 0, 0))
    gn = SSM_GROUPS * SSM_STATE
    return pl.pallas_call(
        functools.partial(_ssd_kernel, reverse=reverse),
        grid=(bsz, nc),
        in_specs=[seq(width), seq(gn), seq(gn), seq(SSM_HEADS),
                  pl.BlockSpec((1, SSM_HEADS, SSM_CHUNK), lambda b, c: (b, 0, cidx(c))), seq(SSM_HEADS), st_spec],
        out_specs=[seq(width), st_spec],
        out_shape=[jax.ShapeDtypeStruct((bsz, t, width), _F32), jax.ShapeDtypeStruct(h0.shape, _F32)],
        compiler_params=pltpu.CompilerParams(dimension_semantics=("parallel", "arbitrary")),
        name="ssd_rev" if reverse else "ssd_fwd",
    )(xs, bm, cm, log_a, jnp.swapaxes(log_a, 1, 2), dt, h0)


SSM_TILE_T = 128


def _ssm_pre_kernel(xc, xp, xn, w_ref, b_ref, xs_o, bm_o, cm_o):
    first = pl.program_id(1) == 0
    last = pl.program_id(1) == pl.num_programs(1) - 1
    x = xc[0]
    tm = x.shape[0]
    row = lax.broadcasted_iota(jnp.int32, x.shape, 0)
    prev = jnp.where(first, 0.0, xp[0])
    nxt = jnp.where(last, 0.0, xn[0])
    half = SSM_CONV // 2
    acc = w_ref[half:half + 1, :] * x
    for off in range(1, half + 1):
        back = pltpu.roll(x, off, 0)
        fwd = pltpu.roll(x, tm - off, 0)
        for i in range(off):
            back = jnp.where(row == i, prev[SUBLANES - off + i:SUBLANES - off + i + 1, :], back)
            fwd = jnp.where(row == tm - off + i, nxt[i:i + 1, :], fwd)
        acc = acc + w_ref[half - off:half - off + 1, :] * back + w_ref[half + off:half + off + 1, :] * fwd
    u = acc + b_ref[...]
    u = u * jax.nn.sigmoid(u)
    gn = SSM_GROUPS * SSM_STATE
    xs_o[0] = u[:, :SSM_WIDTH]
    bm_o[0] = u[:, SSM_WIDTH:SSM_WIDTH + gn]
    cm_o[0] = u[:, SSM_WIDTH + gn:]


def _ssm_pre(xbc, conv_w, conv_b):
    bsz, t, cols = xbc.shape
    tm = SSM_TILE_T
    blocks_per_tile = tm // SUBLANES
    last_block = t // SUBLANES - 1
    gn = SSM_GROUPS * SSM_STATE
    seq = lambda c: pl.BlockSpec((1, tm, c), lambda b, i: (b, i, 0))
    whole = lambda a: pl.BlockSpec(a.shape, lambda b, i: (0,) * a.ndim)
    conv_w = conv_w.astype(_F32)
    conv_b = conv_b.reshape(1, cols).astype(_F32)
    return pl.pallas_call(
        _ssm_pre_kernel,
        grid=(bsz, t // tm),
        in_specs=[seq(cols),
                  pl.BlockSpec((1, SUBLANES, cols), lambda b, i: (b, jnp.maximum(i * blocks_per_tile - 1, 0), 0)),
                  pl.BlockSpec((1, SUBLANES, cols), lambda b, i: (b, jnp.minimum((i + 1) * blocks_per_tile, last_block), 0)),
                  whole(conv_w), whole(conv_b)],
        out_specs=[seq(SSM_WIDTH), seq(gn), seq(gn)],
        out_shape=[jax.ShapeDtypeStruct((bsz, t, c), _F32) for c in (SSM_WIDTH, gn, gn)],
        compiler_params=pltpu.CompilerParams(dimension_semantics=("parallel", "parallel"), vmem_limit_bytes=VMEM_LIMIT_BYTES),
        name="ssm_pre",
    )(xbc, xbc, xbc, conv_w, conv_b)


def _ssm_post_kernel(yf, yb, xs, z, d_ref, gain_ref, gsum, gwide, o_ref):
    zz = z[0]
    u = (yf[0] + yb[0] + d_ref[...] * xs[0]) * (zz * jax.nn.sigmoid(zz))
    ms = _head_widen(_head_sum(u * u, gsum[...]), gwide[...]) * (SSM_GROUPS / SSM_WIDTH)
    o_ref[0] = (u * lax.rsqrt(ms + NORM_EPS) * gain_ref[...]).astype(o_ref.dtype)


def _ssm_post(y_f, y_b, xs, z, d_skip, gain):
    bsz, t, width = y_f.shape
    tm = SSM_TILE_T
    gsum, gwide = _head_matrices(width, width // SSM_GROUPS)
    seq = pl.BlockSpec((1, tm, width), lambda b, i: (b, i, 0))
    whole = lambda a: pl.BlockSpec(a.shape, lambda b, i: (0,) * a.ndim)
    d_wide = jnp.repeat(d_skip, SSM_HEADDIM).reshape(1, width).astype(_F32)
    gain = gain.reshape(1, width).astype(_F32)
    return pl.pallas_call(
        _ssm_post_kernel,
        grid=(bsz, t // tm),
        in_specs=[seq] * 4 + [whole(d_wide), whole(gain), whole(gsum), whole(gwide)],
        out_specs=seq,
        out_shape=jax.ShapeDtypeStruct((bsz, t, width), _BF16),
        compiler_params=pltpu.CompilerParams(dimension_semantics=("parallel", "parallel")),
        name="ssm_post",
    )(y_f, y_b, xs, z, d_wide, gain, gsum, gwide)


def _mamba2_branch(xbc, dt_raw, z, h0_fwd, h0_bwd, conv_w, conv_b, dt_bias, a_log, d_skip, norm_w, want_output):
    bsz, t, _ = xbc.shape
    xs, bm, cm = _ssm_pre(xbc, conv_w, conv_b)
    dt = jax.nn.softplus(dt_raw.astype(_F32).reshape(bsz, t, 2, SSM_HEADS) + dt_bias)
    log_a = -jnp.exp(a_log) * dt
    heads = lambda h: h.reshape(bsz, SSM_HEADS, SSM_HEADDIM, SSM_STATE)
    y_f, h_f = _ssd_scan(xs, bm, cm, log_a[:, :, 0], dt[:, :, 0], heads(h0_fwd), reverse=False)
    y_b, h_b = _ssd_scan(xs, bm, cm, log_a[:, :, 1], dt[:, :, 1], heads(h0_bwd), reverse=True)
    out = _ssm_post(y_f, y_b, xs, z, d_skip, norm_w) if want_output else None
    return out, h_f, h_b


def _gated_pair_kernel(ya_ref, yb_ref, wa_ref, wb_ref, ga_ref, gb_ref, o_ref):
    pa = jnp.dot(ya_ref[...], wa_ref[...], preferred_element_type=_F32)
    pb = jnp.dot(yb_ref[...], wb_ref[...], preferred_element_type=_F32)
    o_ref[...] = (jax.nn.sigmoid(ga_ref[...]) * pa + jax.nn.sigmoid(gb_ref[...]) * pb).astype(o_ref.dtype)


def _merge_branches(y_a, y_b, gates, w_a, w_b, w_o):
    bsz, t, d = y_a.shape
    m, n = bsz * t, w_a.shape[1]
    tm, tn = min(m, NORM_TILE_M), min(n, MM_TILE_N)
    assert m % tm == 0 and n % tn == 0 and gates.shape[-1] == 2 * n
    rows = pl.BlockSpec((tm, d), lambda j, i: (i, 0))
    cols = pl.BlockSpec((d, tn), lambda j, i: (0, j))
    mixed = pl.pallas_call(
        _gated_pair_kernel,
        grid=(n // tn, m // tm),
        in_specs=[rows, rows, cols, cols,
                  pl.BlockSpec((tm, tn), lambda j, i: (i, j)),
                  pl.BlockSpec((tm, tn), lambda j, i: (i, n // tn + j))],
        out_specs=pl.BlockSpec((tm, tn), lambda j, i: (i, j)),
        out_shape=jax.ShapeDtypeStruct((m, n), _BF16),
        compiler_params=pltpu.CompilerParams(dimension_semantics=("parallel", "parallel"), vmem_limit_bytes=VMEM_LIMIT_BYTES),
        name="gated_pair",
    )(y_a.reshape(m, d), y_b.reshape(m, d), w_a.astype(_BF16), w_b.astype(_BF16), gates.reshape(m, 2 * n), gates.reshape(m, 2 * n))
    return _matmul(mixed, w_o.astype(_BF16)).reshape(bsz, t, -1)


PEER_ROUTE_TOKENS = 256
PEER_TOKEN_BLOCK = 512
PEER_EXPERT_BLOCK = 1024
PEER_KEY_GROUP = 4


def _top_values(x, count):
    rows = lax.broadcasted_iota(jnp.int32, x.shape, 0)
    rank = jnp.full(x.shape, float(count), _F32)
    vals = []
    for it in range(count):
        m = jnp.max(x, axis=0, keepdims=True)
        vals.append(m)
        first = jnp.min(jnp.where(x == m, rows, x.shape[0]), axis=0, keepdims=True)
        hit = rows == first
        rank = jnp.where(hit, float(it), rank)
        x = jnp.where(hit, -jnp.inf, x)
    return vals, rank, x


def _peer_route_kernel(h_ref, wqt_ref, sub_ref, cnt1_ref, e1_ref, rank2_ref, e2_ref):
    qt = lax.dot_general(wqt_ref[...], h_ref[...].astype(_BF16), _NT, preferred_element_type=_F32)
    for h in range(PEER_HEADS):
        sc = []
        for s in range(2):
            lo = (2 * h + s) * PEER_HALF
            sc.append(jnp.dot(sub_ref[2 * h + s], qt[lo:lo + PEER_HALF].astype(_BF16), preferred_element_type=_F32))
        top_a, rank_a, _ = _top_values(sc[0], PEER_TOPK)
        top_b, rank_b, _ = _top_values(sc[1], PEER_TOPK)
        top_b = jnp.concatenate(top_b, axis=0)
        width = [PEER_TOPK // (i + 1) for i in range(PEER_TOPK)]
        cand = jnp.concatenate([top_a[i] + top_b[:width[i]] for i in range(PEER_TOPK)], axis=0)
        best, _, left = _top_values(cand, PEER_TOPK)
        taken = jnp.where(left == cand, 0.0, 1.0)
        norm = best[0] * 0.0
        for val in best:
            norm = norm + jnp.exp(val - best[0])
        cnt1 = jnp.zeros_like(rank_a)
        start = 0
        for i in range(PEER_TOPK):
            used = jnp.sum(taken[start:start + width[i]], axis=0, keepdims=True)
            cnt1 = jnp.where(rank_a == float(i), used, cnt1)
            start += width[i]
        cnt1_ref[h] = cnt1.astype(cnt1_ref.dtype)
        rank2_ref[h] = rank_b.astype(rank2_ref.dtype)
        e1_ref[h] = (jnp.exp(sc[0] - top_a[0]) / norm).astype(e1_ref.dtype)
        e2_ref[h] = jnp.exp(sc[1] - top_b[0:1]).astype(e2_ref.dtype)


def _peer_route(tok, wq, subkeys):
    n, d = tok.shape
    tr = PEER_ROUTE_TOKENS
    wqt = wq.T.astype(_BF16)
    sub = subkeys.reshape(PEER_HEADS * 2, PEER_KEYS, PEER_HALF).astype(_BF16)
    key_spec = pl.BlockSpec((PEER_HEADS, PEER_KEYS, tr), lambda i: (0, 0, i))
    row_shape = jax.ShapeDtypeStruct((PEER_HEADS, PEER_KEYS, n), _F32)
    tile_shape = jax.ShapeDtypeStruct((PEER_HEADS, PEER_KEYS, n), _BF16)
    return pl.pallas_call(
        _peer_route_kernel,
        grid=(n // tr,),
        in_specs=[pl.BlockSpec((tr, d), lambda i: (i, 0)),
                  pl.BlockSpec(wqt.shape, lambda i: (0, 0)),
                  pl.BlockSpec(sub.shape, lambda i: (0, 0, 0))],
        out_specs=[key_spec] * 4,
        out_shape=[row_shape, row_shape, tile_shape, tile_shape],
        compiler_params=pltpu.CompilerParams(dimension_semantics=("parallel",), vmem_limit_bytes=VMEM_LIMIT_BYTES),
        name="peer_route",
    )(tok, wqt, sub)


def _gelu_exact(x):
    return 0.5 * x * (1.0 + lax.erf(x * (1.0 / math.sqrt(2.0))))


def _peer_gate_block(a_ref, w_ref, b_ref, cnt1_ref, e1_ref, rank2_ref, e2_ref, block):
    keys_per_block = PEER_EXPERT_BLOCK // PEER_KEYS
    rt = 2 * SUBLANES
    tb = a_ref.shape[1]
    for h in range(PEER_HEADS):
        for j in range(keys_per_block):
            i1 = block * keys_per_block + j
            b_ref[0, j * PEER_HEADS + h] = jnp.broadcast_to(cnt1_ref[h, pl.ds(i1, 1), :], (rt, tb)).astype(_BF16)
            b_ref[1, j * PEER_HEADS + h] = jnp.broadcast_to(e1_ref[h, pl.ds(i1, 1), :], (rt, tb)).astype(_BF16)

    def row_tile(tile, carry):
        r0 = pl.multiple_of(tile * rt, rt)
        for j0 in range(0, keys_per_block, PEER_KEY_GROUP):
            group = range(j0, j0 + PEER_KEY_GROUP)
            gates = {j: None for j in group}
            for h in range(PEER_HEADS):
                rank2 = rank2_ref[h, pl.ds(r0, rt), :]
                e2 = e2_ref[h, pl.ds(r0, rt), :]
                for j in group:
                    term = jnp.where(rank2 < b_ref[0, j * PEER_HEADS + h], b_ref[1, j * PEER_HEADS + h] * e2, jnp.zeros_like(e2))
                    gates[j] = term if gates[j] is None else gates[j] + term
            for j in group:
                rows = pl.ds(pl.multiple_of(j * PEER_KEYS + r0, rt), rt)
                w_ref[rows, :] = gates[j] * _gelu_exact(a_ref[rows, :]).astype(_BF16)
        return carry

    lax.fori_loop(0, PEER_KEYS // rt, row_tile, 0)


def _peer_expert_kernel(x_ref, u_ref, v_ref, cnt1_ref, e1_ref, rank2_ref, e2_ref, o_ref, a_scr, w_scr, b_scr):
    eb = pl.program_id(1)

    @pl.when(eb == 0)
    def _():
        o_ref[...] = jnp.zeros_like(o_ref)

    a_scr[...] = lax.dot_general(u_ref[...], x_ref[...], _NT, preferred_element_type=_F32)
    _peer_gate_block(a_scr, w_scr, b_scr, cnt1_ref, e1_ref, rank2_ref, e2_ref, eb)
    o_ref[...] += lax.dot_general(w_scr[...], v_ref[...], _TN, preferred_element_type=_F32)


def _peer_ffn(h, wq, subkeys, u_tab, v_tab):
    bsz, t, d = h.shape
    n = bsz * t
    tok = h.reshape(n, d)
    routing = _peer_route(tok, wq, subkeys)
    tb, eb = PEER_TOKEN_BLOCK, PEER_EXPERT_BLOCK
    n_experts = u_tab.shape[0]
    key_spec = pl.BlockSpec((PEER_HEADS, PEER_KEYS, tb), lambda i, e: (0, 0, i))
    tab_spec = pl.BlockSpec((eb, d), lambda i, e: (e, 0))
    out = pl.pallas_call(
        _peer_expert_kernel,
        grid=(n // tb, n_experts // eb),
        in_specs=[pl.BlockSpec((tb, d), lambda i, e: (i, 0)), tab_spec, tab_spec] + [key_spec] * 4,
        out_specs=pl.BlockSpec((tb, d), lambda i, e: (i, 0)),
        out_shape=jax.ShapeDtypeStruct((n, d), _F32),
        scratch_shapes=[pltpu.VMEM((eb, tb), _F32), pltpu.VMEM((eb, tb), _BF16),
                        pltpu.VMEM((2, (eb // PEER_KEYS) * PEER_HEADS, 2 * SUBLANES, tb), _BF16)],
        compiler_params=pltpu.CompilerParams(dimension_semantics=("parallel", "arbitrary"), vmem_limit_bytes=VMEM_LIMIT_BYTES),
        name="peer_experts",
    )(tok, u_tab.astype(_BF16), v_tab.astype(_BF16), *routing)
    return out.reshape(bsz, t, d)


def kernel(x, c, ctx, c_ctx, w_mod, b_mod, norm_pre1, norm_post1, norm_pre2, norm_post2, w_in, rw_mu, rw_w0, rw_w2, rw_a0, rw_a2, rw_g2, rw_k_k, rw_k_a, rw_r_k, rw_ln_w, rw_ln_b, ssm_conv_w, ssm_conv_b, ssm_dt_bias, ssm_a_log, ssm_d, ssm_norm_w, w_branch_a, w_branch_b, w_out, peer_wq, peer_subkeys, peer_u, peer_v):
    bsz, seq, _ = x.shape
    rows = seq // GRID_W
    depth = w_mod.shape[0]
    assert depth == 1
    rw_zero = jnp.zeros((bsz, RW_WIDTH // LANES, LANES, LANES), _F32)
    ssm_zero = jnp.zeros((bsz, SSM_GROUPS, SSM_HPG, SSM_HEADDIM, SSM_STATE), _F32)
    l = 0
    mod_x = (jax.nn.silu(c) @ w_mod[l] + b_mod[l])[:, None, :]
    mod_c = (jax.nn.silu(c_ctx) @ w_mod[l] + b_mod[l])[None, None, :]
    sh1x, sc1x, g1x, sh2x, sc2x, g2x = jnp.split(mod_x, N_MOD, axis=-1)
    sh1c, sc1c, g1c, sh2c, sc2c, g2c = jnp.split(mod_c, N_MOD, axis=-1)

    in_w = _in_proj_weights(w_in[l])
    hx = _norm_mod(x, norm_pre1[l], sc1x, sh1x)
    r_x, k_x, v_x, lora_x, z_x, xbc_x, dt_x, gate_x = _in_proj(hx, in_w, h_ssm=_raster_to_column(hx, rows))
    r_c, k_c, v_c, lora_c, z_c, xbc_c, dt_c, gate_c = _in_proj(_norm_mod(ctx, norm_pre1[l], sc1c, sh1c), in_w)

    rw_params = (rw_w0[l], rw_w2[l], rw_a0[l], rw_a2[l], rw_g2[l], rw_k_k[l], rw_k_a[l], rw_r_k[l], rw_ln_w[l], rw_ln_b[l])
    _, s_fwd, s_bwd = _rwkv7_branch(r_c, k_c, v_c, lora_c, rw_mu[l], rw_zero, rw_zero, *rw_params, want_output=False)
    ya_x, _, _ = _rwkv7_branch(r_x, k_x, v_x, lora_x, rw_mu[l], s_fwd, s_bwd, *rw_params, want_output=True)

    ssm_params = (ssm_conv_w[l], ssm_conv_b[l], ssm_dt_bias[l], ssm_a_log[l], ssm_d[l], ssm_norm_w[l])
    _, h_fwd, h_bwd = _mamba2_branch(xbc_c, dt_c, z_c, ssm_zero, ssm_zero, *ssm_params, want_output=False)
    yb_x, _, _ = _mamba2_branch(xbc_x, dt_x, z_x, h_fwd, h_bwd, *ssm_params, want_output=True)
    yb_x = _column_to_raster(yb_x, rows)

    mix_x = _merge_branches(ya_x, yb_x, gate_x, w_branch_a[l], w_branch_b[l], w_out[l])
    x, h2x = _residual(x, mix_x, g1x, norm_post1[l], next_mod=(norm_pre2[l], sc2x, sh2x))
    return _residual(x, _peer_ffn(h2x, peer_wq[l], peer_subkeys[l], peer_u[l], peer_v[l]), g2x, norm_post2[l])
```
